```python
import math
import jax, jax.numpy as jnp
from jax import lax
import numpy as np

D_MODEL = 2048
BATCH = 8
SEQ = 8192
DEPTH = 4

HEAD_DIM = 64
SGU_WIDTH = D_MODEL // 2
N_SGU_HEADS = SGU_WIDTH // HEAD_DIM
SGU_HEAD_DIM = SGU_WIDTH // N_SGU_HEADS
CHUNK = 128
ATTN_WIDTH = D_MODEL - SGU_WIDTH
N_Q_HEADS = ATTN_WIDTH // HEAD_DIM
N_KV_HEADS = 4
GROUP = N_Q_HEADS // N_KV_HEADS
WINDOW = 128
BLOCK = 128
NUM_BUCKETS = 32
MAX_DISTANCE = 128
IN_WIDTH = 2 * SGU_WIDTH + N_Q_HEADS * HEAD_DIM + 2 * N_KV_HEADS * HEAD_DIM
D_FF = -(-8 * D_MODEL // (3 * 256)) * 256
EPS = 1e-6
NEG_INF = -1e30

kernel_name = "hybrid_sgu_swa_sink_trunk"


def rms_norm(x, g):
    xf = x.astype(jnp.float32)
    y = xf * lax.rsqrt(jnp.mean(xf * xf, axis=-1, keepdims=True) + EPS)
    return (y * g.astype(jnp.float32)).astype(x.dtype)


def t5_causal_bucket(dist):
    n = jnp.maximum(dist, 0)
    max_exact = NUM_BUCKETS // 2
    nf = jnp.maximum(n, 1).astype(jnp.float32)
    large = max_exact + (jnp.log(nf / max_exact) / math.log(MAX_DISTANCE / max_exact)
                         * (NUM_BUCKETS - max_exact)).astype(jnp.int32)
    large = jnp.minimum(large, NUM_BUCKETS - 1)
    return jnp.where(n < max_exact, n, large)


def chunked_sgu(z, v_norm_g, w_s, b_s):
    b, s, _ = z.shape
    u, v = jnp.split(z, 2, axis=-1)
    v = v.reshape(b, s // CHUNK, CHUNK, N_SGU_HEADS, SGU_HEAD_DIM)
    v = rms_norm(v, v_norm_g)
    w = w_s * jnp.tril(jnp.ones((CHUNK, CHUNK), w_s.dtype))
    gate = jnp.einsum('hts,bcshd->bcthd', w, v) + b_s.T[None, None, :, :, None]
    return u * gate.reshape(b, s, SGU_WIDTH)


def swa_sink_attention(q, k, v, q_norm_g, k_norm_g, sinks, rel_bias):
    b, s, _ = q.shape
    nb = s // BLOCK
    q = rms_norm(q.reshape(b, nb, BLOCK, N_KV_HEADS, GROUP, HEAD_DIM), q_norm_g)
    k = rms_norm(k.reshape(b, nb, BLOCK, N_KV_HEADS, HEAD_DIM), k_norm_g)
    v = v.reshape(b, nb, BLOCK, N_KV_HEADS, HEAD_DIM)

    def band(t):
        prev = jnp.concatenate([jnp.zeros_like(t[:, :1]), t[:, :-1]], axis=1)
        return jnp.concatenate([prev, t], axis=2)

    kb, vb = band(k), band(v)
    scale = 1.0 / math.sqrt(HEAD_DIM)
    scores = jnp.einsum('bnqkgd,bnskd->bnkgqs', q, kb).astype(jnp.float32) * scale

    qi = jnp.arange(BLOCK)[:, None]
    kj = jnp.arange(2 * BLOCK)[None, :]
    dist = qi + BLOCK - kj
    bias = rel_bias[t5_causal_bucket(dist)].astype(jnp.float32)
    bias = jnp.transpose(bias, (2, 0, 1)).reshape(N_KV_HEADS, GROUP, BLOCK, 2 * BLOCK)
    in_window = (dist >= 0) & (dist < WINDOW)
    key_pos = jnp.arange(nb)[:, None] * BLOCK - BLOCK + kj
    valid = in_window[None] & (key_pos >= 0)[:, None, :]
    scores = jnp.where(valid[None, :, None, None], scores + bias, NEG_INF)

    sink = jnp.broadcast_to(
        sinks.astype(jnp.float32).reshape(N_KV_HEADS, GROUP)[None, None, :, :, None, None],
        scores.shape[:-1] + (1,))
    probs = jax.nn.softmax(jnp.concatenate([scores, sink], axis=-1), axis=-1)[..., :-1]
    out = jnp.einsum('bnkgqs,bnskd->bnqkgd', probs.astype(vb.dtype), vb)
    return out.reshape(b, s, ATTN_WIDTH)


def _fwd_setup_inputs(seed: int = 0) -> dict:
    key = jax.random.key(seed)
    ks = jax.random.split(key, 20)
    f32 = jnp.float32
    nrm = lambda k, shape, sc: jax.random.normal(k, shape, f32) * sc
    gain = lambda k, shape: 1.0 + 0.02 * jax.random.normal(k, shape, f32)
    return {
        "x": jax.random.normal(ks[0], (BATCH, SEQ, D_MODEL), f32),
        "rel_bias": nrm(ks[1], (NUM_BUCKETS, N_Q_HEADS), 0.1),
        "norm1_g": gain(ks[2], (DEPTH, D_MODEL)),
        "w_in": nrm(ks[3], (DEPTH, D_MODEL, IN_WIDTH), D_MODEL ** -0.5),
        "sgu_norm_g": gain(ks[4], (DEPTH, N_SGU_HEADS, SGU_HEAD_DIM)),
        "sgu_w": nrm(ks[5], (DEPTH, N_SGU_HEADS, CHUNK, CHUNK), CHUNK ** -0.5),
        "sgu_b": gain(ks[6], (DEPTH, N_SGU_HEADS, CHUNK)),
        "q_norm_g": gain(ks[7], (DEPTH, HEAD_DIM)),
        "k_norm_g": gain(ks[8], (DEPTH, HEAD_DIM)),
        "sinks": nrm(ks[9], (DEPTH, N_Q_HEADS), 0.5),
        "out_norm_a": gain(ks[10], (DEPTH, SGU_WIDTH)),
        "out_norm_b": gain(ks[11], (DEPTH, ATTN_WIDTH)),
        "w_out": nrm(ks[12], (DEPTH, D_MODEL, D_MODEL), D_MODEL ** -0.5),
        "norm2_g": gain(ks[13], (DEPTH, D_MODEL)),
        "w_gate": nrm(ks[14], (DEPTH, D_MODEL, D_FF), D_MODEL ** -0.5),
        "w_up": nrm(ks[15], (DEPTH, D_MODEL, D_FF), D_MODEL ** -0.5),
        "w_down": nrm(ks[16], (DEPTH, D_FF, D_MODEL), D_FF ** -0.5),
    }


def _fwd_reference(x, rel_bias, norm1_g, w_in, sgu_norm_g, sgu_w, sgu_b, q_norm_g, k_norm_g,
              sinks, out_norm_a, out_norm_b, w_out, norm2_g, w_gate, w_up, w_down):
    q_end = 2 * SGU_WIDTH + N_Q_HEADS * HEAD_DIM
    k_end = q_end + N_KV_HEADS * HEAD_DIM
    for l in range(DEPTH):
        h = rms_norm(x, norm1_g[l])
        z = h @ w_in[l]
        z_sgu = jax.nn.gelu(z[..., :2 * SGU_WIDTH], approximate=False)
        out_a = chunked_sgu(z_sgu, sgu_norm_g[l], sgu_w[l], sgu_b[l])
        out_b = swa_sink_attention(z[..., 2 * SGU_WIDTH:q_end], z[..., q_end:k_end],
                                   z[..., k_end:], q_norm_g[l], k_norm_g[l], sinks[l],
                                   rel_bias)
        mixed = jnp.concatenate([rms_norm(out_a, out_norm_a[l]),
                                 rms_norm(out_b, out_norm_b[l])], axis=-1)
        x = x + mixed @ w_out[l]
        h2 = rms_norm(x, norm2_g[l])
        x = x + (jax.nn.silu(h2 @ w_gate[l]) * (h2 @ w_up[l])) @ w_down[l]
    return x


import jax as _jax
import jax.numpy as _jnp

TWIN_FORMAT = 'train_step'
FWD_PARAMS = ['x', 'rel_bias', 'norm1_g', 'w_in', 'sgu_norm_g', 'sgu_w', 'sgu_b', 'q_norm_g', 'k_norm_g', 'sinks', 'out_norm_a', 'out_norm_b', 'w_out', 'norm2_g', 'w_gate', 'w_up', 'w_down']
TWIN_WEIGHTS = ['rel_bias', 'norm1_g', 'w_in', 'sgu_norm_g', 'sgu_w', 'sgu_b', 'q_norm_g', 'k_norm_g', 'sinks', 'out_norm_a', 'out_norm_b', 'w_out', 'norm2_g', 'w_gate', 'w_up', 'w_down']
TWIN_DIFF_INPUT = 'x'
TWIN_INPUTS = ['x', 'rel_bias', 'norm1_g', 'w_in', 'sgu_norm_g', 'sgu_w', 'sgu_b', 'q_norm_g', 'k_norm_g', 'sinks', 'out_norm_a', 'out_norm_b', 'w_out', 'norm2_g', 'w_gate', 'w_up', 'w_down', 'loss_target', 'm_rel_bias', 'm_norm1_g', 'm_w_in', 'm_sgu_norm_g', 'm_sgu_w', 'm_sgu_b', 'm_q_norm_g', 'm_k_norm_g', 'm_sinks', 'm_out_norm_a', 'm_out_norm_b', 'm_w_out', 'm_norm2_g', 'm_w_gate', 'm_w_up', 'm_w_down', 'v_rel_bias', 'v_norm1_g', 'v_w_in', 'v_sgu_norm_g', 'v_sgu_w', 'v_sgu_b', 'v_q_norm_g', 'v_k_norm_g', 'v_sinks', 'v_out_norm_a', 'v_out_norm_b', 'v_w_out', 'v_norm2_g', 'v_w_gate', 'v_w_up', 'v_w_down']
TWIN_OUTPUTS = ['loss', 'grad_x', 'grad_rel_bias', 'grad_norm1_g', 'grad_w_in', 'grad_sgu_norm_g', 'grad_sgu_w', 'grad_sgu_b', 'grad_q_norm_g', 'grad_k_norm_g', 'grad_sinks', 'grad_out_norm_a', 'grad_out_norm_b', 'grad_w_out', 'grad_norm2_g', 'grad_w_gate', 'grad_w_up', 'grad_w_down', 'delta_rel_bias', 'delta_norm1_g', 'delta_w_in', 'delta_sgu_norm_g', 'delta_sgu_w', 'delta_sgu_b', 'delta_q_norm_g', 'delta_k_norm_g', 'delta_sinks', 'delta_out_norm_a', 'delta_out_norm_b', 'delta_w_out', 'delta_norm2_g', 'delta_w_gate', 'delta_w_up', 'delta_w_down', 'new_m_rel_bias', 'new_m_norm1_g', 'new_m_w_in', 'new_m_sgu_norm_g', 'new_m_sgu_w', 'new_m_sgu_b', 'new_m_q_norm_g', 'new_m_k_norm_g', 'new_m_sinks', 'new_m_out_norm_a', 'new_m_out_norm_b', 'new_m_w_out', 'new_m_norm2_g', 'new_m_w_gate', 'new_m_w_up', 'new_m_w_down', 'new_v_rel_bias', 'new_v_norm1_g', 'new_v_w_in', 'new_v_sgu_norm_g', 'new_v_sgu_w', 'new_v_sgu_b', 'new_v_q_norm_g', 'new_v_k_norm_g', 'new_v_sinks', 'new_v_out_norm_a', 'new_v_out_norm_b', 'new_v_w_out', 'new_v_norm2_g', 'new_v_w_gate', 'new_v_w_up', 'new_v_w_down']
TWIN_LEAF_KINDS = {'loss': 'loss', 'grad_x': 'grad_x', 'grad_rel_bias': 'grad_w', 'grad_norm1_g': 'grad_w', 'grad_w_in': 'grad_w', 'grad_sgu_norm_g': 'grad_w', 'grad_sgu_w': 'grad_w', 'grad_sgu_b': 'grad_w', 'grad_q_norm_g': 'grad_w', 'grad_k_norm_g': 'grad_w', 'grad_sinks': 'grad_w', 'grad_out_norm_a': 'grad_w', 'grad_out_norm_b': 'grad_w', 'grad_w_out': 'grad_w', 'grad_norm2_g': 'grad_w', 'grad_w_gate': 'grad_w', 'grad_w_up': 'grad_w', 'grad_w_down': 'grad_w', 'delta_rel_bias': 'delta_w', 'delta_norm1_g': 'delta_w', 'delta_w_in': 'delta_w', 'delta_sgu_norm_g': 'delta_w', 'delta_sgu_w': 'delta_w', 'delta_sgu_b': 'delta_w', 'delta_q_norm_g': 'delta_w', 'delta_k_norm_g': 'delta_w', 'delta_sinks': 'delta_w', 'delta_out_norm_a': 'delta_w', 'delta_out_norm_b': 'delta_w', 'delta_w_out': 'delta_w', 'delta_norm2_g': 'delta_w', 'delta_w_gate': 'delta_w', 'delta_w_up': 'delta_w', 'delta_w_down': 'delta_w', 'new_m_rel_bias': 'new_m', 'new_m_norm1_g': 'new_m', 'new_m_w_in': 'new_m', 'new_m_sgu_norm_g': 'new_m', 'new_m_sgu_w': 'new_m', 'new_m_sgu_b': 'new_m', 'new_m_q_norm_g': 'new_m', 'new_m_k_norm_g': 'new_m', 'new_m_sinks': 'new_m', 'new_m_out_norm_a': 'new_m', 'new_m_out_norm_b': 'new_m', 'new_m_w_out': 'new_m', 'new_m_norm2_g': 'new_m', 'new_m_w_gate': 'new_m', 'new_m_w_up': 'new_m', 'new_m_w_down': 'new_m', 'new_v_rel_bias': 'new_v', 'new_v_norm1_g': 'new_v', 'new_v_w_in': 'new_v', 'new_v_sgu_norm_g': 'new_v', 'new_v_sgu_w': 'new_v', 'new_v_sgu_b': 'new_v', 'new_v_q_norm_g': 'new_v', 'new_v_k_norm_g': 'new_v', 'new_v_sinks': 'new_v', 'new_v_out_norm_a': 'new_v', 'new_v_out_norm_b': 'new_v', 'new_v_w_out': 'new_v', 'new_v_norm2_g': 'new_v', 'new_v_w_gate': 'new_v', 'new_v_w_up': 'new_v', 'new_v_w_down': 'new_v'}


def _forward(args):
    return _fwd_reference(*[args[k] for k in FWD_PARAMS])


def _output_shape():
    def fwd():
        inp = _fwd_setup_inputs(0)
        return _fwd_reference(*[inp[k] for k in FWD_PARAMS])
    out = _jax.eval_shape(fwd)
    return out.shape, out.dtype

N_MICROBATCH = 1
ADAM_LR = 0.001
ADAM_B1 = 0.9
ADAM_B2 = 0.999
ADAM_EPS = 1e-08
ADAM_WD = 0.01
ADAM_STEP = 10
PER_EXAMPLE_BATCH_AXIS = {'x': 0, 'loss_target': 0}
SHARED_INPUTS = []
_WEIGHT_DTYPES = {'rel_bias': _jnp.float32, 'norm1_g': _jnp.float32, 'w_in': _jnp.float32, 'sgu_norm_g': _jnp.float32, 'sgu_w': _jnp.float32, 'sgu_b': _jnp.float32, 'q_norm_g': _jnp.float32, 'k_norm_g': _jnp.float32, 'sinks': _jnp.float32, 'out_norm_a': _jnp.float32, 'out_norm_b': _jnp.float32, 'w_out': _jnp.float32, 'norm2_g': _jnp.float32, 'w_gate': _jnp.float32, 'w_up': _jnp.float32, 'w_down': _jnp.float32}
MOMENT_SCALE = {'rel_bias': 1.445701e+00, 'norm1_g': 1.354805e+01, 'w_in': 9.839485e+00, 'sgu_norm_g': 6.593761e-01, 'sgu_w': 3.568590e-01, 'sgu_b': 9.240627e-01, 'q_norm_g': 4.052479e+00, 'k_norm_g': 4.063761e+00, 'sinks': 1.113879e+00, 'out_norm_a': 4.003183e+01, 'out_norm_b': 4.030603e+01, 'w_out': 1.962244e+01, 'norm2_g': 2.522859e+01, 'w_gate': 1.438652e+00, 'w_up': 1.898383e+00, 'w_down': 3.057243e+00}


def _to_microbatches(a, axis):
    t = _jnp.moveaxis(a, axis, 0)
    t = t.reshape((N_MICROBATCH, t.shape[0] // N_MICROBATCH) + t.shape[1:])
    return _jnp.moveaxis(t, 1, axis + 1)


def setup_inputs(seed: int = 0) -> dict:
    inp = _fwd_setup_inputs(seed)
    key = _jax.random.fold_in(_jax.random.key(seed), 7919)
    shape, _ = _output_shape()
    out = dict(inp)
    out["loss_target"] = _jax.random.normal(_jax.random.fold_in(key, 0), shape, _jnp.float32)
    for i, name in enumerate(TWIN_WEIGHTS):
        w = inp[name].astype(_jnp.float32)
        if MOMENT_SCALE is None:
            s = _jnp.sqrt(_jnp.mean(_jnp.square(w)) + 1e-30)
        else:
            s = MOMENT_SCALE[name]
        km, kv = _jax.random.split(_jax.random.fold_in(key, i + 1))
        out[name] = w
        out["m_" + name] = s * _jax.random.normal(km, w.shape, _jnp.float32)
        out["v_" + name] = (s * s) * _jax.random.uniform(kv, w.shape, _jnp.float32, 0.5, 1.5)
    if N_MICROBATCH > 1:
        for name, axis in PER_EXAMPLE_BATCH_AXIS.items():
            out[name] = _to_microbatches(out[name], axis)
    return {'x': out['x'], 'rel_bias': out['rel_bias'], 'norm1_g': out['norm1_g'], 'w_in': out['w_in'], 'sgu_norm_g': out['sgu_norm_g'], 'sgu_w': out['sgu_w'], 'sgu_b': out['sgu_b'], 'q_norm_g': out['q_norm_g'], 'k_norm_g': out['k_norm_g'], 'sinks': out['sinks'], 'out_norm_a': out['out_norm_a'], 'out_norm_b': out['out_norm_b'], 'w_out': out['w_out'], 'norm2_g': out['norm2_g'], 'w_gate': out['w_gate'], 'w_up': out['w_up'], 'w_down': out['w_down'], 'loss_target': out['loss_target'], 'm_rel_bias': out['m_rel_bias'], 'm_norm1_g': out['m_norm1_g'], 'm_w_in': out['m_w_in'], 'm_sgu_norm_g': out['m_sgu_norm_g'], 'm_sgu_w': out['m_sgu_w'], 'm_sgu_b': out['m_sgu_b'], 'm_q_norm_g': out['m_q_norm_g'], 'm_k_norm_g': out['m_k_norm_g'], 'm_sinks': out['m_sinks'], 'm_out_norm_a': out['m_out_norm_a'], 'm_out_norm_b': out['m_out_norm_b'], 'm_w_out': out['m_w_out'], 'm_norm2_g': out['m_norm2_g'], 'm_w_gate': out['m_w_gate'], 'm_w_up': out['m_w_up'], 'm_w_down': out['m_w_down'], 'v_rel_bias': out['v_rel_bias'], 'v_norm1_g': out['v_norm1_g'], 'v_w_in': out['v_w_in'], 'v_sgu_norm_g': out['v_sgu_norm_g'], 'v_sgu_w': out['v_sgu_w'], 'v_sgu_b': out['v_sgu_b'], 'v_q_norm_g': out['v_q_norm_g'], 'v_k_norm_g': out['v_k_norm_g'], 'v_sinks': out['v_sinks'], 'v_out_norm_a': out['v_out_norm_a'], 'v_out_norm_b': out['v_out_norm_b'], 'v_w_out': out['v_w_out'], 'v_norm2_g': out['v_norm2_g'], 'v_w_gate': out['v_w_gate'], 'v_w_up': out['v_w_up'], 'v_w_down': out['v_w_down']}


def _loss(weights, diff, rest, loss_target):
    with _jax.named_scope("forward"):
        args = {**rest, TWIN_DIFF_INPUT: diff, **{k: w.astype(_WEIGHT_DTYPES[k]) for k, w in weights.items()}}
        y = _forward(args)
    with _jax.named_scope("loss_head"):
        err = _jnp.square(y.astype(_jnp.float32) - loss_target)
        return 0.5 * _jnp.sum(_jnp.mean(err, axis=-1)) if err.ndim else 0.5 * err


def _adamw(w, g, m, v):
    m = ADAM_B1 * m + (1.0 - ADAM_B1) * g
    v = ADAM_B2 * v + (1.0 - ADAM_B2) * _jnp.square(g)
    m_hat = m / (1.0 - ADAM_B1 ** ADAM_STEP)
    v_hat = v / (1.0 - ADAM_B2 ** ADAM_STEP)
    delta = -ADAM_LR * (m_hat / (_jnp.sqrt(v_hat) + ADAM_EPS) + ADAM_WD * w)
    return delta, m, v


def reference(x, rel_bias, norm1_g, w_in, sgu_norm_g, sgu_w, sgu_b, q_norm_g, k_norm_g, sinks, out_norm_a, out_norm_b, w_out, norm2_g, w_gate, w_up, w_down, loss_target, m_rel_bias, m_norm1_g, m_w_in, m_sgu_norm_g, m_sgu_w, m_sgu_b, m_q_norm_g, m_k_norm_g, m_sinks, m_out_norm_a, m_out_norm_b, m_w_out, m_norm2_g, m_w_gate, m_w_up, m_w_down, v_rel_bias, v_norm1_g, v_w_in, v_sgu_norm_g, v_sgu_w, v_sgu_b, v_q_norm_g, v_k_norm_g, v_sinks, v_out_norm_a, v_out_norm_b, v_w_out, v_norm2_g, v_w_gate, v_w_up, v_w_down):
    given = dict(x=x, rel_bias=rel_bias, norm1_g=norm1_g, w_in=w_in, sgu_norm_g=sgu_norm_g, sgu_w=sgu_w, sgu_b=sgu_b, q_norm_g=q_norm_g, k_norm_g=k_norm_g, sinks=sinks, out_norm_a=out_norm_a, out_norm_b=out_norm_b, w_out=w_out, norm2_g=norm2_g, w_gate=w_gate, w_up=w_up, w_down=w_down, loss_target=loss_target, m_rel_bias=m_rel_bias, m_norm1_g=m_norm1_g, m_w_in=m_w_in, m_sgu_norm_g=m_sgu_norm_g, m_sgu_w=m_sgu_w, m_sgu_b=m_sgu_b, m_q_norm_g=m_q_norm_g, m_k_norm_g=m_k_norm_g, m_sinks=m_sinks, m_out_norm_a=m_out_norm_a, m_out_norm_b=m_out_norm_b, m_w_out=m_w_out, m_norm2_g=m_norm2_g, m_w_gate=m_w_gate, m_w_up=m_w_up, m_w_down=m_w_down, v_rel_bias=v_rel_bias, v_norm1_g=v_norm1_g, v_w_in=v_w_in, v_sgu_norm_g=v_sgu_norm_g, v_sgu_w=v_sgu_w, v_sgu_b=v_sgu_b, v_q_norm_g=v_q_norm_g, v_k_norm_g=v_k_norm_g, v_sinks=v_sinks, v_out_norm_a=v_out_norm_a, v_out_norm_b=v_out_norm_b, v_w_out=v_w_out, v_norm2_g=v_norm2_g, v_w_gate=v_w_gate, v_w_up=v_w_up, v_w_down=v_w_down)
    weights = {n: given[n] for n in TWIN_WEIGHTS}
    shared = {n: given[n] for n in SHARED_INPUTS}
    per_example = {n: given[n] for n in ['x']}
    grad_fn = _jax.value_and_grad(_loss, argnums=(0, 1))

    def one_microbatch(ex, loss_target):
        ex = dict(ex)
        diff = ex.pop(TWIN_DIFF_INPUT)
        return grad_fn(weights, diff, {**shared, **ex}, loss_target)

    if N_MICROBATCH == 1:
        loss, (grad_w, grad_x) = one_microbatch(per_example, given["loss_target"])
    else:
        def body(carry, xs):
            loss_sum, grad_sum = carry
            l_k, (gw_k, gx_k) = one_microbatch(xs[0], xs[1])
            with _jax.named_scope("update"):
                return (loss_sum + l_k, _jax.tree.map(_jnp.add, grad_sum, gw_k)), gx_k

        init = (_jnp.zeros((), _jnp.float32), _jax.tree.map(_jnp.zeros_like, weights))
        (loss, grad_w), grad_x = _jax.lax.scan(body, init, (per_example, given["loss_target"]))
    with _jax.named_scope("update"):
        delta_w, new_m, new_v = {}, {}, {}
        for n in TWIN_WEIGHTS:
            delta_w[n], new_m[n], new_v[n] = _adamw(weights[n], grad_w[n], given["m_" + n], given["v_" + n])
    return (loss, grad_x, *[grad_w[n] for n in TWIN_WEIGHTS], *[delta_w[n] for n in TWIN_WEIGHTS],
            *[new_m[n] for n in TWIN_WEIGHTS], *[new_v[n] for n in TWIN_WEIGHTS])
```

```python
import functools
import math

import numpy as np
import jax
import jax.numpy as jnp
from jax import lax
from jax.experimental import pallas as pl
from jax.experimental.pallas import tpu as pltpu

F32 = jnp.float32
BF16 = jnp.bfloat16

N_DEV = 8
HEAD_DIM = 64
CHUNK = 128
BLOCK = 128
N_KV_HEADS = 4
GROUP = 4
NUM_BUCKETS = 32
MAX_DISTANCE = 128
EPS = 1e-6
NEG_INF = -1e30
LANES = 128
VMEM_LIMIT = 56 * 2 ** 20

ADAM_LR = 0.001
ADAM_B1 = 0.9
ADAM_B2 = 0.999
ADAM_EPS = 1e-08
ADAM_WD = 0.01
ADAM_STEP = 10

MESH = pl.DeviceIdType.MESH
ANY = pl.BlockSpec(memory_space=pl.ANY)


def _params(n_axes):
    return pltpu.CompilerParams(dimension_semantics=("arbitrary",) * n_axes, vmem_limit_bytes=VMEM_LIMIT)


def _my_place():
    return lax.axis_index("x"), lax.axis_index("y"), lax.axis_index("c")


def _all_gather(name, arrs):
    n = len(arrs)

    def body(*refs):
        ins, outs = refs[:n], refs[n:2 * n]
        send_sems, recv_sems, local_sems = refs[2 * n:]
        x, y, c = _my_place()
        sibling = (x, y, 1 - c)
        chips = [(1 - x, y), (x, 1 - y), (1 - x, 1 - y)]

        def slot(a, px, py, pc):
            return outs[a].at[4 * px + 2 * py + pc]

        def copy(a, k, block, to, src=None):
            return pltpu.make_async_remote_copy(
                src_ref=slot(a, *block) if src is None else src, dst_ref=slot(a, *block),
                send_sem=send_sems.at[7 * a + k], recv_sem=recv_sems.at[7 * a + k],
                device_id=to, device_id_type=MESH)

        started = []
        for a in range(n):
            mine = pltpu.make_async_copy(ins[a], slot(a, x, y, c), local_sems.at[a])
            mine.start()
            started.append(mine)
        sends = []
        for a in range(n):
            first = [copy(a, 0, (x, y, c), sibling, src=ins[a])]
            first += [copy(a, 1 + j, (x, y, c), (*chip, c), src=ins[a]) for j, chip in enumerate(chips)]
            for cp in first:
                cp.start()
            sends += first
        for a in range(n):
            for j, chip in enumerate(chips):
                copy(a, 1 + j, (*chip, c), (x, y, c)).wait_recv()
                fwd = copy(a, 4 + j, (*chip, c), sibling)
                fwd.start()
                sends.append(fwd)
        for a in range(n):
            copy(a, 0, sibling, (x, y, c)).wait_recv()
            for j, chip in enumerate(chips):
                copy(a, 4 + j, (*chip, 1 - c), (x, y, c)).wait_recv()
        for cp in sends:
            cp.wait_send()
        for cp in started:
            cp.wait()

    return pl.pallas_call(
        body, name=name,
        out_shape=[jax.ShapeDtypeStruct((N_DEV,) + a.shape, a.dtype) for a in arrs],
        in_specs=[ANY] * n, out_specs=[ANY] * n,
        scratch_shapes=[pltpu.SemaphoreType.DMA((7 * n,)), pltpu.SemaphoreType.DMA((7 * n,)),
                        pltpu.SemaphoreType.DMA((n,))],
    )(*arrs)


def _all_to_all(name, arrs):
    n = len(arrs)

    def body(*refs):
        ins, outs = refs[:n], refs[n:2 * n]
        send_sems, recv_sems, local_sems = refs[2 * n:]
        x, y, c = _my_place()
        me = 4 * x + 2 * y + c

        def peer(k):
            px = 1 - x if k & 4 else x
            py = 1 - y if k & 2 else y
            pc = 1 - c if k & 1 else c
            return px, py, pc

        def copy(a, k):
            px, py, pc = peer(k)
            return pltpu.make_async_remote_copy(
                src_ref=ins[a].at[4 * px + 2 * py + pc], dst_ref=outs[a].at[me],
                send_sem=send_sems.at[7 * a + k - 1], recv_sem=recv_sems.at[7 * a + k - 1],
                device_id=(px, py, pc), device_id_type=MESH)

        def arrival(a, k):
            px, py, pc = peer(k)
            return pltpu.make_async_remote_copy(
                src_ref=ins[a].at[me], dst_ref=outs[a].at[4 * px + 2 * py + pc],
                send_sem=send_sems.at[7 * a + k - 1], recv_sem=recv_sems.at[7 * a + k - 1],
                device_id=(px, py, pc), device_id_type=MESH)

        local = []
        for a in range(n):
            cp = pltpu.make_async_copy(ins[a].at[me], outs[a].at[me], local_sems.at[a])
            cp.start()
            local.append(cp)
        sends = []
        for k in (1, 2, 4, 3, 5, 6, 7):
            for a in range(n):
                cp = copy(a, k)
                cp.start()
                sends.append(cp)
        for a in range(n):
            for k in range(1, N_DEV):
                arrival(a, k).wait_recv()
        for cp in sends:
            cp.wait_send()
        for cp in local:
            cp.wait()

    return pl.pallas_call(
        body, name=name,
        out_shape=[jax.ShapeDtypeStruct(a.shape, a.dtype) for a in arrs],
        in_specs=[ANY] * n, out_specs=[ANY] * n,
        scratch_shapes=[pltpu.SemaphoreType.DMA((7 * n,)), pltpu.SemaphoreType.DMA((7 * n,)),
                        pltpu.SemaphoreType.DMA((n,))],
    )(*arrs)


def _sum_slots(name, a, rows):
    _, R, C = a.shape

    def body(a_ref, o_ref):
        acc = a_ref[0].astype(F32)
        for s in range(1, N_DEV):
            acc = acc + a_ref[s].astype(F32)
        o_ref[...] = acc

    return pl.pallas_call(
        body, name=name, grid=(R // rows,),
        in_specs=[pl.BlockSpec((N_DEV, rows, C), lambda i: (0, i, 0))],
        out_specs=pl.BlockSpec((rows, C), lambda i: (i, 0)),
        out_shape=jax.ShapeDtypeStruct((R, C), F32), compiler_params=_params(1),
    )(a)


_DIMS = {"nn": (((1,), (0,)), ((), ())), "nt": (((1,), (1,)), ((), ())), "tn": (((0,), (0,)), ((), ()))}


def _mm(name, mode, a_list, b_list, pairs, n_acc, M, N, K, tm, tn, tk, epilogue, out_dtypes, extras=()):
    tm, tn, tk = min(tm, M), min(tn, N), min(tk, K)
    assert M % tm == 0 and N % tn == 0 and K % tk == 0, (name, M, N, K, tm, tn, tk)
    nk = K // tk
    na, nb, ne, no = len(a_list), len(b_list), len(extras), len(out_dtypes)
    dims = _DIMS[mode]
    a_spec = (pl.BlockSpec((tk, tm), lambda j, i, k: (k, i)) if mode == "tn"
              else pl.BlockSpec((tm, tk), lambda j, i, k: (i, k)))
    b_spec = (pl.BlockSpec((tn, tk), lambda j, i, k: (j, k)) if mode == "nt"
              else pl.BlockSpec((tk, tn), lambda j, i, k: (k, j)))
    o_spec = pl.BlockSpec((tm, tn), lambda j, i, k: (i, j))

    def body(*refs):
        a_refs, b_refs = refs[:na], refs[na:na + nb]
        e_refs = refs[na + nb:na + nb + ne]
        o_refs = refs[na + nb + ne:na + nb + ne + no]
        acc_refs = refs[na + nb + ne + no:]

        def partial(oi):
            tot = None
            for ai, bi, ti in pairs:
                if ti == oi:
                    d = lax.dot_general(a_refs[ai][...], b_refs[bi][...], dims, preferred_element_type=F32)
                    tot = d if tot is None else tot + d
            return tot

        def finish(accs):
            outs = epilogue(accs, [e[...] for e in e_refs])
            for o_ref, o in zip(o_refs, outs):
                o_ref[...] = o.astype(o_ref.dtype)

        if nk == 1:
            finish([partial(oi) for oi in range(n_acc)])
        else:
            k = pl.program_id(2)

            @pl.when(k == 0)
            def _():
                for r in acc_refs:
                    r[...] = jnp.zeros_like(r)

            for oi in range(n_acc):
                acc_refs[oi][...] += partial(oi)

            @pl.when(k == nk - 1)
            def _():
                finish([r[...] for r in acc_refs])

    return pl.pallas_call(
        body, name=name, grid=(N // tn, M // tm, nk),
        in_specs=[a_spec] * na + [b_spec] * nb + [o_spec] * ne,
        out_specs=[o_spec] * no,
        out_shape=[jax.ShapeDtypeStruct((M, N), dt) for dt in out_dtypes],
        scratch_shapes=[pltpu.VMEM((tm, tn), F32)] * (n_acc if nk > 1 else 0),
        compiler_params=_params(3),
    )(*a_list, *b_list, *extras)


def _sigmoid(x):
    return 1.0 / (1.0 + jnp.exp(-x))


def _ep_plain(accs, ex):
    return (accs[0],)


def _ep_residual(accs, ex):
    return (ex[0] + accs[0],)


def _ep_swiglu(accs, ex):
    g, u = accs
    return g, u, (g * _sigmoid(g)) * u


def _ep_swiglu_bwd(accs, ex):
    dact = accs[0]
    g, u = ex
    sig = _sigmoid(g)
    silu = g * sig
    return dact * u * (sig * (1.0 + g * (1.0 - sig))), dact * silu


def _ep_two(accs, ex):
    return accs[0], accs[1]


ROWS = 256


def _row_spec(tm, width, col=0):
    return pl.BlockSpec((tm, width), lambda i: (i, col))


def _full_spec(shape):
    nd = len(shape)
    return pl.BlockSpec(shape, lambda i: (0,) * nd)


def _rms_fwd(name, x, g):
    T, D = x.shape
    tm = min(ROWS, T)

    def body(x_ref, g_ref, o_ref):
        xv = x_ref[...]
        r = lax.rsqrt(jnp.mean(xv * xv, axis=-1, keepdims=True) + EPS)
        o_ref[...] = ((xv * r) * g_ref[...]).astype(BF16)

    return pl.pallas_call(
        body, name=name, grid=(T // tm,),
        in_specs=[_row_spec(tm, D), _full_spec((1, D))], out_specs=_row_spec(tm, D),
        out_shape=jax.ShapeDtypeStruct((T, D), BF16), compiler_params=_params(1),
    )(x, g)


def _rms_bwd(name, dyn, x, g, res):
    T, D = x.shape
    tm = min(ROWS, T)

    def body(dyn_ref, x_ref, g_ref, res_ref, dx_ref, dx16_ref, dg_ref):
        xv, dv = x_ref[...], dyn_ref[...]
        r = lax.rsqrt(jnp.mean(xv * xv, axis=-1, keepdims=True) + EPS)
        xh = xv * r

        @pl.when(pl.program_id(0) == 0)
        def _():
            dg_ref[...] = jnp.zeros_like(dg_ref)

        dg_ref[...] += jnp.sum(dv * xh, axis=0, keepdims=True)
        dxh = dv * g_ref[...]
        dx = r * (dxh - xh * jnp.mean(dxh * xh, axis=-1, keepdims=True)) + res_ref[...]
        dx_ref[...] = dx
        dx16_ref[...] = dx.astype(BF16)

    return pl.pallas_call(
        body, name=name, grid=(T // tm,),
        in_specs=[_row_spec(tm, D), _row_spec(tm, D), _full_spec((1, D)), _row_spec(tm, D)],
        out_specs=[_row_spec(tm, D), _row_spec(tm, D), _full_spec((1, D))],
        out_shape=[jax.ShapeDtypeStruct((T, D), F32), jax.ShapeDtypeStruct((T, D), BF16),
                   jax.ShapeDtypeStruct((1, D), F32)],
        compiler_params=_params(1),
    )(dyn, x, g, res)


def _outnorm_fwd(name, a, b, ga, gb):
    T, W = a.shape
    tm = min(ROWS, T)

    def body(a_ref, b_ref, ga_ref, gb_ref, o_ref):
        for src, gain, lo in ((a_ref, ga_ref, 0), (b_ref, gb_ref, W)):
            v = src[...]
            r = lax.rsqrt(jnp.mean(v * v, axis=-1, keepdims=True) + EPS)
            o_ref[:, lo:lo + W] = ((v * r) * gain[...]).astype(BF16)

    return pl.pallas_call(
        body, name=name, grid=(T // tm,),
        in_specs=[_row_spec(tm, W), _row_spec(tm, W), _full_spec((1, W)), _full_spec((1, W))],
        out_specs=_row_spec(tm, 2 * W),
        out_shape=jax.ShapeDtypeStruct((T, 2 * W), BF16), compiler_params=_params(1),
    )(a, b, ga, gb)


def _outnorm_bwd(name, dmixed, a, b, ga, gb):
    T, W = a.shape
    tm = min(ROWS, T)

    def body(dm_ref, a_ref, b_ref, ga_ref, gb_ref, da_ref, db_ref, dga_ref, dgb_ref):
        first = pl.program_id(0) == 0
        for src, gain, lo, dsrc, dgain in ((a_ref, ga_ref, 0, da_ref, dga_ref), (b_ref, gb_ref, W, db_ref, dgb_ref)):
            v, dv = src[...], dm_ref[:, lo:lo + W]
            r = lax.rsqrt(jnp.mean(v * v, axis=-1, keepdims=True) + EPS)
            vh = v * r

            @pl.when(first)
            def _():
                dgain[...] = jnp.zeros_like(dgain)

            dgain[...] += jnp.sum(dv * vh, axis=0, keepdims=True)
            dvh = dv * gain[...]
            dsrc[...] = r * (dvh - vh * jnp.mean(dvh * vh, axis=-1, keepdims=True))

    return pl.pallas_call(
        body, name=name, grid=(T // tm,),
        in_specs=[_row_spec(tm, 2 * W), _row_spec(tm, W), _row_spec(tm, W), _full_spec((1, W)), _full_spec((1, W))],
        out_specs=[_row_spec(tm, W), _row_spec(tm, W), _full_spec((1, W)), _full_spec((1, W))],
        out_shape=[jax.ShapeDtypeStruct((T, W), F32), jax.ShapeDtypeStruct((T, W), F32),
                   jax.ShapeDtypeStruct((1, W), F32), jax.ShapeDtypeStruct((1, W), F32)],
        compiler_params=_params(1),
    )(dmixed, a, b, ga, gb)


def _loss_head(name, y, target):
    T, D = y.shape
    tm = min(ROWS, T)

    def body(y_ref, t_ref, loss_ref, dy_ref, dy16_ref):
        d = y_ref[...] - t_ref[...]

        @pl.when(pl.program_id(0) == 0)
        def _():
            loss_ref[...] = jnp.zeros_like(loss_ref)

        per_token = jnp.mean(d * d, axis=-1, keepdims=True)
        loss_ref[...] += 0.5 * jnp.sum(per_token, axis=0, keepdims=True)
        dy = d * (1.0 / D)
        dy_ref[...] = dy
        dy16_ref[...] = dy.astype(BF16)

    return pl.pallas_call(
        body, name=name, grid=(T // tm,),
        in_specs=[_row_spec(tm, D), _row_spec(tm, D)],
        out_specs=[_full_spec((1, 1)), _row_spec(tm, D), _row_spec(tm, D)],
        out_shape=[jax.ShapeDtypeStruct((1, 1), F32), jax.ShapeDtypeStruct((T, D), F32),
                   jax.ShapeDtypeStruct((T, D), BF16)],
        compiler_params=_params(1),
    )(y, target)


def _bf16_dot(a, b, dims):
    return lax.dot_general(a.astype(BF16), b.astype(BF16), dims, preferred_element_type=F32)


@jax.custom_vjp
def _dot_nn(a, b):
    return _bf16_dot(a, b, _DIMS["nn"])


def _dot_nn_fwd(a, b):
    return _dot_nn(a, b), (a, b)


def _dot_nn_bwd(saved, ct):
    a, b = saved
    return _bf16_dot(ct, b, _DIMS["nt"]), _bf16_dot(a, ct, _DIMS["tn"])


_dot_nn.defvjp(_dot_nn_fwd, _dot_nn_bwd)


@jax.custom_vjp
def _dot_nt(a, b):
    return _bf16_dot(a, b, _DIMS["nt"])


def _dot_nt_fwd(a, b):
    return _dot_nt(a, b), (a, b)


def _dot_nt_bwd(saved, ct):
    a, b = saved
    return _bf16_dot(ct, b, _DIMS["nn"]), _bf16_dot(ct, a, _DIMS["tn"])


_dot_nt.defvjp(_dot_nt_fwd, _dot_nt_bwd)


def _iota(shape, dim):
    return lax.broadcasted_iota(jnp.int32, shape, dim)


def _head_sum_impl(x):
    same_head = (_iota((LANES, LANES), 0) // HEAD_DIM == _iota((LANES, LANES), 1) // HEAD_DIM).astype(BF16)
    pieces = []
    for i in range(x.shape[1] // LANES):
        xs = x[:, i * LANES:(i + 1) * LANES]
        hi = xs.astype(BF16)
        r1 = xs - hi.astype(F32)
        mid = r1.astype(BF16)
        lo = (r1 - mid.astype(F32)).astype(BF16)
        tot = None
        for part in (hi, mid, lo):
            d = lax.dot_general(part, same_head, _DIMS["nn"], preferred_element_type=F32)
            tot = d if tot is None else tot + d
        pieces.append(tot)
    return pieces[0] if len(pieces) == 1 else jnp.concatenate(pieces, axis=1)


@jax.custom_vjp
def _head_sum(x):
    return _head_sum_impl(x)


_head_sum.defvjp(lambda x: (_head_sum_impl(x), None), lambda _, ct: (_head_sum_impl(ct),))


def _head_rms(x, g_row):
    ms = _head_sum(x * x) * (1.0 / HEAD_DIM)
    return (x * lax.rsqrt(ms + EPS)) * g_row


@jax.custom_vjp
def _swap_halves(x):
    return pltpu.roll(x, HEAD_DIM, 1)


_swap_halves.defvjp(lambda x: (pltpu.roll(x, HEAD_DIM, 1), None), lambda _, ct: (pltpu.roll(ct, HEAD_DIM, 1),))


def _gelu(x):
    return 0.5 * x * (1.0 + lax.erf(x * (1.0 / math.sqrt(2.0))))


def _sgu_block(u_raw, v_raw, g_row, w, b_full):
    u, v = _gelu(u_raw), _gelu(v_raw)
    vn = _head_rms(v, g_row)
    causal = _iota((CHUNK, CHUNK), 0) >= _iota((CHUNK, CHUNK), 1)
    low_half = _iota((CHUNK, LANES), 1) < HEAD_DIM
    gates = []
    for p in range(v.shape[1] // LANES):
        vp = vn[:, p * LANES:(p + 1) * LANES]
        g0 = _dot_nn(jnp.where(causal, w[2 * p], 0.0), vp)
        g1 = _dot_nn(jnp.where(causal, w[2 * p + 1], 0.0), vp)
        gates.append(jnp.where(low_half, g0, g1))
    gate = jnp.concatenate(gates, axis=1) + b_full
    return u * gate


def _attn_block(q_raw, k_prev, k_cur, v_prev, v_cur, qg_row, kg_row, sinks, bias, blk):
    scale = 1.0 / math.sqrt(HEAD_DIM)
    n_q_heads = q_raw.shape[1] // HEAD_DIM
    qn = _head_rms(q_raw, qg_row)
    kn = _head_rms(jnp.concatenate([k_prev, k_cur], axis=0), kg_row)
    vb = jnp.concatenate([v_prev, v_cur], axis=0)
    qi = _iota((BLOCK, 2 * BLOCK), 0)
    kj = _iota((BLOCK, 2 * BLOCK), 1)
    dist = qi + BLOCK - kj
    valid = (dist >= 0) & (dist < BLOCK) & (kj + (blk * BLOCK - BLOCK) >= 0)
    low_half = _iota((BLOCK, LANES), 1) < HEAD_DIM
    outs = []
    for p in range(n_q_heads // 2):
        q2 = qn[:, p * LANES:(p + 1) * LANES]
        q2_swapped = _swap_halves(q2)
        kv = (2 * p) // GROUP
        grp, kv_low = kv // 2, kv % 2 == 0
        k2 = kn[:, grp * LANES:(grp + 1) * LANES]
        v2 = vb[:, grp * LANES:(grp + 1) * LANES]
        halves = []
        for half in range(2):
            h = 2 * p + half
            q_low = half == 0
            src = q2 if q_low == kv_low else q2_swapped
            qm = jnp.where(low_half if kv_low else ~low_half, src, 0.0)
            s = _dot_nt(qm, k2) * scale + bias[h]
            s = jnp.where(valid, s, NEG_INF)
            sink = sinks[h]
            m = lax.stop_gradient(jnp.maximum(jnp.max(s, axis=1, keepdims=True), sink))
            e = jnp.exp(s - m)
            denom = jnp.sum(e, axis=1, keepdims=True) + jnp.exp(sink - m)
            o = _dot_nn(e / denom, v2)
            halves.append(o if q_low == kv_low else _swap_halves(o))
        outs.append(jnp.where(low_half, halves[0], halves[1]))
    return jnp.concatenate(outs, axis=1)


def _bias_table():
    dist = np.arange(BLOCK)[:, None] + BLOCK - np.arange(2 * BLOCK)[None, :]
    n = np.maximum(dist, 0)
    max_exact = NUM_BUCKETS // 2
    nf = np.maximum(n, 1).astype(np.float64)
    large = max_exact + (np.log(nf / max_exact) / math.log(MAX_DISTANCE / max_exact) * (NUM_BUCKETS - max_exact)).astype(np.int32)
    large = np.minimum(large, NUM_BUCKETS - 1)
    return np.where(n < max_exact, n, large).astype(np.int32)


def _bias_fwd(name, rel_bias, buckets):
    nb_, nh = rel_bias.shape

    def body(rb_ref, bk_ref, o_ref):
        bk = bk_ref[...]
        for h in range(nh):
            acc = jnp.zeros(bk.shape, F32)
            for b in range(nb_):
                acc = jnp.where(bk == b, rb_ref[b, h], acc)
            o_ref[h] = acc

    return pl.pallas_call(
        body, name=name,
        in_specs=[pl.BlockSpec(memory_space=pltpu.SMEM), pl.BlockSpec(memory_space=pltpu.VMEM)],
        out_specs=pl.BlockSpec(memory_space=pltpu.VMEM),
        out_shape=jax.ShapeDtypeStruct((nh,) + buckets.shape, F32),
    )(rel_bias, buckets)


def _bias_bwd(name, dbias, buckets, nb_):
    nh = dbias.shape[0]

    def body(db_ref, bk_ref, o_ref):
        bk = bk_ref[...]
        for h in range(nh):
            d = db_ref[h]
            for b in range(nb_):
                s = jnp.sum(jnp.where(bk == b, d, 0.0), axis=0, keepdims=True)
                s = jnp.sum(s, axis=1, keepdims=True)
                o_ref[b * nh + h:b * nh + h + 1, :] = jnp.broadcast_to(s, (1, LANES))

    return pl.pallas_call(
        body, name=name,
        in_specs=[pl.BlockSpec(memory_space=pltpu.VMEM), pl.BlockSpec(memory_space=pltpu.VMEM)],
        out_specs=pl.BlockSpec(memory_space=pltpu.VMEM),
        out_shape=jax.ShapeDtypeStruct((nb_ * nh, LANES), F32),
    )(dbias, buckets)


def _sgu_fwd(name, z, g_row, w, b_full, W):
    T = z.shape[0]

    def body(u_ref, v_ref, g_ref, w_ref, b_ref, o_ref):
        o_ref[...] = _sgu_block(u_ref[...], v_ref[...], g_ref[...], w_ref[...], b_ref[...])

    return pl.pallas_call(
        body, name=name, grid=(T // CHUNK,),
        in_specs=[_row_spec(CHUNK, W, 0), _row_spec(CHUNK, W, 1), _full_spec((1, W)),
                  _full_spec(w.shape), _full_spec((CHUNK, W))],
        out_specs=_row_spec(CHUNK, W),
        out_shape=jax.ShapeDtypeStruct((T, W), F32), compiler_params=_params(1),
    )(z, z, g_row, w, b_full)


def _sgu_bwd(name, z, g_row, w, b_full, d_out, W):
    T = z.shape[0]

    def body(u_ref, v_ref, g_ref, w_ref, b_ref, do_ref, dz_ref, dg_ref, dw_ref, db_ref):
        _, vjp = jax.vjp(_sgu_block, u_ref[...], v_ref[...], g_ref[...], w_ref[...], b_ref[...])
        du, dv, dg, dw, db = vjp(do_ref[...])
        dz_ref[:, :W] = du.astype(BF16)
        dz_ref[:, W:] = dv.astype(BF16)
        i = pl.program_id(0)

        @pl.when(i == 0)
        def _():
            dg_ref[...] = jnp.zeros_like(dg_ref)
            dw_ref[...] = jnp.zeros_like(dw_ref)
            db_ref[...] = jnp.zeros_like(db_ref)

        dg_ref[...] += dg
        dw_ref[...] += dw
        db_ref[...] += db

        @pl.when(i == pl.num_programs(0) - 1)
        def _():
            db_ref[...] = _head_sum_impl(db_ref[...])

    return pl.pallas_call(
        body, name=name, grid=(T // CHUNK,),
        in_specs=[_row_spec(CHUNK, W, 0), _row_spec(CHUNK, W, 1), _full_spec((1, W)),
                  _full_spec(w.shape), _full_spec((CHUNK, W)), _row_spec(CHUNK, W)],
        out_specs=[_row_spec(CHUNK, 2 * W), _full_spec((1, W)), _full_spec(w.shape), _full_spec((CHUNK, W))],
        out_shape=[jax.ShapeDtypeStruct((T, 2 * W), BF16), jax.ShapeDtypeStruct((1, W), F32),
                   jax.ShapeDtypeStruct(w.shape, F32), jax.ShapeDtypeStruct((CHUNK, W), F32)],
        compiler_params=_params(1),
    )(z, z, g_row, w, b_full, d_out)


def _attn_specs(WQ, WKV, q_col, k_col, v_col, blk_of):
    prev_of = lambda i: jnp.maximum(blk_of(i) - 1, 0)
    return [pl.BlockSpec((BLOCK, WQ), lambda i: (blk_of(i), q_col)),
            pl.BlockSpec((BLOCK, WKV), lambda i: (prev_of(i), k_col)),
            pl.BlockSpec((BLOCK, WKV), lambda i: (blk_of(i), k_col)),
            pl.BlockSpec((BLOCK, WKV), lambda i: (prev_of(i), v_col)),
            pl.BlockSpec((BLOCK, WKV), lambda i: (blk_of(i), v_col))]


def _attn_fwd(name, z, qg_row, kg_row, sinks_col, bias, WQ, WKV, q_col, k_col, v_col):
    T = z.shape[0]
    nh = sinks_col.shape[0]

    def body(q_ref, kp_ref, kc_ref, vp_ref, vc_ref, qg_ref, kg_ref, s_ref, b_ref, o_ref):
        sinks = [s_ref[h:h + 1, :] for h in range(nh)]
        o_ref[...] = _attn_block(q_ref[...], kp_ref[...], kc_ref[...], vp_ref[...], vc_ref[...],
                                 qg_ref[...], kg_ref[...], sinks, b_ref[...], pl.program_id(0))

    return pl.pallas_call(
        body, name=name, grid=(T // BLOCK,),
        in_specs=_attn_specs(WQ, WKV, q_col, k_col, v_col, lambda i: i)
        + [_full_spec((1, WQ)), _full_spec((1, WKV)), _full_spec((nh, 1)), _full_spec(bias.shape)],
        out_specs=_row_spec(BLOCK, WQ),
        out_shape=jax.ShapeDtypeStruct((T, WQ), F32), compiler_params=_params(1),
    )(z, z, z, z, z, qg_row, kg_row, sinks_col, bias)


def _attn_bwd(name, z, qg_row, kg_row, sinks_col, bias, d_out, WQ, WKV, q_col, k_col, v_col):
    T = z.shape[0]
    nblk = T // BLOCK
    nh = sinks_col.shape[0]
    blk_of = lambda i: nblk - 1 - i

    def body(q_ref, kp_ref, kc_ref, vp_ref, vc_ref, qg_ref, kg_ref, s_ref, b_ref, do_ref,
             dq_ref, dk_ref, dv_ref, dqg_ref, dkg_ref, ds_ref, db_ref, carry_k, carry_v):
        i = pl.program_id(0)
        blk = nblk - 1 - i
        sinks = [s_ref[h:h + 1, :] for h in range(nh)]
        fn = lambda q, kp, kc, vp, vc, qg, kg, sk, bs: _attn_block(q, kp, kc, vp, vc, qg, kg, sk, bs, blk)
        _, vjp = jax.vjp(fn, q_ref[...], kp_ref[...], kc_ref[...], vp_ref[...], vc_ref[...],
                         qg_ref[...], kg_ref[...], sinks, b_ref[...])
        dq, dkp, dkc, dvp, dvc, dqg, dkg, dsk, dbs = vjp(do_ref[...])

        @pl.when(i == 0)
        def _():
            carry_k[...] = jnp.zeros_like(carry_k)
            carry_v[...] = jnp.zeros_like(carry_v)
            dqg_ref[...] = jnp.zeros_like(dqg_ref)
            dkg_ref[...] = jnp.zeros_like(dkg_ref)
            ds_ref[...] = jnp.zeros_like(ds_ref)
            db_ref[...] = jnp.zeros_like(db_ref)

        dq_ref[...] = dq.astype(BF16)
        dk_ref[...] = (dkc + carry_k[...]).astype(BF16)
        dv_ref[...] = (dvc + carry_v[...]).astype(BF16)
        carry_k[...] = dkp
        carry_v[...] = dvp
        dqg_ref[...] += dqg
        dkg_ref[...] += dkg
        for h in range(nh):
            ds_ref[h:h + 1, :] += dsk[h]
        db_ref[...] += dbs

    return pl.pallas_call(
        body, name=name, grid=(nblk,),
        in_specs=_attn_specs(WQ, WKV, q_col, k_col, v_col, blk_of)
        + [_full_spec((1, WQ)), _full_spec((1, WKV)), _full_spec((nh, 1)), _full_spec(bias.shape),
           pl.BlockSpec((BLOCK, WQ), lambda i: (blk_of(i), 0))],
        out_specs=[pl.BlockSpec((BLOCK, WQ), lambda i: (blk_of(i), 0)),
                   pl.BlockSpec((BLOCK, WKV), lambda i: (blk_of(i), 0)),
                   pl.BlockSpec((BLOCK, WKV), lambda i: (blk_of(i), 0)),
                   _full_spec((1, WQ)), _full_spec((1, WKV)), _full_spec((nh, 1)), _full_spec(bias.shape)],
        out_shape=[jax.ShapeDtypeStruct((T, WQ), BF16), jax.ShapeDtypeStruct((T, WKV), BF16),
                   jax.ShapeDtypeStruct((T, WKV), BF16), jax.ShapeDtypeStruct((1, WQ), F32),
                   jax.ShapeDtypeStruct((1, WKV), F32), jax.ShapeDtypeStruct((nh, 1), F32),
                   jax.ShapeDtypeStruct(bias.shape, F32)],
        scratch_shapes=[pltpu.VMEM((BLOCK, WKV), F32), pltpu.VMEM((BLOCK, WKV), F32)],
        compiler_params=_params(1),
    )(z, z, z, z, z, qg_row, kg_row, sinks_col, bias, d_out)


def _adamw(name, w, g, m, v):
    shape = w.shape
    C = shape[-1]
    R = int(np.prod(shape[:-1]))
    w2, g2, m2, v2 = (t.reshape(R, C) for t in (w, g, m, v))
    tr = ROWS if R % ROWS == 0 else R

    def body(w_ref, g_ref, m_ref, v_ref, d_ref, nm_ref, nv_ref):
        gv = g_ref[...]
        nm = ADAM_B1 * m_ref[...] + (1.0 - ADAM_B1) * gv
        nv = ADAM_B2 * v_ref[...] + (1.0 - ADAM_B2) * (gv * gv)
        m_hat = nm / (1.0 - ADAM_B1 ** ADAM_STEP)
        v_hat = nv / (1.0 - ADAM_B2 ** ADAM_STEP)
        d_ref[...] = -ADAM_LR * (m_hat / (jnp.sqrt(v_hat) + ADAM_EPS) + ADAM_WD * w_ref[...])
        nm_ref[...] = nm
        nv_ref[...] = nv

    spec = pl.BlockSpec((tr, C), lambda i: (i, 0))
    outs = pl.pallas_call(
        body, name=name, grid=(R // tr,), in_specs=[spec] * 4, out_specs=[spec] * 3,
        out_shape=[jax.ShapeDtypeStruct((R, C), F32)] * 3, compiler_params=_params(1),
    )(w2, g2, m2, v2)
    return tuple(o.reshape(shape) for o in outs)


def _pad_rows(flat):
    n = flat.shape[0]
    tile = 8 * LANES
    padded = -(-n // tile) * tile
    return jnp.pad(flat, (0, padded - n)).reshape(padded // LANES, LANES)


def kernel(x, rel_bias, norm1_g, w_in, sgu_norm_g, sgu_w, sgu_b, q_norm_g, k_norm_g, sinks, out_norm_a, out_norm_b, w_out, norm2_g, w_gate, w_up, w_down, loss_target, m_rel_bias, m_norm1_g, m_w_in, m_sgu_norm_g, m_sgu_w, m_sgu_b, m_q_norm_g, m_k_norm_g, m_sinks, m_out_norm_a, m_out_norm_b, m_w_out, m_norm2_g, m_w_gate, m_w_up, m_w_down, v_rel_bias, v_norm1_g, v_w_in, v_sgu_norm_g, v_sgu_w, v_sgu_b, v_q_norm_g, v_k_norm_g, v_sinks, v_out_norm_a, v_out_norm_b, v_w_out, v_norm2_g, v_w_gate, v_w_up, v_w_down):
    L = w_in.shape[0]
    T, D = x.shape[1], x.shape[2]
    W = D // 2
    NH = W // HEAD_DIM
    WKV = N_KV_HEADS * HEAD_DIM
    IN = N_DEV * w_in.shape[2]
    FF = N_DEV * w_gate.shape[2]
    assert IN == 2 * W + W + 2 * WKV and NH // N_KV_HEADS == GROUP
    q_col, k_col, v_col = 2 * W // W, (3 * W) // WKV, (3 * W + WKV) // WKV
    x0 = x.reshape(T, D)
    target = loss_target.reshape(T, D)

    shards = [jnp.swapaxes(w_in, 1, 2).astype(BF16), w_out.astype(BF16), jnp.swapaxes(w_gate, 1, 2).astype(BF16),
              jnp.swapaxes(w_up, 1, 2).astype(BF16), w_down.astype(BF16)]
    full = []
    for l in range(L):
        got = _all_gather(f"gather_weights_{l}", [s[l] for s in shards])
        full.append([g.reshape(N_DEV * g.shape[1], g.shape[2]) for g in got])

    buckets = jnp.asarray(_bias_table())
    bias = _bias_fwd("bias_table", rel_bias, buckets)

    saved = []
    xl = x0
    for l in range(L):
        w_in_t, w_o, w_g_t, w_u_t, w_d = full[l]
        g1, g2 = norm1_g[l][None], norm2_g[l][None]
        sg_row = sgu_norm_g[l].reshape(1, W)
        b_full = jnp.repeat(sgu_b[l].T, HEAD_DIM, axis=1)
        qg_row = jnp.tile(q_norm_g[l], NH)[None]
        kg_row = jnp.tile(k_norm_g[l], N_KV_HEADS)[None]
        sinks_col = sinks[l][:, None]
        ga, gb = out_norm_a[l][None], out_norm_b[l][None]

        h = _rms_fwd(f"norm1_{l}", xl, g1)
        (z,) = _mm(f"proj_in_{l}", "nt", [h], [w_in_t], [(0, 0, 0)], 1, T, IN, D, 512, 896, D, _ep_plain, [F32])
        out_a = _sgu_fwd(f"sgu_{l}", z, sg_row, sgu_w[l], b_full, W)
        out_b = _attn_fwd(f"attn_{l}", z, qg_row, kg_row, sinks_col, bias, W, WKV, q_col, k_col, v_col)
        mixed = _outnorm_fwd(f"outnorm_{l}", out_a, out_b, ga, gb)
        (x1,) = _mm(f"proj_out_{l}", "nn", [mixed], [w_o], [(0, 0, 0)], 1, T, D, D, 512, 1024, D,
                    _ep_residual, [F32], extras=[xl])
        h2 = _rms_fwd(f"norm2_{l}", x1, g2)
        gate, up, act = _mm(f"mlp_in_{l}", "nt", [h2], [w_g_t, w_u_t], [(0, 0, 0), (0, 1, 1)], 2, T, FF, D,
                            1024, 512, D, _ep_swiglu, [F32, F32, BF16])
        (x2,) = _mm(f"mlp_out_{l}", "nn", [act], [w_d], [(0, 0, 0)], 1, T, D, FF, 512, 512, FF,
                    _ep_residual, [F32], extras=[x1])
        saved.append((xl, h, z, out_a, out_b, mixed, x1, h2, gate, up, act,
                      g1, g2, sg_row, b_full, qg_row, kg_row, sinks_col, ga, gb))
        xl = x2

    loss_part, dy, dy16 = _loss_head("loss_head", xl, target)

    dbias = None
    big = [None] * L
    small = [None] * L
    for l in reversed(range(L)):
        w_in_t, w_o, w_g_t, w_u_t, w_d = full[l]
        (xl, h, z, out_a, out_b, mixed, x1, h2, gate, up, act,
         g1, g2, sg_row, b_full, qg_row, kg_row, sinks_col, ga, gb) = saved[l]

        dgate, dup = _mm(f"d_mlp_out_{l}", "nt", [dy16], [w_d], [(0, 0, 0)], 1, T, FF, D, 1024, 512, D,
                         _ep_swiglu_bwd, [BF16, BF16], extras=[gate, up])
        (dw_d,) = _mm(f"dw_down_{l}", "tn", [act], [dy16], [(0, 0, 0)], 1, FF, D, T, 1408, 1024, 512, _ep_plain, [BF16])
        (dh2,) = _mm(f"d_mlp_in_{l}", "nn", [dgate, dup], [w_g_t, w_u_t], [(0, 0, 0), (1, 1, 0)], 1, T, D, FF,
                     512, 512, FF // 2, _ep_plain, [F32])
        dw_g, dw_u = _mm(f"dw_gate_up_{l}", "tn", [dgate, dup], [h2], [(0, 0, 0), (1, 0, 1)], 2, FF, D, T,
                         1408, 1024, 512, _ep_two, [BF16, BF16])
        dx1, dx1_16, dg2 = _rms_bwd(f"d_norm2_{l}", dh2, x1, g2, dy)
        (dmixed,) = _mm(f"d_proj_out_{l}", "nt", [dx1_16], [w_o], [(0, 0, 0)], 1, T, D, D, 512, 1024, D, _ep_plain, [F32])
        (dw_o,) = _mm(f"dw_out_{l}", "tn", [mixed], [dx1_16], [(0, 0, 0)], 1, D, D, T, 1024, 1024, 512, _ep_plain, [BF16])
        d_a, d_b, dga, dgb = _outnorm_bwd(f"d_outnorm_{l}", dmixed, out_a, out_b, ga, gb)
        dz_uv, dsg, dsw, dsb = _sgu_bwd(f"d_sgu_{l}", z, sg_row, sgu_w[l], b_full, d_a, W)
        dq, dk, dv, dqg, dkg, dsk, dbs = _attn_bwd(f"d_attn_{l}", z, qg_row, kg_row, sinks_col, bias, d_b,
                                                   W, WKV, q_col, k_col, v_col)
        dbias = dbs if dbias is None else dbias + dbs
        dz = jnp.concatenate([dz_uv, dq, dk, dv], axis=1)
        (dh,) = _mm(f"d_proj_in_{l}", "nn", [dz], [w_in_t], [(0, 0, 0)], 1, T, D, IN, 512, 512, IN, _ep_plain, [F32])
        (dw_i,) = _mm(f"dw_in_{l}", "tn", [dz], [h], [(0, 0, 0)], 1, IN, D, T, 896, 1024, 512, _ep_plain, [BF16])
        dy, dy16, dg1 = _rms_bwd(f"d_norm1_{l}", dh, xl, g1, dx1)

        big[l] = [t.reshape(N_DEV, t.shape[0] // N_DEV, D) for t in (dw_i, dw_o, dw_g, dw_u, dw_d)]
        small[l] = dict(norm1_g=dg1[0], sgu_norm_g=dsg.reshape(NH, HEAD_DIM), sgu_w=dsw,
                        sgu_b=dsb[:, ::HEAD_DIM].T, q_norm_g=dqg.reshape(NH, HEAD_DIM).sum(0),
                        k_norm_g=dkg.reshape(N_KV_HEADS, HEAD_DIM).sum(0), sinks=dsk[:, 0],
                        out_norm_a=dga[0], out_norm_b=dgb[0], norm2_g=dg2[0])

    grad_x = dy.reshape(x.shape)
    d_rel = _bias_bwd("d_bias_table", dbias, buckets, NUM_BUCKETS)[:, 0].reshape(NUM_BUCKETS, NH)

    grads_big = [[], [], [], [], []]
    for l in range(L):
        got = _all_to_all(f"scatter_grads_{l}", big[l])
        for i, blocks in enumerate(got):
            rows = blocks.shape[1]
            tr = 64 if rows % 64 == 0 else rows
            grads_big[i].append(_sum_slots(f"sum_grads_{l}_{i}", blocks, tr))
    g_w_in = jnp.swapaxes(jnp.stack(grads_big[0]), 1, 2)
    g_w_out = jnp.stack(grads_big[1])
    g_w_gate = jnp.swapaxes(jnp.stack(grads_big[2]), 1, 2)
    g_w_up = jnp.swapaxes(jnp.stack(grads_big[3]), 1, 2)
    g_w_down = jnp.stack(grads_big[4])

    names = ["norm1_g", "sgu_norm_g", "sgu_w", "sgu_b", "q_norm_g", "k_norm_g", "sinks", "out_norm_a", "out_norm_b", "norm2_g"]
    parts = {"rel_bias": d_rel}
    for nme in names:
        parts[nme] = jnp.stack([small[l][nme] for l in range(L)])
    order = ["rel_bias"] + names
    packed = jnp.concatenate([_pad_rows(parts[nme].reshape(-1)) for nme in order], axis=0)
    (everyone,) = _all_gather("gather_small_grads", [packed])
    rows = packed.shape[0]
    summed = _sum_slots("sum_small_grads", everyone, 64 if rows % 64 == 0 else 8)
    g_small, at = {}, 0
    for nme in order:
        n = int(np.prod(parts[nme].shape))
        n_rows = -(-n // (8 * LANES)) * 8
        g_small[nme] = summed[at:at + n_rows].reshape(-1)[:n].reshape(parts[nme].shape)
        at += n_rows

    loss = lax.psum(loss_part[0, 0], ("x", "y", "c"))

    grads = dict(g_small, w_in=g_w_in, w_out=g_w_out, w_gate=g_w_gate, w_up=g_w_up, w_down=g_w_down)
    weights = dict(rel_bias=rel_bias, norm1_g=norm1_g, w_in=w_in, sgu_norm_g=sgu_norm_g, sgu_w=sgu_w, sgu_b=sgu_b,
                   q_norm_g=q_norm_g, k_norm_g=k_norm_g, sinks=sinks, out_norm_a=out_norm_a, out_norm_b=out_norm_b,
                   w_out=w_out, norm2_g=norm2_g, w_gate=w_gate, w_up=w_up, w_down=w_down)
    ms = dict(rel_bias=m_rel_bias, norm1_g=m_norm1_g, w_in=m_w_in, sgu_norm_g=m_sgu_norm_g, sgu_w=m_sgu_w, sgu_b=m_sgu_b,
              q_norm_g=m_q_norm_g, k_norm_g=m_k_norm_g, sinks=m_sinks, out_norm_a=m_out_norm_a, out_norm_b=m_out_norm_b,
              w_out=m_w_out, norm2_g=m_norm2_g, w_gate=m_w_gate, w_up=m_w_up, w_down=m_w_down)
    vs = dict(rel_bias=v_rel_bias, norm1_g=v_norm1_g, w_in=v_w_in, sgu_norm_g=v_sgu_norm_g, sgu_w=v_sgu_w, sgu_b=v_sgu_b,
              q_norm_g=v_q_norm_g, k_norm_g=v_k_norm_g, sinks=v_sinks, out_norm_a=v_out_norm_a, out_norm_b=v_out_norm_b,
              w_out=v_w_out, norm2_g=v_norm2_g, w_gate=v_w_gate, w_up=v_w_up, w_down=v_w_down)
    all_names = ["rel_bias", "norm1_g", "w_in", "sgu_norm_g", "sgu_w", "sgu_b", "q_norm_g", "k_norm_g", "sinks",
                 "out_norm_a", "out_norm_b", "w_out", "norm2_g", "w_gate", "w_up", "w_down"]
    deltas, new_m, new_v = [], [], []
    for nme in all_names:
        d, nm, nv = _adamw(f"adamw_{nme}", weights[nme], grads[nme], ms[nme], vs[nme])
        deltas.append(d)
        new_m.append(nm)
        new_v.append(nv)
    return (loss, grad_x, *[grads[nme] for nme in all_names], *deltas, *new_m, *new_v)
```

```python
import functools
import math

import numpy as np
import jax
import jax.numpy as jnp
from jax import lax
from jax.experimental import pallas as pl
from jax.experimental.pallas import tpu as pltpu

F32 = jnp.float32
BF16 = jnp.bfloat16

N_DEV = 8
HEAD_DIM = 64
CHUNK = 128
BLOCK = 128
N_KV_HEADS = 4
GROUP = 4
NUM_BUCKETS = 32
MAX_DISTANCE = 128
EPS = 1e-6
NEG_INF = -1e30
LANES = 128
VMEM_LIMIT = 56 * 2 ** 20

ADAM_LR = 0.001
ADAM_B1 = 0.9
ADAM_B2 = 0.999
ADAM_EPS = 1e-08
ADAM_WD = 0.01
ADAM_STEP = 10

MESH = pl.DeviceIdType.MESH
ANY = pl.BlockSpec(memory_space=pl.ANY)
HBM = pl.BlockSpec(memory_space=pltpu.HBM)
SEM = pl.BlockSpec(memory_space=pltpu.SEMAPHORE)
EFFECT = pltpu.SideEffectType.DATAFLOW_SIDE_EFFECTING


def _params(n_axes):
    return pltpu.CompilerParams(dimension_semantics=("arbitrary",) * n_axes, vmem_limit_bytes=VMEM_LIMIT)


def _my_place():
    return lax.axis_index("x"), lax.axis_index("y"), lax.axis_index("c")


def _all_gather(name, arrs):
    n = len(arrs)

    def body(*refs):
        ins, outs = refs[:n], refs[n:2 * n]
        send_sems, recv_sems, local_sems = refs[2 * n:]
        x, y, c = _my_place()
        sibling = (x, y, 1 - c)
        chips = [(1 - x, y), (x, 1 - y), (1 - x, 1 - y)]

        def slot(a, px, py, pc):
            return outs[a].at[4 * px + 2 * py + pc]

        def copy(a, k, block, to, src=None):
            return pltpu.make_async_remote_copy(
                src_ref=slot(a, *block) if src is None else src, dst_ref=slot(a, *block),
                send_sem=send_sems.at[7 * a + k], recv_sem=recv_sems.at[7 * a + k],
                device_id=to, device_id_type=MESH)

        started = []
        for a in range(n):
            mine = pltpu.make_async_copy(ins[a], slot(a, x, y, c), local_sems.at[a])
            mine.start()
            started.append(mine)
        sends = []
        for a in range(n):
            first = [copy(a, 0, (x, y, c), sibling, src=ins[a])]
            first += [copy(a, 1 + j, (x, y, c), (*chip, c), src=ins[a]) for j, chip in enumerate(chips)]
            for cp in first:
                cp.start()
            sends += first
        for a in range(n):
            for j, chip in enumerate(chips):
                copy(a, 1 + j, (*chip, c), (x, y, c)).wait_recv()
                fwd = copy(a, 4 + j, (*chip, c), sibling)
                fwd.start()
                sends.append(fwd)
        for a in range(n):
            copy(a, 0, sibling, (x, y, c)).wait_recv()
            for j, chip in enumerate(chips):
                copy(a, 4 + j, (*chip, 1 - c), (x, y, c)).wait_recv()
        for cp in sends:
            cp.wait_send()
        for cp in started:
            cp.wait()

    return pl.pallas_call(
        body, name=name,
        out_shape=[jax.ShapeDtypeStruct((N_DEV,) + a.shape, a.dtype) for a in arrs],
        in_specs=[ANY] * n, out_specs=[ANY] * n,
        scratch_shapes=[pltpu.SemaphoreType.DMA((7 * n,)), pltpu.SemaphoreType.DMA((7 * n,)),
                        pltpu.SemaphoreType.DMA((n,))],
    )(*arrs)


def _peer(k, x, y, c):
    return (1 - x if k & 4 else x), (1 - y if k & 2 else y), (1 - c if k & 1 else c)


PEER_ORDER = (1, 2, 4, 3, 5, 6, 7)


def _exchange_copy(kind, k, a, srcs, lands, send_sems, recv_sems):
    x, y, c = _my_place()
    me = 4 * x + 2 * y + c
    px, py, pc = _peer(k, x, y, c)
    them = 4 * px + 2 * py + pc
    if kind == "gather":
        src, dst_there, dst_here = srcs[a], lands[a].at[me], lands[a].at[them]
    else:
        src, dst_there, dst_here = srcs[a].at[them], lands[a].at[k - 1], lands[a].at[k - 1]
    sems = dict(send_sem=send_sems.at[7 * a + k - 1], recv_sem=recv_sems.at[7 * a + k - 1],
                device_id=(px, py, pc), device_id_type=MESH)
    return (pltpu.make_async_remote_copy(src_ref=src, dst_ref=dst_there, **sems),
            pltpu.make_async_remote_copy(src_ref=src, dst_ref=dst_here, **sems))


def _exchange_start(name, kind, srcs, lands, after):
    n = len(srcs)

    def body(*refs):
        ins, lnd = refs[:n], refs[n:2 * n]
        send_sems, recv_sems = refs[2 * n + 1], refs[2 * n + 2]
        token = refs[-1]
        for k in PEER_ORDER:
            for a in range(n):
                _exchange_copy(kind, k, a, ins, lnd, send_sems, recv_sems)[0].start()
        token[...] = jnp.zeros_like(token)

    hbm = lambda t: pltpu.with_memory_space_constraint(t, pltpu.HBM)
    out = pl.pallas_call(
        body, name=name,
        out_shape=(pltpu.SemaphoreType.DMA((7 * n,)), pltpu.SemaphoreType.DMA((7 * n,)),
                   *[pltpu.HBM(t.shape, t.dtype) for t in srcs], *[pltpu.HBM(t.shape, t.dtype) for t in lands],
                   jax.ShapeDtypeStruct((8, LANES), F32)),
        in_specs=[HBM] * (2 * n) + [ANY],
        out_specs=(SEM, SEM, *[HBM] * (2 * n), pl.BlockSpec(memory_space=pltpu.VMEM)),
        input_output_aliases={i: 2 + i for i in range(2 * n)},
        compiler_params=pltpu.CompilerParams(has_side_effects=EFFECT),
    )(*[hbm(t) for t in srcs], *[hbm(t) for t in lands], after)
    return out[0], out[1], list(out[2:2 + n]), list(out[2 + n:2 + 2 * n]), out[-1]


def _exchange_wait(name, kind, send_sems, recv_sems, srcs, lands, after):
    n = len(srcs)

    def body(*refs):
        ins, lnd = refs[:n], refs[n:2 * n]
        s_sems, r_sems = refs[2 * n], refs[2 * n + 1]
        for a in range(n):
            for k in PEER_ORDER:
                sent, arriving = _exchange_copy(kind, k, a, ins, lnd, s_sems, r_sems)
                sent.wait_send()
                arriving.wait_recv()

    out = pl.pallas_call(
        body, name=name,
        out_shape=(*[pltpu.HBM(t.shape, t.dtype) for t in srcs], *[pltpu.HBM(t.shape, t.dtype) for t in lands]),
        in_specs=[HBM] * (2 * n) + [SEM, SEM, ANY],
        out_specs=tuple([HBM] * (2 * n)),
        input_output_aliases={i: i for i in range(2 * n)},
        compiler_params=pltpu.CompilerParams(has_side_effects=EFFECT),
    )(*srcs, *lands, send_sems, recv_sems, after)
    return list(out[:n]), list(out[n:])


def _sum_parts(name, own, parts, rows):
    R, C = own.shape

    def body(own_ref, p_ref, o_ref):
        acc = own_ref[...].astype(F32)
        for s in range(N_DEV - 1):
            acc = acc + p_ref[s].astype(F32)
        o_ref[...] = acc

    return pl.pallas_call(
        body, name=name, grid=(R // rows,),
        in_specs=[pl.BlockSpec((rows, C), lambda i: (i, 0)), pl.BlockSpec((N_DEV - 1, rows, C), lambda i: (0, i, 0))],
        out_specs=pl.BlockSpec((rows, C), lambda i: (i, 0)),
        out_shape=jax.ShapeDtypeStruct((R, C), F32), compiler_params=_params(1),
    )(own, parts)


def _sum_slots(name, a, rows):
    _, R, C = a.shape

    def body(a_ref, o_ref):
        acc = a_ref[0].astype(F32)
        for s in range(1, N_DEV):
            acc = acc + a_ref[s].astype(F32)
        o_ref[...] = acc

    return pl.pallas_call(
        body, name=name, grid=(R // rows,),
        in_specs=[pl.BlockSpec((N_DEV, rows, C), lambda i: (0, i, 0))],
        out_specs=pl.BlockSpec((rows, C), lambda i: (i, 0)),
        out_shape=jax.ShapeDtypeStruct((R, C), F32), compiler_params=_params(1),
    )(a)


_DIMS = {"nn": (((1,), (0,)), ((), ())), "nt": (((1,), (1,)), ((), ())), "tn": (((0,), (0,)), ((), ()))}


def _mm(name, mode, a_list, b_list, pairs, n_acc, M, N, K, tm, tn, tk, epilogue, out_dtypes, extras=(), after=None):
    tm, tn, tk = min(tm, M), min(tn, N), min(tk, K)
    assert M % tm == 0 and N % tn == 0 and K % tk == 0, (name, M, N, K, tm, tn, tk)
    nk = K // tk
    na, nb, ne, no = len(a_list), len(b_list), len(extras), len(out_dtypes)
    dims = _DIMS[mode]
    a_spec = (pl.BlockSpec((tk, tm), lambda j, i, k: (k, i)) if mode == "tn"
              else pl.BlockSpec((tm, tk), lambda j, i, k: (i, k)))
    b_spec = (pl.BlockSpec((tn, tk), lambda j, i, k: (j, k)) if mode == "nt"
              else pl.BlockSpec((tk, tn), lambda j, i, k: (k, j)))
    o_spec = pl.BlockSpec((tm, tn), lambda j, i, k: (i, j))
    tail = [] if after is None else [after]

    def body(*refs):
        a_refs, b_refs = refs[:na], refs[na:na + nb]
        e_refs = refs[na + nb:na + nb + ne]
        first_out = na + nb + ne + len(tail)
        o_refs = refs[first_out:first_out + no]
        acc_refs = refs[first_out + no:]

        def partial(oi):
            tot = None
            for ai, bi, ti in pairs:
                if ti == oi:
                    d = lax.dot_general(a_refs[ai][...], b_refs[bi][...], dims, preferred_element_type=F32)
                    tot = d if tot is None else tot + d
            return tot

        def finish(accs):
            outs = epilogue(accs, [e[...] for e in e_refs])
            for o_ref, o in zip(o_refs, outs):
                o_ref[...] = o.astype(o_ref.dtype)

        if nk == 1:
            finish([partial(oi) for oi in range(n_acc)])
        else:
            k = pl.program_id(2)

            @pl.when(k == 0)
            def _():
                for r in acc_refs:
                    r[...] = jnp.zeros_like(r)

            for oi in range(n_acc):
                acc_refs[oi][...] += partial(oi)

            @pl.when(k == nk - 1)
            def _():
                finish([r[...] for r in acc_refs])

    return pl.pallas_call(
        body, name=name, grid=(N // tn, M // tm, nk),
        in_specs=[a_spec] * na + [b_spec] * nb + [o_spec] * ne + [ANY] * len(tail),
        out_specs=[o_spec] * no,
        out_shape=[jax.ShapeDtypeStruct((M, N), dt) for dt in out_dtypes],
        scratch_shapes=[pltpu.VMEM((tm, tn), F32)] * (n_acc if nk > 1 else 0),
        compiler_params=_params(3),
    )(*a_list, *b_list, *extras, *tail)


def _sigmoid(x):
    return 1.0 / (1.0 + jnp.exp(-x))


def _ep_plain(accs, ex):
    return (accs[0],)


def _ep_residual(accs, ex):
    return (ex[0] + accs[0],)


def _ep_swiglu(accs, ex):
    g, u = accs
    return g, u, (g * _sigmoid(g)) * u


def _ep_swiglu_bwd(accs, ex):
    dact = accs[0]
    g, u = ex
    sig = _sigmoid(g)
    silu = g * sig
    return dact * u * (sig * (1.0 + g * (1.0 - sig))), dact * silu


def _ep_two(accs, ex):
    return accs[0], accs[1]


ROWS = 256


def _row_spec(tm, width, col=0):
    return pl.BlockSpec((tm, width), lambda i: (i, col))


def _full_spec(shape):
    nd = len(shape)
    return pl.BlockSpec(shape, lambda i: (0,) * nd)


def _rms_fwd(name, x, g, after):
    T, D = x.shape
    tm = min(ROWS, T)

    def body(x_ref, g_ref, after_ref, o_ref):
        xv = x_ref[...]
        r = lax.rsqrt(jnp.mean(xv * xv, axis=-1, keepdims=True) + EPS)
        o_ref[...] = ((xv * r) * g_ref[...]).astype(BF16)

    return pl.pallas_call(
        body, name=name, grid=(T // tm,),
        in_specs=[_row_spec(tm, D), _full_spec((1, D)), ANY], out_specs=_row_spec(tm, D),
        out_shape=jax.ShapeDtypeStruct((T, D), BF16), compiler_params=_params(1),
    )(x, g, after)


def _rms_bwd(name, dyn, x, g, res, after):
    T, D = x.shape
    tm = min(ROWS, T)

    def body(dyn_ref, x_ref, g_ref, res_ref, after_ref, dx_ref, dx16_ref, dg_ref):
        xv, dv = x_ref[...], dyn_ref[...]
        r = lax.rsqrt(jnp.mean(xv * xv, axis=-1, keepdims=True) + EPS)
        xh = xv * r

        @pl.when(pl.program_id(0) == 0)
        def _():
            dg_ref[...] = jnp.zeros_like(dg_ref)

        dg_ref[...] += jnp.sum(dv * xh, axis=0, keepdims=True)
        dxh = dv * g_ref[...]
        dx = r * (dxh - xh * jnp.mean(dxh * xh, axis=-1, keepdims=True)) + res_ref[...]
        dx_ref[...] = dx
        dx16_ref[...] = dx.astype(BF16)

    return pl.pallas_call(
        body, name=name, grid=(T // tm,),
        in_specs=[_row_spec(tm, D), _row_spec(tm, D), _full_spec((1, D)), _row_spec(tm, D), ANY],
        out_specs=[_row_spec(tm, D), _row_spec(tm, D), _full_spec((1, D))],
        out_shape=[jax.ShapeDtypeStruct((T, D), F32), jax.ShapeDtypeStruct((T, D), BF16),
                   jax.ShapeDtypeStruct((1, D), F32)],
        compiler_params=_params(1),
    )(dyn, x, g, res, after)


def _outnorm_fwd(name, a, b, ga, gb):
    T, W = a.shape
    tm = min(ROWS, T)

    def body(a_ref, b_ref, ga_ref, gb_ref, o_ref):
        for src, gain, lo in ((a_ref, ga_ref, 0), (b_ref, gb_ref, W)):
            v = src[...]
            r = lax.rsqrt(jnp.mean(v * v, axis=-1, keepdims=True) + EPS)
            o_ref[:, lo:lo + W] = ((v * r) * gain[...]).astype(BF16)

    return pl.pallas_call(
        body, name=name, grid=(T // tm,),
        in_specs=[_row_spec(tm, W), _row_spec(tm, W), _full_spec((1, W)), _full_spec((1, W))],
        out_specs=_row_spec(tm, 2 * W),
        out_shape=jax.ShapeDtypeStruct((T, 2 * W), BF16), compiler_params=_params(1),
    )(a, b, ga, gb)


def _outnorm_bwd(name, dmixed, a, b, ga, gb):
    T, W = a.shape
    tm = min(ROWS, T)

    def body(dm_ref, a_ref, b_ref, ga_ref, gb_ref, da_ref, db_ref, dga_ref, dgb_ref):
        first = pl.program_id(0) == 0
        for src, gain, lo, dsrc, dgain in ((a_ref, ga_ref, 0, da_ref, dga_ref), (b_ref, gb_ref, W, db_ref, dgb_ref)):
            v, dv = src[...], dm_ref[:, lo:lo + W]
            r = lax.rsqrt(jnp.mean(v * v, axis=-1, keepdims=True) + EPS)
            vh = v * r

            @pl.when(first)
            def _():
                dgain[...] = jnp.zeros_like(dgain)

            dgain[...] += jnp.sum(dv * vh, axis=0, keepdims=True)
            dvh = dv * gain[...]
            dsrc[...] = r * (dvh - vh * jnp.mean(dvh * vh, axis=-1, keepdims=True))

    return pl.pallas_call(
        body, name=name, grid=(T // tm,),
        in_specs=[_row_spec(tm, 2 * W), _row_spec(tm, W), _row_spec(tm, W), _full_spec((1, W)), _full_spec((1, W))],
        out_specs=[_row_spec(tm, W), _row_spec(tm, W), _full_spec((1, W)), _full_spec((1, W))],
        out_shape=[jax.ShapeDtypeStruct((T, W), F32), jax.ShapeDtypeStruct((T, W), F32),
                   jax.ShapeDtypeStruct((1, W), F32), jax.ShapeDtypeStruct((1, W), F32)],
        compiler_params=_params(1),
    )(dmixed, a, b, ga, gb)


def _loss_head(name, y, target):
    T, D = y.shape
    tm = min(ROWS, T)

    def body(y_ref, t_ref, loss_ref, dy_ref, dy16_ref):
        d = y_ref[...] - t_ref[...]

        @pl.when(pl.program_id(0) == 0)
        def _():
            loss_ref[...] = jnp.zeros_like(loss_ref)

        per_token = jnp.mean(d * d, axis=-1, keepdims=True)
        loss_ref[...] += 0.5 * jnp.sum(per_token, axis=0, keepdims=True)
        dy = d * (1.0 / D)
        dy_ref[...] = dy
        dy16_ref[...] = dy.astype(BF16)

    return pl.pallas_call(
        body, name=name, grid=(T // tm,),
        in_specs=[_row_spec(tm, D), _row_spec(tm, D)],
        out_specs=[_full_spec((1, 1)), _row_spec(tm, D), _row_spec(tm, D)],
        out_shape=[jax.ShapeDtypeStruct((1, 1), F32), jax.ShapeDtypeStruct((T, D), F32),
                   jax.ShapeDtypeStruct((T, D), BF16)],
        compiler_params=_params(1),
    )(y, target)


def _bf16_dot(a, b, dims):
    return lax.dot_general(a.astype(BF16), b.astype(BF16), dims, preferred_element_type=F32)


@jax.custom_vjp
def _dot_nn(a, b):
    return _bf16_dot(a, b, _DIMS["nn"])


def _dot_nn_fwd(a, b):
    return _dot_nn(a, b), (a, b)


def _dot_nn_bwd(saved, ct):
    a, b = saved
    return _bf16_dot(ct, b, _DIMS["nt"]), _bf16_dot(a, ct, _DIMS["tn"])


_dot_nn.defvjp(_dot_nn_fwd, _dot_nn_bwd)


@jax.custom_vjp
def _dot_nt(a, b):
    return _bf16_dot(a, b, _DIMS["nt"])


def _dot_nt_fwd(a, b):
    return _dot_nt(a, b), (a, b)


def _dot_nt_bwd(saved, ct):
    a, b = saved
    return _bf16_dot(ct, b, _DIMS["nn"]), _bf16_dot(ct, a, _DIMS["tn"])


_dot_nt.defvjp(_dot_nt_fwd, _dot_nt_bwd)


def _iota(shape, dim):
    return lax.broadcasted_iota(jnp.int32, shape, dim)


def _head_sum_impl(x):
    same_head = (_iota((LANES, LANES), 0) // HEAD_DIM == _iota((LANES, LANES), 1) // HEAD_DIM).astype(BF16)
    pieces = []
    for i in range(x.shape[1] // LANES):
        xs = x[:, i * LANES:(i + 1) * LANES]
        hi = xs.astype(BF16)
        r1 = xs - hi.astype(F32)
        mid = r1.astype(BF16)
        lo = (r1 - mid.astype(F32)).astype(BF16)
        tot = None
        for part in (hi, mid, lo):
            d = lax.dot_general(part, same_head, _DIMS["nn"], preferred_element_type=F32)
            tot = d if tot is None else tot + d
        pieces.append(tot)
    return pieces[0] if len(pieces) == 1 else jnp.concatenate(pieces, axis=1)


@jax.custom_vjp
def _head_sum(x):
    return _head_sum_impl(x)


_head_sum.defvjp(lambda x: (_head_sum_impl(x), None), lambda _, ct: (_head_sum_impl(ct),))


def _head_rms(x, g_row):
    ms = _head_sum(x * x) * (1.0 / HEAD_DIM)
    return (x * lax.rsqrt(ms + EPS)) * g_row


@jax.custom_vjp
def _swap_halves(x):
    return pltpu.roll(x, HEAD_DIM, 1)


_swap_halves.defvjp(lambda x: (pltpu.roll(x, HEAD_DIM, 1), None), lambda _, ct: (pltpu.roll(ct, HEAD_DIM, 1),))


def _gelu(x):
    return 0.5 * x * (1.0 + lax.erf(x * (1.0 / math.sqrt(2.0))))


def _sgu_block(u_raw, v_raw, g_row, w, b_full):
    u, v = _gelu(u_raw), _gelu(v_raw)
    vn = _head_rms(v, g_row)
    causal = _iota((CHUNK, CHUNK), 0) >= _iota((CHUNK, CHUNK), 1)
    low_half = _iota((CHUNK, LANES), 1) < HEAD_DIM
    gates = []
    for p in range(v.shape[1] // LANES):
        vp = vn[:, p * LANES:(p + 1) * LANES]
        g0 = _dot_nn(jnp.where(causal, w[2 * p], 0.0), vp)
        g1 = _dot_nn(jnp.where(causal, w[2 * p + 1], 0.0), vp)
        gates.append(jnp.where(low_half, g0, g1))
    gate = jnp.concatenate(gates, axis=1) + b_full
    return u * gate


def _attn_block(q_raw, k_prev, k_cur, v_prev, v_cur, qg_row, kg_row, sinks, bias, blk):
    scale = 1.0 / math.sqrt(HEAD_DIM)
    n_q_heads = q_raw.shape[1] // HEAD_DIM
    qn = _head_rms(q_raw, qg_row)
    kn = _head_rms(jnp.concatenate([k_prev, k_cur], axis=0), kg_row)
    vb = jnp.concatenate([v_prev, v_cur], axis=0)
    qi = _iota((BLOCK, 2 * BLOCK), 0)
    kj = _iota((BLOCK, 2 * BLOCK), 1)
    dist = qi + BLOCK - kj
    valid = (dist >= 0) & (dist < BLOCK) & (kj + (blk * BLOCK - BLOCK) >= 0)
    low_half = _iota((BLOCK, LANES), 1) < HEAD_DIM
    outs = []
    for p in range(n_q_heads // 2):
        q2 = qn[:, p * LANES:(p + 1) * LANES]
        q2_swapped = _swap_halves(q2)
        kv = (2 * p) // GROUP
        grp, kv_low = kv // 2, kv % 2 == 0
        k2 = kn[:, grp * LANES:(grp + 1) * LANES]
        v2 = vb[:, grp * LANES:(grp + 1) * LANES]
        halves = []
        for half in range(2):
            h = 2 * p + half
            q_low = half == 0
            src = q2 if q_low == kv_low else q2_swapped
            qm = jnp.where(low_half if kv_low else ~low_half, src, 0.0)
            s = _dot_nt(qm, k2) * scale + bias[h]
            s = jnp.where(valid, s, NEG_INF)
            sink = sinks[h]
            m = lax.stop_gradient(jnp.maximum(jnp.max(s, axis=1, keepdims=True), sink))
            e = jnp.exp(s - m)
            denom = jnp.sum(e, axis=1, keepdims=True) + jnp.exp(sink - m)
            o = _dot_nn(e / denom, v2)
            halves.append(o if q_low == kv_low else _swap_halves(o))
        outs.append(jnp.where(low_half, halves[0], halves[1]))
    return jnp.concatenate(outs, axis=1)


def _bias_table():
    dist = np.arange(BLOCK)[:, None] + BLOCK - np.arange(2 * BLOCK)[None, :]
    n = np.maximum(dist, 0)
    max_exact = NUM_BUCKETS // 2
    nf = np.maximum(n, 1).astype(np.float64)
    large = max_exact + (np.log(nf / max_exact) / math.log(MAX_DISTANCE / max_exact) * (NUM_BUCKETS - max_exact)).astype(np.int32)
    large = np.minimum(large, NUM_BUCKETS - 1)
    return np.where(n < max_exact, n, large).astype(np.int32)


def _bias_fwd(name, rel_bias, buckets):
    nb_, nh = rel_bias.shape

    def body(rb_ref, bk_ref, o_ref):
        bk = bk_ref[...]
        for h in range(nh):
            acc = jnp.zeros(bk.shape, F32)
            for b in range(nb_):
                acc = jnp.where(bk == b, rb_ref[b, h], acc)
            o_ref[h] = acc

    return pl.pallas_call(
        body, name=name,
        in_specs=[pl.BlockSpec(memory_space=pltpu.SMEM), pl.BlockSpec(memory_space=pltpu.VMEM)],
        out_specs=pl.BlockSpec(memory_space=pltpu.VMEM),
        out_shape=jax.ShapeDtypeStruct((nh,) + buckets.shape, F32),
    )(rel_bias, buckets)


def _bias_bwd(name, dbias, buckets, nb_):
    nh = dbias.shape[0]

    def body(db_ref, bk_ref, o_ref):
        bk = bk_ref[...]
        for h in range(nh):
            d = db_ref[h]
            for b in range(nb_):
                s = jnp.sum(jnp.where(bk == b, d, 0.0), axis=0, keepdims=True)
                s = jnp.sum(s, axis=1, keepdims=True)
                o_ref[b * nh + h:b * nh + h + 1, :] = jnp.broadcast_to(s, (1, LANES))

    return pl.pallas_call(
        body, name=name,
        in_specs=[pl.BlockSpec(memory_space=pltpu.VMEM), pl.BlockSpec(memory_space=pltpu.VMEM)],
        out_specs=pl.BlockSpec(memory_space=pltpu.VMEM),
        out_shape=jax.ShapeDtypeStruct((nb_ * nh, LANES), F32),
    )(dbias, buckets)


def _sgu_fwd(name, z, g_row, w, b_full, W):
    T = z.shape[0]

    def body(u_ref, v_ref, g_ref, w_ref, b_ref, o_ref):
        o_ref[...] = _sgu_block(u_ref[...], v_ref[...], g_ref[...], w_ref[...], b_ref[...])

    return pl.pallas_call(
        body, name=name, grid=(T // CHUNK,),
        in_specs=[_row_spec(CHUNK, W, 0), _row_spec(CHUNK, W, 1), _full_spec((1, W)),
                  _full_spec(w.shape), _full_spec((CHUNK, W))],
        out_specs=_row_spec(CHUNK, W),
        out_shape=jax.ShapeDtypeStruct((T, W), F32), compiler_params=_params(1),
    )(z, z, g_row, w, b_full)


def _sgu_bwd(name, z, g_row, w, b_full, d_out, W):
    T = z.shape[0]

    def body(u_ref, v_ref, g_ref, w_ref, b_ref, do_ref, dz_ref, dg_ref, dw_ref, db_ref):
        _, vjp = jax.vjp(_sgu_block, u_ref[...], v_ref[...], g_ref[...], w_ref[...], b_ref[...])
        du, dv, dg, dw, db = vjp(do_ref[...])
        dz_ref[:, :W] = du.astype(BF16)
        dz_ref[:, W:] = dv.astype(BF16)
        i = pl.program_id(0)

        @pl.when(i == 0)
        def _():
            dg_ref[...] = jnp.zeros_like(dg_ref)
            dw_ref[...] = jnp.zeros_like(dw_ref)
            db_ref[...] = jnp.zeros_like(db_ref)

        dg_ref[...] += dg
        dw_ref[...] += dw
        db_ref[...] += db

        @pl.when(i == pl.num_programs(0) - 1)
        def _():
            db_ref[...] = _head_sum_impl(db_ref[...])

    return pl.pallas_call(
        body, name=name, grid=(T // CHUNK,),
        in_specs=[_row_spec(CHUNK, W, 0), _row_spec(CHUNK, W, 1), _full_spec((1, W)),
                  _full_spec(w.shape), _full_spec((CHUNK, W)), _row_spec(CHUNK, W)],
        out_specs=[_row_spec(CHUNK, 2 * W), _full_spec((1, W)), _full_spec(w.shape), _full_spec((CHUNK, W))],
        out_shape=[jax.ShapeDtypeStruct((T, 2 * W), BF16), jax.ShapeDtypeStruct((1, W), F32),
                   jax.ShapeDtypeStruct(w.shape, F32), jax.ShapeDtypeStruct((CHUNK, W), F32)],
        compiler_params=_params(1),
    )(z, z, g_row, w, b_full, d_out)


def _attn_specs(WQ, WKV, q_col, k_col, v_col, blk_of):
    prev_of = lambda i: jnp.maximum(blk_of(i) - 1, 0)
    return [pl.BlockSpec((BLOCK, WQ), lambda i: (blk_of(i), q_col)),
            pl.BlockSpec((BLOCK, WKV), lambda i: (prev_of(i), k_col)),
            pl.BlockSpec((BLOCK, WKV), lambda i: (blk_of(i), k_col)),
            pl.BlockSpec((BLOCK, WKV), lambda i: (prev_of(i), v_col)),
            pl.BlockSpec((BLOCK, WKV), lambda i: (blk_of(i), v_col))]


def _attn_fwd(name, z, qg_row, kg_row, sinks_col, bias, WQ, WKV, q_col, k_col, v_col):
    T = z.shape[0]
    nh = sinks_col.shape[0]

    def body(q_ref, kp_ref, kc_ref, vp_ref, vc_ref, qg_ref, kg_ref, s_ref, b_ref, o_ref):
        sinks = [s_ref[h:h + 1, :] for h in range(nh)]
        o_ref[...] = _attn_block(q_ref[...], kp_ref[...], kc_ref[...], vp_ref[...], vc_ref[...],
                                 qg_ref[...], kg_ref[...], sinks, b_ref[...], pl.program_id(0))

    return pl.pallas_call(
        body, name=name, grid=(T // BLOCK,),
        in_specs=_attn_specs(WQ, WKV, q_col, k_col, v_col, lambda i: i)
        + [_full_spec((1, WQ)), _full_spec((1, WKV)), _full_spec((nh, 1)), _full_spec(bias.shape)],
        out_specs=_row_spec(BLOCK, WQ),
        out_shape=jax.ShapeDtypeStruct((T, WQ), F32), compiler_params=_params(1),
    )(z, z, z, z, z, qg_row, kg_row, sinks_col, bias)


def _attn_bwd(name, z, qg_row, kg_row, sinks_col, bias, d_out, WQ, WKV, q_col, k_col, v_col):
    T = z.shape[0]
    nblk = T // BLOCK
    nh = sinks_col.shape[0]
    blk_of = lambda i: nblk - 1 - i

    def body(q_ref, kp_ref, kc_ref, vp_ref, vc_ref, qg_ref, kg_ref, s_ref, b_ref, do_ref,
             dq_ref, dk_ref, dv_ref, dqg_ref, dkg_ref, ds_ref, db_ref, carry_k, carry_v):
        i = pl.program_id(0)
        blk = nblk - 1 - i
        sinks = [s_ref[h:h + 1, :] for h in range(nh)]
        fn = lambda q, kp, kc, vp, vc, qg, kg, sk, bs: _attn_block(q, kp, kc, vp, vc, qg, kg, sk, bs, blk)
        _, vjp = jax.vjp(fn, q_ref[...], kp_ref[...], kc_ref[...], vp_ref[...], vc_ref[...],
                         qg_ref[...], kg_ref[...], sinks, b_ref[...])
        dq, dkp, dkc, dvp, dvc, dqg, dkg, dsk, dbs = vjp(do_ref[...])

        @pl.when(i == 0)
        def _():
            carry_k[...] = jnp.zeros_like(carry_k)
            carry_v[...] = jnp.zeros_like(carry_v)
            dqg_ref[...] = jnp.zeros_like(dqg_ref)
            dkg_ref[...] = jnp.zeros_like(dkg_ref)
            ds_ref[...] = jnp.zeros_like(ds_ref)
            db_ref[...] = jnp.zeros_like(db_ref)

        dq_ref[...] = dq.astype(BF16)
        dk_ref[...] = (dkc + carry_k[...]).astype(BF16)
        dv_ref[...] = (dvc + carry_v[...]).astype(BF16)
        carry_k[...] = dkp
        carry_v[...] = dvp
        dqg_ref[...] += dqg
        dkg_ref[...] += dkg
        for h in range(nh):
            ds_ref[h:h + 1, :] += dsk[h]
        db_ref[...] += dbs

    return pl.pallas_call(
        body, name=name, grid=(nblk,),
        in_specs=_attn_specs(WQ, WKV, q_col, k_col, v_col, blk_of)
        + [_full_spec((1, WQ)), _full_spec((1, WKV)), _full_spec((nh, 1)), _full_spec(bias.shape),
           pl.BlockSpec((BLOCK, WQ), lambda i: (blk_of(i), 0))],
        out_specs=[pl.BlockSpec((BLOCK, WQ), lambda i: (blk_of(i), 0)),
                   pl.BlockSpec((BLOCK, WKV), lambda i: (blk_of(i), 0)),
                   pl.BlockSpec((BLOCK, WKV), lambda i: (blk_of(i), 0)),
                   _full_spec((1, WQ)), _full_spec((1, WKV)), _full_spec((nh, 1)), _full_spec(bias.shape)],
        out_shape=[jax.ShapeDtypeStruct((T, WQ), BF16), jax.ShapeDtypeStruct((T, WKV), BF16),
                   jax.ShapeDtypeStruct((T, WKV), BF16), jax.ShapeDtypeStruct((1, WQ), F32),
                   jax.ShapeDtypeStruct((1, WKV), F32), jax.ShapeDtypeStruct((nh, 1), F32),
                   jax.ShapeDtypeStruct(bias.shape, F32)],
        scratch_shapes=[pltpu.VMEM((BLOCK, WKV), F32), pltpu.VMEM((BLOCK, WKV), F32)],
        compiler_params=_params(1),
    )(z, z, z, z, z, qg_row, kg_row, sinks_col, bias, d_out)


def _adamw(name, w, g, m, v):
    shape = w.shape
    C = shape[-1]
    R = int(np.prod(shape[:-1]))
    w2, g2, m2, v2 = (t.reshape(R, C) for t in (w, g, m, v))
    tr = ROWS if R % ROWS == 0 else R

    def body(w_ref, g_ref, m_ref, v_ref, d_ref, nm_ref, nv_ref):
        gv = g_ref[...]
        nm = ADAM_B1 * m_ref[...] + (1.0 - ADAM_B1) * gv
        nv = ADAM_B2 * v_ref[...] + (1.0 - ADAM_B2) * (gv * gv)
        m_hat = nm / (1.0 - ADAM_B1 ** ADAM_STEP)
        v_hat = nv / (1.0 - ADAM_B2 ** ADAM_STEP)
        d_ref[...] = -ADAM_LR * (m_hat / (jnp.sqrt(v_hat) + ADAM_EPS) + ADAM_WD * w_ref[...])
        nm_ref[...] = nm
        nv_ref[...] = nv

    spec = pl.BlockSpec((tr, C), lambda i: (i, 0))
    outs = pl.pallas_call(
        body, name=name, grid=(R // tr,), in_specs=[spec] * 4, out_specs=[spec] * 3,
        out_shape=[jax.ShapeDtypeStruct((R, C), F32)] * 3, compiler_params=_params(1),
    )(w2, g2, m2, v2)
    return tuple(o.reshape(shape) for o in outs)


def _pad_rows(flat):
    n = flat.shape[0]
    tile = 8 * LANES
    padded = -(-n // tile) * tile
    return jnp.pad(flat, (0, padded - n)).reshape(padded // LANES, LANES)


def kernel(x, rel_bias, norm1_g, w_in, sgu_norm_g, sgu_w, sgu_b, q_norm_g, k_norm_g, sinks, out_norm_a, out_norm_b, w_out, norm2_g, w_gate, w_up, w_down, loss_target, m_rel_bias, m_norm1_g, m_w_in, m_sgu_norm_g, m_sgu_w, m_sgu_b, m_q_norm_g, m_k_norm_g, m_sinks, m_out_norm_a, m_out_norm_b, m_w_out, m_norm2_g, m_w_gate, m_w_up, m_w_down, v_rel_bias, v_norm1_g, v_w_in, v_sgu_norm_g, v_sgu_w, v_sgu_b, v_q_norm_g, v_k_norm_g, v_sinks, v_out_norm_a, v_out_norm_b, v_w_out, v_norm2_g, v_w_gate, v_w_up, v_w_down):
    L = w_in.shape[0]
    T, D = x.shape[1], x.shape[2]
    W = D // 2
    NH = W // HEAD_DIM
    WKV = N_KV_HEADS * HEAD_DIM
    IN = N_DEV * w_in.shape[2]
    FF = N_DEV * w_gate.shape[2]
    assert IN == 2 * W + W + 2 * WKV and NH // N_KV_HEADS == GROUP
    q_col, k_col, v_col = 2 * W // W, (3 * W) // WKV, (3 * W + WKV) // WKV
    x0 = x.reshape(T, D)
    target = loss_target.reshape(T, D)

    shards = [jnp.swapaxes(w_in, 1, 2).astype(BF16), w_out.astype(BF16), jnp.swapaxes(w_gate, 1, 2).astype(BF16),
              jnp.swapaxes(w_up, 1, 2).astype(BF16), w_down.astype(BF16)]
    me = 4 * lax.axis_index("x") + 2 * lax.axis_index("y") + lax.axis_index("c")

    def landing(block):
        return lax.dynamic_update_slice(lax.empty((N_DEV,) + block.shape, block.dtype), block[None], (me, 0, 0))

    def as_matrices(got):
        return [g.reshape(N_DEV * g.shape[1], g.shape[2]) for g in got]

    full = {0: as_matrices(_all_gather("gather_weights_0", [s[0] for s in shards]))}
    flight = {}

    def start_gather(l, after):
        srcs = [s[l] for s in shards]
        flight[l] = _exchange_start(f"gather_start_{l}", "gather", srcs, [landing(s) for s in srcs], after)
        return flight[l][4]

    def finish_gather(l, after):
        s_sem, r_sem, srcs, lands, _ = flight.pop(l)
        full[l] = as_matrices(_exchange_wait(f"gather_wait_{l}", "gather", s_sem, r_sem, srcs, lands, after)[1])

    buckets = jnp.asarray(_bias_table())
    bias = _bias_fwd("bias_table", rel_bias, buckets)

    saved = []
    xl = x0
    for l in range(L):
        if l > 0:
            finish_gather(l, xl)
        token = start_gather(l + 1, full[l][0] if l == 0 else xl) if l + 1 < L else xl
        w_in_t, w_o, w_g_t, w_u_t, w_d = full[l]
        g1, g2 = norm1_g[l][None], norm2_g[l][None]
        sg_row = sgu_norm_g[l].reshape(1, W)
        b_full = jnp.repeat(sgu_b[l].T, HEAD_DIM, axis=1)
        qg_row = jnp.tile(q_norm_g[l], NH)[None]
        kg_row = jnp.tile(k_norm_g[l], N_KV_HEADS)[None]
        sinks_col = sinks[l][:, None]
        ga, gb = out_norm_a[l][None], out_norm_b[l][None]

        h = _rms_fwd(f"norm1_{l}", xl, g1, token)
        (z,) = _mm(f"proj_in_{l}", "nt", [h], [w_in_t], [(0, 0, 0)], 1, T, IN, D, 512, 896, D, _ep_plain, [F32])
        out_a = _sgu_fwd(f"sgu_{l}", z, sg_row, sgu_w[l], b_full, W)
        out_b = _attn_fwd(f"attn_{l}", z, qg_row, kg_row, sinks_col, bias, W, WKV, q_col, k_col, v_col)
        mixed = _outnorm_fwd(f"outnorm_{l}", out_a, out_b, ga, gb)
        (x1,) = _mm(f"proj_out_{l}", "nn", [mixed], [w_o], [(0, 0, 0)], 1, T, D, D, 512, 1024, D,
                    _ep_residual, [F32], extras=[xl])
        h2 = _rms_fwd(f"norm2_{l}", x1, g2, g2)
        gate, up, act = _mm(f"mlp_in_{l}", "nt", [h2], [w_g_t, w_u_t], [(0, 0, 0), (0, 1, 1)], 2, T, FF, D,
                            1024, 512, D, _ep_swiglu, [F32, F32, BF16])
        (x2,) = _mm(f"mlp_out_{l}", "nn", [act], [w_d], [(0, 0, 0)], 1, T, D, FF, 512, 512, FF,
                    _ep_residual, [F32], extras=[x1])
        saved.append((xl, h, z, out_a, out_b, mixed, x1, h2, gate, up, act,
                      g1, g2, sg_row, b_full, qg_row, kg_row, sinks_col, ga, gb))
        xl = x2

    loss_part, dy, dy16 = _loss_head("loss_head", xl, target)

    dbias = None
    small = [None] * L
    scatters = []

    def start_scatter(name, which, grads_t, after):
        srcs = [t.reshape(N_DEV, t.shape[0] // N_DEV, D) for t in grads_t]
        lands = [lax.empty((N_DEV - 1,) + s.shape[1:], BF16) for s in srcs]
        s_sem, r_sem, srcs, lands, tok = _exchange_start(name, "scatter", srcs, lands, after)
        scatters.append((name, which, s_sem, r_sem, srcs, lands))
        return tok

    token = None
    for l in reversed(range(L)):
        w_in_t, w_o, w_g_t, w_u_t, w_d = full[l]
        (xl, h, z, out_a, out_b, mixed, x1, h2, gate, up, act,
         g1, g2, sg_row, b_full, qg_row, kg_row, sinks_col, ga, gb) = saved[l]

        dgate, dup = _mm(f"d_mlp_out_{l}", "nt", [dy16], [w_d], [(0, 0, 0)], 1, T, FF, D, 1024, 512, D,
                         _ep_swiglu_bwd, [BF16, BF16], extras=[gate, up], after=token)
        (dw_d,) = _mm(f"dw_down_{l}", "tn", [act], [dy16], [(0, 0, 0)], 1, FF, D, T, 1408, 1024, 512, _ep_plain, [BF16])
        (dh2,) = _mm(f"d_mlp_in_{l}", "nn", [dgate, dup], [w_g_t, w_u_t], [(0, 0, 0), (1, 1, 0)], 1, T, D, FF,
                     512, 512, FF // 2, _ep_plain, [F32])
        dw_g, dw_u = _mm(f"dw_gate_up_{l}", "tn", [dgate, dup], [h2], [(0, 0, 0), (1, 0, 1)], 2, FF, D, T,
                         1408, 1024, 512, _ep_two, [BF16, BF16])
        token = start_scatter(f"scatter_mlp_start_{l}", (l, (4, 2, 3)), [dw_d, dw_g, dw_u], dw_u)
        dx1, dx1_16, dg2 = _rms_bwd(f"d_norm2_{l}", dh2, x1, g2, dy, token)
        (dmixed,) = _mm(f"d_proj_out_{l}", "nt", [dx1_16], [w_o], [(0, 0, 0)], 1, T, D, D, 512, 1024, D, _ep_plain, [F32])
        (dw_o,) = _mm(f"dw_out_{l}", "tn", [mixed], [dx1_16], [(0, 0, 0)], 1, D, D, T, 1024, 1024, 512, _ep_plain, [BF16])
        d_a, d_b, dga, dgb = _outnorm_bwd(f"d_outnorm_{l}", dmixed, out_a, out_b, ga, gb)
        dz_uv, dsg, dsw, dsb = _sgu_bwd(f"d_sgu_{l}", z, sg_row, sgu_w[l], b_full, d_a, W)
        dq, dk, dv, dqg, dkg, dsk, dbs = _attn_bwd(f"d_attn_{l}", z, qg_row, kg_row, sinks_col, bias, d_b,
                                                   W, WKV, q_col, k_col, v_col)
        dbias = dbs if dbias is None else dbias + dbs
        dz = jnp.concatenate([dz_uv, dq, dk, dv], axis=1)
        (dh,) = _mm(f"d_proj_in_{l}", "nn", [dz], [w_in_t], [(0, 0, 0)], 1, T, D, IN, 512, 512, IN, _ep_plain, [F32])
        (dw_i,) = _mm(f"dw_in_{l}", "tn", [dz], [h], [(0, 0, 0)], 1, IN, D, T, 896, 1024, 512, _ep_plain, [BF16])
        dy, dy16, dg1 = _rms_bwd(f"d_norm1_{l}", dh, xl, g1, dx1, g1)
        token = start_scatter(f"scatter_mix_start_{l}", (l, (1, 0)), [dw_o, dw_i], dw_i)

        small[l] = dict(norm1_g=dg1[0], sgu_norm_g=dsg.reshape(NH, HEAD_DIM), sgu_w=dsw,
                        sgu_b=dsb[:, ::HEAD_DIM].T, q_norm_g=dqg.reshape(NH, HEAD_DIM).sum(0),
                        k_norm_g=dkg.reshape(N_KV_HEADS, HEAD_DIM).sum(0), sinks=dsk[:, 0],
                        out_norm_a=dga[0], out_norm_b=dgb[0], norm2_g=dg2[0])

    grad_x = dy.reshape(x.shape)
    d_rel = _bias_bwd("d_bias_table", dbias, buckets, NUM_BUCKETS)[:, 0].reshape(NUM_BUCKETS, NH)

    names = ["norm1_g", "sgu_norm_g", "sgu_w", "sgu_b", "q_norm_g", "k_norm_g", "sinks", "out_norm_a", "out_norm_b", "norm2_g"]
    parts = {"rel_bias": d_rel}
    for nme in names:
        parts[nme] = jnp.stack([small[l][nme] for l in range(L)])
    order = ["rel_bias"] + names
    packed = jnp.concatenate([_pad_rows(parts[nme].reshape(-1)) for nme in order], axis=0)
    small_flight = _exchange_start("gather_small_start", "gather", [packed], [landing(packed)], token)
    after = small_flight[4]

    grads_big = {}
    for name, (l, which), s_sem, r_sem, srcs, lands in scatters:
        srcs, lands = _exchange_wait(name.replace("start", "wait"), "scatter", s_sem, r_sem, srcs, lands, after)
        for i, src, land in zip(which, srcs, lands):
            own = lax.dynamic_index_in_dim(src, me, 0, keepdims=False)
            rows = own.shape[0]
            after = _sum_parts(f"sum_grads_{l}_{i}", own, land, 64 if rows % 64 == 0 else rows)
            grads_big[i, l] = after
    g_w_in = jnp.swapaxes(jnp.stack([grads_big[0, l] for l in range(L)]), 1, 2)
    g_w_out = jnp.stack([grads_big[1, l] for l in range(L)])
    g_w_gate = jnp.swapaxes(jnp.stack([grads_big[2, l] for l in range(L)]), 1, 2)
    g_w_up = jnp.swapaxes(jnp.stack([grads_big[3, l] for l in range(L)]), 1, 2)
    g_w_down = jnp.stack([grads_big[4, l] for l in range(L)])

    s_sem, r_sem, srcs, lands, _ = small_flight
    _, (everyone,) = _exchange_wait("gather_small_wait", "gather", s_sem, r_sem, srcs, lands, after)
    rows = packed.shape[0]
    summed = _sum_slots("sum_small_grads", everyone, 64 if rows % 64 == 0 else 8)
    g_small, at = {}, 0
    for nme in order:
        n = int(np.prod(parts[nme].shape))
        n_rows = -(-n // (8 * LANES)) * 8
        g_small[nme] = summed[at:at + n_rows].reshape(-1)[:n].reshape(parts[nme].shape)
        at += n_rows

    loss = lax.psum(loss_part[0, 0], ("x", "y", "c"))

    grads = dict(g_small, w_in=g_w_in, w_out=g_w_out, w_gate=g_w_gate, w_up=g_w_up, w_down=g_w_down)
    weights = dict(rel_bias=rel_bias, norm1_g=norm1_g, w_in=w_in, sgu_norm_g=sgu_norm_g, sgu_w=sgu_w, sgu_b=sgu_b,
                   q_norm_g=q_norm_g, k_norm_g=k_norm_g, sinks=sinks, out_norm_a=out_norm_a, out_norm_b=out_norm_b,
                   w_out=w_out, norm2_g=norm2_g, w_gate=w_gate, w_up=w_up, w_down=w_down)
    ms = dict(rel_bias=m_rel_bias, norm1_g=m_norm1_g, w_in=m_w_in, sgu_norm_g=m_sgu_norm_g, sgu_w=m_sgu_w, sgu_b=m_sgu_b,
              q_norm_g=m_q_norm_g, k_norm_g=m_k_norm_g, sinks=m_sinks, out_norm_a=m_out_norm_a, out_norm_b=m_out_norm_b,
              w_out=m_w_out, norm2_g=m_norm2_g, w_gate=m_w_gate, w_up=m_w_up, w_down=m_w_down)
    vs = dict(rel_bias=v_rel_bias, norm1_g=v_norm1_g, w_in=v_w_in, sgu_norm_g=v_sgu_norm_g, sgu_w=v_sgu_w, sgu_b=v_sgu_b,
              q_norm_g=v_q_norm_g, k_norm_g=v_k_norm_g, sinks=v_sinks, out_norm_a=v_out_norm_a, out_norm_b=v_out_norm_b,
              w_out=v_w_out, norm2_g=v_norm2_g, w_gate=v_w_gate, w_up=v_w_up, w_down=v_w_down)
    all_names = ["rel_bias", "norm1_g", "w_in", "sgu_norm_g", "sgu_w", "sgu_b", "q_norm_g", "k_norm_g", "sinks",
                 "out_norm_a", "out_norm_b", "w_out", "norm2_g", "w_gate", "w_up", "w_down"]
    deltas, new_m, new_v = [], [], []
    for nme in all_names:
        d, nm, nv = _adamw(f"adamw_{nme}", weights[nme], grads[nme], ms[nme], vs[nme])
        deltas.append(d)
        new_m.append(nm)
        new_v.append(nv)
    return (loss, grad_x, *[grads[nme] for nme in all_names], *deltas, *new_m, *new_v)
```

```python
import functools
import math

import numpy as np
import jax
import jax.numpy as jnp
from jax import lax
from jax.experimental import pallas as pl
from jax.experimental.pallas import tpu as pltpu

F32 = jnp.float32
BF16 = jnp.bfloat16

N_DEV = 8
HEAD_DIM = 64
CHUNK = 128
BLOCK = 128
N_KV_HEADS = 4
GROUP = 4
NUM_BUCKETS = 32
MAX_DISTANCE = 128
EPS = 1e-6
NEG_INF = -1e30
LANES = 128
VMEM_LIMIT = 56 * 2 ** 20

ADAM_LR = 0.001
ADAM_B1 = 0.9
ADAM_B2 = 0.999
ADAM_EPS = 1e-08
ADAM_WD = 0.01
ADAM_STEP = 10

MESH = pl.DeviceIdType.MESH
ANY = pl.BlockSpec(memory_space=pl.ANY)
HBM = pl.BlockSpec(memory_space=pltpu.HBM)
SEM = pl.BlockSpec(memory_space=pltpu.SEMAPHORE)
EFFECT = pltpu.SideEffectType.DATAFLOW_SIDE_EFFECTING


def _params(n_axes):
    return pltpu.CompilerParams(dimension_semantics=("arbitrary",) * n_axes, vmem_limit_bytes=VMEM_LIMIT)


def _my_place():
    return lax.axis_index("x"), lax.axis_index("y"), lax.axis_index("c")


def _all_gather(name, arrs):
    n = len(arrs)

    def body(*refs):
        ins, outs = refs[:n], refs[n:2 * n]
        send_sems, recv_sems, local_sems = refs[2 * n:]
        x, y, c = _my_place()
        sibling = (x, y, 1 - c)
        chips = [(1 - x, y), (x, 1 - y), (1 - x, 1 - y)]

        def slot(a, px, py, pc):
            return outs[a].at[4 * px + 2 * py + pc]

        def copy(a, k, block, to, src=None):
            return pltpu.make_async_remote_copy(
                src_ref=slot(a, *block) if src is None else src, dst_ref=slot(a, *block),
                send_sem=send_sems.at[7 * a + k], recv_sem=recv_sems.at[7 * a + k],
                device_id=to, device_id_type=MESH)

        started = []
        for a in range(n):
            mine = pltpu.make_async_copy(ins[a], slot(a, x, y, c), local_sems.at[a])
            mine.start()
            started.append(mine)
        sends = []
        for a in range(n):
            first = [copy(a, 0, (x, y, c), sibling, src=ins[a])]
            first += [copy(a, 1 + j, (x, y, c), (*chip, c), src=ins[a]) for j, chip in enumerate(chips)]
            for cp in first:
                cp.start()
            sends += first
        for a in range(n):
            for j, chip in enumerate(chips):
                copy(a, 1 + j, (*chip, c), (x, y, c)).wait_recv()
                fwd = copy(a, 4 + j, (*chip, c), sibling)
                fwd.start()
                sends.append(fwd)
        for a in range(n):
            copy(a, 0, sibling, (x, y, c)).wait_recv()
            for j, chip in enumerate(chips):
                copy(a, 4 + j, (*chip, 1 - c), (x, y, c)).wait_recv()
        for cp in sends:
            cp.wait_send()
        for cp in started:
            cp.wait()

    return pl.pallas_call(
        body, name=name,
        out_shape=[jax.ShapeDtypeStruct((N_DEV,) + a.shape, a.dtype) for a in arrs],
        in_specs=[ANY] * n, out_specs=[ANY] * n,
        scratch_shapes=[pltpu.SemaphoreType.DMA((7 * n,)), pltpu.SemaphoreType.DMA((7 * n,)),
                        pltpu.SemaphoreType.DMA((n,))],
    )(*arrs)


def _peer(k, x, y, c):
    return (1 - x if k & 4 else x), (1 - y if k & 2 else y), (1 - c if k & 1 else c)


PEER_ORDER = (1, 2, 4, 3, 5, 6, 7)


def _exchange_copy(kind, k, a, srcs, lands, send_sems, recv_sems, arriving=False):
    x, y, c = _my_place()
    me = 4 * x + 2 * y + c
    px, py, pc = _peer(k, x, y, c)
    them = 4 * px + 2 * py + pc
    if kind == "gather":
        src, dst_there, dst_here = srcs[a], lands[a].at[me], lands[a].at[them]
    else:
        src, dst_there, dst_here = srcs[a].at[them], lands[a].at[k - 1], lands[a].at[k - 1]
    return pltpu.make_async_remote_copy(
        src_ref=src, dst_ref=dst_here if arriving else dst_there,
        send_sem=send_sems.at[7 * a + k - 1], recv_sem=recv_sems.at[7 * a + k - 1],
        device_id=(px, py, pc), device_id_type=MESH)


def _exchange_start(name, kind, srcs, lands, after):
    n = len(srcs)

    def body(*refs):
        ins, lnd = refs[:n], refs[n:2 * n]
        send_sems, recv_sems = refs[2 * n + 1], refs[2 * n + 2]
        token = refs[-1]
        for k in PEER_ORDER:
            for a in range(n):
                _exchange_copy(kind, k, a, ins, lnd, send_sems, recv_sems).start()
        token[...] = jnp.zeros_like(token)

    hbm = lambda t: pltpu.with_memory_space_constraint(t, pltpu.HBM)
    out = pl.pallas_call(
        body, name=name,
        out_shape=(pltpu.SemaphoreType.DMA((7 * n,)), pltpu.SemaphoreType.DMA((7 * n,)),
                   *[pltpu.HBM(t.shape, t.dtype) for t in srcs], *[pltpu.HBM(t.shape, t.dtype) for t in lands],
                   jax.ShapeDtypeStruct((8, LANES), F32)),
        in_specs=[HBM] * (2 * n) + [ANY],
        out_specs=(SEM, SEM, *[HBM] * (2 * n), pl.BlockSpec(memory_space=pltpu.VMEM)),
        input_output_aliases={i: 2 + i for i in range(2 * n)},
        compiler_params=pltpu.CompilerParams(has_side_effects=EFFECT),
    )(*[hbm(t) for t in srcs], *[hbm(t) for t in lands], after)
    return out[0], out[1], list(out[2:2 + n]), list(out[2 + n:2 + 2 * n]), out[-1]


def _exchange_wait(name, kind, send_sems, recv_sems, srcs, lands, after):
    n = len(srcs)

    def body(*refs):
        ins, lnd = refs[:n], refs[n:2 * n]
        s_sems, r_sems = refs[2 * n], refs[2 * n + 1]
        for a in range(n):
            for k in PEER_ORDER:
                _exchange_copy(kind, k, a, ins, lnd, s_sems, r_sems).wait_send()
                _exchange_copy(kind, k, a, ins, lnd, s_sems, r_sems, arriving=True).wait_recv()

    out = pl.pallas_call(
        body, name=name,
        out_shape=(*[pltpu.HBM(t.shape, t.dtype) for t in srcs], *[pltpu.HBM(t.shape, t.dtype) for t in lands]),
        in_specs=[HBM] * (2 * n) + [SEM, SEM, ANY],
        out_specs=tuple([HBM] * (2 * n)),
        input_output_aliases={i: i for i in range(2 * n)},
        compiler_params=pltpu.CompilerParams(has_side_effects=EFFECT),
    )(*srcs, *lands, send_sems, recv_sems, after)
    return list(out[:n]), list(out[n:])


def _sum_parts(name, own, parts, rows):
    R, C = own.shape

    def body(own_ref, p_ref, o_ref):
        acc = own_ref[...].astype(F32)
        for s in range(N_DEV - 1):
            acc = acc + p_ref[s].astype(F32)
        o_ref[...] = acc

    return pl.pallas_call(
        body, name=name, grid=(R // rows,),
        in_specs=[pl.BlockSpec((rows, C), lambda i: (i, 0)), pl.BlockSpec((N_DEV - 1, rows, C), lambda i: (0, i, 0))],
        out_specs=pl.BlockSpec((rows, C), lambda i: (i, 0)),
        out_shape=jax.ShapeDtypeStruct((R, C), F32), compiler_params=_params(1),
    )(own, parts)


def _sum_slots(name, a, rows):
    _, R, C = a.shape

    def body(a_ref, o_ref):
        acc = a_ref[0].astype(F32)
        for s in range(1, N_DEV):
            acc = acc + a_ref[s].astype(F32)
        o_ref[...] = acc

    return pl.pallas_call(
        body, name=name, grid=(R // rows,),
        in_specs=[pl.BlockSpec((N_DEV, rows, C), lambda i: (0, i, 0))],
        out_specs=pl.BlockSpec((rows, C), lambda i: (i, 0)),
        out_shape=jax.ShapeDtypeStruct((R, C), F32), compiler_params=_params(1),
    )(a)


_DIMS = {"nn": (((1,), (0,)), ((), ())), "nt": (((1,), (1,)), ((), ())), "tn": (((0,), (0,)), ((), ()))}


def _mm(name, mode, a_list, b_list, pairs, n_acc, M, N, K, tm, tn, tk, epilogue, out_dtypes, extras=(), after=None):
    tm, tn, tk = min(tm, M), min(tn, N), min(tk, K)
    assert M % tm == 0 and N % tn == 0 and K % tk == 0, (name, M, N, K, tm, tn, tk)
    nk = K // tk
    na, nb, ne, no = len(a_list), len(b_list), len(extras), len(out_dtypes)
    dims = _DIMS[mode]
    a_spec = (pl.BlockSpec((tk, tm), lambda j, i, k: (k, i)) if mode == "tn"
              else pl.BlockSpec((tm, tk), lambda j, i, k: (i, k)))
    b_spec = (pl.BlockSpec((tn, tk), lambda j, i, k: (j, k)) if mode == "nt"
              else pl.BlockSpec((tk, tn), lambda j, i, k: (k, j)))
    o_spec = pl.BlockSpec((tm, tn), lambda j, i, k: (i, j))
    tail = [] if after is None else [after]

    def body(*refs):
        a_refs, b_refs = refs[:na], refs[na:na + nb]
        e_refs = refs[na + nb:na + nb + ne]
        first_out = na + nb + ne + len(tail)
        o_refs = refs[first_out:first_out + no]
        acc_refs = refs[first_out + no:]

        def partial(oi):
            tot = None
            for ai, bi, ti in pairs:
                if ti == oi:
                    d = lax.dot_general(a_refs[ai][...], b_refs[bi][...], dims, preferred_element_type=F32)
                    tot = d if tot is None else tot + d
            return tot

        def finish(accs):
            outs = epilogue(accs, [e[...] for e in e_refs])
            for o_ref, o in zip(o_refs, outs):
                o_ref[...] = o.astype(o_ref.dtype)

        if nk == 1:
            finish([partial(oi) for oi in range(n_acc)])
        else:
            k = pl.program_id(2)

            @pl.when(k == 0)
            def _():
                for r in acc_refs:
                    r[...] = jnp.zeros_like(r)

            for oi in range(n_acc):
                acc_refs[oi][...] += partial(oi)

            @pl.when(k == nk - 1)
            def _():
                finish([r[...] for r in acc_refs])

    return pl.pallas_call(
        body, name=name, grid=(N // tn, M // tm, nk),
        in_specs=[a_spec] * na + [b_spec] * nb + [o_spec] * ne + [ANY] * len(tail),
        out_specs=[o_spec] * no,
        out_shape=[jax.ShapeDtypeStruct((M, N), dt) for dt in out_dtypes],
        scratch_shapes=[pltpu.VMEM((tm, tn), F32)] * (n_acc if nk > 1 else 0),
        compiler_params=_params(3),
    )(*a_list, *b_list, *extras, *tail)


EPILOGUE_ROWS = 128


def _mm_rows(name, a_list, b_list, M, N, K, tm, tk, epilogue, out_dtypes, n_row_out, extras=(), rows=(), after=None):
    tm, tk = min(tm, M), min(tk, K)
    assert M % tm == 0 and K % tk == 0, (name, M, K, tm, tk)
    nk = K // tk
    rc = min(EPILOGUE_ROWS, tm)
    na, ne, nr, no = len(a_list), len(extras), len(rows), len(out_dtypes)
    tail = [] if after is None else [after]
    a_spec = pl.BlockSpec((tm, tk), lambda i, k: (i, k))
    b_spec = pl.BlockSpec((tk, N), lambda i, k: (k, 0))
    o_spec = pl.BlockSpec((tm, N), lambda i, k: (i, 0))
    r_spec = pl.BlockSpec((1, N), lambda i, k: (0, 0))

    def body(*refs):
        a_refs, b_refs = refs[:na], refs[na:2 * na]
        e_refs = refs[2 * na:2 * na + ne]
        r_refs = refs[2 * na + ne:2 * na + ne + nr]
        first_out = 2 * na + ne + nr + len(tail)
        o_refs = refs[first_out:first_out + no]
        ro_refs = refs[first_out + no:first_out + no + n_row_out]
        acc_ref = refs[first_out + no + n_row_out] if nk > 1 else None
        i, k = pl.program_id(0), pl.program_id(1)
        part = None
        for a_ref, b_ref in zip(a_refs, b_refs):
            d = lax.dot_general(a_ref[...], b_ref[...], _DIMS["nn"], preferred_element_type=F32)
            part = d if part is None else part + d

        def finish(acc):
            sums = None
            for c in range(tm // rc):
                sl = slice(c * rc, (c + 1) * rc)
                outs, row_sums = epilogue(acc[sl, :], [e[sl, :] for e in e_refs], [r[...] for r in r_refs])
                for o_ref, o in zip(o_refs, outs):
                    o_ref[sl, :] = o.astype(o_ref.dtype)
                sums = row_sums if sums is None else tuple(s + t for s, t in zip(sums, row_sums))
            for ro_ref, val in zip(ro_refs, sums):
                @pl.when(i == 0)
                def _():
                    ro_ref[...] = val

                @pl.when(i > 0)
                def _():
                    ro_ref[...] += val

        if nk == 1:
            finish(part)
        else:
            @pl.when(k == 0)
            def _():
                acc_ref[...] = part

            @pl.when(k > 0)
            def _():
                acc_ref[...] += part

            @pl.when(k == nk - 1)
            def _():
                finish(acc_ref)

    outs = pl.pallas_call(
        body, name=name, grid=(M // tm, nk),
        in_specs=[a_spec] * na + [b_spec] * na + [o_spec] * ne + [r_spec] * nr + [ANY] * len(tail),
        out_specs=[o_spec] * no + [r_spec] * n_row_out,
        out_shape=[jax.ShapeDtypeStruct((M, N), dt) for dt in out_dtypes] + [jax.ShapeDtypeStruct((1, N), F32)] * n_row_out,
        scratch_shapes=[pltpu.VMEM((tm, N), F32)] * (1 if nk > 1 else 0),
        compiler_params=_params(2),
    )(*a_list, *b_list, *extras, *rows, *tail)
    return outs[:no], outs[no:]


def _ep_rows_residual(acc, ex, rows):
    return (ex[0] + acc,), ()


def _ep_rows_residual_norm(acc, ex, rows):
    x = ex[0] + acc
    r = lax.rsqrt(jnp.mean(x * x, axis=-1, keepdims=True) + EPS)
    return (x, (x * r) * rows[0]), ()


def _ep_rows_rms_bwd(acc, ex, rows):
    xv, res = ex
    r = lax.rsqrt(jnp.mean(xv * xv, axis=-1, keepdims=True) + EPS)
    xh = xv * r
    dg = jnp.sum(acc * xh, axis=0, keepdims=True)
    dxh = acc * rows[0]
    dx = r * (dxh - xh * jnp.mean(dxh * xh, axis=-1, keepdims=True)) + res
    return (dx, dx), (dg,)


def _sigmoid(x):
    return 1.0 / (1.0 + jnp.exp(-x))


def _ep_plain(accs, ex):
    return (accs[0],)


def _ep_swiglu(accs, ex):
    g, u = accs
    return g, u, (g * _sigmoid(g)) * u


def _ep_swiglu_bwd(accs, ex):
    dact = accs[0]
    g, u = ex
    sig = _sigmoid(g)
    silu = g * sig
    return dact * u * (sig * (1.0 + g * (1.0 - sig))), dact * silu


def _ep_two(accs, ex):
    return accs[0], accs[1]


ROWS = 256


def _row_spec(tm, width, col=0):
    return pl.BlockSpec((tm, width), lambda i: (i, col))


def _full_spec(shape):
    nd = len(shape)
    return pl.BlockSpec(shape, lambda i: (0,) * nd)


def _rms_fwd(name, x, g):
    T, D = x.shape
    tm = min(ROWS, T)

    def body(x_ref, g_ref, o_ref):
        xv = x_ref[...]
        r = lax.rsqrt(jnp.mean(xv * xv, axis=-1, keepdims=True) + EPS)
        o_ref[...] = ((xv * r) * g_ref[...]).astype(BF16)

    return pl.pallas_call(
        body, name=name, grid=(T // tm,),
        in_specs=[_row_spec(tm, D), _full_spec((1, D))], out_specs=_row_spec(tm, D),
        out_shape=jax.ShapeDtypeStruct((T, D), BF16), compiler_params=_params(1),
    )(x, g)


def _outnorm_fwd(name, a, b, ga, gb):
    T, W = a.shape
    tm = min(ROWS, T)

    def body(a_ref, b_ref, ga_ref, gb_ref, o_ref):
        for src, gain, lo in ((a_ref, ga_ref, 0), (b_ref, gb_ref, W)):
            v = src[...]
            r = lax.rsqrt(jnp.mean(v * v, axis=-1, keepdims=True) + EPS)
            o_ref[:, lo:lo + W] = ((v * r) * gain[...]).astype(BF16)

    return pl.pallas_call(
        body, name=name, grid=(T // tm,),
        in_specs=[_row_spec(tm, W), _row_spec(tm, W), _full_spec((1, W)), _full_spec((1, W))],
        out_specs=_row_spec(tm, 2 * W),
        out_shape=jax.ShapeDtypeStruct((T, 2 * W), BF16), compiler_params=_params(1),
    )(a, b, ga, gb)


def _outnorm_bwd(name, dmixed, a, b, ga, gb):
    T, W = a.shape
    tm = min(ROWS, T)

    def body(dm_ref, a_ref, b_ref, ga_ref, gb_ref, da_ref, db_ref, dga_ref, dgb_ref):
        first = pl.program_id(0) == 0
        for src, gain, lo, dsrc, dgain in ((a_ref, ga_ref, 0, da_ref, dga_ref), (b_ref, gb_ref, W, db_ref, dgb_ref)):
            v, dv = src[...], dm_ref[:, lo:lo + W]
            r = lax.rsqrt(jnp.mean(v * v, axis=-1, keepdims=True) + EPS)
            vh = v * r

            @pl.when(first)
            def _():
                dgain[...] = jnp.zeros_like(dgain)

            dgain[...] += jnp.sum(dv * vh, axis=0, keepdims=True)
            dvh = dv * gain[...]
            dsrc[...] = r * (dvh - vh * jnp.mean(dvh * vh, axis=-1, keepdims=True))

    return pl.pallas_call(
        body, name=name, grid=(T // tm,),
        in_specs=[_row_spec(tm, 2 * W), _row_spec(tm, W), _row_spec(tm, W), _full_spec((1, W)), _full_spec((1, W))],
        out_specs=[_row_spec(tm, W), _row_spec(tm, W), _full_spec((1, W)), _full_spec((1, W))],
        out_shape=[jax.ShapeDtypeStruct((T, W), F32), jax.ShapeDtypeStruct((T, W), F32),
                   jax.ShapeDtypeStruct((1, W), F32), jax.ShapeDtypeStruct((1, W), F32)],
        compiler_params=_params(1),
    )(dmixed, a, b, ga, gb)


def _loss_head(name, y, target):
    T, D = y.shape
    tm = min(ROWS, T)

    def body(y_ref, t_ref, loss_ref, dy_ref, dy16_ref):
        d = y_ref[...] - t_ref[...]

        @pl.when(pl.program_id(0) == 0)
        def _():
            loss_ref[...] = jnp.zeros_like(loss_ref)

        per_token = jnp.mean(d * d, axis=-1, keepdims=True)
        loss_ref[...] += 0.5 * jnp.sum(per_token, axis=0, keepdims=True)
        dy = d * (1.0 / D)
        dy_ref[...] = dy
        dy16_ref[...] = dy.astype(BF16)

    return pl.pallas_call(
        body, name=name, grid=(T // tm,),
        in_specs=[_row_spec(tm, D), _row_spec(tm, D)],
        out_specs=[_full_spec((1, 1)), _row_spec(tm, D), _row_spec(tm, D)],
        out_shape=[jax.ShapeDtypeStruct((1, 1), F32), jax.ShapeDtypeStruct((T, D), F32),
                   jax.ShapeDtypeStruct((T, D), BF16)],
        compiler_params=_params(1),
    )(y, target)


def _bf16_dot(a, b, dims):
    return lax.dot_general(a.astype(BF16), b.astype(BF16), dims, preferred_element_type=F32)


@jax.custom_vjp
def _dot_nn(a, b):
    return _bf16_dot(a, b, _DIMS["nn"])


def _dot_nn_fwd(a, b):
    return _dot_nn(a, b), (a, b)


def _dot_nn_bwd(saved, ct):
    a, b = saved
    return _bf16_dot(ct, b, _DIMS["nt"]), _bf16_dot(a, ct, _DIMS["tn"])


_dot_nn.defvjp(_dot_nn_fwd, _dot_nn_bwd)


@jax.custom_vjp
def _dot_nt(a, b):
    return _bf16_dot(a, b, _DIMS["nt"])


def _dot_nt_fwd(a, b):
    return _dot_nt(a, b), (a, b)


def _dot_nt_bwd(saved, ct):
    a, b = saved
    return _bf16_dot(ct, b, _DIMS["nn"]), _bf16_dot(ct, a, _DIMS["tn"])


_dot_nt.defvjp(_dot_nt_fwd, _dot_nt_bwd)


def _iota(shape, dim):
    return lax.broadcasted_iota(jnp.int32, shape, dim)


def _head_sum_impl(x):
    same_head = (_iota((LANES, LANES), 0) // HEAD_DIM == _iota((LANES, LANES), 1) // HEAD_DIM).astype(BF16)
    pieces = []
    for i in range(x.shape[1] // LANES):
        xs = x[:, i * LANES:(i + 1) * LANES]
        hi = xs.astype(BF16)
        r1 = xs - hi.astype(F32)
        mid = r1.astype(BF16)
        lo = (r1 - mid.astype(F32)).astype(BF16)
        tot = None
        for part in (hi, mid, lo):
            d = lax.dot_general(part, same_head, _DIMS["nn"], preferred_element_type=F32)
            tot = d if tot is None else tot + d
        pieces.append(tot)
    return pieces[0] if len(pieces) == 1 else jnp.concatenate(pieces, axis=1)


@jax.custom_vjp
def _head_sum(x):
    return _head_sum_impl(x)


_head_sum.defvjp(lambda x: (_head_sum_impl(x), None), lambda _, ct: (_head_sum_impl(ct),))


def _head_rms(x, g_row):
    ms = _head_sum(x * x) * (1.0 / HEAD_DIM)
    return (x * lax.rsqrt(ms + EPS)) * g_row


@jax.custom_vjp
def _swap_halves(x):
    return pltpu.roll(x, HEAD_DIM, 1)


_swap_halves.defvjp(lambda x: (pltpu.roll(x, HEAD_DIM, 1), None), lambda _, ct: (pltpu.roll(ct, HEAD_DIM, 1),))


def _gelu(x):
    return 0.5 * x * (1.0 + lax.erf(x * (1.0 / math.sqrt(2.0))))


def _sgu_block(u_raw, v_raw, g_row, w, b_full):
    u, v = _gelu(u_raw), _gelu(v_raw)
    vn = _head_rms(v, g_row)
    causal = _iota((CHUNK, CHUNK), 0) >= _iota((CHUNK, CHUNK), 1)
    low_half = _iota((CHUNK, LANES), 1) < HEAD_DIM
    gates = []
    for p in range(v.shape[1] // LANES):
        vp = vn[:, p * LANES:(p + 1) * LANES]
        g0 = _dot_nn(jnp.where(causal, w[2 * p], 0.0), vp)
        g1 = _dot_nn(jnp.where(causal, w[2 * p + 1], 0.0), vp)
        gates.append(jnp.where(low_half, g0, g1))
    gate = jnp.concatenate(gates, axis=1) + b_full
    return u * gate


def _attn_block(q_raw, k_prev, k_cur, v_prev, v_cur, qg_row, kg_row, sinks, bias, blk):
    scale = 1.0 / math.sqrt(HEAD_DIM)
    n_q_heads = q_raw.shape[1] // HEAD_DIM
    qn = _head_rms(q_raw, qg_row)
    kn = _head_rms(jnp.concatenate([k_prev, k_cur], axis=0), kg_row)
    vb = jnp.concatenate([v_prev, v_cur], axis=0)
    qi = _iota((GROUP * BLOCK, 2 * BLOCK), 0) & (BLOCK - 1)
    kj = _iota((GROUP * BLOCK, 2 * BLOCK), 1)
    dist = qi + BLOCK - kj
    valid = (dist >= 0) & (dist < BLOCK) & (kj + (blk * BLOCK - BLOCK) >= 0)
    low_half = _iota((BLOCK, LANES), 1) < HEAD_DIM
    heads = [None] * n_q_heads
    for kv in range(n_q_heads // GROUP):
        grp, kv_low = kv // 2, kv % 2 == 0
        k2 = kn[:, grp * LANES:(grp + 1) * LANES]
        v2 = vb[:, grp * LANES:(grp + 1) * LANES]
        keep = low_half if kv_low else ~low_half
        stacked = []
        for g in range(GROUP):
            h = GROUP * kv + g
            q2 = qn[:, (h // 2) * LANES:(h // 2 + 1) * LANES]
            src = q2 if (h % 2 == 0) == kv_low else _swap_halves(q2)
            stacked.append(jnp.where(keep, src, 0.0))
        q4 = jnp.concatenate(stacked, axis=0)
        bias4 = jnp.concatenate([bias[GROUP * kv + g] for g in range(GROUP)], axis=0)
        sink4 = jnp.concatenate([jnp.broadcast_to(sinks[GROUP * kv + g], (BLOCK, 1)) for g in range(GROUP)], axis=0)
        s = _dot_nt(q4, k2) * scale + bias4
        s = jnp.where(valid, s, NEG_INF)
        m = lax.stop_gradient(jnp.maximum(jnp.max(s, axis=1, keepdims=True), sink4))
        e = jnp.exp(s - m)
        denom = jnp.sum(e, axis=1, keepdims=True) + jnp.exp(sink4 - m)
        o4 = _dot_nn(e * (1.0 / denom), v2)
        for g in range(GROUP):
            h = GROUP * kv + g
            o = o4[g * BLOCK:(g + 1) * BLOCK]
            heads[h] = o if (h % 2 == 0) == kv_low else _swap_halves(o)
    outs = [jnp.where(low_half, heads[2 * p], heads[2 * p + 1]) for p in range(n_q_heads // 2)]
    return jnp.concatenate(outs, axis=1)


def _bias_table():
    dist = np.arange(BLOCK)[:, None] + BLOCK - np.arange(2 * BLOCK)[None, :]
    n = np.maximum(dist, 0)
    max_exact = NUM_BUCKETS // 2
    nf = np.maximum(n, 1).astype(np.float64)
    large = max_exact + (np.log(nf / max_exact) / math.log(MAX_DISTANCE / max_exact) * (NUM_BUCKETS - max_exact)).astype(np.int32)
    large = np.minimum(large, NUM_BUCKETS - 1)
    return np.where(n < max_exact, n, large).astype(np.int32)


def _bias_fwd(name, rel_bias, buckets):
    nb_, nh = rel_bias.shape

    def body(rb_ref, bk_ref, o_ref):
        bk = bk_ref[...]
        for h in range(nh):
            acc = jnp.zeros(bk.shape, F32)
            for b in range(nb_):
                acc = jnp.where(bk == b, rb_ref[b, h], acc)
            o_ref[h] = acc

    return pl.pallas_call(
        body, name=name,
        in_specs=[pl.BlockSpec(memory_space=pltpu.SMEM), pl.BlockSpec(memory_space=pltpu.VMEM)],
        out_specs=pl.BlockSpec(memory_space=pltpu.VMEM),
        out_shape=jax.ShapeDtypeStruct((nh,) + buckets.shape, F32),
    )(rel_bias, buckets)


def _bias_bwd(name, dbias, buckets, nb_):
    nh = dbias.shape[0]

    def body(db_ref, bk_ref, o_ref):
        bk = bk_ref[...]
        for h in range(nh):
            d = db_ref[h]
            for b in range(nb_):
                s = jnp.sum(jnp.where(bk == b, d, 0.0), axis=0, keepdims=True)
                s = jnp.sum(s, axis=1, keepdims=True)
                o_ref[b * nh + h:b * nh + h + 1, :] = jnp.broadcast_to(s, (1, LANES))

    return pl.pallas_call(
        body, name=name,
        in_specs=[pl.BlockSpec(memory_space=pltpu.VMEM), pl.BlockSpec(memory_space=pltpu.VMEM)],
        out_specs=pl.BlockSpec(memory_space=pltpu.VMEM),
        out_shape=jax.ShapeDtypeStruct((nb_ * nh, LANES), F32),
    )(dbias, buckets)


def _sgu_fwd(name, z, g_row, w, b_full, W):
    T = z.shape[0]

    def body(u_ref, v_ref, g_ref, w_ref, b_ref, o_ref):
        o_ref[...] = _sgu_block(u_ref[...], v_ref[...], g_ref[...], w_ref[...], b_ref[...])

    return pl.pallas_call(
        body, name=name, grid=(T // CHUNK,),
        in_specs=[_row_spec(CHUNK, W, 0), _row_spec(CHUNK, W, 1), _full_spec((1, W)),
                  _full_spec(w.shape), _full_spec((CHUNK, W))],
        out_specs=_row_spec(CHUNK, W),
        out_shape=jax.ShapeDtypeStruct((T, W), F32), compiler_params=_params(1),
    )(z, z, g_row, w, b_full)


def _sgu_bwd(name, z, g_row, w, b_full, d_out, W):
    T = z.shape[0]

    def body(u_ref, v_ref, g_ref, w_ref, b_ref, do_ref, dz_ref, dg_ref, dw_ref, db_ref):
        _, vjp = jax.vjp(_sgu_block, u_ref[...], v_ref[...], g_ref[...], w_ref[...], b_ref[...])
        du, dv, dg, dw, db = vjp(do_ref[...])
        dz_ref[:, :W] = du.astype(BF16)
        dz_ref[:, W:] = dv.astype(BF16)
        i = pl.program_id(0)

        @pl.when(i == 0)
        def _():
            dg_ref[...] = jnp.zeros_like(dg_ref)
            dw_ref[...] = jnp.zeros_like(dw_ref)
            db_ref[...] = jnp.zeros_like(db_ref)

        dg_ref[...] += dg
        dw_ref[...] += dw
        db_ref[...] += db

        @pl.when(i == pl.num_programs(0) - 1)
        def _():
            db_ref[...] = _head_sum_impl(db_ref[...])

    return pl.pallas_call(
        body, name=name, grid=(T // CHUNK,),
        in_specs=[_row_spec(CHUNK, W, 0), _row_spec(CHUNK, W, 1), _full_spec((1, W)),
                  _full_spec(w.shape), _full_spec((CHUNK, W)), _row_spec(CHUNK, W)],
        out_specs=[_row_spec(CHUNK, 2 * W), _full_spec((1, W)), _full_spec(w.shape), _full_spec((CHUNK, W))],
        out_shape=[jax.ShapeDtypeStruct((T, 2 * W), BF16), jax.ShapeDtypeStruct((1, W), F32),
                   jax.ShapeDtypeStruct(w.shape, F32), jax.ShapeDtypeStruct((CHUNK, W), F32)],
        compiler_params=_params(1),
    )(z, z, g_row, w, b_full, d_out)


def _attn_specs(WQ, WKV, q_col, k_col, v_col, blk_of):
    prev_of = lambda i: jnp.maximum(blk_of(i) - 1, 0)
    return [pl.BlockSpec((BLOCK, WQ), lambda i: (blk_of(i), q_col)),
            pl.BlockSpec((BLOCK, WKV), lambda i: (prev_of(i), k_col)),
            pl.BlockSpec((BLOCK, WKV), lambda i: (blk_of(i), k_col)),
            pl.BlockSpec((BLOCK, WKV), lambda i: (prev_of(i), v_col)),
            pl.BlockSpec((BLOCK, WKV), lambda i: (blk_of(i), v_col))]


def _attn_fwd(name, z, qg_row, kg_row, sinks_col, bias, WQ, WKV, q_col, k_col, v_col):
    T = z.shape[0]
    nh = sinks_col.shape[0]

    def body(q_ref, kp_ref, kc_ref, vp_ref, vc_ref, qg_ref, kg_ref, s_ref, b_ref, o_ref):
        sinks = [s_ref[h:h + 1, :] for h in range(nh)]
        o_ref[...] = _attn_block(q_ref[...], kp_ref[...], kc_ref[...], vp_ref[...], vc_ref[...],
                                 qg_ref[...], kg_ref[...], sinks, b_ref[...], pl.program_id(0))

    return pl.pallas_call(
        body, name=name, grid=(T // BLOCK,),
        in_specs=_attn_specs(WQ, WKV, q_col, k_col, v_col, lambda i: i)
        + [_full_spec((1, WQ)), _full_spec((1, WKV)), _full_spec((nh, 1)), _full_spec(bias.shape)],
        out_specs=_row_spec(BLOCK, WQ),
        out_shape=jax.ShapeDtypeStruct((T, WQ), F32), compiler_params=_params(1),
    )(z, z, z, z, z, qg_row, kg_row, sinks_col, bias)


def _attn_bwd(name, z, qg_row, kg_row, sinks_col, bias, d_out, WQ, WKV, q_col, k_col, v_col):
    T = z.shape[0]
    nblk = T // BLOCK
    nh = sinks_col.shape[0]
    blk_of = lambda i: nblk - 1 - i

    def body(q_ref, kp_ref, kc_ref, vp_ref, vc_ref, qg_ref, kg_ref, s_ref, b_ref, do_ref,
             dq_ref, dk_ref, dv_ref, dqg_ref, dkg_ref, ds_ref, db_ref, carry_k, carry_v):
        i = pl.program_id(0)
        blk = nblk - 1 - i
        sinks = [s_ref[h:h + 1, :] for h in range(nh)]
        fn = lambda q, kp, kc, vp, vc, qg, kg, sk, bs: _attn_block(q, kp, kc, vp, vc, qg, kg, sk, bs, blk)
        _, vjp = jax.vjp(fn, q_ref[...], kp_ref[...], kc_ref[...], vp_ref[...], vc_ref[...],
                         qg_ref[...], kg_ref[...], sinks, b_ref[...])
        dq, dkp, dkc, dvp, dvc, dqg, dkg, dsk, dbs = vjp(do_ref[...])

        @pl.when(i == 0)
        def _():
            carry_k[...] = jnp.zeros_like(carry_k)
            carry_v[...] = jnp.zeros_like(carry_v)
            dqg_ref[...] = jnp.zeros_like(dqg_ref)
            dkg_ref[...] = jnp.zeros_like(dkg_ref)
            ds_ref[...] = jnp.zeros_like(ds_ref)
            db_ref[...] = jnp.zeros_like(db_ref)

        dq_ref[...] = dq.astype(BF16)
        dk_ref[...] = (dkc + carry_k[...]).astype(BF16)
        dv_ref[...] = (dvc + carry_v[...]).astype(BF16)
        carry_k[...] = dkp
        carry_v[...] = dvp
        dqg_ref[...] += dqg
        dkg_ref[...] += dkg
        for h in range(nh):
            ds_ref[h:h + 1, :] += dsk[h]
        db_ref[...] += dbs

    return pl.pallas_call(
        body, name=name, grid=(nblk,),
        in_specs=_attn_specs(WQ, WKV, q_col, k_col, v_col, blk_of)
        + [_full_spec((1, WQ)), _full_spec((1, WKV)), _full_spec((nh, 1)), _full_spec(bias.shape),
           pl.BlockSpec((BLOCK, WQ), lambda i: (blk_of(i), 0))],
        out_specs=[pl.BlockSpec((BLOCK, WQ), lambda i: (blk_of(i), 0)),
                   pl.BlockSpec((BLOCK, WKV), lambda i: (blk_of(i), 0)),
                   pl.BlockSpec((BLOCK, WKV), lambda i: (blk_of(i), 0)),
                   _full_spec((1, WQ)), _full_spec((1, WKV)), _full_spec((nh, 1)), _full_spec(bias.shape)],
        out_shape=[jax.ShapeDtypeStruct((T, WQ), BF16), jax.ShapeDtypeStruct((T, WKV), BF16),
                   jax.ShapeDtypeStruct((T, WKV), BF16), jax.ShapeDtypeStruct((1, WQ), F32),
                   jax.ShapeDtypeStruct((1, WKV), F32), jax.ShapeDtypeStruct((nh, 1), F32),
                   jax.ShapeDtypeStruct(bias.shape, F32)],
        scratch_shapes=[pltpu.VMEM((BLOCK, WKV), F32), pltpu.VMEM((BLOCK, WKV), F32)],
        compiler_params=_params(1),
    )(z, z, z, z, z, qg_row, kg_row, sinks_col, bias, d_out)


def _adamw(name, w, g, m, v):
    shape = w.shape
    C = shape[-1]
    R = int(np.prod(shape[:-1]))
    w2, g2, m2, v2 = (t.reshape(R, C) for t in (w, g, m, v))
    tr = ROWS if R % ROWS == 0 else R

    def body(w_ref, g_ref, m_ref, v_ref, d_ref, nm_ref, nv_ref):
        gv = g_ref[...]
        nm = ADAM_B1 * m_ref[...] + (1.0 - ADAM_B1) * gv
        nv = ADAM_B2 * v_ref[...] + (1.0 - ADAM_B2) * (gv * gv)
        m_hat = nm / (1.0 - ADAM_B1 ** ADAM_STEP)
        v_hat = nv / (1.0 - ADAM_B2 ** ADAM_STEP)
        d_ref[...] = -ADAM_LR * (m_hat / (jnp.sqrt(v_hat) + ADAM_EPS) + ADAM_WD * w_ref[...])
        nm_ref[...] = nm
        nv_ref[...] = nv

    spec = pl.BlockSpec((tr, C), lambda i: (i, 0))
    outs = pl.pallas_call(
        body, name=name, grid=(R // tr,), in_specs=[spec] * 4, out_specs=[spec] * 3,
        out_shape=[jax.ShapeDtypeStruct((R, C), F32)] * 3, compiler_params=_params(1),
    )(w2, g2, m2, v2)
    return tuple(o.reshape(shape) for o in outs)


def _pad_rows(flat):
    n = flat.shape[0]
    tile = 8 * LANES
    padded = -(-n // tile) * tile
    return jnp.pad(flat, (0, padded - n)).reshape(padded // LANES, LANES)


def kernel(x, rel_bias, norm1_g, w_in, sgu_norm_g, sgu_w, sgu_b, q_norm_g, k_norm_g, sinks, out_norm_a, out_norm_b, w_out, norm2_g, w_gate, w_up, w_down, loss_target, m_rel_bias, m_norm1_g, m_w_in, m_sgu_norm_g, m_sgu_w, m_sgu_b, m_q_norm_g, m_k_norm_g, m_sinks, m_out_norm_a, m_out_norm_b, m_w_out, m_norm2_g, m_w_gate, m_w_up, m_w_down, v_rel_bias, v_norm1_g, v_w_in, v_sgu_norm_g, v_sgu_w, v_sgu_b, v_q_norm_g, v_k_norm_g, v_sinks, v_out_norm_a, v_out_norm_b, v_w_out, v_norm2_g, v_w_gate, v_w_up, v_w_down):
    L = w_in.shape[0]
    T, D = x.shape[1], x.shape[2]
    W = D // 2
    NH = W // HEAD_DIM
    WKV = N_KV_HEADS * HEAD_DIM
    IN = N_DEV * w_in.shape[2]
    FF = N_DEV * w_gate.shape[2]
    assert IN == 2 * W + W + 2 * WKV and NH // N_KV_HEADS == GROUP
    q_col, k_col, v_col = 2 * W // W, (3 * W) // WKV, (3 * W + WKV) // WKV
    x0 = x.reshape(T, D)
    target = loss_target.reshape(T, D)

    shards = [jnp.swapaxes(w_in, 1, 2).astype(BF16), w_out.astype(BF16), jnp.swapaxes(w_gate, 1, 2).astype(BF16),
              jnp.swapaxes(w_up, 1, 2).astype(BF16), w_down.astype(BF16)]
    me = 4 * lax.axis_index("x") + 2 * lax.axis_index("y") + lax.axis_index("c")

    def landing(block):
        return lax.dynamic_update_slice(lax.empty((N_DEV,) + block.shape, block.dtype), block[None], (me, 0, 0))

    def as_matrices(got):
        return [g.reshape(N_DEV * g.shape[1], g.shape[2]) for g in got]

    full = {0: as_matrices(_all_gather("gather_weights_0", [s[0] for s in shards]))}
    flight = {}

    def start_gather(l, after):
        srcs = [s[l] for s in shards]
        flight[l] = _exchange_start(f"gather_start_{l}", "gather", srcs, [landing(s) for s in srcs], after)
        return flight[l][4]

    def finish_gather(l, after):
        s_sem, r_sem, srcs, lands, _ = flight.pop(l)
        full[l] = as_matrices(_exchange_wait(f"gather_wait_{l}", "gather", s_sem, r_sem, srcs, lands, after)[1])

    buckets = jnp.asarray(_bias_table())
    bias = _bias_fwd("bias_table", rel_bias, buckets)

    saved = []
    xl = x0
    for l in range(L):
        if l > 0:
            finish_gather(l, xl)
        token = start_gather(l + 1, full[l][0] if l == 0 else xl) if l + 1 < L else xl
        w_in_t, w_o, w_g_t, w_u_t, w_d = full[l]
        g1, g2 = norm1_g[l][None], norm2_g[l][None]
        sg_row = sgu_norm_g[l].reshape(1, W)
        b_full = jnp.repeat(sgu_b[l].T, HEAD_DIM, axis=1)
        qg_row = jnp.tile(q_norm_g[l], NH)[None]
        kg_row = jnp.tile(k_norm_g[l], N_KV_HEADS)[None]
        sinks_col = sinks[l][:, None]
        ga, gb = out_norm_a[l][None], out_norm_b[l][None]

        if l == 0:
            h = _rms_fwd("norm1_0", xl, g1)
        (z,) = _mm(f"proj_in_{l}", "nt", [h], [w_in_t], [(0, 0, 0)], 1, T, IN, D, 512, 896, D, _ep_plain, [F32],
                   after=token)
        out_a = _sgu_fwd(f"sgu_{l}", z, sg_row, sgu_w[l], b_full, W)
        out_b = _attn_fwd(f"attn_{l}", z, qg_row, kg_row, sinks_col, bias, W, WKV, q_col, k_col, v_col)
        mixed = _outnorm_fwd(f"outnorm_{l}", out_a, out_b, ga, gb)
        (x1, h2), _ = _mm_rows(f"proj_out_{l}", [mixed], [w_o], T, D, D, 512, 1024, _ep_rows_residual_norm,
                               [F32, BF16], 0, extras=[xl], rows=[g2])
        gate, up, act = _mm(f"mlp_in_{l}", "nt", [h2], [w_g_t, w_u_t], [(0, 0, 0), (0, 1, 1)], 2, T, FF, D,
                            1024, 512, D, _ep_swiglu, [BF16, BF16, BF16])
        saved.append((xl, h, z, out_a, out_b, mixed, x1, h2, gate, up, act,
                      g1, g2, sg_row, b_full, qg_row, kg_row, sinks_col, ga, gb))
        if l + 1 < L:
            (xl, h), _ = _mm_rows(f"mlp_out_{l}", [act], [w_d], T, D, FF, 512, 1408, _ep_rows_residual_norm,
                                  [F32, BF16], 0, extras=[x1], rows=[norm1_g[l + 1][None]])
        else:
            (xl,), _ = _mm_rows(f"mlp_out_{l}", [act], [w_d], T, D, FF, 512, 1408, _ep_rows_residual, [F32], 0,
                                extras=[x1])

    loss_part, dy, dy16 = _loss_head("loss_head", xl, target)

    dbias = None
    small = [None] * L
    scatters = []

    def start_scatter(name, which, grads_t, after):
        srcs = [t.reshape(N_DEV, t.shape[0] // N_DEV, D) for t in grads_t]
        lands = [lax.empty((N_DEV - 1,) + s.shape[1:], BF16) for s in srcs]
        s_sem, r_sem, srcs, lands, tok = _exchange_start(name, "scatter", srcs, lands, after)
        scatters.append((name, which, s_sem, r_sem, srcs, lands))
        return tok

    token = None
    for l in reversed(range(L)):
        w_in_t, w_o, w_g_t, w_u_t, w_d = full[l]
        (xl, h, z, out_a, out_b, mixed, x1, h2, gate, up, act,
         g1, g2, sg_row, b_full, qg_row, kg_row, sinks_col, ga, gb) = saved[l]

        dgate, dup = _mm(f"d_mlp_out_{l}", "nt", [dy16], [w_d], [(0, 0, 0)], 1, T, FF, D, 1024, 1408, D,
                         _ep_swiglu_bwd, [BF16, BF16], extras=[gate, up], after=token)
        (dw_d,) = _mm(f"dw_down_{l}", "tn", [act], [dy16], [(0, 0, 0)], 1, FF, D, T, 1408, 1024, 512, _ep_plain, [BF16])
        (dx1, dx1_16), (dg2,) = _mm_rows(f"d_mlp_in_{l}", [dgate, dup], [w_g_t, w_u_t], T, D, FF, 512, 512,
                                         _ep_rows_rms_bwd, [F32, BF16], 1, extras=[x1, dy], rows=[g2])
        dw_g, dw_u = _mm(f"dw_gate_up_{l}", "tn", [dgate, dup], [h2], [(0, 0, 0), (1, 0, 1)], 2, FF, D, T,
                         1408, 1024, 512, _ep_two, [BF16, BF16])
        token = start_scatter(f"scatter_mlp_start_{l}", (l, (4, 2, 3)), [dw_d, dw_g, dw_u], dw_u)
        (dmixed,) = _mm(f"d_proj_out_{l}", "nt", [dx1_16], [w_o], [(0, 0, 0)], 1, T, D, D, 512, 1024, D, _ep_plain, [F32],
                        after=token)
        (dw_o,) = _mm(f"dw_out_{l}", "tn", [mixed], [dx1_16], [(0, 0, 0)], 1, D, D, T, 1024, 1024, 512, _ep_plain, [BF16])
        d_a, d_b, dga, dgb = _outnorm_bwd(f"d_outnorm_{l}", dmixed, out_a, out_b, ga, gb)
        dz_uv, dsg, dsw, dsb = _sgu_bwd(f"d_sgu_{l}", z, sg_row, sgu_w[l], b_full, d_a, W)
        dq, dk, dv, dqg, dkg, dsk, dbs = _attn_bwd(f"d_attn_{l}", z, qg_row, kg_row, sinks_col, bias, d_b,
                                                   W, WKV, q_col, k_col, v_col)
        dbias = dbs if dbias is None else dbias + dbs
        dz = jnp.concatenate([dz_uv, dq, dk, dv], axis=1)
        (dy, dy16), (dg1,) = _mm_rows(f"d_proj_in_{l}", [dz], [w_in_t], T, D, IN, 512, 896, _ep_rows_rms_bwd,
                                      [F32, BF16], 1, extras=[xl, dx1], rows=[g1])
        (dw_i,) = _mm(f"dw_in_{l}", "tn", [dz], [h], [(0, 0, 0)], 1, IN, D, T, 896, 1024, 512, _ep_plain, [BF16])
        token = start_scatter(f"scatter_mix_start_{l}", (l, (1, 0)), [dw_o, dw_i], dw_i)

        small[l] = dict(norm1_g=dg1[0], sgu_norm_g=dsg.reshape(NH, HEAD_DIM), sgu_w=dsw,
                        sgu_b=dsb[:, ::HEAD_DIM].T, q_norm_g=dqg.reshape(NH, HEAD_DIM).sum(0),
                        k_norm_g=dkg.reshape(N_KV_HEADS, HEAD_DIM).sum(0), sinks=dsk[:, 0],
                        out_norm_a=dga[0], out_norm_b=dgb[0], norm2_g=dg2[0])

    grad_x = dy.reshape(x.shape)
    d_rel = _bias_bwd("d_bias_table", dbias, buckets, NUM_BUCKETS)[:, 0].reshape(NUM_BUCKETS, NH)

    names = ["norm1_g", "sgu_norm_g", "sgu_w", "sgu_b", "q_norm_g", "k_norm_g", "sinks", "out_norm_a", "out_norm_b", "norm2_g"]
    parts = {"rel_bias": d_rel}
    for nme in names:
        parts[nme] = jnp.stack([small[l][nme] for l in range(L)])
    order = ["rel_bias"] + names
    packed = jnp.concatenate([_pad_rows(parts[nme].reshape(-1)) for nme in order], axis=0)
    small_flight = _exchange_start("gather_small_start", "gather", [packed], [landing(packed)], token)
    after = small_flight[4]

    grads_big = {}
    for name, (l, which), s_sem, r_sem, srcs, lands in scatters:
        srcs, lands = _exchange_wait(name.replace("start", "wait"), "scatter", s_sem, r_sem, srcs, lands, after)
        for i, src, land in zip(which, srcs, lands):
            own = lax.dynamic_index_in_dim(src, me, 0, keepdims=False)
            rows = own.shape[0]
            after = _sum_parts(f"sum_grads_{l}_{i}", own, land, 64 if rows % 64 == 0 else rows)
            grads_big[i, l] = after
    g_w_in = jnp.swapaxes(jnp.stack([grads_big[0, l] for l in range(L)]), 1, 2)
    g_w_out = jnp.stack([grads_big[1, l] for l in range(L)])
    g_w_gate = jnp.swapaxes(jnp.stack([grads_big[2, l] for l in range(L)]), 1, 2)
    g_w_up = jnp.swapaxes(jnp.stack([grads_big[3, l] for l in range(L)]), 1, 2)
    g_w_down = jnp.stack([grads_big[4, l] for l in range(L)])

    s_sem, r_sem, srcs, lands, _ = small_flight
    _, (everyone,) = _exchange_wait("gather_small_wait", "gather", s_sem, r_sem, srcs, lands, after)
    rows = packed.shape[0]
    summed = _sum_slots("sum_small_grads", everyone, 64 if rows % 64 == 0 else 8)
    g_small, at = {}, 0
    for nme in order:
        n = int(np.prod(parts[nme].shape))
        n_rows = -(-n // (8 * LANES)) * 8
        g_small[nme] = summed[at:at + n_rows].reshape(-1)[:n].reshape(parts[nme].shape)
        at += n_rows

    loss = lax.psum(loss_part[0, 0], ("x", "y", "c"))

    grads = dict(g_small, w_in=g_w_in, w_out=g_w_out, w_gate=g_w_gate, w_up=g_w_up, w_down=g_w_down)
    weights = dict(rel_bias=rel_bias, norm1_g=norm1_g, w_in=w_in, sgu_norm_g=sgu_norm_g, sgu_w=sgu_w, sgu_b=sgu_b,
                   q_norm_g=q_norm_g, k_norm_g=k_norm_g, sinks=sinks, out_norm_a=out_norm_a, out_norm_b=out_norm_b,
                   w_out=w_out, norm2_g=norm2_g, w_gate=w_gate, w_up=w_up, w_down=w_down)
    ms = dict(rel_bias=m_rel_bias, norm1_g=m_norm1_g, w_in=m_w_in, sgu_norm_g=m_sgu_norm_g, sgu_w=m_sgu_w, sgu_b=m_sgu_b,
              q_norm_g=m_q_norm_g, k_norm_g=m_k_norm_g, sinks=m_sinks, out_norm_a=m_out_norm_a, out_norm_b=m_out_norm_b,
              w_out=m_w_out, norm2_g=m_norm2_g, w_gate=m_w_gate, w_up=m_w_up, w_down=m_w_down)
    vs = dict(rel_bias=v_rel_bias, norm1_g=v_norm1_g, w_in=v_w_in, sgu_norm_g=v_sgu_norm_g, sgu_w=v_sgu_w, sgu_b=v_sgu_b,
              q_norm_g=v_q_norm_g, k_norm_g=v_k_norm_g, sinks=v_sinks, out_norm_a=v_out_norm_a, out_norm_b=v_out_norm_b,
              w_out=v_w_out, norm2_g=v_norm2_g, w_gate=v_w_gate, w_up=v_w_up, w_down=v_w_down)
    all_names = ["rel_bias", "norm1_g", "w_in", "sgu_norm_g", "sgu_w", "sgu_b", "q_norm_g", "k_norm_g", "sinks",
                 "out_norm_a", "out_norm_b", "w_out", "norm2_g", "w_gate", "w_up", "w_down"]
    deltas, new_m, new_v = [], [], []
    for nme in all_names:
        d, nm, nv = _adamw(f"adamw_{nme}", weights[nme], grads[nme], ms[nme], vs[nme])
        deltas.append(d)
        new_m.append(nm)
        new_v.append(nv)
    return (loss, grad_x, *[grads[nme] for nme in all_names], *deltas, *new_m, *new_v)
```

```python
import functools
import math

import numpy as np
import jax
import jax.numpy as jnp
from jax import lax
from jax.experimental import pallas as pl
from jax.experimental.pallas import tpu as pltpu

F32 = jnp.float32
BF16 = jnp.bfloat16

N_DEV = 8
HEAD_DIM = 64
CHUNK = 128
BLOCK = 128
N_KV_HEADS = 4
GROUP = 4
NUM_BUCKETS = 32
MAX_DISTANCE = 128
EPS = 1e-6
NEG_INF = -1e30
LANES = 128
VMEM_LIMIT = 56 * 2 ** 20

ADAM_LR = 0.001
ADAM_B1 = 0.9
ADAM_B2 = 0.999
ADAM_EPS = 1e-08
ADAM_WD = 0.01
ADAM_STEP = 10

MESH = pl.DeviceIdType.MESH
ANY = pl.BlockSpec(memory_space=pl.ANY)
HBM = pl.BlockSpec(memory_space=pltpu.HBM)
SEM = pl.BlockSpec(memory_space=pltpu.SEMAPHORE)
EFFECT = pltpu.SideEffectType.DATAFLOW_SIDE_EFFECTING


def _params(n_axes):
    return pltpu.CompilerParams(dimension_semantics=("arbitrary",) * n_axes, vmem_limit_bytes=VMEM_LIMIT)


def _my_place():
    return lax.axis_index("x"), lax.axis_index("y"), lax.axis_index("c")


def _all_gather(name, arrs):
    n = len(arrs)

    def body(*refs):
        ins, outs = refs[:n], refs[n:2 * n]
        send_sems, recv_sems, local_sems = refs[2 * n:]
        x, y, c = _my_place()
        sibling = (x, y, 1 - c)
        chips = [(1 - x, y), (x, 1 - y), (1 - x, 1 - y)]

        def slot(a, px, py, pc):
            return outs[a].at[4 * px + 2 * py + pc]

        def copy(a, k, block, to, src=None):
            return pltpu.make_async_remote_copy(
                src_ref=slot(a, *block) if src is None else src, dst_ref=slot(a, *block),
                send_sem=send_sems.at[7 * a + k], recv_sem=recv_sems.at[7 * a + k],
                device_id=to, device_id_type=MESH)

        started = []
        for a in range(n):
            mine = pltpu.make_async_copy(ins[a], slot(a, x, y, c), local_sems.at[a])
            mine.start()
            started.append(mine)
        sends = []
        for a in range(n):
            first = [copy(a, 0, (x, y, c), sibling, src=ins[a])]
            first += [copy(a, 1 + j, (x, y, c), (*chip, c), src=ins[a]) for j, chip in enumerate(chips)]
            for cp in first:
                cp.start()
            sends += first
        for a in range(n):
            for j, chip in enumerate(chips):
                copy(a, 1 + j, (*chip, c), (x, y, c)).wait_recv()
                fwd = copy(a, 4 + j, (*chip, c), sibling)
                fwd.start()
                sends.append(fwd)
        for a in range(n):
            copy(a, 0, sibling, (x, y, c)).wait_recv()
            for j, chip in enumerate(chips):
                copy(a, 4 + j, (*chip, 1 - c), (x, y, c)).wait_recv()
        for cp in sends:
            cp.wait_send()
        for cp in started:
            cp.wait()

    return pl.pallas_call(
        body, name=name,
        out_shape=[jax.ShapeDtypeStruct((N_DEV,) + a.shape, a.dtype) for a in arrs],
        in_specs=[ANY] * n, out_specs=[ANY] * n,
        scratch_shapes=[pltpu.SemaphoreType.DMA((7 * n,)), pltpu.SemaphoreType.DMA((7 * n,)),
                        pltpu.SemaphoreType.DMA((n,))],
    )(*arrs)


def _peer(k, x, y, c):
    return (1 - x if k & 4 else x), (1 - y if k & 2 else y), (1 - c if k & 1 else c)


PEER_ORDER = (1, 2, 4, 3, 5, 6, 7)


def _exchange_copy(kind, k, a, srcs, lands, send_sems, recv_sems, arriving=False):
    x, y, c = _my_place()
    me = 4 * x + 2 * y + c
    px, py, pc = _peer(k, x, y, c)
    them = 4 * px + 2 * py + pc
    if kind == "gather":
        src, dst_there, dst_here = srcs[a], lands[a].at[me], lands[a].at[them]
    else:
        src, dst_there, dst_here = srcs[a].at[them], lands[a].at[k - 1], lands[a].at[k - 1]
    return pltpu.make_async_remote_copy(
        src_ref=src, dst_ref=dst_here if arriving else dst_there,
        send_sem=send_sems.at[7 * a + k - 1], recv_sem=recv_sems.at[7 * a + k - 1],
        device_id=(px, py, pc), device_id_type=MESH)


def _exchange_start(name, kind, srcs, lands, after):
    n = len(srcs)

    def body(*refs):
        ins, lnd = refs[:n], refs[n:2 * n]
        send_sems, recv_sems = refs[2 * n + 1], refs[2 * n + 2]
        token = refs[-1]
        for k in PEER_ORDER:
            for a in range(n):
                _exchange_copy(kind, k, a, ins, lnd, send_sems, recv_sems).start()
        token[...] = jnp.zeros_like(token)

    hbm = lambda t: pltpu.with_memory_space_constraint(t, pltpu.HBM)
    out = pl.pallas_call(
        body, name=name,
        out_shape=(pltpu.SemaphoreType.DMA((7 * n,)), pltpu.SemaphoreType.DMA((7 * n,)),
                   *[pltpu.HBM(t.shape, t.dtype) for t in srcs], *[pltpu.HBM(t.shape, t.dtype) for t in lands],
                   jax.ShapeDtypeStruct((8, LANES), F32)),
        in_specs=[HBM] * (2 * n) + [ANY],
        out_specs=(SEM, SEM, *[HBM] * (2 * n), pl.BlockSpec(memory_space=pltpu.VMEM)),
        input_output_aliases={i: 2 + i for i in range(2 * n)},
        compiler_params=pltpu.CompilerParams(has_side_effects=EFFECT),
    )(*[hbm(t) for t in srcs], *[hbm(t) for t in lands], after)
    return out[0], out[1], list(out[2:2 + n]), list(out[2 + n:2 + 2 * n]), out[-1]


def _exchange_wait(name, kind, send_sems, recv_sems, srcs, lands, after):
    n = len(srcs)

    def body(*refs):
        ins, lnd = refs[:n], refs[n:2 * n]
        s_sems, r_sems = refs[2 * n], refs[2 * n + 1]
        for a in range(n):
            for k in PEER_ORDER:
                _exchange_copy(kind, k, a, ins, lnd, s_sems, r_sems).wait_send()
                _exchange_copy(kind, k, a, ins, lnd, s_sems, r_sems, arriving=True).wait_recv()

    out = pl.pallas_call(
        body, name=name,
        out_shape=(*[pltpu.HBM(t.shape, t.dtype) for t in srcs], *[pltpu.HBM(t.shape, t.dtype) for t in lands]),
        in_specs=[HBM] * (2 * n) + [SEM, SEM, ANY],
        out_specs=tuple([HBM] * (2 * n)),
        input_output_aliases={i: i for i in range(2 * n)},
        compiler_params=pltpu.CompilerParams(has_side_effects=EFFECT),
    )(*srcs, *lands, send_sems, recv_sems, after)
    return list(out[:n]), list(out[n:])


def _sum_parts(name, own, parts, rows):
    R, C = own.shape

    def body(own_ref, p_ref, o_ref):
        acc = own_ref[...].astype(F32)
        for s in range(N_DEV - 1):
            acc = acc + p_ref[s].astype(F32)
        o_ref[...] = acc

    return pl.pallas_call(
        body, name=name, grid=(R // rows,),
        in_specs=[pl.BlockSpec((rows, C), lambda i: (i, 0)), pl.BlockSpec((N_DEV - 1, rows, C), lambda i: (0, i, 0))],
        out_specs=pl.BlockSpec((rows, C), lambda i: (i, 0)),
        out_shape=jax.ShapeDtypeStruct((R, C), F32), compiler_params=_params(1),
    )(own, parts)


def _sum_slots(name, a, rows):
    _, R, C = a.shape

    def body(a_ref, o_ref):
        acc = a_ref[0].astype(F32)
        for s in range(1, N_DEV):
            acc = acc + a_ref[s].astype(F32)
        o_ref[...] = acc

    return pl.pallas_call(
        body, name=name, grid=(R // rows,),
        in_specs=[pl.BlockSpec((N_DEV, rows, C), lambda i: (0, i, 0))],
        out_specs=pl.BlockSpec((rows, C), lambda i: (i, 0)),
        out_shape=jax.ShapeDtypeStruct((R, C), F32), compiler_params=_params(1),
    )(a)


_DIMS = {"nn": (((1,), (0,)), ((), ())), "nt": (((1,), (1,)), ((), ())), "tn": (((0,), (0,)), ((), ()))}


def _mm(name, mode, a_list, b_list, pairs, n_acc, M, N, K, tm, tn, tk, epilogue, out_dtypes, extras=(), after=None):
    tm, tn, tk = min(tm, M), min(tn, N), min(tk, K)
    assert M % tm == 0 and N % tn == 0 and K % tk == 0, (name, M, N, K, tm, tn, tk)
    nk = K // tk
    na, nb, ne, no = len(a_list), len(b_list), len(extras), len(out_dtypes)
    dims = _DIMS[mode]
    a_spec = (pl.BlockSpec((tk, tm), lambda j, i, k: (k, i)) if mode == "tn"
              else pl.BlockSpec((tm, tk), lambda j, i, k: (i, k)))
    b_spec = (pl.BlockSpec((tn, tk), lambda j, i, k: (j, k)) if mode == "nt"
              else pl.BlockSpec((tk, tn), lambda j, i, k: (k, j)))
    o_spec = pl.BlockSpec((tm, tn), lambda j, i, k: (i, j))
    tail = [] if after is None else [after]

    def body(*refs):
        a_refs, b_refs = refs[:na], refs[na:na + nb]
        e_refs = refs[na + nb:na + nb + ne]
        first_out = na + nb + ne + len(tail)
        o_refs = refs[first_out:first_out + no]
        acc_refs = refs[first_out + no:]

        def partial(oi):
            tot = None
            for ai, bi, ti in pairs:
                if ti == oi:
                    d = lax.dot_general(a_refs[ai][...], b_refs[bi][...], dims, preferred_element_type=F32)
                    tot = d if tot is None else tot + d
            return tot

        def finish(accs):
            outs = epilogue(accs, [e[...] for e in e_refs])
            for o_ref, o in zip(o_refs, outs):
                o_ref[...] = o.astype(o_ref.dtype)

        if nk == 1:
            finish([partial(oi) for oi in range(n_acc)])
        else:
            k = pl.program_id(2)

            @pl.when(k == 0)
            def _():
                for r in acc_refs:
                    r[...] = jnp.zeros_like(r)

            for oi in range(n_acc):
                acc_refs[oi][...] += partial(oi)

            @pl.when(k == nk - 1)
            def _():
                finish([r[...] for r in acc_refs])

    return pl.pallas_call(
        body, name=name, grid=(N // tn, M // tm, nk),
        in_specs=[a_spec] * na + [b_spec] * nb + [o_spec] * ne + [ANY] * len(tail),
        out_specs=[o_spec] * no,
        out_shape=[jax.ShapeDtypeStruct((M, N), dt) for dt in out_dtypes],
        scratch_shapes=[pltpu.VMEM((tm, tn), F32)] * (n_acc if nk > 1 else 0),
        compiler_params=_params(3),
    )(*a_list, *b_list, *extras, *tail)


EPILOGUE_ROWS = 128


def _mm_rows(name, a_list, b_list, M, N, K, tm, tk, epilogue, out_dtypes, n_row_out, extras=(), rows=(), after=None):
    tm, tk = min(tm, M), min(tk, K)
    assert M % tm == 0 and K % tk == 0, (name, M, K, tm, tk)
    nk = K // tk
    rc = min(EPILOGUE_ROWS, tm)
    na, ne, nr, no = len(a_list), len(extras), len(rows), len(out_dtypes)
    tail = [] if after is None else [after]
    a_spec = pl.BlockSpec((tm, tk), lambda i, k: (i, k))
    b_spec = pl.BlockSpec((tk, N), lambda i, k: (k, 0))
    o_spec = pl.BlockSpec((tm, N), lambda i, k: (i, 0))
    r_spec = pl.BlockSpec((1, N), lambda i, k: (0, 0))

    def body(*refs):
        a_refs, b_refs = refs[:na], refs[na:2 * na]
        e_refs = refs[2 * na:2 * na + ne]
        r_refs = refs[2 * na + ne:2 * na + ne + nr]
        first_out = 2 * na + ne + nr + len(tail)
        o_refs = refs[first_out:first_out + no]
        ro_refs = refs[first_out + no:first_out + no + n_row_out]
        acc_ref = refs[first_out + no + n_row_out] if nk > 1 else None
        i, k = pl.program_id(0), pl.program_id(1)
        part = None
        for a_ref, b_ref in zip(a_refs, b_refs):
            d = lax.dot_general(a_ref[...], b_ref[...], _DIMS["nn"], preferred_element_type=F32)
            part = d if part is None else part + d

        def finish(acc):
            sums = None
            for c in range(tm // rc):
                sl = slice(c * rc, (c + 1) * rc)
                outs, row_sums = epilogue(acc[sl, :], [e[sl, :] for e in e_refs], [r[...] for r in r_refs])
                for o_ref, o in zip(o_refs, outs):
                    o_ref[sl, :] = o.astype(o_ref.dtype)
                sums = row_sums if sums is None else tuple(s + t for s, t in zip(sums, row_sums))
            for ro_ref, val in zip(ro_refs, sums):
                @pl.when(i == 0)
                def _():
                    ro_ref[...] = val

                @pl.when(i > 0)
                def _():
                    ro_ref[...] += val

        if nk == 1:
            finish(part)
        else:
            @pl.when(k == 0)
            def _():
                acc_ref[...] = part

            @pl.when(k > 0)
            def _():
                acc_ref[...] += part

            @pl.when(k == nk - 1)
            def _():
                finish(acc_ref)

    outs = pl.pallas_call(
        body, name=name, grid=(M // tm, nk),
        in_specs=[a_spec] * na + [b_spec] * na + [o_spec] * ne + [r_spec] * nr + [ANY] * len(tail),
        out_specs=[o_spec] * no + [r_spec] * n_row_out,
        out_shape=[jax.ShapeDtypeStruct((M, N), dt) for dt in out_dtypes] + [jax.ShapeDtypeStruct((1, N), F32)] * n_row_out,
        scratch_shapes=[pltpu.VMEM((tm, N), F32)] * (1 if nk > 1 else 0),
        compiler_params=_params(2),
    )(*a_list, *b_list, *extras, *rows, *tail)
    return outs[:no], outs[no:]


def _ep_rows_residual(acc, ex, rows):
    return (ex[0] + acc,), ()


def _ep_rows_residual_norm(acc, ex, rows):
    x = ex[0] + acc
    r = lax.rsqrt(jnp.mean(x * x, axis=-1, keepdims=True) + EPS)
    return (x, (x * r) * rows[0]), ()


def _ep_rows_rms_bwd(acc, ex, rows):
    xv, res = ex
    r = lax.rsqrt(jnp.mean(xv * xv, axis=-1, keepdims=True) + EPS)
    xh = xv * r
    dg = jnp.sum(acc * xh, axis=0, keepdims=True)
    dxh = acc * rows[0]
    dx = r * (dxh - xh * jnp.mean(dxh * xh, axis=-1, keepdims=True)) + res
    return (dx, dx), (dg,)


def _sigmoid(x):
    return 1.0 / (1.0 + jnp.exp(-x))


def _ep_plain(accs, ex):
    return (accs[0],)


def _ep_swiglu(accs, ex):
    g, u = accs
    return g, u, (g * _sigmoid(g)) * u


def _ep_swiglu_bwd(accs, ex):
    dact = accs[0]
    g, u = ex
    sig = _sigmoid(g)
    silu = g * sig
    return dact * u * (sig * (1.0 + g * (1.0 - sig))), dact * silu


def _ep_two(accs, ex):
    return accs[0], accs[1]


ROWS = 256


def _row_spec(tm, width, col=0):
    return pl.BlockSpec((tm, width), lambda i: (i, col))


def _full_spec(shape):
    nd = len(shape)
    return pl.BlockSpec(shape, lambda i: (0,) * nd)


def _rms_fwd(name, x, g):
    T, D = x.shape
    tm = min(ROWS, T)

    def body(x_ref, g_ref, o_ref):
        xv = x_ref[...]
        r = lax.rsqrt(jnp.mean(xv * xv, axis=-1, keepdims=True) + EPS)
        o_ref[...] = ((xv * r) * g_ref[...]).astype(BF16)

    return pl.pallas_call(
        body, name=name, grid=(T // tm,),
        in_specs=[_row_spec(tm, D), _full_spec((1, D))], out_specs=_row_spec(tm, D),
        out_shape=jax.ShapeDtypeStruct((T, D), BF16), compiler_params=_params(1),
    )(x, g)


def _rms_bwd(name, dyn, x, g, res):
    T, D = x.shape
    tm = min(ROWS, T)

    def body(dyn_ref, x_ref, g_ref, res_ref, dx_ref, dx16_ref, dg_ref):
        outs, (dg,) = _ep_rows_rms_bwd(dyn_ref[...], [x_ref[...], res_ref[...]], [g_ref[...]])

        @pl.when(pl.program_id(0) == 0)
        def _():
            dg_ref[...] = jnp.zeros_like(dg_ref)

        dg_ref[...] += dg
        dx_ref[...] = outs[0]
        dx16_ref[...] = outs[1].astype(BF16)

    return pl.pallas_call(
        body, name=name, grid=(T // tm,),
        in_specs=[_row_spec(tm, D), _row_spec(tm, D), _full_spec((1, D)), _row_spec(tm, D)],
        out_specs=[_row_spec(tm, D), _row_spec(tm, D), _full_spec((1, D))],
        out_shape=[jax.ShapeDtypeStruct((T, D), F32), jax.ShapeDtypeStruct((T, D), BF16),
                   jax.ShapeDtypeStruct((1, D), F32)],
        compiler_params=_params(1),
    )(dyn, x, g, res)


def _outnorm_fwd(name, a, b, ga, gb):
    T, W = a.shape
    tm = min(ROWS, T)

    def body(a_ref, b_ref, ga_ref, gb_ref, o_ref):
        for src, gain, lo in ((a_ref, ga_ref, 0), (b_ref, gb_ref, W)):
            v = src[...]
            r = lax.rsqrt(jnp.mean(v * v, axis=-1, keepdims=True) + EPS)
            o_ref[:, lo:lo + W] = ((v * r) * gain[...]).astype(BF16)

    return pl.pallas_call(
        body, name=name, grid=(T // tm,),
        in_specs=[_row_spec(tm, W), _row_spec(tm, W), _full_spec((1, W)), _full_spec((1, W))],
        out_specs=_row_spec(tm, 2 * W),
        out_shape=jax.ShapeDtypeStruct((T, 2 * W), BF16), compiler_params=_params(1),
    )(a, b, ga, gb)


def _outnorm_bwd(name, dmixed, a, b, ga, gb):
    T, W = a.shape
    tm = min(ROWS, T)

    def body(dm_ref, a_ref, b_ref, ga_ref, gb_ref, da_ref, db_ref, dga_ref, dgb_ref):
        first = pl.program_id(0) == 0
        for src, gain, lo, dsrc, dgain in ((a_ref, ga_ref, 0, da_ref, dga_ref), (b_ref, gb_ref, W, db_ref, dgb_ref)):
            v, dv = src[...], dm_ref[:, lo:lo + W]
            r = lax.rsqrt(jnp.mean(v * v, axis=-1, keepdims=True) + EPS)
            vh = v * r

            @pl.when(first)
            def _():
                dgain[...] = jnp.zeros_like(dgain)

            dgain[...] += jnp.sum(dv * vh, axis=0, keepdims=True)
            dvh = dv * gain[...]
            dsrc[...] = r * (dvh - vh * jnp.mean(dvh * vh, axis=-1, keepdims=True))

    return pl.pallas_call(
        body, name=name, grid=(T // tm,),
        in_specs=[_row_spec(tm, 2 * W), _row_spec(tm, W), _row_spec(tm, W), _full_spec((1, W)), _full_spec((1, W))],
        out_specs=[_row_spec(tm, W), _row_spec(tm, W), _full_spec((1, W)), _full_spec((1, W))],
        out_shape=[jax.ShapeDtypeStruct((T, W), F32), jax.ShapeDtypeStruct((T, W), F32),
                   jax.ShapeDtypeStruct((1, W), F32), jax.ShapeDtypeStruct((1, W), F32)],
        compiler_params=_params(1),
    )(dmixed, a, b, ga, gb)


def _loss_head(name, y, target):
    T, D = y.shape
    tm = min(ROWS, T)

    def body(y_ref, t_ref, loss_ref, dy_ref, dy16_ref):
        d = y_ref[...] - t_ref[...]

        @pl.when(pl.program_id(0) == 0)
        def _():
            loss_ref[...] = jnp.zeros_like(loss_ref)

        per_token = jnp.mean(d * d, axis=-1, keepdims=True)
        loss_ref[...] += 0.5 * jnp.sum(per_token, axis=0, keepdims=True)
        dy = d * (1.0 / D)
        dy_ref[...] = dy
        dy16_ref[...] = dy.astype(BF16)

    return pl.pallas_call(
        body, name=name, grid=(T // tm,),
        in_specs=[_row_spec(tm, D), _row_spec(tm, D)],
        out_specs=[_full_spec((1, 1)), _row_spec(tm, D), _row_spec(tm, D)],
        out_shape=[jax.ShapeDtypeStruct((1, 1), F32), jax.ShapeDtypeStruct((T, D), F32),
                   jax.ShapeDtypeStruct((T, D), BF16)],
        compiler_params=_params(1),
    )(y, target)


def _bf16_dot(a, b, dims):
    return lax.dot_general(a.astype(BF16), b.astype(BF16), dims, preferred_element_type=F32)


@jax.custom_vjp
def _dot_nn(a, b):
    return _bf16_dot(a, b, _DIMS["nn"])


def _dot_nn_fwd(a, b):
    return _dot_nn(a, b), (a, b)


def _dot_nn_bwd(saved, ct):
    a, b = saved
    return _bf16_dot(ct, b, _DIMS["nt"]), _bf16_dot(a, ct, _DIMS["tn"])


_dot_nn.defvjp(_dot_nn_fwd, _dot_nn_bwd)


@jax.custom_vjp
def _dot_nt(a, b):
    return _bf16_dot(a, b, _DIMS["nt"])


def _dot_nt_fwd(a, b):
    return _dot_nt(a, b), (a, b)


def _dot_nt_bwd(saved, ct):
    a, b = saved
    return _bf16_dot(ct, b, _DIMS["nn"]), _bf16_dot(ct, a, _DIMS["tn"])


_dot_nt.defvjp(_dot_nt_fwd, _dot_nt_bwd)


def _iota(shape, dim):
    return lax.broadcasted_iota(jnp.int32, shape, dim)


def _head_sum_impl(x):
    same_head = (_iota((LANES, LANES), 0) // HEAD_DIM == _iota((LANES, LANES), 1) // HEAD_DIM).astype(BF16)
    pieces = []
    for i in range(x.shape[1] // LANES):
        xs = x[:, i * LANES:(i + 1) * LANES]
        hi = xs.astype(BF16)
        r1 = xs - hi.astype(F32)
        mid = r1.astype(BF16)
        lo = (r1 - mid.astype(F32)).astype(BF16)
        tot = None
        for part in (hi, mid, lo):
            d = lax.dot_general(part, same_head, _DIMS["nn"], preferred_element_type=F32)
            tot = d if tot is None else tot + d
        pieces.append(tot)
    return pieces[0] if len(pieces) == 1 else jnp.concatenate(pieces, axis=1)


@jax.custom_vjp
def _head_sum(x):
    return _head_sum_impl(x)


_head_sum.defvjp(lambda x: (_head_sum_impl(x), None), lambda _, ct: (_head_sum_impl(ct),))


def _head_rms(x, g_row):
    ms = _head_sum(x * x) * (1.0 / HEAD_DIM)
    return (x * lax.rsqrt(ms + EPS)) * g_row


@jax.custom_vjp
def _swap_halves(x):
    return pltpu.roll(x, HEAD_DIM, 1)


_swap_halves.defvjp(lambda x: (pltpu.roll(x, HEAD_DIM, 1), None), lambda _, ct: (pltpu.roll(ct, HEAD_DIM, 1),))


def _gelu(x):
    return 0.5 * x * (1.0 + lax.erf(x * (1.0 / math.sqrt(2.0))))


def _sgu_block(u_raw, v_raw, g_row, w, b_full):
    u, v = _gelu(u_raw), _gelu(v_raw)
    vn = _head_rms(v, g_row)
    causal = _iota((CHUNK, CHUNK), 0) >= _iota((CHUNK, CHUNK), 1)
    low_half = _iota((CHUNK, LANES), 1) < HEAD_DIM
    gates = []
    for p in range(v.shape[1] // LANES):
        vp = vn[:, p * LANES:(p + 1) * LANES]
        g0 = _dot_nn(jnp.where(causal, w[2 * p], 0.0), vp)
        g1 = _dot_nn(jnp.where(causal, w[2 * p + 1], 0.0), vp)
        gates.append(jnp.where(low_half, g0, g1))
    gate = jnp.concatenate(gates, axis=1) + b_full
    return u * gate


def _attn_block(q_raw, k_prev, k_cur, v_prev, v_cur, qg_row, kg_row, sinks, bias, blk):
    scale = 1.0 / math.sqrt(HEAD_DIM)
    n_q_heads = q_raw.shape[1] // HEAD_DIM
    qn = _head_rms(q_raw, qg_row)
    kn = _head_rms(jnp.concatenate([k_prev, k_cur], axis=0), kg_row)
    vb = jnp.concatenate([v_prev, v_cur], axis=0)
    qi = _iota((GROUP * BLOCK, 2 * BLOCK), 0) & (BLOCK - 1)
    kj = _iota((GROUP * BLOCK, 2 * BLOCK), 1)
    dist = qi + BLOCK - kj
    valid = (dist >= 0) & (dist < BLOCK) & (kj + (blk * BLOCK - BLOCK) >= 0)
    low_half = _iota((BLOCK, LANES), 1) < HEAD_DIM
    heads = [None] * n_q_heads
    for kv in range(n_q_heads // GROUP):
        grp, kv_low = kv // 2, kv % 2 == 0
        k2 = kn[:, grp * LANES:(grp + 1) * LANES]
        v2 = vb[:, grp * LANES:(grp + 1) * LANES]
        keep = low_half if kv_low else ~low_half
        stacked = []
        for g in range(GROUP):
            h = GROUP * kv + g
            q2 = qn[:, (h // 2) * LANES:(h // 2 + 1) * LANES]
            src = q2 if (h % 2 == 0) == kv_low else _swap_halves(q2)
            stacked.append(jnp.where(keep, src, 0.0))
        q4 = jnp.concatenate(stacked, axis=0)
        bias4 = jnp.concatenate([bias[GROUP * kv + g] for g in range(GROUP)], axis=0)
        sink4 = jnp.concatenate([jnp.broadcast_to(sinks[GROUP * kv + g], (BLOCK, 1)) for g in range(GROUP)], axis=0)
        s = _dot_nt(q4, k2) * scale + bias4
        s = jnp.where(valid, s, NEG_INF)
        m = lax.stop_gradient(jnp.maximum(jnp.max(s, axis=1, keepdims=True), sink4))
        e = jnp.exp(s - m)
        denom = jnp.sum(e, axis=1, keepdims=True) + jnp.exp(sink4 - m)
        o4 = _dot_nn(e * (1.0 / denom), v2)
        for g in range(GROUP):
            h = GROUP * kv + g
            o = o4[g * BLOCK:(g + 1) * BLOCK]
            heads[h] = o if (h % 2 == 0) == kv_low else _swap_halves(o)
    outs = [jnp.where(low_half, heads[2 * p], heads[2 * p + 1]) for p in range(n_q_heads // 2)]
    return jnp.concatenate(outs, axis=1)


def _bias_table():
    dist = np.arange(BLOCK)[:, None] + BLOCK - np.arange(2 * BLOCK)[None, :]
    n = np.maximum(dist, 0)
    max_exact = NUM_BUCKETS // 2
    nf = np.maximum(n, 1).astype(np.float64)
    large = max_exact + (np.log(nf / max_exact) / math.log(MAX_DISTANCE / max_exact) * (NUM_BUCKETS - max_exact)).astype(np.int32)
    large = np.minimum(large, NUM_BUCKETS - 1)
    return np.where(n < max_exact, n, large).astype(np.int32)


def _bias_fwd(name, rel_bias, buckets):
    nb_, nh = rel_bias.shape

    def body(rb_ref, bk_ref, o_ref):
        bk = bk_ref[...]
        for h in range(nh):
            acc = jnp.zeros(bk.shape, F32)
            for b in range(nb_):
                acc = jnp.where(bk == b, rb_ref[b, h], acc)
            o_ref[h] = acc

    return pl.pallas_call(
        body, name=name,
        in_specs=[pl.BlockSpec(memory_space=pltpu.SMEM), pl.BlockSpec(memory_space=pltpu.VMEM)],
        out_specs=pl.BlockSpec(memory_space=pltpu.VMEM),
        out_shape=jax.ShapeDtypeStruct((nh,) + buckets.shape, F32),
    )(rel_bias, buckets)


def _bias_bwd(name, dbias, buckets, nb_):
    nh = dbias.shape[0]

    def body(db_ref, bk_ref, o_ref):
        bk = bk_ref[...]
        for h in range(nh):
            d = db_ref[h]
            for b in range(nb_):
                s = jnp.sum(jnp.where(bk == b, d, 0.0), axis=0, keepdims=True)
                s = jnp.sum(s, axis=1, keepdims=True)
                o_ref[b * nh + h:b * nh + h + 1, :] = jnp.broadcast_to(s, (1, LANES))

    return pl.pallas_call(
        body, name=name,
        in_specs=[pl.BlockSpec(memory_space=pltpu.VMEM), pl.BlockSpec(memory_space=pltpu.VMEM)],
        out_specs=pl.BlockSpec(memory_space=pltpu.VMEM),
        out_shape=jax.ShapeDtypeStruct((nb_ * nh, LANES), F32),
    )(dbias, buckets)


def _sgu_fwd(name, z, g_row, w, b_full, W):
    T = z.shape[0]

    def body(u_ref, v_ref, g_ref, w_ref, b_ref, o_ref):
        o_ref[...] = _sgu_block(u_ref[...], v_ref[...], g_ref[...], w_ref[...], b_ref[...])

    return pl.pallas_call(
        body, name=name, grid=(T // CHUNK,),
        in_specs=[_row_spec(CHUNK, W, 0), _row_spec(CHUNK, W, 1), _full_spec((1, W)),
                  _full_spec(w.shape), _full_spec((CHUNK, W))],
        out_specs=_row_spec(CHUNK, W),
        out_shape=jax.ShapeDtypeStruct((T, W), F32), compiler_params=_params(1),
    )(z, z, g_row, w, b_full)


def _sgu_bwd(name, z, g_row, w, b_full, d_out, W):
    T = z.shape[0]

    def body(u_ref, v_ref, g_ref, w_ref, b_ref, do_ref, dz_ref, dg_ref, dw_ref, db_ref):
        _, vjp = jax.vjp(_sgu_block, u_ref[...], v_ref[...], g_ref[...], w_ref[...], b_ref[...])
        du, dv, dg, dw, db = vjp(do_ref[...])
        dz_ref[:, :W] = du.astype(BF16)
        dz_ref[:, W:] = dv.astype(BF16)
        i = pl.program_id(0)

        @pl.when(i == 0)
        def _():
            dg_ref[...] = jnp.zeros_like(dg_ref)
            dw_ref[...] = jnp.zeros_like(dw_ref)
            db_ref[...] = jnp.zeros_like(db_ref)

        dg_ref[...] += dg
        dw_ref[...] += dw
        db_ref[...] += db

        @pl.when(i == pl.num_programs(0) - 1)
        def _():
            db_ref[...] = _head_sum_impl(db_ref[...])

    return pl.pallas_call(
        body, name=name, grid=(T // CHUNK,),
        in_specs=[_row_spec(CHUNK, W, 0), _row_spec(CHUNK, W, 1), _full_spec((1, W)),
                  _full_spec(w.shape), _full_spec((CHUNK, W)), _row_spec(CHUNK, W)],
        out_specs=[_row_spec(CHUNK, 2 * W), _full_spec((1, W)), _full_spec(w.shape), _full_spec((CHUNK, W))],
        out_shape=[jax.ShapeDtypeStruct((T, 2 * W), BF16), jax.ShapeDtypeStruct((1, W), F32),
                   jax.ShapeDtypeStruct(w.shape, F32), jax.ShapeDtypeStruct((CHUNK, W), F32)],
        compiler_params=_params(1),
    )(z, z, g_row, w, b_full, d_out)


def _attn_specs(WQ, WKV, q_col, k_col, v_col, blk_of):
    prev_of = lambda i: jnp.maximum(blk_of(i) - 1, 0)
    return [pl.BlockSpec((BLOCK, WQ), lambda i: (blk_of(i), q_col)),
            pl.BlockSpec((BLOCK, WKV), lambda i: (prev_of(i), k_col)),
            pl.BlockSpec((BLOCK, WKV), lambda i: (blk_of(i), k_col)),
            pl.BlockSpec((BLOCK, WKV), lambda i: (prev_of(i), v_col)),
            pl.BlockSpec((BLOCK, WKV), lambda i: (blk_of(i), v_col))]


def _attn_fwd(name, z, qg_row, kg_row, sinks_col, bias, WQ, WKV, q_col, k_col, v_col):
    T = z.shape[0]
    nh = sinks_col.shape[0]

    def body(q_ref, kp_ref, kc_ref, vp_ref, vc_ref, qg_ref, kg_ref, s_ref, b_ref, o_ref):
        sinks = [s_ref[h:h + 1, :] for h in range(nh)]
        o_ref[...] = _attn_block(q_ref[...], kp_ref[...], kc_ref[...], vp_ref[...], vc_ref[...],
                                 qg_ref[...], kg_ref[...], sinks, b_ref[...], pl.program_id(0))

    return pl.pallas_call(
        body, name=name, grid=(T // BLOCK,),
        in_specs=_attn_specs(WQ, WKV, q_col, k_col, v_col, lambda i: i)
        + [_full_spec((1, WQ)), _full_spec((1, WKV)), _full_spec((nh, 1)), _full_spec(bias.shape)],
        out_specs=_row_spec(BLOCK, WQ),
        out_shape=jax.ShapeDtypeStruct((T, WQ), F32), compiler_params=_params(1),
    )(z, z, z, z, z, qg_row, kg_row, sinks_col, bias)


def _attn_bwd(name, z, qg_row, kg_row, sinks_col, bias, d_out, WQ, WKV, q_col, k_col, v_col):
    T = z.shape[0]
    nblk = T // BLOCK
    nh = sinks_col.shape[0]
    blk_of = lambda i: nblk - 1 - i

    def body(q_ref, kp_ref, kc_ref, vp_ref, vc_ref, qg_ref, kg_ref, s_ref, b_ref, do_ref,
             dq_ref, dk_ref, dv_ref, dqg_ref, dkg_ref, ds_ref, db_ref, carry_k, carry_v):
        i = pl.program_id(0)
        blk = nblk - 1 - i
        sinks = [s_ref[h:h + 1, :] for h in range(nh)]
        fn = lambda q, kp, kc, vp, vc, qg, kg, sk, bs: _attn_block(q, kp, kc, vp, vc, qg, kg, sk, bs, blk)
        _, vjp = jax.vjp(fn, q_ref[...], kp_ref[...], kc_ref[...], vp_ref[...], vc_ref[...],
                         qg_ref[...], kg_ref[...], sinks, b_ref[...])
        dq, dkp, dkc, dvp, dvc, dqg, dkg, dsk, dbs = vjp(do_ref[...])

        @pl.when(i == 0)
        def _():
            carry_k[...] = jnp.zeros_like(carry_k)
            carry_v[...] = jnp.zeros_like(carry_v)
            dqg_ref[...] = jnp.zeros_like(dqg_ref)
            dkg_ref[...] = jnp.zeros_like(dkg_ref)
            ds_ref[...] = jnp.zeros_like(ds_ref)
            db_ref[...] = jnp.zeros_like(db_ref)

        dq_ref[...] = dq.astype(BF16)
        dk_ref[...] = (dkc + carry_k[...]).astype(BF16)
        dv_ref[...] = (dvc + carry_v[...]).astype(BF16)
        carry_k[...] = dkp
        carry_v[...] = dvp
        dqg_ref[...] += dqg
        dkg_ref[...] += dkg
        for h in range(nh):
            ds_ref[h:h + 1, :] += dsk[h]
        db_ref[...] += dbs

    return pl.pallas_call(
        body, name=name, grid=(nblk,),
        in_specs=_attn_specs(WQ, WKV, q_col, k_col, v_col, blk_of)
        + [_full_spec((1, WQ)), _full_spec((1, WKV)), _full_spec((nh, 1)), _full_spec(bias.shape),
           pl.BlockSpec((BLOCK, WQ), lambda i: (blk_of(i), 0))],
        out_specs=[pl.BlockSpec((BLOCK, WQ), lambda i: (blk_of(i), 0)),
                   pl.BlockSpec((BLOCK, WKV), lambda i: (blk_of(i), 0)),
                   pl.BlockSpec((BLOCK, WKV), lambda i: (blk_of(i), 0)),
                   _full_spec((1, WQ)), _full_spec((1, WKV)), _full_spec((nh, 1)), _full_spec(bias.shape)],
        out_shape=[jax.ShapeDtypeStruct((T, WQ), BF16), jax.ShapeDtypeStruct((T, WKV), BF16),
                   jax.ShapeDtypeStruct((T, WKV), BF16), jax.ShapeDtypeStruct((1, WQ), F32),
                   jax.ShapeDtypeStruct((1, WKV), F32), jax.ShapeDtypeStruct((nh, 1), F32),
                   jax.ShapeDtypeStruct(bias.shape, F32)],
        scratch_shapes=[pltpu.VMEM((BLOCK, WKV), F32), pltpu.VMEM((BLOCK, WKV), F32)],
        compiler_params=_params(1),
    )(z, z, z, z, z, qg_row, kg_row, sinks_col, bias, d_out)


def _adamw(name, w, g, m, v):
    shape = w.shape
    C = shape[-1]
    R = int(np.prod(shape[:-1]))
    w2, g2, m2, v2 = (t.reshape(R, C) for t in (w, g, m, v))
    tr = ROWS if R % ROWS == 0 else R

    def body(w_ref, g_ref, m_ref, v_ref, d_ref, nm_ref, nv_ref):
        gv = g_ref[...]
        nm = ADAM_B1 * m_ref[...] + (1.0 - ADAM_B1) * gv
        nv = ADAM_B2 * v_ref[...] + (1.0 - ADAM_B2) * (gv * gv)
        m_hat = nm / (1.0 - ADAM_B1 ** ADAM_STEP)
        v_hat = nv / (1.0 - ADAM_B2 ** ADAM_STEP)
        d_ref[...] = -ADAM_LR * (m_hat / (jnp.sqrt(v_hat) + ADAM_EPS) + ADAM_WD * w_ref[...])
        nm_ref[...] = nm
        nv_ref[...] = nv

    spec = pl.BlockSpec((tr, C), lambda i: (i, 0))
    outs = pl.pallas_call(
        body, name=name, grid=(R // tr,), in_specs=[spec] * 4, out_specs=[spec] * 3,
        out_shape=[jax.ShapeDtypeStruct((R, C), F32)] * 3, compiler_params=_params(1),
    )(w2, g2, m2, v2)
    return tuple(o.reshape(shape) for o in outs)


def _pad_rows(flat):
    n = flat.shape[0]
    tile = 8 * LANES
    padded = -(-n // tile) * tile
    return jnp.pad(flat, (0, padded - n)).reshape(padded // LANES, LANES)


def kernel(x, rel_bias, norm1_g, w_in, sgu_norm_g, sgu_w, sgu_b, q_norm_g, k_norm_g, sinks, out_norm_a, out_norm_b, w_out, norm2_g, w_gate, w_up, w_down, loss_target, m_rel_bias, m_norm1_g, m_w_in, m_sgu_norm_g, m_sgu_w, m_sgu_b, m_q_norm_g, m_k_norm_g, m_sinks, m_out_norm_a, m_out_norm_b, m_w_out, m_norm2_g, m_w_gate, m_w_up, m_w_down, v_rel_bias, v_norm1_g, v_w_in, v_sgu_norm_g, v_sgu_w, v_sgu_b, v_q_norm_g, v_k_norm_g, v_sinks, v_out_norm_a, v_out_norm_b, v_w_out, v_norm2_g, v_w_gate, v_w_up, v_w_down):
    L = w_in.shape[0]
    T, D = x.shape[1], x.shape[2]
    W = D // 2
    NH = W // HEAD_DIM
    WKV = N_KV_HEADS * HEAD_DIM
    IN = N_DEV * w_in.shape[2]
    FF = N_DEV * w_gate.shape[2]
    assert IN == 2 * W + W + 2 * WKV and NH // N_KV_HEADS == GROUP
    q_col, k_col, v_col = 2 * W // W, (3 * W) // WKV, (3 * W + WKV) // WKV
    x0 = x.reshape(T, D)
    target = loss_target.reshape(T, D)

    shards = [jnp.swapaxes(w_in, 1, 2).astype(BF16), w_out.astype(BF16), jnp.swapaxes(w_gate, 1, 2).astype(BF16),
              jnp.swapaxes(w_up, 1, 2).astype(BF16), w_down.astype(BF16)]
    me = 4 * lax.axis_index("x") + 2 * lax.axis_index("y") + lax.axis_index("c")

    def landing(block):
        return lax.dynamic_update_slice(lax.empty((N_DEV,) + block.shape, block.dtype), block[None], (me, 0, 0))

    def as_matrices(got):
        return [g.reshape(N_DEV * g.shape[1], g.shape[2]) for g in got]

    full = {0: as_matrices(_all_gather("gather_weights_0", [s[0] for s in shards]))}
    flight = {}

    def start_gather(l, after):
        srcs = [s[l] for s in shards]
        flight[l] = _exchange_start(f"gather_start_{l}", "gather", srcs, [landing(s) for s in srcs], after)
        return flight[l][4]

    def finish_gather(l, after):
        s_sem, r_sem, srcs, lands, _ = flight.pop(l)
        full[l] = as_matrices(_exchange_wait(f"gather_wait_{l}", "gather", s_sem, r_sem, srcs, lands, after)[1])

    buckets = jnp.asarray(_bias_table())
    bias = _bias_fwd("bias_table", rel_bias, buckets)

    saved = []
    xl = x0
    for l in range(L):
        if l > 0:
            finish_gather(l, xl)
        token = start_gather(l + 1, full[l][0] if l == 0 else xl) if l + 1 < L else xl
        w_in_t, w_o, w_g_t, w_u_t, w_d = full[l]
        g1, g2 = norm1_g[l][None], norm2_g[l][None]
        sg_row = sgu_norm_g[l].reshape(1, W)
        b_full = jnp.repeat(sgu_b[l].T, HEAD_DIM, axis=1)
        qg_row = jnp.tile(q_norm_g[l], NH)[None]
        kg_row = jnp.tile(k_norm_g[l], N_KV_HEADS)[None]
        sinks_col = sinks[l][:, None]
        ga, gb = out_norm_a[l][None], out_norm_b[l][None]

        if l == 0:
            h = _rms_fwd("norm1_0", xl, g1)
        (z,) = _mm(f"proj_in_{l}", "nt", [h], [w_in_t], [(0, 0, 0)], 1, T, IN, D, 1024, 896, D, _ep_plain, [F32],
                   after=token)
        out_a = _sgu_fwd(f"sgu_{l}", z, sg_row, sgu_w[l], b_full, W)
        out_b = _attn_fwd(f"attn_{l}", z, qg_row, kg_row, sinks_col, bias, W, WKV, q_col, k_col, v_col)
        mixed = _outnorm_fwd(f"outnorm_{l}", out_a, out_b, ga, gb)
        (x1, h2), _ = _mm_rows(f"proj_out_{l}", [mixed], [w_o], T, D, D, 512, 1024, _ep_rows_residual_norm,
                               [F32, BF16], 0, extras=[xl], rows=[g2])
        gate, up, act = _mm(f"mlp_in_{l}", "nt", [h2], [w_g_t, w_u_t], [(0, 0, 0), (0, 1, 1)], 2, T, FF, D,
                            1024, 512, D, _ep_swiglu, [BF16, BF16, BF16])
        saved.append((xl, h, z, out_a, out_b, mixed, x1, h2, gate, up, act,
                      g1, g2, sg_row, b_full, qg_row, kg_row, sinks_col, ga, gb))
        if l + 1 < L:
            (xl, h), _ = _mm_rows(f"mlp_out_{l}", [act], [w_d], T, D, FF, 512, 1408, _ep_rows_residual_norm,
                                  [F32, BF16], 0, extras=[x1], rows=[norm1_g[l + 1][None]])
        else:
            (xl,), _ = _mm_rows(f"mlp_out_{l}", [act], [w_d], T, D, FF, 512, 1408, _ep_rows_residual, [F32], 0,
                                extras=[x1])

    loss_part, dy, dy16 = _loss_head("loss_head", xl, target)

    dbias = None
    small = [None] * L
    scatters = []

    def start_scatter(name, which, grads_t, after):
        srcs = [t.reshape(N_DEV, t.shape[0] // N_DEV, D) for t in grads_t]
        lands = [lax.empty((N_DEV - 1,) + s.shape[1:], BF16) for s in srcs]
        s_sem, r_sem, srcs, lands, tok = _exchange_start(name, "scatter", srcs, lands, after)
        scatters.append((name, which, s_sem, r_sem, srcs, lands))
        return tok

    token = None
    for l in reversed(range(L)):
        w_in_t, w_o, w_g_t, w_u_t, w_d = full[l]
        (xl, h, z, out_a, out_b, mixed, x1, h2, gate, up, act,
         g1, g2, sg_row, b_full, qg_row, kg_row, sinks_col, ga, gb) = saved[l]

        dgate, dup = _mm(f"d_mlp_out_{l}", "nt", [dy16], [w_d], [(0, 0, 0)], 1, T, FF, D, 1024, 1408, D,
                         _ep_swiglu_bwd, [BF16, BF16], extras=[gate, up], after=token)
        (dw_d,) = _mm(f"dw_down_{l}", "tn", [act], [dy16], [(0, 0, 0)], 1, FF, D, T, 1408, 1024, 2048, _ep_plain, [BF16])
        (dh2,) = _mm(f"d_mlp_in_{l}", "nn", [dgate, dup], [w_g_t, w_u_t], [(0, 0, 0), (1, 1, 0)], 1, T, D, FF,
                     512, 512, FF, _ep_plain, [F32])
        dx1, dx1_16, dg2 = _rms_bwd(f"d_norm2_{l}", dh2, x1, g2, dy)
        dw_g, dw_u = _mm(f"dw_gate_up_{l}", "tn", [dgate, dup], [h2], [(0, 0, 0), (1, 0, 1)], 2, FF, D, T,
                         1408, 1024, 1024, _ep_two, [BF16, BF16])
        token = start_scatter(f"scatter_mlp_start_{l}", (l, (4, 2, 3)), [dw_d, dw_g, dw_u], dw_u)
        (dmixed,) = _mm(f"d_proj_out_{l}", "nt", [dx1_16], [w_o], [(0, 0, 0)], 1, T, D, D, 1024, 1024, D, _ep_plain, [F32],
                        after=token)
        (dw_o,) = _mm(f"dw_out_{l}", "tn", [mixed], [dx1_16], [(0, 0, 0)], 1, D, D, T, 1024, 1024, 2048, _ep_plain, [BF16])
        d_a, d_b, dga, dgb = _outnorm_bwd(f"d_outnorm_{l}", dmixed, out_a, out_b, ga, gb)
        dz_uv, dsg, dsw, dsb = _sgu_bwd(f"d_sgu_{l}", z, sg_row, sgu_w[l], b_full, d_a, W)
        dq, dk, dv, dqg, dkg, dsk, dbs = _attn_bwd(f"d_attn_{l}", z, qg_row, kg_row, sinks_col, bias, d_b,
                                                   W, WKV, q_col, k_col, v_col)
        dbias = dbs if dbias is None else dbias + dbs
        dz = jnp.concatenate([dz_uv, dq, dk, dv], axis=1)
        (dy, dy16), (dg1,) = _mm_rows(f"d_proj_in_{l}", [dz], [w_in_t], T, D, IN, 512, 896, _ep_rows_rms_bwd,
                                      [F32, BF16], 1, extras=[xl, dx1], rows=[g1])
        (dw_i,) = _mm(f"dw_in_{l}", "tn", [dz], [h], [(0, 0, 0)], 1, IN, D, T, 896, 1024, 2048, _ep_plain, [BF16])
        token = start_scatter(f"scatter_mix_start_{l}", (l, (1, 0)), [dw_o, dw_i], dw_i)

        small[l] = dict(norm1_g=dg1[0], sgu_norm_g=dsg.reshape(NH, HEAD_DIM), sgu_w=dsw,
                        sgu_b=dsb[:, ::HEAD_DIM].T, q_norm_g=dqg.reshape(NH, HEAD_DIM).sum(0),
                        k_norm_g=dkg.reshape(N_KV_HEADS, HEAD_DIM).sum(0), sinks=dsk[:, 0],
                        out_norm_a=dga[0], out_norm_b=dgb[0], norm2_g=dg2[0])

    grad_x = dy.reshape(x.shape)
    d_rel = _bias_bwd("d_bias_table", dbias, buckets, NUM_BUCKETS)[:, 0].reshape(NUM_BUCKETS, NH)

    names = ["norm1_g", "sgu_norm_g", "sgu_w", "sgu_b", "q_norm_g", "k_norm_g", "sinks", "out_norm_a", "out_norm_b", "norm2_g"]
    parts = {"rel_bias": d_rel}
    for nme in names:
        parts[nme] = jnp.stack([small[l][nme] for l in range(L)])
    order = ["rel_bias"] + names
    packed = jnp.concatenate([_pad_rows(parts[nme].reshape(-1)) for nme in order], axis=0)
    small_flight = _exchange_start("gather_small_start", "gather", [packed], [landing(packed)], token)
    after = small_flight[4]

    grads_big = {}
    for name, (l, which), s_sem, r_sem, srcs, lands in scatters:
        srcs, lands = _exchange_wait(name.replace("start", "wait"), "scatter", s_sem, r_sem, srcs, lands, after)
        for i, src, land in zip(which, srcs, lands):
            own = lax.dynamic_index_in_dim(src, me, 0, keepdims=False)
            rows = own.shape[0]
            after = _sum_parts(f"sum_grads_{l}_{i}", own, land, 64 if rows % 64 == 0 else rows)
            grads_big[i, l] = after
    g_w_in = jnp.swapaxes(jnp.stack([grads_big[0, l] for l in range(L)]), 1, 2)
    g_w_out = jnp.stack([grads_big[1, l] for l in range(L)])
    g_w_gate = jnp.swapaxes(jnp.stack([grads_big[2, l] for l in range(L)]), 1, 2)
    g_w_up = jnp.swapaxes(jnp.stack([grads_big[3, l] for l in range(L)]), 1, 2)
    g_w_down = jnp.stack([grads_big[4, l] for l in range(L)])

    s_sem, r_sem, srcs, lands, _ = small_flight
    _, (everyone,) = _exchange_wait("gather_small_wait", "gather", s_sem, r_sem, srcs, lands, after)
    rows = packed.shape[0]
    summed = _sum_slots("sum_small_grads", everyone, 64 if rows % 64 == 0 else 8)
    g_small, at = {}, 0
    for nme in order:
        n = int(np.prod(parts[nme].shape))
        n_rows = -(-n // (8 * LANES)) * 8
        g_small[nme] = summed[at:at + n_rows].reshape(-1)[:n].reshape(parts[nme].shape)
        at += n_rows

    loss = lax.psum(loss_part[0, 0], ("x", "y", "c"))

    grads = dict(g_small, w_in=g_w_in, w_out=g_w_out, w_gate=g_w_gate, w_up=g_w_up, w_down=g_w_down)
    weights = dict(rel_bias=rel_bias, norm1_g=norm1_g, w_in=w_in, sgu_norm_g=sgu_norm_g, sgu_w=sgu_w, sgu_b=sgu_b,
                   q_norm_g=q_norm_g, k_norm_g=k_norm_g, sinks=sinks, out_norm_a=out_norm_a, out_norm_b=out_norm_b,
                   w_out=w_out, norm2_g=norm2_g, w_gate=w_gate, w_up=w_up, w_down=w_down)
    ms = dict(rel_bias=m_rel_bias, norm1_g=m_norm1_g, w_in=m_w_in, sgu_norm_g=m_sgu_norm_g, sgu_w=m_sgu_w, sgu_b=m_sgu_b,
              q_norm_g=m_q_norm_g, k_norm_g=m_k_norm_g, sinks=m_sinks, out_norm_a=m_out_norm_a, out_norm_b=m_out_norm_b,
              w_out=m_w_out, norm2_g=m_norm2_g, w_gate=m_w_gate, w_up=m_w_up, w_down=m_w_down)
    vs = dict(rel_bias=v_rel_bias, norm1_g=v_norm1_g, w_in=v_w_in, sgu_norm_g=v_sgu_norm_g, sgu_w=v_sgu_w, sgu_b=v_sgu_b,
              q_norm_g=v_q_norm_g, k_norm_g=v_k_norm_g, sinks=v_sinks, out_norm_a=v_out_norm_a, out_norm_b=v_out_norm_b,
              w_out=v_w_out, norm2_g=v_norm2_g, w_gate=v_w_gate, w_up=v_w_up, w_down=v_w_down)
    all_names = ["rel_bias", "norm1_g", "w_in", "sgu_norm_g", "sgu_w", "sgu_b", "q_norm_g", "k_norm_g", "sinks",
                 "out_norm_a", "out_norm_b", "w_out", "norm2_g", "w_gate", "w_up", "w_down"]
    deltas, new_m, new_v = [], [], []
    for nme in all_names:
        d, nm, nv = _adamw(f"adamw_{nme}", weights[nme], grads[nme], ms[nme], vs[nme])
        deltas.append(d)
        new_m.append(nm)
        new_v.append(nv)
    return (loss, grad_x, *[grads[nme] for nme in all_names], *deltas, *new_m, *new_v)
```

```python
import functools
import math

import numpy as np
import jax
import jax.numpy as jnp
from jax import lax
from jax.experimental import pallas as pl
from jax.experimental.pallas import tpu as pltpu

F32 = jnp.float32
BF16 = jnp.bfloat16

N_DEV = 8
HEAD_DIM = 64
CHUNK = 128
BLOCK = 128
N_KV_HEADS = 4
GROUP = 4
NUM_BUCKETS = 32
MAX_DISTANCE = 128
EPS = 1e-6
NEG_INF = -1e30
LANES = 128
VMEM_LIMIT = 56 * 2 ** 20

ADAM_LR = 0.001
ADAM_B1 = 0.9
ADAM_B2 = 0.999
ADAM_EPS = 1e-08
ADAM_WD = 0.01
ADAM_STEP = 10

MESH = pl.DeviceIdType.MESH
ANY = pl.BlockSpec(memory_space=pl.ANY)
HBM = pl.BlockSpec(memory_space=pltpu.HBM)
SEM = pl.BlockSpec(memory_space=pltpu.SEMAPHORE)
EFFECT = pltpu.SideEffectType.DATAFLOW_SIDE_EFFECTING


def _params(n_axes):
    return pltpu.CompilerParams(dimension_semantics=("arbitrary",) * n_axes, vmem_limit_bytes=VMEM_LIMIT)


def _my_place():
    return lax.axis_index("x"), lax.axis_index("y"), lax.axis_index("c")


def _all_gather(name, arrs):
    n = len(arrs)

    def body(*refs):
        ins, outs = refs[:n], refs[n:2 * n]
        send_sems, recv_sems, local_sems = refs[2 * n:]
        x, y, c = _my_place()
        sibling = (x, y, 1 - c)
        chips = [(1 - x, y), (x, 1 - y), (1 - x, 1 - y)]

        def slot(a, px, py, pc):
            return outs[a].at[4 * px + 2 * py + pc]

        def copy(a, k, block, to, src=None):
            return pltpu.make_async_remote_copy(
                src_ref=slot(a, *block) if src is None else src, dst_ref=slot(a, *block),
                send_sem=send_sems.at[7 * a + k], recv_sem=recv_sems.at[7 * a + k],
                device_id=to, device_id_type=MESH)

        started = []
        for a in range(n):
            mine = pltpu.make_async_copy(ins[a], slot(a, x, y, c), local_sems.at[a])
            mine.start()
            started.append(mine)
        sends = []
        for a in range(n):
            first = [copy(a, 0, (x, y, c), sibling, src=ins[a])]
            first += [copy(a, 1 + j, (x, y, c), (*chip, c), src=ins[a]) for j, chip in enumerate(chips)]
            for cp in first:
                cp.start()
            sends += first
        for a in range(n):
            for j, chip in enumerate(chips):
                copy(a, 1 + j, (*chip, c), (x, y, c)).wait_recv()
                fwd = copy(a, 4 + j, (*chip, c), sibling)
                fwd.start()
                sends.append(fwd)
        for a in range(n):
            copy(a, 0, sibling, (x, y, c)).wait_recv()
            for j, chip in enumerate(chips):
                copy(a, 4 + j, (*chip, 1 - c), (x, y, c)).wait_recv()
        for cp in sends:
            cp.wait_send()
        for cp in started:
            cp.wait()

    return pl.pallas_call(
        body, name=name,
        out_shape=[jax.ShapeDtypeStruct((N_DEV,) + a.shape, a.dtype) for a in arrs],
        in_specs=[ANY] * n, out_specs=[ANY] * n,
        scratch_shapes=[pltpu.SemaphoreType.DMA((7 * n,)), pltpu.SemaphoreType.DMA((7 * n,)),
                        pltpu.SemaphoreType.DMA((n,))],
    )(*arrs)


def _peer(k, x, y, c):
    return (1 - x if k & 4 else x), (1 - y if k & 2 else y), (1 - c if k & 1 else c)


PEER_ORDER = (1, 2, 4, 3, 5, 6, 7)


def _exchange_copy(kind, k, a, srcs, lands, send_sems, recv_sems, arriving=False):
    x, y, c = _my_place()
    me = 4 * x + 2 * y + c
    px, py, pc = _peer(k, x, y, c)
    them = 4 * px + 2 * py + pc
    if kind == "gather":
        src, dst_there, dst_here = srcs[a], lands[a].at[me], lands[a].at[them]
    else:
        src, dst_there, dst_here = srcs[a].at[them], lands[a].at[k - 1], lands[a].at[k - 1]
    return pltpu.make_async_remote_copy(
        src_ref=src, dst_ref=dst_here if arriving else dst_there,
        send_sem=send_sems.at[7 * a + k - 1], recv_sem=recv_sems.at[7 * a + k - 1],
        device_id=(px, py, pc), device_id_type=MESH)


def _exchange_start(name, kind, srcs, lands, after):
    n = len(srcs)

    def body(*refs):
        ins, lnd = refs[:n], refs[n:2 * n]
        send_sems, recv_sems = refs[2 * n + 1], refs[2 * n + 2]
        token = refs[-1]
        for k in PEER_ORDER:
            for a in range(n):
                _exchange_copy(kind, k, a, ins, lnd, send_sems, recv_sems).start()
        token[...] = jnp.zeros_like(token)

    hbm = lambda t: pltpu.with_memory_space_constraint(t, pltpu.HBM)
    out = pl.pallas_call(
        body, name=name,
        out_shape=(pltpu.SemaphoreType.DMA((7 * n,)), pltpu.SemaphoreType.DMA((7 * n,)),
                   *[pltpu.HBM(t.shape, t.dtype) for t in srcs], *[pltpu.HBM(t.shape, t.dtype) for t in lands],
                   jax.ShapeDtypeStruct((8, LANES), F32)),
        in_specs=[HBM] * (2 * n) + [ANY],
        out_specs=(SEM, SEM, *[HBM] * (2 * n), pl.BlockSpec(memory_space=pltpu.VMEM)),
        input_output_aliases={i: 2 + i for i in range(2 * n)},
        compiler_params=pltpu.CompilerParams(has_side_effects=EFFECT),
    )(*[hbm(t) for t in srcs], *[hbm(t) for t in lands], after)
    return out[0], out[1], list(out[2:2 + n]), list(out[2 + n:2 + 2 * n]), out[-1]


def _exchange_wait(name, kind, send_sems, recv_sems, srcs, lands, after):
    n = len(srcs)

    def body(*refs):
        ins, lnd = refs[:n], refs[n:2 * n]
        s_sems, r_sems = refs[2 * n], refs[2 * n + 1]
        for a in range(n):
            for k in PEER_ORDER:
                _exchange_copy(kind, k, a, ins, lnd, s_sems, r_sems).wait_send()
                _exchange_copy(kind, k, a, ins, lnd, s_sems, r_sems, arriving=True).wait_recv()

    out = pl.pallas_call(
        body, name=name,
        out_shape=(*[pltpu.HBM(t.shape, t.dtype) for t in srcs], *[pltpu.HBM(t.shape, t.dtype) for t in lands]),
        in_specs=[HBM] * (2 * n) + [SEM, SEM, ANY],
        out_specs=tuple([HBM] * (2 * n)),
        input_output_aliases={i: i for i in range(2 * n)},
        compiler_params=pltpu.CompilerParams(has_side_effects=EFFECT),
    )(*srcs, *lands, send_sems, recv_sems, after)
    return list(out[:n]), list(out[n:])


def _sum_parts(name, me, mine, parts, rows):
    _, R, C = mine.shape

    def body(me_ref, own_ref, p_ref, o_ref):
        acc = own_ref[0].astype(F32)
        for s in range(N_DEV - 1):
            acc = acc + p_ref[s].astype(F32)
        o_ref[...] = acc

    return pl.pallas_call(
        body, name=name,
        grid_spec=pltpu.PrefetchScalarGridSpec(
            num_scalar_prefetch=1, grid=(R // rows,),
            in_specs=[pl.BlockSpec((1, rows, C), lambda i, me_ref: (me_ref[0], i, 0)),
                      pl.BlockSpec((N_DEV - 1, rows, C), lambda i, me_ref: (0, i, 0))],
            out_specs=pl.BlockSpec((rows, C), lambda i, me_ref: (i, 0))),
        out_shape=jax.ShapeDtypeStruct((R, C), F32), compiler_params=_params(1),
    )(me, mine, parts)


def _sum_slots(name, a, rows):
    _, R, C = a.shape

    def body(a_ref, o_ref):
        acc = a_ref[0].astype(F32)
        for s in range(1, N_DEV):
            acc = acc + a_ref[s].astype(F32)
        o_ref[...] = acc

    return pl.pallas_call(
        body, name=name, grid=(R // rows,),
        in_specs=[pl.BlockSpec((N_DEV, rows, C), lambda i: (0, i, 0))],
        out_specs=pl.BlockSpec((rows, C), lambda i: (i, 0)),
        out_shape=jax.ShapeDtypeStruct((R, C), F32), compiler_params=_params(1),
    )(a)


_DIMS = {"nn": (((1,), (0,)), ((), ())), "nt": (((1,), (1,)), ((), ())), "tn": (((0,), (0,)), ((), ()))}


def _mm(name, mode, a_list, b_list, pairs, n_acc, M, N, K, tm, tn, tk, epilogue, out_dtypes, extras=(), after=None):
    tm, tn, tk = min(tm, M), min(tn, N), min(tk, K)
    assert M % tm == 0 and N % tn == 0 and K % tk == 0, (name, M, N, K, tm, tn, tk)
    nk = K // tk
    na, nb, ne, no = len(a_list), len(b_list), len(extras), len(out_dtypes)
    dims = _DIMS[mode]
    a_spec = (pl.BlockSpec((tk, tm), lambda j, i, k: (k, i)) if mode == "tn"
              else pl.BlockSpec((tm, tk), lambda j, i, k: (i, k)))
    b_spec = (pl.BlockSpec((tn, tk), lambda j, i, k: (j, k)) if mode == "nt"
              else pl.BlockSpec((tk, tn), lambda j, i, k: (k, j)))
    o_spec = pl.BlockSpec((tm, tn), lambda j, i, k: (i, j))
    tail = [] if after is None else [after]

    def body(*refs):
        a_refs, b_refs = refs[:na], refs[na:na + nb]
        e_refs = refs[na + nb:na + nb + ne]
        first_out = na + nb + ne + len(tail)
        o_refs = refs[first_out:first_out + no]
        acc_refs = refs[first_out + no:]

        def partial(oi):
            tot = None
            for ai, bi, ti in pairs:
                if ti == oi:
                    d = lax.dot_general(a_refs[ai][...], b_refs[bi][...], dims, preferred_element_type=F32)
                    tot = d if tot is None else tot + d
            return tot

        def finish(accs):
            outs = epilogue(accs, [e[...] for e in e_refs])
            for o_ref, o in zip(o_refs, outs):
                o_ref[...] = o.astype(o_ref.dtype)

        if nk == 1:
            finish([partial(oi) for oi in range(n_acc)])
        else:
            k = pl.program_id(2)

            @pl.when(k == 0)
            def _():
                for r in acc_refs:
                    r[...] = jnp.zeros_like(r)

            for oi in range(n_acc):
                acc_refs[oi][...] += partial(oi)

            @pl.when(k == nk - 1)
            def _():
                finish([r[...] for r in acc_refs])

    return pl.pallas_call(
        body, name=name, grid=(N // tn, M // tm, nk),
        in_specs=[a_spec] * na + [b_spec] * nb + [o_spec] * ne + [ANY] * len(tail),
        out_specs=[o_spec] * no,
        out_shape=[jax.ShapeDtypeStruct((M, N), dt) for dt in out_dtypes],
        scratch_shapes=[pltpu.VMEM((tm, tn), F32)] * (n_acc if nk > 1 else 0),
        compiler_params=_params(3),
    )(*a_list, *b_list, *extras, *tail)


def _ep_residual(accs, ex):
    return (ex[0] + accs[0],)


def _ep_rows_rms_bwd(acc, ex, rows):
    xv, res = ex
    r = lax.rsqrt(jnp.mean(xv * xv, axis=-1, keepdims=True) + EPS)
    xh = xv * r
    dg = jnp.sum(acc * xh, axis=0, keepdims=True)
    dxh = acc * rows[0]
    dx = r * (dxh - xh * jnp.mean(dxh * xh, axis=-1, keepdims=True)) + res
    return (dx, dx), (dg,)


def _sigmoid(x):
    return 0.5 * (jnp.tanh(0.5 * x) + 1.0)


def _ep_plain(accs, ex):
    return (accs[0],)


def _ep_swiglu(accs, ex):
    g, u = accs
    return g, u, (g * _sigmoid(g)) * u


def _ep_swiglu_bwd(accs, ex):
    dact = accs[0]
    g, u = ex
    sig = _sigmoid(g)
    silu = g * sig
    return dact * u * (sig * (1.0 + g * (1.0 - sig))), dact * silu


def _ep_two(accs, ex):
    return accs[0], accs[1]


ROWS = 256


def _row_spec(tm, width, col=0):
    return pl.BlockSpec((tm, width), lambda i: (i, col))


def _full_spec(shape):
    nd = len(shape)
    return pl.BlockSpec(shape, lambda i: (0,) * nd)


def _rms_fwd(name, x, g):
    T, D = x.shape
    tm = min(ROWS, T)

    def body(x_ref, g_ref, o_ref):
        xv = x_ref[...]
        r = lax.rsqrt(jnp.mean(xv * xv, axis=-1, keepdims=True) + EPS)
        o_ref[...] = ((xv * r) * g_ref[...]).astype(BF16)

    return pl.pallas_call(
        body, name=name, grid=(T // tm,),
        in_specs=[_row_spec(tm, D), _full_spec((1, D))], out_specs=_row_spec(tm, D),
        out_shape=jax.ShapeDtypeStruct((T, D), BF16), compiler_params=_params(1),
    )(x, g)


def _rms_bwd(name, dyn, x, g, res):
    T, D = x.shape
    tm = min(ROWS, T)

    def body(dyn_ref, x_ref, g_ref, res_ref, dx_ref, dx16_ref, dg_ref):
        outs, (dg,) = _ep_rows_rms_bwd(dyn_ref[...], [x_ref[...], res_ref[...]], [g_ref[...]])

        @pl.when(pl.program_id(0) == 0)
        def _():
            dg_ref[...] = jnp.zeros_like(dg_ref)

        dg_ref[...] += dg
        dx_ref[...] = outs[0]
        dx16_ref[...] = outs[1].astype(BF16)

    return pl.pallas_call(
        body, name=name, grid=(T // tm,),
        in_specs=[_row_spec(tm, D), _row_spec(tm, D), _full_spec((1, D)), _row_spec(tm, D)],
        out_specs=[_row_spec(tm, D), _row_spec(tm, D), _full_spec((1, D))],
        out_shape=[jax.ShapeDtypeStruct((T, D), F32), jax.ShapeDtypeStruct((T, D), BF16),
                   jax.ShapeDtypeStruct((1, D), F32)],
        compiler_params=_params(1),
    )(dyn, x, g, res)


def _outnorm_fwd(name, a, b, ga, gb):
    T, W = a.shape
    tm = min(ROWS, T)

    def body(a_ref, b_ref, ga_ref, gb_ref, o_ref):
        for src, gain, lo in ((a_ref, ga_ref, 0), (b_ref, gb_ref, W)):
            v = src[...]
            r = lax.rsqrt(jnp.mean(v * v, axis=-1, keepdims=True) + EPS)
            o_ref[:, lo:lo + W] = ((v * r) * gain[...]).astype(BF16)

    return pl.pallas_call(
        body, name=name, grid=(T // tm,),
        in_specs=[_row_spec(tm, W), _row_spec(tm, W), _full_spec((1, W)), _full_spec((1, W))],
        out_specs=_row_spec(tm, 2 * W),
        out_shape=jax.ShapeDtypeStruct((T, 2 * W), BF16), compiler_params=_params(1),
    )(a, b, ga, gb)


def _outnorm_bwd(name, dmixed, a, b, ga, gb):
    T, W = a.shape
    tm = min(ROWS, T)

    def body(dm_ref, a_ref, b_ref, ga_ref, gb_ref, da_ref, db_ref, dga_ref, dgb_ref):
        first = pl.program_id(0) == 0
        for src, gain, lo, dsrc, dgain in ((a_ref, ga_ref, 0, da_ref, dga_ref), (b_ref, gb_ref, W, db_ref, dgb_ref)):
            v, dv = src[...], dm_ref[:, lo:lo + W]
            r = lax.rsqrt(jnp.mean(v * v, axis=-1, keepdims=True) + EPS)
            vh = v * r

            @pl.when(first)
            def _():
                dgain[...] = jnp.zeros_like(dgain)

            dgain[...] += jnp.sum(dv * vh, axis=0, keepdims=True)
            dvh = dv * gain[...]
            dsrc[...] = r * (dvh - vh * jnp.mean(dvh * vh, axis=-1, keepdims=True))

    return pl.pallas_call(
        body, name=name, grid=(T // tm,),
        in_specs=[_row_spec(tm, 2 * W), _row_spec(tm, W), _row_spec(tm, W), _full_spec((1, W)), _full_spec((1, W))],
        out_specs=[_row_spec(tm, W), _row_spec(tm, W), _full_spec((1, W)), _full_spec((1, W))],
        out_shape=[jax.ShapeDtypeStruct((T, W), F32), jax.ShapeDtypeStruct((T, W), F32),
                   jax.ShapeDtypeStruct((1, W), F32), jax.ShapeDtypeStruct((1, W), F32)],
        compiler_params=_params(1),
    )(dmixed, a, b, ga, gb)


def _loss_head(name, y, target):
    T, D = y.shape
    tm = min(ROWS, T)

    def body(y_ref, t_ref, loss_ref, dy_ref, dy16_ref):
        d = y_ref[...] - t_ref[...]

        @pl.when(pl.program_id(0) == 0)
        def _():
            loss_ref[...] = jnp.zeros_like(loss_ref)

        per_token = jnp.mean(d * d, axis=-1, keepdims=True)
        loss_ref[...] += 0.5 * jnp.sum(per_token, axis=0, keepdims=True)
        dy = d * (1.0 / D)
        dy_ref[...] = dy
        dy16_ref[...] = dy.astype(BF16)

    return pl.pallas_call(
        body, name=name, grid=(T // tm,),
        in_specs=[_row_spec(tm, D), _row_spec(tm, D)],
        out_specs=[_full_spec((1, 1)), _row_spec(tm, D), _row_spec(tm, D)],
        out_shape=[jax.ShapeDtypeStruct((1, 1), F32), jax.ShapeDtypeStruct((T, D), F32),
                   jax.ShapeDtypeStruct((T, D), BF16)],
        compiler_params=_params(1),
    )(y, target)


def _bf16_dot(a, b, dims):
    return lax.dot_general(a.astype(BF16), b.astype(BF16), dims, preferred_element_type=F32)


@jax.custom_vjp
def _dot_nn(a, b):
    return _bf16_dot(a, b, _DIMS["nn"])


def _dot_nn_fwd(a, b):
    return _dot_nn(a, b), (a, b)


def _dot_nn_bwd(saved, ct):
    a, b = saved
    return _bf16_dot(ct, b, _DIMS["nt"]), _bf16_dot(a, ct, _DIMS["tn"])


_dot_nn.defvjp(_dot_nn_fwd, _dot_nn_bwd)


@jax.custom_vjp
def _dot_nt(a, b):
    return _bf16_dot(a, b, _DIMS["nt"])


def _dot_nt_fwd(a, b):
    return _dot_nt(a, b), (a, b)


def _dot_nt_bwd(saved, ct):
    a, b = saved
    return _bf16_dot(ct, b, _DIMS["nn"]), _bf16_dot(ct, a, _DIMS["tn"])


_dot_nt.defvjp(_dot_nt_fwd, _dot_nt_bwd)


def _iota(shape, dim):
    return lax.broadcasted_iota(jnp.int32, shape, dim)


def _head_sum_impl(x):
    same_head = (_iota((LANES, LANES), 0) // HEAD_DIM == _iota((LANES, LANES), 1) // HEAD_DIM).astype(BF16)
    pieces = []
    for i in range(x.shape[1] // LANES):
        xs = x[:, i * LANES:(i + 1) * LANES]
        hi = xs.astype(BF16)
        r1 = xs - hi.astype(F32)
        mid = r1.astype(BF16)
        lo = (r1 - mid.astype(F32)).astype(BF16)
        tot = None
        for part in (hi, mid, lo):
            d = lax.dot_general(part, same_head, _DIMS["nn"], preferred_element_type=F32)
            tot = d if tot is None else tot + d
        pieces.append(tot)
    return pieces[0] if len(pieces) == 1 else jnp.concatenate(pieces, axis=1)


@jax.custom_vjp
def _head_sum(x):
    return _head_sum_impl(x)


_head_sum.defvjp(lambda x: (_head_sum_impl(x), None), lambda _, ct: (_head_sum_impl(ct),))


def _head_rms(x, g_row):
    ms = _head_sum(x * x) * (1.0 / HEAD_DIM)
    return (x * lax.rsqrt(ms + EPS)) * g_row


@jax.custom_vjp
def _swap_halves(x):
    return pltpu.roll(x, HEAD_DIM, 1)


_swap_halves.defvjp(lambda x: (pltpu.roll(x, HEAD_DIM, 1), None), lambda _, ct: (pltpu.roll(ct, HEAD_DIM, 1),))


def _gelu(x):
    return 0.5 * x * (1.0 + lax.erf(x * (1.0 / math.sqrt(2.0))))


def _sgu_block(u_raw, v_raw, g_row, w, b_full):
    u, v = _gelu(u_raw), _gelu(v_raw)
    vn = _head_rms(v, g_row)
    causal = _iota((CHUNK, CHUNK), 0) >= _iota((CHUNK, CHUNK), 1)
    low_half = _iota((CHUNK, LANES), 1) < HEAD_DIM
    gates = []
    for p in range(v.shape[1] // LANES):
        vp = vn[:, p * LANES:(p + 1) * LANES]
        g0 = _dot_nn(jnp.where(causal, w[2 * p], 0.0), vp)
        g1 = _dot_nn(jnp.where(causal, w[2 * p + 1], 0.0), vp)
        gates.append(jnp.where(low_half, g0, g1))
    gate = jnp.concatenate(gates, axis=1) + b_full
    return u * gate


def _attn_block(q_raw, k_prev, k_cur, v_prev, v_cur, qg_row, kg_row, sinks, bias, blk):
    scale = 1.0 / math.sqrt(HEAD_DIM)
    n_q_heads = q_raw.shape[1] // HEAD_DIM
    qn = _head_rms(q_raw, qg_row)
    kn = _head_rms(jnp.concatenate([k_prev, k_cur], axis=0), kg_row)
    vb = jnp.concatenate([v_prev, v_cur], axis=0)
    qi = _iota((GROUP * BLOCK, 2 * BLOCK), 0) & (BLOCK - 1)
    kj = _iota((GROUP * BLOCK, 2 * BLOCK), 1)
    dist = qi + BLOCK - kj
    valid = (dist >= 0) & (dist < BLOCK) & (kj + (blk * BLOCK - BLOCK) >= 0)
    low_half = _iota((BLOCK, LANES), 1) < HEAD_DIM
    heads = [None] * n_q_heads
    for kv in range(n_q_heads // GROUP):
        grp, kv_low = kv // 2, kv % 2 == 0
        k2 = kn[:, grp * LANES:(grp + 1) * LANES]
        v2 = vb[:, grp * LANES:(grp + 1) * LANES]
        keep = low_half if kv_low else ~low_half
        stacked = []
        for g in range(GROUP):
            h = GROUP * kv + g
            q2 = qn[:, (h // 2) * LANES:(h // 2 + 1) * LANES]
            src = q2 if (h % 2 == 0) == kv_low else _swap_halves(q2)
            stacked.append(jnp.where(keep, src, 0.0))
        q4 = jnp.concatenate(stacked, axis=0)
        bias4 = jnp.concatenate([bias[GROUP * kv + g] for g in range(GROUP)], axis=0)
        sink4 = jnp.concatenate([jnp.broadcast_to(sinks[GROUP * kv + g], (BLOCK, 1)) for g in range(GROUP)], axis=0)
        s = _dot_nt(q4, k2) * scale + bias4
        s = jnp.where(valid, s, NEG_INF)
        m = lax.stop_gradient(jnp.maximum(jnp.max(s, axis=1, keepdims=True), sink4))
        e = jnp.exp(s - m)
        denom = jnp.sum(e, axis=1, keepdims=True) + jnp.exp(sink4 - m)
        o4 = _dot_nn(e * (1.0 / denom), v2)
        for g in range(GROUP):
            h = GROUP * kv + g
            o = o4[g * BLOCK:(g + 1) * BLOCK]
            heads[h] = o if (h % 2 == 0) == kv_low else _swap_halves(o)
    outs = [jnp.where(low_half, heads[2 * p], heads[2 * p + 1]) for p in range(n_q_heads // 2)]
    return jnp.concatenate(outs, axis=1)


def _bias_table():
    dist = np.arange(BLOCK)[:, None] + BLOCK - np.arange(2 * BLOCK)[None, :]
    n = np.maximum(dist, 0)
    max_exact = NUM_BUCKETS // 2
    nf = np.maximum(n, 1).astype(np.float64)
    large = max_exact + (np.log(nf / max_exact) / math.log(MAX_DISTANCE / max_exact) * (NUM_BUCKETS - max_exact)).astype(np.int32)
    large = np.minimum(large, NUM_BUCKETS - 1)
    return np.where(n < max_exact, n, large).astype(np.int32)


def _bias_fwd(name, rel_bias, buckets):
    nb_, nh = rel_bias.shape

    def body(rb_ref, bk_ref, o_ref):
        bk = bk_ref[...]
        for h in range(nh):
            acc = jnp.zeros(bk.shape, F32)
            for b in range(nb_):
                acc = jnp.where(bk == b, rb_ref[b, h], acc)
            o_ref[h] = acc

    return pl.pallas_call(
        body, name=name,
        in_specs=[pl.BlockSpec(memory_space=pltpu.SMEM), pl.BlockSpec(memory_space=pltpu.VMEM)],
        out_specs=pl.BlockSpec(memory_space=pltpu.VMEM),
        out_shape=jax.ShapeDtypeStruct((nh,) + buckets.shape, F32),
    )(rel_bias, buckets)


def _bias_bwd(name, dbias, buckets, nb_):
    nh = dbias.shape[0]

    def body(db_ref, bk_ref, o_ref):
        bk = bk_ref[...]
        for h in range(nh):
            d = db_ref[h]
            for b in range(nb_):
                s = jnp.sum(jnp.where(bk == b, d, 0.0), axis=0, keepdims=True)
                s = jnp.sum(s, axis=1, keepdims=True)
                o_ref[b * nh + h:b * nh + h + 1, :] = jnp.broadcast_to(s, (1, LANES))

    return pl.pallas_call(
        body, name=name,
        in_specs=[pl.BlockSpec(memory_space=pltpu.VMEM), pl.BlockSpec(memory_space=pltpu.VMEM)],
        out_specs=pl.BlockSpec(memory_space=pltpu.VMEM),
        out_shape=jax.ShapeDtypeStruct((nb_ * nh, LANES), F32),
    )(dbias, buckets)


def _sgu_fwd(name, z, g_row, w, b_full, W):
    T = z.shape[0]

    def body(u_ref, v_ref, g_ref, w_ref, b_ref, o_ref):
        o_ref[...] = _sgu_block(u_ref[...], v_ref[...], g_ref[...], w_ref[...], b_ref[...])

    return pl.pallas_call(
        body, name=name, grid=(T // CHUNK,),
        in_specs=[_row_spec(CHUNK, W, 0), _row_spec(CHUNK, W, 1), _full_spec((1, W)),
                  _full_spec(w.shape), _full_spec((CHUNK, W))],
        out_specs=_row_spec(CHUNK, W),
        out_shape=jax.ShapeDtypeStruct((T, W), F32), compiler_params=_params(1),
    )(z, z, g_row, w, b_full)


def _sgu_bwd(name, z, g_row, w, b_full, d_out, W):
    T = z.shape[0]

    def body(u_ref, v_ref, g_ref, w_ref, b_ref, do_ref, dz_ref, dg_ref, dw_ref, db_ref):
        _, vjp = jax.vjp(_sgu_block, u_ref[...], v_ref[...], g_ref[...], w_ref[...], b_ref[...])
        du, dv, dg, dw, db = vjp(do_ref[...])
        dz_ref[:, :W] = du.astype(BF16)
        dz_ref[:, W:] = dv.astype(BF16)
        i = pl.program_id(0)

        @pl.when(i == 0)
        def _():
            dg_ref[...] = jnp.zeros_like(dg_ref)
            dw_ref[...] = jnp.zeros_like(dw_ref)
            db_ref[...] = jnp.zeros_like(db_ref)

        dg_ref[...] += dg
        dw_ref[...] += dw
        db_ref[...] += db

        @pl.when(i == pl.num_programs(0) - 1)
        def _():
            db_ref[...] = _head_sum_impl(db_ref[...])

    return pl.pallas_call(
        body, name=name, grid=(T // CHUNK,),
        in_specs=[_row_spec(CHUNK, W, 0), _row_spec(CHUNK, W, 1), _full_spec((1, W)),
                  _full_spec(w.shape), _full_spec((CHUNK, W)), _row_spec(CHUNK, W)],
        out_specs=[_row_spec(CHUNK, 2 * W), _full_spec((1, W)), _full_spec(w.shape), _full_spec((CHUNK, W))],
        out_shape=[jax.ShapeDtypeStruct((T, 2 * W), BF16), jax.ShapeDtypeStruct((1, W), F32),
                   jax.ShapeDtypeStruct(w.shape, F32), jax.ShapeDtypeStruct((CHUNK, W), F32)],
        compiler_params=_params(1),
    )(z, z, g_row, w, b_full, d_out)


def _attn_specs(WQ, WKV, q_col, k_col, v_col, blk_of):
    prev_of = lambda i: jnp.maximum(blk_of(i) - 1, 0)
    return [pl.BlockSpec((BLOCK, WQ), lambda i: (blk_of(i), q_col)),
            pl.BlockSpec((BLOCK, WKV), lambda i: (prev_of(i), k_col)),
            pl.BlockSpec((BLOCK, WKV), lambda i: (blk_of(i), k_col)),
            pl.BlockSpec((BLOCK, WKV), lambda i: (prev_of(i), v_col)),
            pl.BlockSpec((BLOCK, WKV), lambda i: (blk_of(i), v_col))]


def _attn_fwd(name, z, qg_row, kg_row, sinks_col, bias, WQ, WKV, q_col, k_col, v_col):
    T = z.shape[0]
    nh = sinks_col.shape[0]

    def body(q_ref, kp_ref, kc_ref, vp_ref, vc_ref, qg_ref, kg_ref, s_ref, b_ref, o_ref):
        sinks = [s_ref[h:h + 1, :] for h in range(nh)]
        o_ref[...] = _attn_block(q_ref[...], kp_ref[...], kc_ref[...], vp_ref[...], vc_ref[...],
                                 qg_ref[...], kg_ref[...], sinks, b_ref[...], pl.program_id(0))

    return pl.pallas_call(
        body, name=name, grid=(T // BLOCK,),
        in_specs=_attn_specs(WQ, WKV, q_col, k_col, v_col, lambda i: i)
        + [_full_spec((1, WQ)), _full_spec((1, WKV)), _full_spec((nh, 1)), _full_spec(bias.shape)],
        out_specs=_row_spec(BLOCK, WQ),
        out_shape=jax.ShapeDtypeStruct((T, WQ), F32), compiler_params=_params(1),
    )(z, z, z, z, z, qg_row, kg_row, sinks_col, bias)


def _attn_bwd(name, z, qg_row, kg_row, sinks_col, bias, d_out, WQ, WKV, q_col, k_col, v_col):
    T = z.shape[0]
    nblk = T // BLOCK
    nh = sinks_col.shape[0]
    blk_of = lambda i: nblk - 1 - i

    def body(q_ref, kp_ref, kc_ref, vp_ref, vc_ref, qg_ref, kg_ref, s_ref, b_ref, do_ref,
             dq_ref, dk_ref, dv_ref, dqg_ref, dkg_ref, ds_ref, db_ref, carry_k, carry_v):
        i = pl.program_id(0)
        blk = nblk - 1 - i
        sinks = [s_ref[h:h + 1, :] for h in range(nh)]
        fn = lambda q, kp, kc, vp, vc, qg, kg, sk, bs: _attn_block(q, kp, kc, vp, vc, qg, kg, sk, bs, blk)
        _, vjp = jax.vjp(fn, q_ref[...], kp_ref[...], kc_ref[...], vp_ref[...], vc_ref[...],
                         qg_ref[...], kg_ref[...], sinks, b_ref[...])
        dq, dkp, dkc, dvp, dvc, dqg, dkg, dsk, dbs = vjp(do_ref[...])

        @pl.when(i == 0)
        def _():
            carry_k[...] = jnp.zeros_like(carry_k)
            carry_v[...] = jnp.zeros_like(carry_v)
            dqg_ref[...] = jnp.zeros_like(dqg_ref)
            dkg_ref[...] = jnp.zeros_like(dkg_ref)
            ds_ref[...] = jnp.zeros_like(ds_ref)
            db_ref[...] = jnp.zeros_like(db_ref)

        dq_ref[...] = dq.astype(BF16)
        dk_ref[...] = (dkc + carry_k[...]).astype(BF16)
        dv_ref[...] = (dvc + carry_v[...]).astype(BF16)
        carry_k[...] = dkp
        carry_v[...] = dvp
        dqg_ref[...] += dqg
        dkg_ref[...] += dkg
        for h in range(nh):
            ds_ref[h:h + 1, :] += dsk[h]
        db_ref[...] += dbs

    return pl.pallas_call(
        body, name=name, grid=(nblk,),
        in_specs=_attn_specs(WQ, WKV, q_col, k_col, v_col, blk_of)
        + [_full_spec((1, WQ)), _full_spec((1, WKV)), _full_spec((nh, 1)), _full_spec(bias.shape),
           pl.BlockSpec((BLOCK, WQ), lambda i: (blk_of(i), 0))],
        out_specs=[pl.BlockSpec((BLOCK, WQ), lambda i: (blk_of(i), 0)),
                   pl.BlockSpec((BLOCK, WKV), lambda i: (blk_of(i), 0)),
                   pl.BlockSpec((BLOCK, WKV), lambda i: (blk_of(i), 0)),
                   _full_spec((1, WQ)), _full_spec((1, WKV)), _full_spec((nh, 1)), _full_spec(bias.shape)],
        out_shape=[jax.ShapeDtypeStruct((T, WQ), BF16), jax.ShapeDtypeStruct((T, WKV), BF16),
                   jax.ShapeDtypeStruct((T, WKV), BF16), jax.ShapeDtypeStruct((1, WQ), F32),
                   jax.ShapeDtypeStruct((1, WKV), F32), jax.ShapeDtypeStruct((nh, 1), F32),
                   jax.ShapeDtypeStruct(bias.shape, F32)],
        scratch_shapes=[pltpu.VMEM((BLOCK, WKV), F32), pltpu.VMEM((BLOCK, WKV), F32)],
        compiler_params=_params(1),
    )(z, z, z, z, z, qg_row, kg_row, sinks_col, bias, d_out)


def _adamw(name, w, g, m, v):
    shape = w.shape
    C = shape[-1]
    if w.ndim == 3 and shape[1] % ROWS == 0:
        work = shape
        grid = (shape[0], shape[1] // ROWS)
        spec = pl.BlockSpec((1, ROWS, C), lambda l, i: (l, i, 0))
    else:
        R = int(np.prod(shape[:-1]))
        tr = ROWS if R % ROWS == 0 else R
        work = (R, C)
        grid = (R // tr,)
        spec = pl.BlockSpec((tr, C), lambda i: (i, 0))
    w2, g2, m2, v2 = (t.reshape(work) for t in (w, g, m, v))

    def body(w_ref, g_ref, m_ref, v_ref, d_ref, nm_ref, nv_ref):
        gv = g_ref[...]
        nm = ADAM_B1 * m_ref[...] + (1.0 - ADAM_B1) * gv
        nv = ADAM_B2 * v_ref[...] + (1.0 - ADAM_B2) * (gv * gv)
        m_hat = nm / (1.0 - ADAM_B1 ** ADAM_STEP)
        v_hat = nv / (1.0 - ADAM_B2 ** ADAM_STEP)
        d_ref[...] = -ADAM_LR * (m_hat / (jnp.sqrt(v_hat) + ADAM_EPS) + ADAM_WD * w_ref[...])
        nm_ref[...] = nm
        nv_ref[...] = nv

    outs = pl.pallas_call(
        body, name=name, grid=grid, in_specs=[spec] * 4, out_specs=[spec] * 3,
        out_shape=[jax.ShapeDtypeStruct(work, F32)] * 3, compiler_params=_params(len(grid)),
    )(w2, g2, m2, v2)
    return tuple(o.reshape(shape) for o in outs)


def _pad_rows(flat):
    n = flat.shape[0]
    tile = 8 * LANES
    padded = -(-n // tile) * tile
    return jnp.pad(flat, (0, padded - n)).reshape(padded // LANES, LANES)


def kernel(x, rel_bias, norm1_g, w_in, sgu_norm_g, sgu_w, sgu_b, q_norm_g, k_norm_g, sinks, out_norm_a, out_norm_b, w_out, norm2_g, w_gate, w_up, w_down, loss_target, m_rel_bias, m_norm1_g, m_w_in, m_sgu_norm_g, m_sgu_w, m_sgu_b, m_q_norm_g, m_k_norm_g, m_sinks, m_out_norm_a, m_out_norm_b, m_w_out, m_norm2_g, m_w_gate, m_w_up, m_w_down, v_rel_bias, v_norm1_g, v_w_in, v_sgu_norm_g, v_sgu_w, v_sgu_b, v_q_norm_g, v_k_norm_g, v_sinks, v_out_norm_a, v_out_norm_b, v_w_out, v_norm2_g, v_w_gate, v_w_up, v_w_down):
    L = w_in.shape[0]
    T, D = x.shape[1], x.shape[2]
    W = D // 2
    NH = W // HEAD_DIM
    WKV = N_KV_HEADS * HEAD_DIM
    IN = N_DEV * w_in.shape[2]
    FF = N_DEV * w_gate.shape[2]
    assert IN == 2 * W + W + 2 * WKV and NH // N_KV_HEADS == GROUP
    q_col, k_col, v_col = 2 * W // W, (3 * W) // WKV, (3 * W + WKV) // WKV
    x0 = x.reshape(T, D)
    target = loss_target.reshape(T, D)

    shards = [jnp.swapaxes(w_in, 1, 2).astype(BF16), w_out.astype(BF16), jnp.swapaxes(w_gate, 1, 2).astype(BF16),
              jnp.swapaxes(w_up, 1, 2).astype(BF16), w_down.astype(BF16)]
    me = 4 * lax.axis_index("x") + 2 * lax.axis_index("y") + lax.axis_index("c")

    def landing(block):
        return lax.dynamic_update_slice(lax.empty((N_DEV,) + block.shape, block.dtype), block[None], (me, 0, 0))

    def as_matrices(got):
        return [g.reshape(N_DEV * g.shape[1], g.shape[2]) for g in got]

    full = {0: as_matrices(_all_gather("gather_weights_0", [s[0] for s in shards]))}
    flight = {}

    def start_gather(l, after):
        srcs = [s[l] for s in shards]
        flight[l] = _exchange_start(f"gather_start_{l}", "gather", srcs, [landing(s) for s in srcs], after)
        return flight[l][4]

    def finish_gather(l, after):
        s_sem, r_sem, srcs, lands, _ = flight.pop(l)
        full[l] = as_matrices(_exchange_wait(f"gather_wait_{l}", "gather", s_sem, r_sem, srcs, lands, after)[1])

    buckets = jnp.asarray(_bias_table())
    bias = _bias_fwd("bias_table", rel_bias, buckets)

    saved = []
    xl = x0
    for l in range(L):
        if l > 0:
            finish_gather(l, xl)
        token = start_gather(l + 1, full[l][0] if l == 0 else xl) if l + 1 < L else xl
        w_in_t, w_o, w_g_t, w_u_t, w_d = full[l]
        g1, g2 = norm1_g[l][None], norm2_g[l][None]
        sg_row = sgu_norm_g[l].reshape(1, W)
        b_full = jnp.repeat(sgu_b[l].T, HEAD_DIM, axis=1)
        qg_row = jnp.tile(q_norm_g[l], NH)[None]
        kg_row = jnp.tile(k_norm_g[l], N_KV_HEADS)[None]
        sinks_col = sinks[l][:, None]
        ga, gb = out_norm_a[l][None], out_norm_b[l][None]

        h = _rms_fwd(f"norm1_{l}", xl, g1)
        (z,) = _mm(f"proj_in_{l}", "nt", [h], [w_in_t], [(0, 0, 0)], 1, T, IN, D, 1024, 896, D, _ep_plain, [F32],
                   after=token)
        out_a = _sgu_fwd(f"sgu_{l}", z, sg_row, sgu_w[l], b_full, W)
        out_b = _attn_fwd(f"attn_{l}", z, qg_row, kg_row, sinks_col, bias, W, WKV, q_col, k_col, v_col)
        mixed = _outnorm_fwd(f"outnorm_{l}", out_a, out_b, ga, gb)
        (x1,) = _mm(f"proj_out_{l}", "nn", [mixed], [w_o], [(0, 0, 0)], 1, T, D, D, 1024, 1024, D,
                    _ep_residual, [F32], extras=[xl])
        h2 = _rms_fwd(f"norm2_{l}", x1, g2)
        gate, up, act = _mm(f"mlp_in_{l}", "nt", [h2], [w_g_t, w_u_t], [(0, 0, 0), (0, 1, 1)], 2, T, FF, D,
                            1024, 512, D, _ep_swiglu, [BF16, BF16, BF16])
        (x2,) = _mm(f"mlp_out_{l}", "nn", [act], [w_d], [(0, 0, 0)], 1, T, D, FF, 512, 1024, FF,
                    _ep_residual, [F32], extras=[x1])
        saved.append((xl, h, z, out_a, out_b, mixed, x1, h2, gate, up, act,
                      g1, g2, sg_row, b_full, qg_row, kg_row, sinks_col, ga, gb))
        xl = x2

    loss_part, dy, dy16 = _loss_head("loss_head", xl, target)

    dbias = None
    small = [None] * L
    scatters = []

    def start_scatter(name, which, grads_t, after):
        srcs = [t.reshape(N_DEV, t.shape[0] // N_DEV, D) for t in grads_t]
        lands = [lax.empty((N_DEV - 1,) + s.shape[1:], BF16) for s in srcs]
        s_sem, r_sem, srcs, lands, tok = _exchange_start(name, "scatter", srcs, lands, after)
        scatters.append((name, which, s_sem, r_sem, srcs, lands))
        return tok

    token = None
    for l in reversed(range(L)):
        w_in_t, w_o, w_g_t, w_u_t, w_d = full[l]
        (xl, h, z, out_a, out_b, mixed, x1, h2, gate, up, act,
         g1, g2, sg_row, b_full, qg_row, kg_row, sinks_col, ga, gb) = saved[l]

        dgate, dup = _mm(f"d_mlp_out_{l}", "nt", [dy16], [w_d], [(0, 0, 0)], 1, T, FF, D, 1024, 1408, D,
                         _ep_swiglu_bwd, [BF16, BF16], extras=[gate, up], after=token)
        (dw_d,) = _mm(f"dw_down_{l}", "tn", [act], [dy16], [(0, 0, 0)], 1, FF, D, T, 1408, 1024, 2048, _ep_plain, [BF16])
        (dh2,) = _mm(f"d_mlp_in_{l}", "nn", [dgate, dup], [w_g_t, w_u_t], [(0, 0, 0), (1, 1, 0)], 1, T, D, FF,
                     512, 512, FF, _ep_plain, [F32])
        dx1, dx1_16, dg2 = _rms_bwd(f"d_norm2_{l}", dh2, x1, g2, dy)
        dw_g, dw_u = _mm(f"dw_gate_up_{l}", "tn", [dgate, dup], [h2], [(0, 0, 0), (1, 0, 1)], 2, FF, D, T,
                         1408, 1024, 1024, _ep_two, [BF16, BF16])
        token = start_scatter(f"scatter_mlp_start_{l}", (l, (4, 2, 3)), [dw_d, dw_g, dw_u], dw_u)
        (dmixed,) = _mm(f"d_proj_out_{l}", "nt", [dx1_16], [w_o], [(0, 0, 0)], 1, T, D, D, 1024, 1024, D, _ep_plain, [F32],
                        after=token)
        (dw_o,) = _mm(f"dw_out_{l}", "tn", [mixed], [dx1_16], [(0, 0, 0)], 1, D, D, T, 1024, 1024, 2048, _ep_plain, [BF16])
        d_a, d_b, dga, dgb = _outnorm_bwd(f"d_outnorm_{l}", dmixed, out_a, out_b, ga, gb)
        dz_uv, dsg, dsw, dsb = _sgu_bwd(f"d_sgu_{l}", z, sg_row, sgu_w[l], b_full, d_a, W)
        dq, dk, dv, dqg, dkg, dsk, dbs = _attn_bwd(f"d_attn_{l}", z, qg_row, kg_row, sinks_col, bias, d_b,
                                                   W, WKV, q_col, k_col, v_col)
        dbias = dbs if dbias is None else dbias + dbs
        dz = jnp.concatenate([dz_uv, dq, dk, dv], axis=1)
        (dh,) = _mm(f"d_proj_in_{l}", "nn", [dz], [w_in_t], [(0, 0, 0)], 1, T, D, IN, 1024, 1024, IN, _ep_plain, [F32])
        dy, dy16, dg1 = _rms_bwd(f"d_norm1_{l}", dh, xl, g1, dx1)
        (dw_i,) = _mm(f"dw_in_{l}", "tn", [dz], [h], [(0, 0, 0)], 1, IN, D, T, 896, 1024, 2048, _ep_plain, [BF16])
        token = start_scatter(f"scatter_mix_start_{l}", (l, (1, 0)), [dw_o, dw_i], dw_i)

        small[l] = dict(norm1_g=dg1[0], sgu_norm_g=dsg.reshape(NH, HEAD_DIM), sgu_w=dsw,
                        sgu_b=dsb[:, ::HEAD_DIM].T, q_norm_g=dqg.reshape(NH, HEAD_DIM).sum(0),
                        k_norm_g=dkg.reshape(N_KV_HEADS, HEAD_DIM).sum(0), sinks=dsk[:, 0],
                        out_norm_a=dga[0], out_norm_b=dgb[0], norm2_g=dg2[0])

    grad_x = dy.reshape(x.shape)
    d_rel = _bias_bwd("d_bias_table", dbias, buckets, NUM_BUCKETS)[:, 0].reshape(NUM_BUCKETS, NH)

    names = ["norm1_g", "sgu_norm_g", "sgu_w", "sgu_b", "q_norm_g", "k_norm_g", "sinks", "out_norm_a", "out_norm_b", "norm2_g"]
    parts = {"rel_bias": d_rel}
    for nme in names:
        parts[nme] = jnp.stack([small[l][nme] for l in range(L)])
    order = ["rel_bias"] + names
    packed = jnp.concatenate([_pad_rows(parts[nme].reshape(-1)) for nme in order], axis=0)
    small_flight = _exchange_start("gather_small_start", "gather", [packed], [landing(packed)], token)
    after = small_flight[4]

    grads_big = {}
    for name, (l, which), s_sem, r_sem, srcs, lands in scatters:
        srcs, lands = _exchange_wait(name.replace("start", "wait"), "scatter", s_sem, r_sem, srcs, lands, after)
        for i, src, land in zip(which, srcs, lands):
            rows = src.shape[1]
            after = _sum_parts(f"sum_grads_{l}_{i}", me.reshape(1).astype(jnp.int32), src, land,
                               64 if rows % 64 == 0 else rows)
            grads_big[i, l] = after
    g_w_in = jnp.swapaxes(jnp.stack([grads_big[0, l] for l in range(L)]), 1, 2)
    g_w_out = jnp.stack([grads_big[1, l] for l in range(L)])
    g_w_gate = jnp.swapaxes(jnp.stack([grads_big[2, l] for l in range(L)]), 1, 2)
    g_w_up = jnp.swapaxes(jnp.stack([grads_big[3, l] for l in range(L)]), 1, 2)
    g_w_down = jnp.stack([grads_big[4, l] for l in range(L)])

    s_sem, r_sem, srcs, lands, _ = small_flight
    _, (everyone,) = _exchange_wait("gather_small_wait", "gather", s_sem, r_sem, srcs, lands, after)
    rows = packed.shape[0]
    summed = _sum_slots("sum_small_grads", everyone, 64 if rows % 64 == 0 else 8)
    g_small, at = {}, 0
    for nme in order:
        n = int(np.prod(parts[nme].shape))
        n_rows = -(-n // (8 * LANES)) * 8
        g_small[nme] = summed[at:at + n_rows].reshape(-1)[:n].reshape(parts[nme].shape)
        at += n_rows

    loss = lax.psum(loss_part[0, 0], ("x", "y", "c"))

    grads = dict(g_small, w_in=g_w_in, w_out=g_w_out, w_gate=g_w_gate, w_up=g_w_up, w_down=g_w_down)
    weights = dict(rel_bias=rel_bias, norm1_g=norm1_g, w_in=w_in, sgu_norm_g=sgu_norm_g, sgu_w=sgu_w, sgu_b=sgu_b,
                   q_norm_g=q_norm_g, k_norm_g=k_norm_g, sinks=sinks, out_norm_a=out_norm_a, out_norm_b=out_norm_b,
                   w_out=w_out, norm2_g=norm2_g, w_gate=w_gate, w_up=w_up, w_down=w_down)
    ms = dict(rel_bias=m_rel_bias, norm1_g=m_norm1_g, w_in=m_w_in, sgu_norm_g=m_sgu_norm_g, sgu_w=m_sgu_w, sgu_b=m_sgu_b,
              q_norm_g=m_q_norm_g, k_norm_g=m_k_norm_g, sinks=m_sinks, out_norm_a=m_out_norm_a, out_norm_b=m_out_norm_b,
              w_out=m_w_out, norm2_g=m_norm2_g, w_gate=m_w_gate, w_up=m_w_up, w_down=m_w_down)
    vs = dict(rel_bias=v_rel_bias, norm1_g=v_norm1_g, w_in=v_w_in, sgu_norm_g=v_sgu_norm_g, sgu_w=v_sgu_w, sgu_b=v_sgu_b,
              q_norm_g=v_q_norm_g, k_norm_g=v_k_norm_g, sinks=v_sinks, out_norm_a=v_out_norm_a, out_norm_b=v_out_norm_b,
              w_out=v_w_out, norm2_g=v_norm2_g, w_gate=v_w_gate, w_up=v_w_up, w_down=v_w_down)
    all_names = ["rel_bias", "norm1_g", "w_in", "sgu_norm_g", "sgu_w", "sgu_b", "q_norm_g", "k_norm_g", "sinks",
                 "out_norm_a", "out_norm_b", "w_out", "norm2_g", "w_gate", "w_up", "w_down"]
    deltas, new_m, new_v = [], [], []
    for nme in all_names:
        d, nm, nv = _adamw(f"adamw_{nme}", weights[nme], grads[nme], ms[nme], vs[nme])
        deltas.append(d)
        new_m.append(nm)
        new_v.append(nv)
    return (loss, grad_x, *[grads[nme] for nme in all_names], *deltas, *new_m, *new_v)
```

```python
import functools
import math

import numpy as np
import jax
import jax.numpy as jnp
from jax import lax
from jax.experimental import pallas as pl
from jax.experimental.pallas import tpu as pltpu

F32 = jnp.float32
BF16 = jnp.bfloat16

N_DEV = 8
HEAD_DIM = 64
CHUNK = 128
BLOCK = 128
N_KV_HEADS = 4
GROUP = 4
NUM_BUCKETS = 32
MAX_DISTANCE = 128
EPS = 1e-6
NEG_INF = -1e30
LANES = 128
VMEM_LIMIT = 56 * 2 ** 20

ADAM_LR = 0.001
ADAM_B1 = 0.9
ADAM_B2 = 0.999
ADAM_EPS = 1e-08
ADAM_WD = 0.01
ADAM_STEP = 10

MESH = pl.DeviceIdType.MESH
ANY = pl.BlockSpec(memory_space=pl.ANY)
HBM = pl.BlockSpec(memory_space=pltpu.HBM)
SEM = pl.BlockSpec(memory_space=pltpu.SEMAPHORE)
EFFECT = pltpu.SideEffectType.DATAFLOW_SIDE_EFFECTING


def _params(n_axes):
    return pltpu.CompilerParams(dimension_semantics=("arbitrary",) * n_axes, vmem_limit_bytes=VMEM_LIMIT)


def _my_place():
    return lax.axis_index("x"), lax.axis_index("y"), lax.axis_index("c")


def _all_gather(name, arrs):
    n = len(arrs)

    def body(*refs):
        ins, outs = refs[:n], refs[n:2 * n]
        send_sems, recv_sems, local_sems = refs[2 * n:]
        x, y, c = _my_place()
        sibling = (x, y, 1 - c)
        chips = [(1 - x, y), (x, 1 - y), (1 - x, 1 - y)]

        def slot(a, px, py, pc):
            return outs[a].at[4 * px + 2 * py + pc]

        def copy(a, k, block, to, src=None):
            return pltpu.make_async_remote_copy(
                src_ref=slot(a, *block) if src is None else src, dst_ref=slot(a, *block),
                send_sem=send_sems.at[7 * a + k], recv_sem=recv_sems.at[7 * a + k],
                device_id=to, device_id_type=MESH)

        started = []
        for a in range(n):
            mine = pltpu.make_async_copy(ins[a], slot(a, x, y, c), local_sems.at[a])
            mine.start()
            started.append(mine)
        sends = []
        for a in range(n):
            first = [copy(a, 0, (x, y, c), sibling, src=ins[a])]
            first += [copy(a, 1 + j, (x, y, c), (*chip, c), src=ins[a]) for j, chip in enumerate(chips)]
            for cp in first:
                cp.start()
            sends += first
        for a in range(n):
            for j, chip in enumerate(chips):
                copy(a, 1 + j, (*chip, c), (x, y, c)).wait_recv()
                fwd = copy(a, 4 + j, (*chip, c), sibling)
                fwd.start()
                sends.append(fwd)
        for a in range(n):
            copy(a, 0, sibling, (x, y, c)).wait_recv()
            for j, chip in enumerate(chips):
                copy(a, 4 + j, (*chip, 1 - c), (x, y, c)).wait_recv()
        for cp in sends:
            cp.wait_send()
        for cp in started:
            cp.wait()

    return pl.pallas_call(
        body, name=name,
        out_shape=[jax.ShapeDtypeStruct((N_DEV,) + a.shape, a.dtype) for a in arrs],
        in_specs=[ANY] * n, out_specs=[ANY] * n,
        scratch_shapes=[pltpu.SemaphoreType.DMA((7 * n,)), pltpu.SemaphoreType.DMA((7 * n,)),
                        pltpu.SemaphoreType.DMA((n,))],
    )(*arrs)


def _peer(k, x, y, c):
    return (1 - x if k & 4 else x), (1 - y if k & 2 else y), (1 - c if k & 1 else c)


PEER_ORDER = (1, 2, 4, 3, 5, 6, 7)


def _exchange_copy(kind, k, a, srcs, lands, send_sems, recv_sems, arriving=False):
    x, y, c = _my_place()
    me = 4 * x + 2 * y + c
    px, py, pc = _peer(k, x, y, c)
    them = 4 * px + 2 * py + pc
    if kind == "gather":
        src, dst_there, dst_here = srcs[a], lands[a].at[me], lands[a].at[them]
    else:
        src, dst_there, dst_here = srcs[a].at[them], lands[a].at[k - 1], lands[a].at[k - 1]
    return pltpu.make_async_remote_copy(
        src_ref=src, dst_ref=dst_here if arriving else dst_there,
        send_sem=send_sems.at[7 * a + k - 1], recv_sem=recv_sems.at[7 * a + k - 1],
        device_id=(px, py, pc), device_id_type=MESH)


def _exchange_start(name, kind, srcs, lands, after):
    n = len(srcs)

    def body(*refs):
        ins, lnd = refs[:n], refs[n:2 * n]
        send_sems, recv_sems = refs[2 * n + 1], refs[2 * n + 2]
        token = refs[-1]
        for k in PEER_ORDER:
            for a in range(n):
                _exchange_copy(kind, k, a, ins, lnd, send_sems, recv_sems).start()
        token[...] = jnp.zeros_like(token)

    hbm = lambda t: pltpu.with_memory_space_constraint(t, pltpu.HBM)
    out = pl.pallas_call(
        body, name=name,
        out_shape=(pltpu.SemaphoreType.DMA((7 * n,)), pltpu.SemaphoreType.DMA((7 * n,)),
                   *[pltpu.HBM(t.shape, t.dtype) for t in srcs], *[pltpu.HBM(t.shape, t.dtype) for t in lands],
                   jax.ShapeDtypeStruct((8, LANES), F32)),
        in_specs=[HBM] * (2 * n) + [ANY],
        out_specs=(SEM, SEM, *[HBM] * (2 * n), pl.BlockSpec(memory_space=pltpu.VMEM)),
        input_output_aliases={i: 2 + i for i in range(2 * n)},
        compiler_params=pltpu.CompilerParams(has_side_effects=EFFECT),
    )(*[hbm(t) for t in srcs], *[hbm(t) for t in lands], after)
    return out[0], out[1], list(out[2:2 + n]), list(out[2 + n:2 + 2 * n]), out[-1]


def _exchange_wait(name, kind, send_sems, recv_sems, srcs, lands, after):
    n = len(srcs)

    def body(*refs):
        ins, lnd = refs[:n], refs[n:2 * n]
        s_sems, r_sems = refs[2 * n], refs[2 * n + 1]
        for a in range(n):
            for k in PEER_ORDER:
                _exchange_copy(kind, k, a, ins, lnd, s_sems, r_sems).wait_send()
                _exchange_copy(kind, k, a, ins, lnd, s_sems, r_sems, arriving=True).wait_recv()

    out = pl.pallas_call(
        body, name=name,
        out_shape=(*[pltpu.HBM(t.shape, t.dtype) for t in srcs], *[pltpu.HBM(t.shape, t.dtype) for t in lands]),
        in_specs=[HBM] * (2 * n) + [SEM, SEM, ANY],
        out_specs=tuple([HBM] * (2 * n)),
        input_output_aliases={i: i for i in range(2 * n)},
        compiler_params=pltpu.CompilerParams(has_side_effects=EFFECT),
    )(*srcs, *lands, send_sems, recv_sems, after)
    return list(out[:n]), list(out[n:])


def _sum_parts(name, me, mine, parts, rows):
    _, R, C = mine.shape

    def body(me_ref, own_ref, p_ref, o_ref):
        acc = own_ref[0].astype(F32)
        for s in range(N_DEV - 1):
            acc = acc + p_ref[s].astype(F32)
        o_ref[...] = acc

    return pl.pallas_call(
        body, name=name,
        grid_spec=pltpu.PrefetchScalarGridSpec(
            num_scalar_prefetch=1, grid=(R // rows,),
            in_specs=[pl.BlockSpec((1, rows, C), lambda i, me_ref: (me_ref[0], i, 0)),
                      pl.BlockSpec((N_DEV - 1, rows, C), lambda i, me_ref: (0, i, 0))],
            out_specs=pl.BlockSpec((rows, C), lambda i, me_ref: (i, 0))),
        out_shape=jax.ShapeDtypeStruct((R, C), F32), compiler_params=_params(1),
    )(me, mine, parts)


def _sum_slots(name, a, rows):
    _, R, C = a.shape

    def body(a_ref, o_ref):
        acc = a_ref[0].astype(F32)
        for s in range(1, N_DEV):
            acc = acc + a_ref[s].astype(F32)
        o_ref[...] = acc

    return pl.pallas_call(
        body, name=name, grid=(R // rows,),
        in_specs=[pl.BlockSpec((N_DEV, rows, C), lambda i: (0, i, 0))],
        out_specs=pl.BlockSpec((rows, C), lambda i: (i, 0)),
        out_shape=jax.ShapeDtypeStruct((R, C), F32), compiler_params=_params(1),
    )(a)


_DIMS = {"nn": (((1,), (0,)), ((), ())), "nt": (((1,), (1,)), ((), ())), "tn": (((0,), (0,)), ((), ()))}


def _mm(name, mode, a_list, b_list, pairs, n_acc, M, N, K, tm, tn, tk, epilogue, out_dtypes, extras=(), after=None):
    tm, tn, tk = min(tm, M), min(tn, N), min(tk, K)
    assert M % tm == 0 and N % tn == 0 and K % tk == 0, (name, M, N, K, tm, tn, tk)
    nk = K // tk
    na, nb, ne, no = len(a_list), len(b_list), len(extras), len(out_dtypes)
    dims = _DIMS[mode]
    a_spec = (pl.BlockSpec((tk, tm), lambda j, i, k: (k, i)) if mode == "tn"
              else pl.BlockSpec((tm, tk), lambda j, i, k: (i, k)))
    b_spec = (pl.BlockSpec((tn, tk), lambda j, i, k: (j, k)) if mode == "nt"
              else pl.BlockSpec((tk, tn), lambda j, i, k: (k, j)))
    o_spec = pl.BlockSpec((tm, tn), lambda j, i, k: (i, j))
    tail = [] if after is None else [after]

    def body(*refs):
        a_refs, b_refs = refs[:na], refs[na:na + nb]
        e_refs = refs[na + nb:na + nb + ne]
        first_out = na + nb + ne + len(tail)
        o_refs = refs[first_out:first_out + no]
        acc_refs = refs[first_out + no:]

        def partial(oi):
            tot = None
            for ai, bi, ti in pairs:
                if ti == oi:
                    d = lax.dot_general(a_refs[ai][...], b_refs[bi][...], dims, preferred_element_type=F32)
                    tot = d if tot is None else tot + d
            return tot

        def finish(accs):
            outs = epilogue(accs, [e[...] for e in e_refs])
            for o_ref, o in zip(o_refs, outs):
                o_ref[...] = o.astype(o_ref.dtype)

        if nk == 1:
            finish([partial(oi) for oi in range(n_acc)])
        else:
            k = pl.program_id(2)

            @pl.when(k == 0)
            def _():
                for r in acc_refs:
                    r[...] = jnp.zeros_like(r)

            for oi in range(n_acc):
                acc_refs[oi][...] += partial(oi)

            @pl.when(k == nk - 1)
            def _():
                finish([r[...] for r in acc_refs])

    return pl.pallas_call(
        body, name=name, grid=(N // tn, M // tm, nk),
        in_specs=[a_spec] * na + [b_spec] * nb + [o_spec] * ne + [ANY] * len(tail),
        out_specs=[o_spec] * no,
        out_shape=[jax.ShapeDtypeStruct((M, N), dt) for dt in out_dtypes],
        scratch_shapes=[pltpu.VMEM((tm, tn), F32)] * (n_acc if nk > 1 else 0),
        compiler_params=_params(3),
    )(*a_list, *b_list, *extras, *tail)


def _ep_residual(accs, ex):
    return (ex[0] + accs[0],)


def _ep_rows_rms_bwd(acc, ex, rows):
    xv, res = ex
    r = lax.rsqrt(jnp.mean(xv * xv, axis=-1, keepdims=True) + EPS)
    xh = xv * r
    dg = jnp.sum(acc * xh, axis=0, keepdims=True)
    dxh = acc * rows[0]
    dx = r * (dxh - xh * jnp.mean(dxh * xh, axis=-1, keepdims=True)) + res
    return (dx, dx), (dg,)


def _sigmoid(x):
    return 0.5 * (jnp.tanh(0.5 * x) + 1.0)


def _ep_plain(accs, ex):
    return (accs[0],)


def _ep_swiglu(accs, ex):
    g, u = accs
    return g, u, (g * _sigmoid(g)) * u


def _ep_swiglu_bwd(accs, ex):
    dact = accs[0]
    g, u = ex
    sig = _sigmoid(g)
    silu = g * sig
    return dact * u * (sig * (1.0 + g * (1.0 - sig))), dact * silu


def _ep_two(accs, ex):
    return accs[0], accs[1]


ROWS = 256


def _row_spec(tm, width, col=0):
    return pl.BlockSpec((tm, width), lambda i: (i, col))


def _full_spec(shape):
    nd = len(shape)
    return pl.BlockSpec(shape, lambda i: (0,) * nd)


def _rms_fwd(name, x, g):
    T, D = x.shape
    tm = min(ROWS, T)

    def body(x_ref, g_ref, o_ref):
        xv = x_ref[...]
        r = lax.rsqrt(jnp.mean(xv * xv, axis=-1, keepdims=True) + EPS)
        o_ref[...] = ((xv * r) * g_ref[...]).astype(BF16)

    return pl.pallas_call(
        body, name=name, grid=(T // tm,),
        in_specs=[_row_spec(tm, D), _full_spec((1, D))], out_specs=_row_spec(tm, D),
        out_shape=jax.ShapeDtypeStruct((T, D), BF16), compiler_params=_params(1),
    )(x, g)


def _rms_bwd(name, dyn, x, g, res):
    T, D = x.shape
    tm = min(ROWS, T)

    def body(dyn_ref, x_ref, g_ref, res_ref, dx_ref, dx16_ref, dg_ref):
        outs, (dg,) = _ep_rows_rms_bwd(dyn_ref[...], [x_ref[...], res_ref[...]], [g_ref[...]])

        @pl.when(pl.program_id(0) == 0)
        def _():
            dg_ref[...] = jnp.zeros_like(dg_ref)

        dg_ref[...] += dg
        dx_ref[...] = outs[0]
        dx16_ref[...] = outs[1].astype(BF16)

    return pl.pallas_call(
        body, name=name, grid=(T // tm,),
        in_specs=[_row_spec(tm, D), _row_spec(tm, D), _full_spec((1, D)), _row_spec(tm, D)],
        out_specs=[_row_spec(tm, D), _row_spec(tm, D), _full_spec((1, D))],
        out_shape=[jax.ShapeDtypeStruct((T, D), F32), jax.ShapeDtypeStruct((T, D), BF16),
                   jax.ShapeDtypeStruct((1, D), F32)],
        compiler_params=_params(1),
    )(dyn, x, g, res)


def _outnorm_fwd(name, a, b, ga, gb):
    T, W = a.shape
    tm = min(ROWS, T)

    def body(a_ref, b_ref, ga_ref, gb_ref, o_ref):
        for src, gain, lo in ((a_ref, ga_ref, 0), (b_ref, gb_ref, W)):
            v = src[...]
            r = lax.rsqrt(jnp.mean(v * v, axis=-1, keepdims=True) + EPS)
            o_ref[:, lo:lo + W] = ((v * r) * gain[...]).astype(BF16)

    return pl.pallas_call(
        body, name=name, grid=(T // tm,),
        in_specs=[_row_spec(tm, W), _row_spec(tm, W), _full_spec((1, W)), _full_spec((1, W))],
        out_specs=_row_spec(tm, 2 * W),
        out_shape=jax.ShapeDtypeStruct((T, 2 * W), BF16), compiler_params=_params(1),
    )(a, b, ga, gb)


def _outnorm_bwd(name, dmixed, a, b, ga, gb):
    T, W = a.shape
    tm = min(ROWS, T)

    def body(dm_ref, a_ref, b_ref, ga_ref, gb_ref, da_ref, db_ref, dga_ref, dgb_ref):
        first = pl.program_id(0) == 0
        for src, gain, lo, dsrc, dgain in ((a_ref, ga_ref, 0, da_ref, dga_ref), (b_ref, gb_ref, W, db_ref, dgb_ref)):
            v, dv = src[...], dm_ref[:, lo:lo + W]
            r = lax.rsqrt(jnp.mean(v * v, axis=-1, keepdims=True) + EPS)
            vh = v * r

            @pl.when(first)
            def _():
                dgain[...] = jnp.zeros_like(dgain)

            dgain[...] += jnp.sum(dv * vh, axis=0, keepdims=True)
            dvh = dv * gain[...]
            dsrc[...] = r * (dvh - vh * jnp.mean(dvh * vh, axis=-1, keepdims=True))

    return pl.pallas_call(
        body, name=name, grid=(T // tm,),
        in_specs=[_row_spec(tm, 2 * W), _row_spec(tm, W), _row_spec(tm, W), _full_spec((1, W)), _full_spec((1, W))],
        out_specs=[_row_spec(tm, W), _row_spec(tm, W), _full_spec((1, W)), _full_spec((1, W))],
        out_shape=[jax.ShapeDtypeStruct((T, W), F32), jax.ShapeDtypeStruct((T, W), F32),
                   jax.ShapeDtypeStruct((1, W), F32), jax.ShapeDtypeStruct((1, W), F32)],
        compiler_params=_params(1),
    )(dmixed, a, b, ga, gb)


def _loss_head(name, y, target):
    T, D = y.shape
    tm = min(ROWS, T)

    def body(y_ref, t_ref, loss_ref, dy_ref, dy16_ref):
        d = y_ref[...] - t_ref[...]

        @pl.when(pl.program_id(0) == 0)
        def _():
            loss_ref[...] = jnp.zeros_like(loss_ref)

        per_token = jnp.mean(d * d, axis=-1, keepdims=True)
        loss_ref[...] += 0.5 * jnp.sum(per_token, axis=0, keepdims=True)
        dy = d * (1.0 / D)
        dy_ref[...] = dy
        dy16_ref[...] = dy.astype(BF16)

    return pl.pallas_call(
        body, name=name, grid=(T // tm,),
        in_specs=[_row_spec(tm, D), _row_spec(tm, D)],
        out_specs=[_full_spec((1, 1)), _row_spec(tm, D), _row_spec(tm, D)],
        out_shape=[jax.ShapeDtypeStruct((1, 1), F32), jax.ShapeDtypeStruct((T, D), F32),
                   jax.ShapeDtypeStruct((T, D), BF16)],
        compiler_params=_params(1),
    )(y, target)


def _bf16_dot(a, b, dims):
    return lax.dot_general(a.astype(BF16), b.astype(BF16), dims, preferred_element_type=F32)


@jax.custom_vjp
def _dot_nn(a, b):
    return _bf16_dot(a, b, _DIMS["nn"])


def _dot_nn_fwd(a, b):
    return _dot_nn(a, b), (a, b)


def _dot_nn_bwd(saved, ct):
    a, b = saved
    return _bf16_dot(ct, b, _DIMS["nt"]), _bf16_dot(a, ct, _DIMS["tn"])


_dot_nn.defvjp(_dot_nn_fwd, _dot_nn_bwd)


@jax.custom_vjp
def _dot_nt(a, b):
    return _bf16_dot(a, b, _DIMS["nt"])


def _dot_nt_fwd(a, b):
    return _dot_nt(a, b), (a, b)


def _dot_nt_bwd(saved, ct):
    a, b = saved
    return _bf16_dot(ct, b, _DIMS["nn"]), _bf16_dot(ct, a, _DIMS["tn"])


_dot_nt.defvjp(_dot_nt_fwd, _dot_nt_bwd)


def _iota(shape, dim):
    return lax.broadcasted_iota(jnp.int32, shape, dim)


def _head_sum_impl(x):
    same_head = (_iota((LANES, LANES), 0) // HEAD_DIM == _iota((LANES, LANES), 1) // HEAD_DIM).astype(BF16)
    pieces = []
    for i in range(x.shape[1] // LANES):
        xs = x[:, i * LANES:(i + 1) * LANES]
        hi = xs.astype(BF16)
        r1 = xs - hi.astype(F32)
        mid = r1.astype(BF16)
        lo = (r1 - mid.astype(F32)).astype(BF16)
        tot = None
        for part in (hi, mid, lo):
            d = lax.dot_general(part, same_head, _DIMS["nn"], preferred_element_type=F32)
            tot = d if tot is None else tot + d
        pieces.append(tot)
    return pieces[0] if len(pieces) == 1 else jnp.concatenate(pieces, axis=1)


@jax.custom_vjp
def _head_sum(x):
    return _head_sum_impl(x)


_head_sum.defvjp(lambda x: (_head_sum_impl(x), None), lambda _, ct: (_head_sum_impl(ct),))


def _head_rms(x, g_row):
    ms = _head_sum(x * x) * (1.0 / HEAD_DIM)
    return (x * lax.rsqrt(ms + EPS)) * g_row


@jax.custom_vjp
def _swap_halves(x):
    return pltpu.roll(x, HEAD_DIM, 1)


_swap_halves.defvjp(lambda x: (pltpu.roll(x, HEAD_DIM, 1), None), lambda _, ct: (pltpu.roll(ct, HEAD_DIM, 1),))


def _gelu(x):
    return 0.5 * x * (1.0 + lax.erf(x * (1.0 / math.sqrt(2.0))))


def _sgu_block(u_raw, v_raw, g_row, w, b_full):
    u, v = _gelu(u_raw), _gelu(v_raw)
    vn = _head_rms(v, g_row)
    causal = _iota((CHUNK, CHUNK), 0) >= _iota((CHUNK, CHUNK), 1)
    low_half = _iota((CHUNK, LANES), 1) < HEAD_DIM
    gates = []
    for p in range(v.shape[1] // LANES):
        vp = vn[:, p * LANES:(p + 1) * LANES]
        g0 = _dot_nn(jnp.where(causal, w[2 * p], 0.0), vp)
        g1 = _dot_nn(jnp.where(causal, w[2 * p + 1], 0.0), vp)
        gates.append(jnp.where(low_half, g0, g1))
    gate = jnp.concatenate(gates, axis=1) + b_full
    return u * gate


def _attn_block(q_raw, k_prev, k_cur, v_prev, v_cur, qg_row, kg_row, sinks, bias, blk):
    scale = 1.0 / math.sqrt(HEAD_DIM)
    n_q_heads = q_raw.shape[1] // HEAD_DIM
    qn = _head_rms(q_raw, qg_row)
    kn = _head_rms(jnp.concatenate([k_prev, k_cur], axis=0), kg_row)
    vb = jnp.concatenate([v_prev, v_cur], axis=0)
    qi = _iota((GROUP * BLOCK, 2 * BLOCK), 0) & (BLOCK - 1)
    kj = _iota((GROUP * BLOCK, 2 * BLOCK), 1)
    dist = qi + BLOCK - kj
    valid = (dist >= 0) & (dist < BLOCK) & (kj + (blk * BLOCK - BLOCK) >= 0)
    low_half = _iota((BLOCK, LANES), 1) < HEAD_DIM
    heads = [None] * n_q_heads
    for kv in range(n_q_heads // GROUP):
        grp, kv_low = kv // 2, kv % 2 == 0
        k2 = kn[:, grp * LANES:(grp + 1) * LANES]
        v2 = vb[:, grp * LANES:(grp + 1) * LANES]
        keep = low_half if kv_low else ~low_half
        stacked = []
        for g in range(GROUP):
            h = GROUP * kv + g
            q2 = qn[:, (h // 2) * LANES:(h // 2 + 1) * LANES]
            src = q2 if (h % 2 == 0) == kv_low else _swap_halves(q2)
            stacked.append(jnp.where(keep, src, 0.0))
        q4 = jnp.concatenate(stacked, axis=0)
        bias4 = jnp.concatenate([bias[GROUP * kv + g] for g in range(GROUP)], axis=0)
        sink4 = jnp.concatenate([jnp.broadcast_to(sinks[GROUP * kv + g], (BLOCK, 1)) for g in range(GROUP)], axis=0)
        s = _dot_nt(q4, k2) * scale + bias4
        s = jnp.where(valid, s, NEG_INF)
        m = lax.stop_gradient(jnp.maximum(jnp.max(s, axis=1, keepdims=True), sink4))
        e = jnp.exp(s - m)
        denom = jnp.sum(e, axis=1, keepdims=True) + jnp.exp(sink4 - m)
        o4 = _dot_nn(e * (1.0 / denom), v2)
        for g in range(GROUP):
            h = GROUP * kv + g
            o = o4[g * BLOCK:(g + 1) * BLOCK]
            heads[h] = o if (h % 2 == 0) == kv_low else _swap_halves(o)
    outs = [jnp.where(low_half, heads[2 * p], heads[2 * p + 1]) for p in range(n_q_heads // 2)]
    return jnp.concatenate(outs, axis=1)


def _bias_table():
    dist = np.arange(BLOCK)[:, None] + BLOCK - np.arange(2 * BLOCK)[None, :]
    n = np.maximum(dist, 0)
    max_exact = NUM_BUCKETS // 2
    nf = np.maximum(n, 1).astype(np.float64)
    large = max_exact + (np.log(nf / max_exact) / math.log(MAX_DISTANCE / max_exact) * (NUM_BUCKETS - max_exact)).astype(np.int32)
    large = np.minimum(large, NUM_BUCKETS - 1)
    return np.where(n < max_exact, n, large).astype(np.int32)


def _bias_fwd(name, rel_bias, buckets):
    nb_, nh = rel_bias.shape

    def body(rb_ref, bk_ref, o_ref):
        bk = bk_ref[...]
        for h in range(nh):
            acc = jnp.zeros(bk.shape, F32)
            for b in range(nb_):
                acc = jnp.where(bk == b, rb_ref[b, h], acc)
            o_ref[h] = acc

    return pl.pallas_call(
        body, name=name,
        in_specs=[pl.BlockSpec(memory_space=pltpu.SMEM), pl.BlockSpec(memory_space=pltpu.VMEM)],
        out_specs=pl.BlockSpec(memory_space=pltpu.VMEM),
        out_shape=jax.ShapeDtypeStruct((nh,) + buckets.shape, F32),
    )(rel_bias, buckets)


def _bias_bwd(name, dbias, buckets, nb_):
    nh = dbias.shape[0]

    def body(db_ref, bk_ref, o_ref):
        bk = bk_ref[...]
        for h in range(nh):
            d = db_ref[h]
            for b in range(nb_):
                s = jnp.sum(jnp.where(bk == b, d, 0.0), axis=0, keepdims=True)
                s = jnp.sum(s, axis=1, keepdims=True)
                o_ref[b * nh + h:b * nh + h + 1, :] = jnp.broadcast_to(s, (1, LANES))

    return pl.pallas_call(
        body, name=name,
        in_specs=[pl.BlockSpec(memory_space=pltpu.VMEM), pl.BlockSpec(memory_space=pltpu.VMEM)],
        out_specs=pl.BlockSpec(memory_space=pltpu.VMEM),
        out_shape=jax.ShapeDtypeStruct((nb_ * nh, LANES), F32),
    )(dbias, buckets)


def _sgu_fwd(name, z, g_row, w, b_full, W):
    T = z.shape[0]

    def body(u_ref, v_ref, g_ref, w_ref, b_ref, o_ref):
        o_ref[...] = _sgu_block(u_ref[...], v_ref[...], g_ref[...], w_ref[...], b_ref[...])

    return pl.pallas_call(
        body, name=name, grid=(T // CHUNK,),
        in_specs=[_row_spec(CHUNK, W, 0), _row_spec(CHUNK, W, 1), _full_spec((1, W)),
                  _full_spec(w.shape), _full_spec((CHUNK, W))],
        out_specs=_row_spec(CHUNK, W),
        out_shape=jax.ShapeDtypeStruct((T, W), F32), compiler_params=_params(1),
    )(z, z, g_row, w, b_full)


def _sgu_bwd(name, z, g_row, w, b_full, d_out, W):
    T = z.shape[0]

    def body(u_ref, v_ref, g_ref, w_ref, b_ref, do_ref, dz_ref, dg_ref, dw_ref, db_ref):
        _, vjp = jax.vjp(_sgu_block, u_ref[...], v_ref[...], g_ref[...], w_ref[...], b_ref[...])
        du, dv, dg, dw, db = vjp(do_ref[...])
        dz_ref[:, :W] = du.astype(BF16)
        dz_ref[:, W:] = dv.astype(BF16)
        i = pl.program_id(0)

        @pl.when(i == 0)
        def _():
            dg_ref[...] = jnp.zeros_like(dg_ref)
            dw_ref[...] = jnp.zeros_like(dw_ref)
            db_ref[...] = jnp.zeros_like(db_ref)

        dg_ref[...] += dg
        dw_ref[...] += dw
        db_ref[...] += db

        @pl.when(i == pl.num_programs(0) - 1)
        def _():
            db_ref[...] = _head_sum_impl(db_ref[...])

    return pl.pallas_call(
        body, name=name, grid=(T // CHUNK,),
        in_specs=[_row_spec(CHUNK, W, 0), _row_spec(CHUNK, W, 1), _full_spec((1, W)),
                  _full_spec(w.shape), _full_spec((CHUNK, W)), _row_spec(CHUNK, W)],
        out_specs=[_row_spec(CHUNK, 2 * W), _full_spec((1, W)), _full_spec(w.shape), _full_spec((CHUNK, W))],
        out_shape=[jax.ShapeDtypeStruct((T, 2 * W), BF16), jax.ShapeDtypeStruct((1, W), F32),
                   jax.ShapeDtypeStruct(w.shape, F32), jax.ShapeDtypeStruct((CHUNK, W), F32)],
        compiler_params=_params(1),
    )(z, z, g_row, w, b_full, d_out)


def _attn_specs(WQ, WKV, q_col, k_col, v_col, blk_of):
    prev_of = lambda i: jnp.maximum(blk_of(i) - 1, 0)
    return [pl.BlockSpec((BLOCK, WQ), lambda i: (blk_of(i), q_col)),
            pl.BlockSpec((BLOCK, WKV), lambda i: (prev_of(i), k_col)),
            pl.BlockSpec((BLOCK, WKV), lambda i: (blk_of(i), k_col)),
            pl.BlockSpec((BLOCK, WKV), lambda i: (prev_of(i), v_col)),
            pl.BlockSpec((BLOCK, WKV), lambda i: (blk_of(i), v_col))]


def _attn_fwd(name, z, qg_row, kg_row, sinks_col, bias, WQ, WKV, q_col, k_col, v_col):
    T = z.shape[0]
    nh = sinks_col.shape[0]

    def body(q_ref, kp_ref, kc_ref, vp_ref, vc_ref, qg_ref, kg_ref, s_ref, b_ref, o_ref):
        sinks = [s_ref[h:h + 1, :] for h in range(nh)]
        o_ref[...] = _attn_block(q_ref[...], kp_ref[...], kc_ref[...], vp_ref[...], vc_ref[...],
                                 qg_ref[...], kg_ref[...], sinks, b_ref[...], pl.program_id(0))

    return pl.pallas_call(
        body, name=name, grid=(T // BLOCK,),
        in_specs=_attn_specs(WQ, WKV, q_col, k_col, v_col, lambda i: i)
        + [_full_spec((1, WQ)), _full_spec((1, WKV)), _full_spec((nh, 1)), _full_spec(bias.shape)],
        out_specs=_row_spec(BLOCK, WQ),
        out_shape=jax.ShapeDtypeStruct((T, WQ), F32), compiler_params=_params(1),
    )(z, z, z, z, z, qg_row, kg_row, sinks_col, bias)


def _attn_bwd(name, z, qg_row, kg_row, sinks_col, bias, d_out, WQ, WKV, q_col, k_col, v_col):
    T = z.shape[0]
    nblk = T // BLOCK
    nh = sinks_col.shape[0]
    blk_of = lambda i: nblk - 1 - i

    def body(q_ref, kp_ref, kc_ref, vp_ref, vc_ref, qg_ref, kg_ref, s_ref, b_ref, do_ref,
             dq_ref, dk_ref, dv_ref, dqg_ref, dkg_ref, ds_ref, db_ref, carry_k, carry_v):
        i = pl.program_id(0)
        blk = nblk - 1 - i
        sinks = [s_ref[h:h + 1, :] for h in range(nh)]
        fn = lambda q, kp, kc, vp, vc, qg, kg, sk, bs: _attn_block(q, kp, kc, vp, vc, qg, kg, sk, bs, blk)
        _, vjp = jax.vjp(fn, q_ref[...], kp_ref[...], kc_ref[...], vp_ref[...], vc_ref[...],
                         qg_ref[...], kg_ref[...], sinks, b_ref[...])
        dq, dkp, dkc, dvp, dvc, dqg, dkg, dsk, dbs = vjp(do_ref[...])

        @pl.when(i == 0)
        def _():
            carry_k[...] = jnp.zeros_like(carry_k)
            carry_v[...] = jnp.zeros_like(carry_v)
            dqg_ref[...] = jnp.zeros_like(dqg_ref)
            dkg_ref[...] = jnp.zeros_like(dkg_ref)
            ds_ref[...] = jnp.zeros_like(ds_ref)
            db_ref[...] = jnp.zeros_like(db_ref)

        dq_ref[...] = dq.astype(BF16)
        dk_ref[...] = (dkc + carry_k[...]).astype(BF16)
        dv_ref[...] = (dvc + carry_v[...]).astype(BF16)
        carry_k[...] = dkp
        carry_v[...] = dvp
        dqg_ref[...] += dqg
        dkg_ref[...] += dkg
        for h in range(nh):
            ds_ref[h:h + 1, :] += dsk[h]
        db_ref[...] += dbs

    return pl.pallas_call(
        body, name=name, grid=(nblk,),
        in_specs=_attn_specs(WQ, WKV, q_col, k_col, v_col, blk_of)
        + [_full_spec((1, WQ)), _full_spec((1, WKV)), _full_spec((nh, 1)), _full_spec(bias.shape),
           pl.BlockSpec((BLOCK, WQ), lambda i: (blk_of(i), 0))],
        out_specs=[pl.BlockSpec((BLOCK, WQ), lambda i: (blk_of(i), 0)),
                   pl.BlockSpec((BLOCK, WKV), lambda i: (blk_of(i), 0)),
                   pl.BlockSpec((BLOCK, WKV), lambda i: (blk_of(i), 0)),
                   _full_spec((1, WQ)), _full_spec((1, WKV)), _full_spec((nh, 1)), _full_spec(bias.shape)],
        out_shape=[jax.ShapeDtypeStruct((T, WQ), BF16), jax.ShapeDtypeStruct((T, WKV), BF16),
                   jax.ShapeDtypeStruct((T, WKV), BF16), jax.ShapeDtypeStruct((1, WQ), F32),
                   jax.ShapeDtypeStruct((1, WKV), F32), jax.ShapeDtypeStruct((nh, 1), F32),
                   jax.ShapeDtypeStruct(bias.shape, F32)],
        scratch_shapes=[pltpu.VMEM((BLOCK, WKV), F32), pltpu.VMEM((BLOCK, WKV), F32)],
        compiler_params=_params(1),
    )(z, z, z, z, z, qg_row, kg_row, sinks_col, bias, d_out)


def _adamw(name, w, g, m, v):
    shape = w.shape
    C = shape[-1]
    if w.ndim == 3 and shape[1] % ROWS == 0:
        work = shape
        grid = (shape[0], shape[1] // ROWS)
        spec = pl.BlockSpec((1, ROWS, C), lambda l, i: (l, i, 0))
    else:
        R = int(np.prod(shape[:-1]))
        tr = ROWS if R % ROWS == 0 else R
        work = (R, C)
        grid = (R // tr,)
        spec = pl.BlockSpec((tr, C), lambda i: (i, 0))
    w2, g2, m2, v2 = (t.reshape(work) for t in (w, g, m, v))

    def body(w_ref, g_ref, m_ref, v_ref, d_ref, nm_ref, nv_ref):
        gv = g_ref[...]
        nm = ADAM_B1 * m_ref[...] + (1.0 - ADAM_B1) * gv
        nv = ADAM_B2 * v_ref[...] + (1.0 - ADAM_B2) * (gv * gv)
        m_hat = nm / (1.0 - ADAM_B1 ** ADAM_STEP)
        v_hat = nv / (1.0 - ADAM_B2 ** ADAM_STEP)
        d_ref[...] = -ADAM_LR * (m_hat / (jnp.sqrt(v_hat) + ADAM_EPS) + ADAM_WD * w_ref[...])
        nm_ref[...] = nm
        nv_ref[...] = nv

    outs = pl.pallas_call(
        body, name=name, grid=grid, in_specs=[spec] * 4, out_specs=[spec] * 3,
        out_shape=[jax.ShapeDtypeStruct(work, F32)] * 3, compiler_params=_params(len(grid)),
    )(w2, g2, m2, v2)
    return tuple(o.reshape(shape) for o in outs)


def _pad_rows(flat):
    n = flat.shape[0]
    tile = 8 * LANES
    padded = -(-n // tile) * tile
    return jnp.pad(flat, (0, padded - n)).reshape(padded // LANES, LANES)


def kernel(x, rel_bias, norm1_g, w_in, sgu_norm_g, sgu_w, sgu_b, q_norm_g, k_norm_g, sinks, out_norm_a, out_norm_b, w_out, norm2_g, w_gate, w_up, w_down, loss_target, m_rel_bias, m_norm1_g, m_w_in, m_sgu_norm_g, m_sgu_w, m_sgu_b, m_q_norm_g, m_k_norm_g, m_sinks, m_out_norm_a, m_out_norm_b, m_w_out, m_norm2_g, m_w_gate, m_w_up, m_w_down, v_rel_bias, v_norm1_g, v_w_in, v_sgu_norm_g, v_sgu_w, v_sgu_b, v_q_norm_g, v_k_norm_g, v_sinks, v_out_norm_a, v_out_norm_b, v_w_out, v_norm2_g, v_w_gate, v_w_up, v_w_down):
    L = w_in.shape[0]
    T, D = x.shape[1], x.shape[2]
    W = D // 2
    NH = W // HEAD_DIM
    WKV = N_KV_HEADS * HEAD_DIM
    IN = N_DEV * w_in.shape[2]
    FF = N_DEV * w_gate.shape[2]
    assert IN == 2 * W + W + 2 * WKV and NH // N_KV_HEADS == GROUP
    q_col, k_col, v_col = 2 * W // W, (3 * W) // WKV, (3 * W + WKV) // WKV
    x0 = x.reshape(T, D)
    target = loss_target.reshape(T, D)

    shards = [jnp.swapaxes(w_in, 1, 2).astype(BF16), w_out.astype(BF16), jnp.swapaxes(w_gate, 1, 2).astype(BF16),
              jnp.swapaxes(w_up, 1, 2).astype(BF16), w_down.astype(BF16)]
    me = 4 * lax.axis_index("x") + 2 * lax.axis_index("y") + lax.axis_index("c")

    def landing(block):
        return lax.dynamic_update_slice(lax.empty((N_DEV,) + block.shape, block.dtype), block[None], (me, 0, 0))

    def as_matrices(got):
        return [g.reshape(N_DEV * g.shape[1], g.shape[2]) for g in got]

    full = {0: as_matrices(_all_gather("gather_weights_0", [s[0] for s in shards]))}
    flight = {}

    def start_gather(l, after):
        srcs = [s[l] for s in shards]
        flight[l] = _exchange_start(f"gather_start_{l}", "gather", srcs, [landing(s) for s in srcs], after)
        return flight[l][4]

    def finish_gather(l, after):
        s_sem, r_sem, srcs, lands, _ = flight.pop(l)
        full[l] = as_matrices(_exchange_wait(f"gather_wait_{l}", "gather", s_sem, r_sem, srcs, lands, after)[1])

    buckets = jnp.asarray(_bias_table())
    bias = _bias_fwd("bias_table", rel_bias, buckets)

    saved = []
    xl = x0
    for l in range(L):
        if l > 0:
            finish_gather(l, xl)
        token = start_gather(l + 1, full[l][0] if l == 0 else xl) if l + 1 < L else xl
        w_in_t, w_o, w_g_t, w_u_t, w_d = full[l]
        g1, g2 = norm1_g[l][None], norm2_g[l][None]
        sg_row = sgu_norm_g[l].reshape(1, W)
        b_full = jnp.repeat(sgu_b[l].T, HEAD_DIM, axis=1)
        qg_row = jnp.tile(q_norm_g[l], NH)[None]
        kg_row = jnp.tile(k_norm_g[l], N_KV_HEADS)[None]
        sinks_col = sinks[l][:, None]
        ga, gb = out_norm_a[l][None], out_norm_b[l][None]

        h = _rms_fwd(f"norm1_{l}", xl, g1)
        (z,) = _mm(f"proj_in_{l}", "nt", [h], [w_in_t], [(0, 0, 0)], 1, T, IN, D, 1024, 896, D, _ep_plain, [F32],
                   after=token)
        out_a = _sgu_fwd(f"sgu_{l}", z, sg_row, sgu_w[l], b_full, W)
        out_b = _attn_fwd(f"attn_{l}", z, qg_row, kg_row, sinks_col, bias, W, WKV, q_col, k_col, v_col)
        mixed = _outnorm_fwd(f"outnorm_{l}", out_a, out_b, ga, gb)
        (x1,) = _mm(f"proj_out_{l}", "nn", [mixed], [w_o], [(0, 0, 0)], 1, T, D, D, 1024, 1024, D,
                    _ep_residual, [F32], extras=[xl])
        h2 = _rms_fwd(f"norm2_{l}", x1, g2)
        gate, up, act = _mm(f"mlp_in_{l}", "nt", [h2], [w_g_t, w_u_t], [(0, 0, 0), (0, 1, 1)], 2, T, FF, D,
                            1024, 512, D, _ep_swiglu, [BF16, BF16, BF16])
        (x2,) = _mm(f"mlp_out_{l}", "nn", [act], [w_d], [(0, 0, 0)], 1, T, D, FF, 512, 1024, FF,
                    _ep_residual, [F32], extras=[x1])
        saved.append((xl, h, z, out_a, out_b, mixed, x1, h2, gate, up, act,
                      g1, g2, sg_row, b_full, qg_row, kg_row, sinks_col, ga, gb))
        xl = x2

    loss_part, dy, dy16 = _loss_head("loss_head", xl, target)

    dbias = None
    small = [None] * L
    scatters = []

    def start_scatter(name, which, grads_t, after):
        srcs = [t.reshape(N_DEV, t.shape[0] // N_DEV, D) for t in grads_t]
        lands = [lax.empty((N_DEV - 1,) + s.shape[1:], BF16) for s in srcs]
        s_sem, r_sem, srcs, lands, tok = _exchange_start(name, "scatter", srcs, lands, after)
        scatters.append((name, which, s_sem, r_sem, srcs, lands))
        return tok

    token = None
    for l in reversed(range(L)):
        w_in_t, w_o, w_g_t, w_u_t, w_d = full[l]
        (xl, h, z, out_a, out_b, mixed, x1, h2, gate, up, act,
         g1, g2, sg_row, b_full, qg_row, kg_row, sinks_col, ga, gb) = saved[l]

        dgate, dup = _mm(f"d_mlp_out_{l}", "nt", [dy16], [w_d], [(0, 0, 0)], 1, T, FF, D, 1024, 1408, D,
                         _ep_swiglu_bwd, [BF16, BF16], extras=[gate, up], after=token)
        (dw_d,) = _mm(f"dw_down_{l}", "tn", [act], [dy16], [(0, 0, 0)], 1, FF, D, T, 1408, 1024, 2048, _ep_plain, [BF16])
        (dh2,) = _mm(f"d_mlp_in_{l}", "nn", [dgate, dup], [w_g_t, w_u_t], [(0, 0, 0), (1, 1, 0)], 1, T, D, FF,
                     512, 512, FF, _ep_plain, [F32])
        dx1, dx1_16, dg2 = _rms_bwd(f"d_norm2_{l}", dh2, x1, g2, dy)
        dw_g, dw_u = _mm(f"dw_gate_up_{l}", "tn", [dgate, dup], [h2], [(0, 0, 0), (1, 0, 1)], 2, FF, D, T,
                         1408, 1024, 1024, _ep_two, [BF16, BF16])
        token = start_scatter(f"scatter_mlp_start_{l}", (l, (4, 2, 3)), [dw_d, dw_g, dw_u], dw_u)
        (dmixed,) = _mm(f"d_proj_out_{l}", "nt", [dx1_16], [w_o], [(0, 0, 0)], 1, T, D, D, 1024, 1024, D, _ep_plain, [F32],
                        after=token)
        (dw_o,) = _mm(f"dw_out_{l}", "tn", [mixed], [dx1_16], [(0, 0, 0)], 1, D, D, T, 1024, 1024, 2048, _ep_plain, [BF16])
        d_a, d_b, dga, dgb = _outnorm_bwd(f"d_outnorm_{l}", dmixed, out_a, out_b, ga, gb)
        dz_uv, dsg, dsw, dsb = _sgu_bwd(f"d_sgu_{l}", z, sg_row, sgu_w[l], b_full, d_a, W)
        dq, dk, dv, dqg, dkg, dsk, dbs = _attn_bwd(f"d_attn_{l}", z, qg_row, kg_row, sinks_col, bias, d_b,
                                                   W, WKV, q_col, k_col, v_col)
        dbias = dbs if dbias is None else dbias + dbs
        dz = jnp.concatenate([dz_uv, dq, dk, dv], axis=1)
        (dh,) = _mm(f"d_proj_in_{l}", "nn", [dz], [w_in_t], [(0, 0, 0)], 1, T, D, IN, 1024, 1024, IN, _ep_plain, [F32])
        dy, dy16, dg1 = _rms_bwd(f"d_norm1_{l}", dh, xl, g1, dx1)
        (dw_i,) = _mm(f"dw_in_{l}", "tn", [dz], [h], [(0, 0, 0)], 1, IN, D, T, 896, 1024, 2048, _ep_plain, [BF16])
        token = start_scatter(f"scatter_mix_start_{l}", (l, (1, 0)), [dw_o, dw_i], dw_i)

        small[l] = dict(norm1_g=dg1[0], sgu_norm_g=dsg.reshape(NH, HEAD_DIM), sgu_w=dsw,
                        sgu_b=dsb[:, ::HEAD_DIM].T, q_norm_g=dqg.reshape(NH, HEAD_DIM).sum(0),
                        k_norm_g=dkg.reshape(N_KV_HEADS, HEAD_DIM).sum(0), sinks=dsk[:, 0],
                        out_norm_a=dga[0], out_norm_b=dgb[0], norm2_g=dg2[0])

    grad_x = dy.reshape(x.shape)
    d_rel = _bias_bwd("d_bias_table", dbias, buckets, NUM_BUCKETS)[:, 0].reshape(NUM_BUCKETS, NH)

    names = ["norm1_g", "sgu_norm_g", "sgu_w", "sgu_b", "q_norm_g", "k_norm_g", "sinks", "out_norm_a", "out_norm_b", "norm2_g"]
    parts = {"rel_bias": d_rel}
    for nme in names:
        parts[nme] = jnp.stack([small[l][nme] for l in range(L)])
    order = ["rel_bias"] + names
    packed = jnp.concatenate([_pad_rows(parts[nme].reshape(-1)) for nme in order], axis=0)
    small_flight = _exchange_start("gather_small_start", "gather", [packed], [landing(packed)], token)
    after = small_flight[4]

    me1 = me.reshape(1).astype(jnp.int32)
    grads_big = {}

    def finish_scatter(entry, after):
        name, (l, which), s_sem, r_sem, srcs, lands = entry
        srcs, lands = _exchange_wait(name.replace("start", "wait"), "scatter", s_sem, r_sem, srcs, lands, after)
        for i, src, land in zip(which, srcs, lands):
            rows = src.shape[1]
            after = _sum_parts(f"sum_grads_{l}_{i}", me1, src, land, 64 if rows % 64 == 0 else rows)
            grads_big[i, l] = after
        return after

    def stacked(i):
        return jnp.stack([grads_big[i, l] for l in range(L)])

    loss = lax.psum(loss_part[0, 0], ("x", "y", "c"))
    tr = lambda t: jnp.swapaxes(t, 1, 2)
    weights = dict(rel_bias=rel_bias, norm1_g=norm1_g, w_in=w_in, sgu_norm_g=sgu_norm_g, sgu_w=sgu_w, sgu_b=sgu_b,
                   q_norm_g=q_norm_g, k_norm_g=k_norm_g, sinks=sinks, out_norm_a=out_norm_a, out_norm_b=out_norm_b,
                   w_out=w_out, norm2_g=norm2_g, w_gate=w_gate, w_up=w_up, w_down=w_down)
    ms = dict(rel_bias=m_rel_bias, norm1_g=m_norm1_g, w_in=m_w_in, sgu_norm_g=m_sgu_norm_g, sgu_w=m_sgu_w, sgu_b=m_sgu_b,
              q_norm_g=m_q_norm_g, k_norm_g=m_k_norm_g, sinks=m_sinks, out_norm_a=m_out_norm_a, out_norm_b=m_out_norm_b,
              w_out=m_w_out, norm2_g=m_norm2_g, w_gate=m_w_gate, w_up=m_w_up, w_down=m_w_down)
    vs = dict(rel_bias=v_rel_bias, norm1_g=v_norm1_g, w_in=v_w_in, sgu_norm_g=v_sgu_norm_g, sgu_w=v_sgu_w, sgu_b=v_sgu_b,
              q_norm_g=v_q_norm_g, k_norm_g=v_k_norm_g, sinks=v_sinks, out_norm_a=v_out_norm_a, out_norm_b=v_out_norm_b,
              w_out=v_w_out, norm2_g=v_norm2_g, w_gate=v_w_gate, w_up=v_w_up, w_down=v_w_down)
    all_names = ["rel_bias", "norm1_g", "w_in", "sgu_norm_g", "sgu_w", "sgu_b", "q_norm_g", "k_norm_g", "sinks",
                 "out_norm_a", "out_norm_b", "w_out", "norm2_g", "w_gate", "w_up", "w_down"]
    transposed = ("w_in", "w_gate", "w_up")
    grads, deltas, new_m, new_v = {}, {}, {}, {}

    def update(nme, g):
        if nme in transposed:
            outs = _adamw(f"adamw_{nme}", tr(weights[nme]), g, tr(ms[nme]), tr(vs[nme]))
            grads[nme], (deltas[nme], new_m[nme], new_v[nme]) = tr(g), [tr(o) for o in outs]
        else:
            grads[nme] = g
            deltas[nme], new_m[nme], new_v[nme] = _adamw(f"adamw_{nme}", weights[nme], g, ms[nme], vs[nme])
        return new_v[nme]

    for entry in scatters[:-1]:
        after = finish_scatter(entry, after)
    for nme, i in (("w_gate", 2), ("w_up", 3), ("w_down", 4)):
        after = update(nme, stacked(i))

    s_sem, r_sem, srcs, lands, _ = small_flight
    _, (everyone,) = _exchange_wait("gather_small_wait", "gather", s_sem, r_sem, srcs, lands, after)
    rows = packed.shape[0]
    summed = _sum_slots("sum_small_grads", everyone, 64 if rows % 64 == 0 else 8)
    at = 0
    for nme in order:
        n = int(np.prod(parts[nme].shape))
        n_rows = -(-n // (8 * LANES)) * 8
        after = update(nme, summed[at:at + n_rows].reshape(-1)[:n].reshape(parts[nme].shape))
        at += n_rows

    finish_scatter(scatters[-1], after)
    update("w_out", stacked(1))
    update("w_in", stacked(0))
    return (loss, grad_x, *[grads[nme] for nme in all_names], *[deltas[nme] for nme in all_names],
            *[new_m[nme] for nme in all_names], *[new_v[nme] for nme in all_names])
```

```python
import functools
import math

import numpy as np
import jax
import jax.numpy as jnp
from jax import lax
from jax.experimental import pallas as pl
from jax.experimental.pallas import tpu as pltpu

F32 = jnp.float32
BF16 = jnp.bfloat16

N_DEV = 8
HEAD_DIM = 64
CHUNK = 128
BLOCK = 128
N_KV_HEADS = 4
GROUP = 4
NUM_BUCKETS = 32
MAX_DISTANCE = 128
EPS = 1e-6
NEG_INF = -1e30
LANES = 128
VMEM_LIMIT = 56 * 2 ** 20

ADAM_LR = 0.001
ADAM_B1 = 0.9
ADAM_B2 = 0.999
ADAM_EPS = 1e-08
ADAM_WD = 0.01
ADAM_STEP = 10

MESH = pl.DeviceIdType.MESH
ANY = pl.BlockSpec(memory_space=pl.ANY)
HBM = pl.BlockSpec(memory_space=pltpu.HBM)
SEM = pl.BlockSpec(memory_space=pltpu.SEMAPHORE)
EFFECT = pltpu.SideEffectType.DATAFLOW_SIDE_EFFECTING


def _params(n_axes):
    return pltpu.CompilerParams(dimension_semantics=("arbitrary",) * n_axes, vmem_limit_bytes=VMEM_LIMIT)


def _my_place():
    return lax.axis_index("x"), lax.axis_index("y"), lax.axis_index("c")


def _all_gather(name, arrs):
    n = len(arrs)

    def body(*refs):
        ins, outs = refs[:n], refs[n:2 * n]
        send_sems, recv_sems, local_sems = refs[2 * n:]
        x, y, c = _my_place()
        sibling = (x, y, 1 - c)
        chips = [(1 - x, y), (x, 1 - y), (1 - x, 1 - y)]

        def slot(a, px, py, pc):
            return outs[a].at[4 * px + 2 * py + pc]

        def copy(a, k, block, to, src=None):
            return pltpu.make_async_remote_copy(
                src_ref=slot(a, *block) if src is None else src, dst_ref=slot(a, *block),
                send_sem=send_sems.at[7 * a + k], recv_sem=recv_sems.at[7 * a + k],
                device_id=to, device_id_type=MESH)

        started = []
        for a in range(n):
            mine = pltpu.make_async_copy(ins[a], slot(a, x, y, c), local_sems.at[a])
            mine.start()
            started.append(mine)
        sends = []
        for a in range(n):
            first = [copy(a, 0, (x, y, c), sibling, src=ins[a])]
            first += [copy(a, 1 + j, (x, y, c), (*chip, c), src=ins[a]) for j, chip in enumerate(chips)]
            for cp in first:
                cp.start()
            sends += first
        for a in range(n):
            for j, chip in enumerate(chips):
                copy(a, 1 + j, (*chip, c), (x, y, c)).wait_recv()
                fwd = copy(a, 4 + j, (*chip, c), sibling)
                fwd.start()
                sends.append(fwd)
        for a in range(n):
            copy(a, 0, sibling, (x, y, c)).wait_recv()
            for j, chip in enumerate(chips):
                copy(a, 4 + j, (*chip, 1 - c), (x, y, c)).wait_recv()
        for cp in sends:
            cp.wait_send()
        for cp in started:
            cp.wait()

    return pl.pallas_call(
        body, name=name,
        out_shape=[jax.ShapeDtypeStruct((N_DEV,) + a.shape, a.dtype) for a in arrs],
        in_specs=[ANY] * n, out_specs=[ANY] * n,
        scratch_shapes=[pltpu.SemaphoreType.DMA((7 * n,)), pltpu.SemaphoreType.DMA((7 * n,)),
                        pltpu.SemaphoreType.DMA((n,))],
    )(*arrs)


def _peer(k, x, y, c):
    return (1 - x if k & 4 else x), (1 - y if k & 2 else y), (1 - c if k & 1 else c)


PEER_ORDER = (1, 2, 4, 3, 5, 6, 7)


def _exchange_copy(kind, k, a, srcs, lands, send_sems, recv_sems, arriving=False):
    x, y, c = _my_place()
    me = 4 * x + 2 * y + c
    px, py, pc = _peer(k, x, y, c)
    them = 4 * px + 2 * py + pc
    if kind == "gather":
        src, dst_there, dst_here = srcs[a], lands[a].at[me], lands[a].at[them]
    else:
        src, dst_there, dst_here = srcs[a].at[them], lands[a].at[k - 1], lands[a].at[k - 1]
    return pltpu.make_async_remote_copy(
        src_ref=src, dst_ref=dst_here if arriving else dst_there,
        send_sem=send_sems.at[7 * a + k - 1], recv_sem=recv_sems.at[7 * a + k - 1],
        device_id=(px, py, pc), device_id_type=MESH)


def _exchange_start(name, kind, srcs, lands, after):
    n = len(srcs)

    def body(*refs):
        ins, lnd = refs[:n], refs[n:2 * n]
        send_sems, recv_sems = refs[2 * n + 1], refs[2 * n + 2]
        token = refs[-1]
        for k in PEER_ORDER:
            for a in range(n):
                _exchange_copy(kind, k, a, ins, lnd, send_sems, recv_sems).start()
        token[...] = jnp.zeros_like(token)

    hbm = lambda t: pltpu.with_memory_space_constraint(t, pltpu.HBM)
    out = pl.pallas_call(
        body, name=name,
        out_shape=(pltpu.SemaphoreType.DMA((7 * n,)), pltpu.SemaphoreType.DMA((7 * n,)),
                   *[pltpu.HBM(t.shape, t.dtype) for t in srcs], *[pltpu.HBM(t.shape, t.dtype) for t in lands],
                   jax.ShapeDtypeStruct((8, LANES), F32)),
        in_specs=[HBM] * (2 * n) + [ANY],
        out_specs=(SEM, SEM, *[HBM] * (2 * n), pl.BlockSpec(memory_space=pltpu.VMEM)),
        input_output_aliases={i: 2 + i for i in range(2 * n)},
        compiler_params=pltpu.CompilerParams(has_side_effects=EFFECT),
    )(*[hbm(t) for t in srcs], *[hbm(t) for t in lands], after)
    return out[0], out[1], list(out[2:2 + n]), list(out[2 + n:2 + 2 * n]), out[-1]


def _exchange_wait(name, kind, send_sems, recv_sems, srcs, lands, after):
    n = len(srcs)

    def body(*refs):
        ins, lnd = refs[:n], refs[n:2 * n]
        s_sems, r_sems = refs[2 * n], refs[2 * n + 1]
        for a in range(n):
            for k in PEER_ORDER:
                _exchange_copy(kind, k, a, ins, lnd, s_sems, r_sems).wait_send()
                _exchange_copy(kind, k, a, ins, lnd, s_sems, r_sems, arriving=True).wait_recv()

    out = pl.pallas_call(
        body, name=name,
        out_shape=(*[pltpu.HBM(t.shape, t.dtype) for t in srcs], *[pltpu.HBM(t.shape, t.dtype) for t in lands]),
        in_specs=[HBM] * (2 * n) + [SEM, SEM, ANY],
        out_specs=tuple([HBM] * (2 * n)),
        input_output_aliases={i: i for i in range(2 * n)},
        compiler_params=pltpu.CompilerParams(has_side_effects=EFFECT),
    )(*srcs, *lands, send_sems, recv_sems, after)
    return list(out[:n]), list(out[n:])


def _sum_parts(name, me, mine, parts, rows):
    _, R, C = mine.shape

    def body(me_ref, own_ref, p_ref, o_ref):
        acc = own_ref[0].astype(F32)
        for s in range(N_DEV - 1):
            acc = acc + p_ref[s].astype(F32)
        o_ref[...] = acc

    return pl.pallas_call(
        body, name=name,
        grid_spec=pltpu.PrefetchScalarGridSpec(
            num_scalar_prefetch=1, grid=(R // rows,),
            in_specs=[pl.BlockSpec((1, rows, C), lambda i, me_ref: (me_ref[0], i, 0)),
                      pl.BlockSpec((N_DEV - 1, rows, C), lambda i, me_ref: (0, i, 0))],
            out_specs=pl.BlockSpec((rows, C), lambda i, me_ref: (i, 0))),
        out_shape=jax.ShapeDtypeStruct((R, C), F32), compiler_params=_params(1),
    )(me, mine, parts)


def _sum_slots(name, a, rows):
    _, R, C = a.shape

    def body(a_ref, o_ref):
        acc = a_ref[0].astype(F32)
        for s in range(1, N_DEV):
            acc = acc + a_ref[s].astype(F32)
        o_ref[...] = acc

    return pl.pallas_call(
        body, name=name, grid=(R // rows,),
        in_specs=[pl.BlockSpec((N_DEV, rows, C), lambda i: (0, i, 0))],
        out_specs=pl.BlockSpec((rows, C), lambda i: (i, 0)),
        out_shape=jax.ShapeDtypeStruct((R, C), F32), compiler_params=_params(1),
    )(a)


_DIMS = {"nn": (((1,), (0,)), ((), ())), "nt": (((1,), (1,)), ((), ())), "tn": (((0,), (0,)), ((), ()))}


def _mm(name, mode, a_list, b_list, pairs, n_acc, M, N, K, tm, tn, tk, epilogue, out_dtypes, extras=(), after=None):
    tm, tn, tk = min(tm, M), min(tn, N), min(tk, K)
    assert M % tm == 0 and N % tn == 0 and K % tk == 0, (name, M, N, K, tm, tn, tk)
    nk = K // tk
    na, nb, ne, no = len(a_list), len(b_list), len(extras), len(out_dtypes)
    dims = _DIMS[mode]
    a_spec = (pl.BlockSpec((tk, tm), lambda j, i, k: (k, i)) if mode == "tn"
              else pl.BlockSpec((tm, tk), lambda j, i, k: (i, k)))
    b_spec = (pl.BlockSpec((tn, tk), lambda j, i, k: (j, k)) if mode == "nt"
              else pl.BlockSpec((tk, tn), lambda j, i, k: (k, j)))
    o_spec = pl.BlockSpec((tm, tn), lambda j, i, k: (i, j))
    tail = [] if after is None else [after]

    def body(*refs):
        a_refs, b_refs = refs[:na], refs[na:na + nb]
        e_refs = refs[na + nb:na + nb + ne]
        first_out = na + nb + ne + len(tail)
        o_refs = refs[first_out:first_out + no]
        acc_refs = refs[first_out + no:]

        def partial(oi):
            tot = None
            for ai, bi, ti in pairs:
                if ti == oi:
                    d = lax.dot_general(a_refs[ai][...], b_refs[bi][...], dims, preferred_element_type=F32)
                    tot = d if tot is None else tot + d
            return tot

        def finish(accs):
            outs = epilogue(accs, [e[...] for e in e_refs])
            for o_ref, o in zip(o_refs, outs):
                o_ref[...] = o.astype(o_ref.dtype)

        if nk == 1:
            finish([partial(oi) for oi in range(n_acc)])
        else:
            k = pl.program_id(2)

            @pl.when(k == 0)
            def _():
                for r in acc_refs:
                    r[...] = jnp.zeros_like(r)

            for oi in range(n_acc):
                acc_refs[oi][...] += partial(oi)

            @pl.when(k == nk - 1)
            def _():
                finish([r[...] for r in acc_refs])

    return pl.pallas_call(
        body, name=name, grid=(N // tn, M // tm, nk),
        in_specs=[a_spec] * na + [b_spec] * nb + [o_spec] * ne + [ANY] * len(tail),
        out_specs=[o_spec] * no,
        out_shape=[jax.ShapeDtypeStruct((M, N), dt) for dt in out_dtypes],
        scratch_shapes=[pltpu.VMEM((tm, tn), F32)] * (n_acc if nk > 1 else 0),
        compiler_params=_params(3),
    )(*a_list, *b_list, *extras, *tail)


def _ep_residual(accs, ex):
    return (ex[0] + accs[0],)


def _ep_rows_rms_bwd(acc, ex, rows):
    xv, res = ex
    r = lax.rsqrt(jnp.mean(xv * xv, axis=-1, keepdims=True) + EPS)
    xh = xv * r
    dg = jnp.sum(acc * xh, axis=0, keepdims=True)
    dxh = acc * rows[0]
    dx = r * (dxh - xh * jnp.mean(dxh * xh, axis=-1, keepdims=True)) + res
    return (dx, dx), (dg,)


def _sigmoid(x):
    return 0.5 * (jnp.tanh(0.5 * x) + 1.0)


def _ep_plain(accs, ex):
    return (accs[0],)


def _ep_swiglu(accs, ex):
    g, u = accs
    return g, u, (g * _sigmoid(g)) * u


def _ep_swiglu_bwd(accs, ex):
    dact = accs[0]
    g, u = ex
    sig = _sigmoid(g)
    silu = g * sig
    return dact * u * (sig * (1.0 + g * (1.0 - sig))), dact * silu


def _ep_two(accs, ex):
    return accs[0], accs[1]


ROWS = 256


def _row_spec(tm, width, col=0):
    return pl.BlockSpec((tm, width), lambda i: (i, col))


def _full_spec(shape):
    nd = len(shape)
    return pl.BlockSpec(shape, lambda i: (0,) * nd)


def _rms_fwd(name, x, g):
    T, D = x.shape
    tm = min(ROWS, T)

    def body(x_ref, g_ref, o_ref):
        xv = x_ref[...]
        r = lax.rsqrt(jnp.mean(xv * xv, axis=-1, keepdims=True) + EPS)
        o_ref[...] = ((xv * r) * g_ref[...]).astype(BF16)

    return pl.pallas_call(
        body, name=name, grid=(T // tm,),
        in_specs=[_row_spec(tm, D), _full_spec((1, D))], out_specs=_row_spec(tm, D),
        out_shape=jax.ShapeDtypeStruct((T, D), BF16), compiler_params=_params(1),
    )(x, g)


def _rms_bwd(name, dyn, x, g, res):
    T, D = x.shape
    tm = min(ROWS, T)

    def body(dyn_ref, x_ref, g_ref, res_ref, dx_ref, dx16_ref, dg_ref):
        outs, (dg,) = _ep_rows_rms_bwd(dyn_ref[...], [x_ref[...], res_ref[...]], [g_ref[...]])

        @pl.when(pl.program_id(0) == 0)
        def _():
            dg_ref[...] = jnp.zeros_like(dg_ref)

        dg_ref[...] += dg
        dx_ref[...] = outs[0]
        dx16_ref[...] = outs[1].astype(BF16)

    return pl.pallas_call(
        body, name=name, grid=(T // tm,),
        in_specs=[_row_spec(tm, D), _row_spec(tm, D), _full_spec((1, D)), _row_spec(tm, D)],
        out_specs=[_row_spec(tm, D), _row_spec(tm, D), _full_spec((1, D))],
        out_shape=[jax.ShapeDtypeStruct((T, D), F32), jax.ShapeDtypeStruct((T, D), BF16),
                   jax.ShapeDtypeStruct((1, D), F32)],
        compiler_params=_params(1),
    )(dyn, x, g, res)


def _outnorm_fwd(name, a, b, ga, gb):
    T, W = a.shape
    tm = min(ROWS, T)

    def body(a_ref, b_ref, ga_ref, gb_ref, o_ref):
        for src, gain, lo in ((a_ref, ga_ref, 0), (b_ref, gb_ref, W)):
            v = src[...]
            r = lax.rsqrt(jnp.mean(v * v, axis=-1, keepdims=True) + EPS)
            o_ref[:, lo:lo + W] = ((v * r) * gain[...]).astype(BF16)

    return pl.pallas_call(
        body, name=name, grid=(T // tm,),
        in_specs=[_row_spec(tm, W), _row_spec(tm, W), _full_spec((1, W)), _full_spec((1, W))],
        out_specs=_row_spec(tm, 2 * W),
        out_shape=jax.ShapeDtypeStruct((T, 2 * W), BF16), compiler_params=_params(1),
    )(a, b, ga, gb)


def _outnorm_bwd(name, dmixed, a, b, ga, gb):
    T, W = a.shape
    tm = min(ROWS, T)

    def body(dm_ref, a_ref, b_ref, ga_ref, gb_ref, da_ref, db_ref, dga_ref, dgb_ref):
        first = pl.program_id(0) == 0
        for src, gain, lo, dsrc, dgain in ((a_ref, ga_ref, 0, da_ref, dga_ref), (b_ref, gb_ref, W, db_ref, dgb_ref)):
            v, dv = src[...], dm_ref[:, lo:lo + W]
            r = lax.rsqrt(jnp.mean(v * v, axis=-1, keepdims=True) + EPS)
            vh = v * r

            @pl.when(first)
            def _():
                dgain[...] = jnp.zeros_like(dgain)

            dgain[...] += jnp.sum(dv * vh, axis=0, keepdims=True)
            dvh = dv * gain[...]
            dsrc[...] = r * (dvh - vh * jnp.mean(dvh * vh, axis=-1, keepdims=True))

    return pl.pallas_call(
        body, name=name, grid=(T // tm,),
        in_specs=[_row_spec(tm, 2 * W), _row_spec(tm, W), _row_spec(tm, W), _full_spec((1, W)), _full_spec((1, W))],
        out_specs=[_row_spec(tm, W), _row_spec(tm, W), _full_spec((1, W)), _full_spec((1, W))],
        out_shape=[jax.ShapeDtypeStruct((T, W), F32), jax.ShapeDtypeStruct((T, W), F32),
                   jax.ShapeDtypeStruct((1, W), F32), jax.ShapeDtypeStruct((1, W), F32)],
        compiler_params=_params(1),
    )(dmixed, a, b, ga, gb)


def _loss_head(name, y, target):
    T, D = y.shape
    tm = min(ROWS, T)

    def body(y_ref, t_ref, loss_ref, dy_ref, dy16_ref):
        d = y_ref[...] - t_ref[...]

        @pl.when(pl.program_id(0) == 0)
        def _():
            loss_ref[...] = jnp.zeros_like(loss_ref)

        per_token = jnp.mean(d * d, axis=-1, keepdims=True)
        loss_ref[...] += 0.5 * jnp.sum(per_token, axis=0, keepdims=True)
        dy = d * (1.0 / D)
        dy_ref[...] = dy
        dy16_ref[...] = dy.astype(BF16)

    return pl.pallas_call(
        body, name=name, grid=(T // tm,),
        in_specs=[_row_spec(tm, D), _row_spec(tm, D)],
        out_specs=[_full_spec((1, 1)), _row_spec(tm, D), _row_spec(tm, D)],
        out_shape=[jax.ShapeDtypeStruct((1, 1), F32), jax.ShapeDtypeStruct((T, D), F32),
                   jax.ShapeDtypeStruct((T, D), BF16)],
        compiler_params=_params(1),
    )(y, target)


def _bf16_dot(a, b, dims):
    return lax.dot_general(a.astype(BF16), b.astype(BF16), dims, preferred_element_type=F32)


@jax.custom_vjp
def _dot_nn(a, b):
    return _bf16_dot(a, b, _DIMS["nn"])


def _dot_nn_fwd(a, b):
    return _dot_nn(a, b), (a, b)


def _dot_nn_bwd(saved, ct):
    a, b = saved
    return _bf16_dot(ct, b, _DIMS["nt"]), _bf16_dot(a, ct, _DIMS["tn"])


_dot_nn.defvjp(_dot_nn_fwd, _dot_nn_bwd)


@jax.custom_vjp
def _dot_nt(a, b):
    return _bf16_dot(a, b, _DIMS["nt"])


def _dot_nt_fwd(a, b):
    return _dot_nt(a, b), (a, b)


def _dot_nt_bwd(saved, ct):
    a, b = saved
    return _bf16_dot(ct, b, _DIMS["nn"]), _bf16_dot(ct, a, _DIMS["tn"])


_dot_nt.defvjp(_dot_nt_fwd, _dot_nt_bwd)


def _iota(shape, dim):
    return lax.broadcasted_iota(jnp.int32, shape, dim)


def _head_sum_impl(x):
    same_head = (_iota((LANES, LANES), 0) // HEAD_DIM == _iota((LANES, LANES), 1) // HEAD_DIM).astype(BF16)
    pieces = []
    for i in range(x.shape[1] // LANES):
        xs = x[:, i * LANES:(i + 1) * LANES]
        hi = xs.astype(BF16)
        r1 = xs - hi.astype(F32)
        mid = r1.astype(BF16)
        lo = (r1 - mid.astype(F32)).astype(BF16)
        tot = None
        for part in (hi, mid, lo):
            d = lax.dot_general(part, same_head, _DIMS["nn"], preferred_element_type=F32)
            tot = d if tot is None else tot + d
        pieces.append(tot)
    return pieces[0] if len(pieces) == 1 else jnp.concatenate(pieces, axis=1)


@jax.custom_vjp
def _head_sum(x):
    return _head_sum_impl(x)


_head_sum.defvjp(lambda x: (_head_sum_impl(x), None), lambda _, ct: (_head_sum_impl(ct),))


def _head_rms(x, g_row):
    ms = _head_sum(x * x) * (1.0 / HEAD_DIM)
    return (x * lax.rsqrt(ms + EPS)) * g_row


@jax.custom_vjp
def _swap_halves(x):
    return pltpu.roll(x, HEAD_DIM, 1)


_swap_halves.defvjp(lambda x: (pltpu.roll(x, HEAD_DIM, 1), None), lambda _, ct: (pltpu.roll(ct, HEAD_DIM, 1),))


def _gelu(x):
    return 0.5 * x * (1.0 + lax.erf(x * (1.0 / math.sqrt(2.0))))


def _sgu_block(u_raw, v_raw, g_row, w, b_full):
    u, v = _gelu(u_raw), _gelu(v_raw)
    vn = _head_rms(v, g_row)
    causal = _iota((CHUNK, CHUNK), 0) >= _iota((CHUNK, CHUNK), 1)
    low_half = _iota((CHUNK, LANES), 1) < HEAD_DIM
    gates = []
    for p in range(v.shape[1] // LANES):
        vp = vn[:, p * LANES:(p + 1) * LANES]
        g0 = _dot_nn(jnp.where(causal, w[2 * p], 0.0), vp)
        g1 = _dot_nn(jnp.where(causal, w[2 * p + 1], 0.0), vp)
        gates.append(jnp.where(low_half, g0, g1))
    gate = jnp.concatenate(gates, axis=1) + b_full
    return u * gate


def _attn_block(q_raw, k_prev, k_cur, v_prev, v_cur, qg_row, kg_row, sinks, bias):
    scale = 1.0 / math.sqrt(HEAD_DIM)
    n_q_heads = q_raw.shape[1] // HEAD_DIM
    qn = _head_rms(q_raw, qg_row) * scale
    kn_prev, kn_cur = _head_rms(k_prev, kg_row), _head_rms(k_cur, kg_row)
    own = _iota((GROUP * BLOCK, BLOCK), 1) <= (_iota((GROUP * BLOCK, BLOCK), 0) & (BLOCK - 1))
    low_half = _iota((BLOCK, LANES), 1) < HEAD_DIM
    heads = [None] * n_q_heads
    for kv in range(n_q_heads // GROUP):
        grp, kv_low = kv // 2, kv % 2 == 0
        lanes = slice(grp * LANES, (grp + 1) * LANES)
        keep = low_half if kv_low else ~low_half
        stacked = []
        for g in range(GROUP):
            h = GROUP * kv + g
            q2 = qn[:, (h // 2) * LANES:(h // 2 + 1) * LANES]
            src = q2 if (h % 2 == 0) == kv_low else _swap_halves(q2)
            stacked.append(jnp.where(keep, src, 0.0))
        q4 = jnp.concatenate(stacked, axis=0)
        bias4 = jnp.concatenate([bias[GROUP * kv + g] for g in range(GROUP)], axis=0)
        sink4 = jnp.concatenate([jnp.broadcast_to(sinks[GROUP * kv + g], (BLOCK, 1)) for g in range(GROUP)], axis=0)
        s = jnp.where(own, _dot_nt(q4, kn_cur[:, lanes]), _dot_nt(q4, kn_prev[:, lanes])) + bias4
        m = lax.stop_gradient(jnp.maximum(jnp.max(s, axis=1, keepdims=True), sink4))
        e = jnp.exp(s - m)
        denom = jnp.sum(e, axis=1, keepdims=True) + jnp.exp(sink4 - m)
        p = e * (1.0 / denom)
        o4 = _dot_nn(jnp.where(own, p, 0.0), v_cur[:, lanes]) + _dot_nn(jnp.where(own, 0.0, p), v_prev[:, lanes])
        for g in range(GROUP):
            h = GROUP * kv + g
            o = o4[g * BLOCK:(g + 1) * BLOCK]
            heads[h] = o if (h % 2 == 0) == kv_low else _swap_halves(o)
    outs = [jnp.where(low_half, heads[2 * p], heads[2 * p + 1]) for p in range(n_q_heads // 2)]
    return jnp.concatenate(outs, axis=1)


def _bias_table():
    i, j = np.arange(BLOCK)[:, None], np.arange(BLOCK)[None, :]
    n = np.where(j <= i, i - j, i + BLOCK - j)
    max_exact = NUM_BUCKETS // 2
    nf = np.maximum(n, 1).astype(np.float64)
    large = max_exact + (np.log(nf / max_exact) / math.log(MAX_DISTANCE / max_exact) * (NUM_BUCKETS - max_exact)).astype(np.int32)
    large = np.minimum(large, NUM_BUCKETS - 1)
    return np.where(n < max_exact, n, large).astype(np.int32)


def _bias_fwd(name, rel_bias, buckets):
    nb_, nh = rel_bias.shape

    def body(rb_ref, bk_ref, o_ref):
        bk = bk_ref[...]
        own = _iota(bk.shape, 1) <= _iota(bk.shape, 0)
        for h in range(nh):
            acc = jnp.zeros(bk.shape, F32)
            for b in range(nb_):
                acc = jnp.where(bk == b, rb_ref[b, h], acc)
            o_ref[1, h] = acc
            o_ref[0, h] = jnp.where(own, acc, NEG_INF)

    return pl.pallas_call(
        body, name=name,
        in_specs=[pl.BlockSpec(memory_space=pltpu.SMEM), pl.BlockSpec(memory_space=pltpu.VMEM)],
        out_specs=pl.BlockSpec(memory_space=pltpu.VMEM),
        out_shape=jax.ShapeDtypeStruct((2, nh) + buckets.shape, F32),
    )(rel_bias, buckets)


def _bias_bwd(name, dbias, buckets, nb_):
    nh = dbias.shape[0]

    def body(db_ref, bk_ref, o_ref):
        bk = bk_ref[...]
        for h in range(nh):
            d = db_ref[h]
            for b in range(nb_):
                s = jnp.sum(jnp.where(bk == b, d, 0.0), axis=0, keepdims=True)
                s = jnp.sum(s, axis=1, keepdims=True)
                o_ref[b * nh + h:b * nh + h + 1, :] = jnp.broadcast_to(s, (1, LANES))

    return pl.pallas_call(
        body, name=name,
        in_specs=[pl.BlockSpec(memory_space=pltpu.VMEM), pl.BlockSpec(memory_space=pltpu.VMEM)],
        out_specs=pl.BlockSpec(memory_space=pltpu.VMEM),
        out_shape=jax.ShapeDtypeStruct((nb_ * nh, LANES), F32),
    )(dbias, buckets)


def _sgu_fwd(name, z, g_row, w, b_full, W):
    T = z.shape[0]

    def body(u_ref, v_ref, g_ref, w_ref, b_ref, o_ref):
        o_ref[...] = _sgu_block(u_ref[...], v_ref[...], g_ref[...], w_ref[...], b_ref[...])

    return pl.pallas_call(
        body, name=name, grid=(T // CHUNK,),
        in_specs=[_row_spec(CHUNK, W, 0), _row_spec(CHUNK, W, 1), _full_spec((1, W)),
                  _full_spec(w.shape), _full_spec((CHUNK, W))],
        out_specs=_row_spec(CHUNK, W),
        out_shape=jax.ShapeDtypeStruct((T, W), F32), compiler_params=_params(1),
    )(z, z, g_row, w, b_full)


def _sgu_bwd(name, z, g_row, w, b_full, d_out, W):
    T = z.shape[0]

    def body(u_ref, v_ref, g_ref, w_ref, b_ref, do_ref, dz_ref, dg_ref, dw_ref, db_ref):
        _, vjp = jax.vjp(_sgu_block, u_ref[...], v_ref[...], g_ref[...], w_ref[...], b_ref[...])
        du, dv, dg, dw, db = vjp(do_ref[...])
        dz_ref[:, :W] = du.astype(BF16)
        dz_ref[:, W:] = dv.astype(BF16)
        i = pl.program_id(0)

        @pl.when(i == 0)
        def _():
            dg_ref[...] = jnp.zeros_like(dg_ref)
            dw_ref[...] = jnp.zeros_like(dw_ref)
            db_ref[...] = jnp.zeros_like(db_ref)

        dg_ref[...] += dg
        dw_ref[...] += dw
        db_ref[...] += db

        @pl.when(i == pl.num_programs(0) - 1)
        def _():
            db_ref[...] = _head_sum_impl(db_ref[...])

    return pl.pallas_call(
        body, name=name, grid=(T // CHUNK,),
        in_specs=[_row_spec(CHUNK, W, 0), _row_spec(CHUNK, W, 1), _full_spec((1, W)),
                  _full_spec(w.shape), _full_spec((CHUNK, W)), _row_spec(CHUNK, W)],
        out_specs=[_row_spec(CHUNK, 2 * W), _full_spec((1, W)), _full_spec(w.shape), _full_spec((CHUNK, W))],
        out_shape=[jax.ShapeDtypeStruct((T, 2 * W), BF16), jax.ShapeDtypeStruct((1, W), F32),
                   jax.ShapeDtypeStruct(w.shape, F32), jax.ShapeDtypeStruct((CHUNK, W), F32)],
        compiler_params=_params(1),
    )(z, z, g_row, w, b_full, d_out)


def _attn_specs(WQ, WKV, q_col, k_col, v_col, blk_of):
    prev_of = lambda i: jnp.maximum(blk_of(i) - 1, 0)
    return [pl.BlockSpec((BLOCK, WQ), lambda i: (blk_of(i), q_col)),
            pl.BlockSpec((BLOCK, WKV), lambda i: (prev_of(i), k_col)),
            pl.BlockSpec((BLOCK, WKV), lambda i: (blk_of(i), k_col)),
            pl.BlockSpec((BLOCK, WKV), lambda i: (prev_of(i), v_col)),
            pl.BlockSpec((BLOCK, WKV), lambda i: (blk_of(i), v_col))]


def _bias_spec(bias, blk_of):
    return pl.BlockSpec((1,) + bias.shape[1:], lambda i: (jnp.minimum(blk_of(i), 1), 0, 0, 0))


def _attn_fwd(name, z, qg_row, kg_row, sinks_col, bias, WQ, WKV, q_col, k_col, v_col):
    T = z.shape[0]
    nh = sinks_col.shape[0]

    def body(q_ref, kp_ref, kc_ref, vp_ref, vc_ref, qg_ref, kg_ref, s_ref, b_ref, o_ref):
        sinks = [s_ref[h:h + 1, :] for h in range(nh)]
        o_ref[...] = _attn_block(q_ref[...], kp_ref[...], kc_ref[...], vp_ref[...], vc_ref[...],
                                 qg_ref[...], kg_ref[...], sinks, b_ref[0])

    return pl.pallas_call(
        body, name=name, grid=(T // BLOCK,),
        in_specs=_attn_specs(WQ, WKV, q_col, k_col, v_col, lambda i: i)
        + [_full_spec((1, WQ)), _full_spec((1, WKV)), _full_spec((nh, 1)), _bias_spec(bias, lambda i: i)],
        out_specs=_row_spec(BLOCK, WQ),
        out_shape=jax.ShapeDtypeStruct((T, WQ), F32), compiler_params=_params(1),
    )(z, z, z, z, z, qg_row, kg_row, sinks_col, bias)


def _attn_bwd(name, z, qg_row, kg_row, sinks_col, bias, d_out, WQ, WKV, q_col, k_col, v_col):
    T = z.shape[0]
    nblk = T // BLOCK
    nh = sinks_col.shape[0]
    blk_of = lambda i: nblk - 1 - i

    def body(q_ref, kp_ref, kc_ref, vp_ref, vc_ref, qg_ref, kg_ref, s_ref, b_ref, do_ref,
             dq_ref, dk_ref, dv_ref, dqg_ref, dkg_ref, ds_ref, db_ref, carry_k, carry_v):
        i = pl.program_id(0)
        sinks = [s_ref[h:h + 1, :] for h in range(nh)]
        _, vjp = jax.vjp(_attn_block, q_ref[...], kp_ref[...], kc_ref[...], vp_ref[...], vc_ref[...],
                         qg_ref[...], kg_ref[...], sinks, b_ref[0])
        dq, dkp, dkc, dvp, dvc, dqg, dkg, dsk, dbs = vjp(do_ref[...])

        @pl.when(i == 0)
        def _():
            carry_k[...] = jnp.zeros_like(carry_k)
            carry_v[...] = jnp.zeros_like(carry_v)
            dqg_ref[...] = jnp.zeros_like(dqg_ref)
            dkg_ref[...] = jnp.zeros_like(dkg_ref)
            ds_ref[...] = jnp.zeros_like(ds_ref)
            db_ref[...] = jnp.zeros_like(db_ref)

        dq_ref[...] = dq.astype(BF16)
        dk_ref[...] = (dkc + carry_k[...]).astype(BF16)
        dv_ref[...] = (dvc + carry_v[...]).astype(BF16)
        carry_k[...] = dkp
        carry_v[...] = dvp
        dqg_ref[...] += dqg
        dkg_ref[...] += dkg
        for h in range(nh):
            ds_ref[h:h + 1, :] += dsk[h]
        db_ref[...] += dbs

    return pl.pallas_call(
        body, name=name, grid=(nblk,),
        in_specs=_attn_specs(WQ, WKV, q_col, k_col, v_col, blk_of)
        + [_full_spec((1, WQ)), _full_spec((1, WKV)), _full_spec((nh, 1)), _bias_spec(bias, blk_of),
           pl.BlockSpec((BLOCK, WQ), lambda i: (blk_of(i), 0))],
        out_specs=[pl.BlockSpec((BLOCK, WQ), lambda i: (blk_of(i), 0)),
                   pl.BlockSpec((BLOCK, WKV), lambda i: (blk_of(i), 0)),
                   pl.BlockSpec((BLOCK, WKV), lambda i: (blk_of(i), 0)),
                   _full_spec((1, WQ)), _full_spec((1, WKV)), _full_spec((nh, 1)), _full_spec(bias.shape[1:])],
        out_shape=[jax.ShapeDtypeStruct((T, WQ), BF16), jax.ShapeDtypeStruct((T, WKV), BF16),
                   jax.ShapeDtypeStruct((T, WKV), BF16), jax.ShapeDtypeStruct((1, WQ), F32),
                   jax.ShapeDtypeStruct((1, WKV), F32), jax.ShapeDtypeStruct((nh, 1), F32),
                   jax.ShapeDtypeStruct(bias.shape[1:], F32)],
        scratch_shapes=[pltpu.VMEM((BLOCK, WKV), F32), pltpu.VMEM((BLOCK, WKV), F32)],
        compiler_params=_params(1),
    )(z, z, z, z, z, qg_row, kg_row, sinks_col, bias, d_out)


def _adamw(name, w, g, m, v):
    shape = w.shape
    C = shape[-1]
    if w.ndim == 3 and shape[1] % ROWS == 0:
        work = shape
        grid = (shape[0], shape[1] // ROWS)
        spec = pl.BlockSpec((1, ROWS, C), lambda l, i: (l, i, 0))
    else:
        R = int(np.prod(shape[:-1]))
        tr = ROWS if R % ROWS == 0 else R
        work = (R, C)
        grid = (R // tr,)
        spec = pl.BlockSpec((tr, C), lambda i: (i, 0))
    w2, g2, m2, v2 = (t.reshape(work) for t in (w, g, m, v))

    def body(w_ref, g_ref, m_ref, v_ref, d_ref, nm_ref, nv_ref):
        gv = g_ref[...]
        nm = ADAM_B1 * m_ref[...] + (1.0 - ADAM_B1) * gv
        nv = ADAM_B2 * v_ref[...] + (1.0 - ADAM_B2) * (gv * gv)
        m_hat = nm / (1.0 - ADAM_B1 ** ADAM_STEP)
        v_hat = nv / (1.0 - ADAM_B2 ** ADAM_STEP)
        d_ref[...] = -ADAM_LR * (m_hat / (jnp.sqrt(v_hat) + ADAM_EPS) + ADAM_WD * w_ref[...])
        nm_ref[...] = nm
        nv_ref[...] = nv

    outs = pl.pallas_call(
        body, name=name, grid=grid, in_specs=[spec] * 4, out_specs=[spec] * 3,
        out_shape=[jax.ShapeDtypeStruct(work, F32)] * 3, compiler_params=_params(len(grid)),
    )(w2, g2, m2, v2)
    return tuple(o.reshape(shape) for o in outs)


def _pad_rows(flat):
    n = flat.shape[0]
    tile = 8 * LANES
    padded = -(-n // tile) * tile
    return jnp.pad(flat, (0, padded - n)).reshape(padded // LANES, LANES)


def kernel(x, rel_bias, norm1_g, w_in, sgu_norm_g, sgu_w, sgu_b, q_norm_g, k_norm_g, sinks, out_norm_a, out_norm_b, w_out, norm2_g, w_gate, w_up, w_down, loss_target, m_rel_bias, m_norm1_g, m_w_in, m_sgu_norm_g, m_sgu_w, m_sgu_b, m_q_norm_g, m_k_norm_g, m_sinks, m_out_norm_a, m_out_norm_b, m_w_out, m_norm2_g, m_w_gate, m_w_up, m_w_down, v_rel_bias, v_norm1_g, v_w_in, v_sgu_norm_g, v_sgu_w, v_sgu_b, v_q_norm_g, v_k_norm_g, v_sinks, v_out_norm_a, v_out_norm_b, v_w_out, v_norm2_g, v_w_gate, v_w_up, v_w_down):
    L = w_in.shape[0]
    T, D = x.shape[1], x.shape[2]
    W = D // 2
    NH = W // HEAD_DIM
    WKV = N_KV_HEADS * HEAD_DIM
    IN = N_DEV * w_in.shape[2]
    FF = N_DEV * w_gate.shape[2]
    assert IN == 2 * W + W + 2 * WKV and NH // N_KV_HEADS == GROUP
    q_col, k_col, v_col = 2 * W // W, (3 * W) // WKV, (3 * W + WKV) // WKV
    x0 = x.reshape(T, D)
    target = loss_target.reshape(T, D)

    shards = [jnp.swapaxes(w_in, 1, 2).astype(BF16), w_out.astype(BF16), jnp.swapaxes(w_gate, 1, 2).astype(BF16),
              jnp.swapaxes(w_up, 1, 2).astype(BF16), w_down.astype(BF16)]
    me = 4 * lax.axis_index("x") + 2 * lax.axis_index("y") + lax.axis_index("c")

    def landing(block):
        return lax.dynamic_update_slice(lax.empty((N_DEV,) + block.shape, block.dtype), block[None], (me, 0, 0))

    def as_matrices(got):
        return [g.reshape(N_DEV * g.shape[1], g.shape[2]) for g in got]

    full = {0: as_matrices(_all_gather("gather_weights_0", [s[0] for s in shards]))}
    flight = {}

    def start_gather(l, after):
        srcs = [s[l] for s in shards]
        flight[l] = _exchange_start(f"gather_start_{l}", "gather", srcs, [landing(s) for s in srcs], after)
        return flight[l][4]

    def finish_gather(l, after):
        s_sem, r_sem, srcs, lands, _ = flight.pop(l)
        full[l] = as_matrices(_exchange_wait(f"gather_wait_{l}", "gather", s_sem, r_sem, srcs, lands, after)[1])

    buckets = jnp.asarray(_bias_table())
    bias = _bias_fwd("bias_table", rel_bias, buckets)

    saved = []
    xl = x0
    for l in range(L):
        if l > 0:
            finish_gather(l, xl)
        token = start_gather(l + 1, full[l][0] if l == 0 else xl) if l + 1 < L else xl
        w_in_t, w_o, w_g_t, w_u_t, w_d = full[l]
        g1, g2 = norm1_g[l][None], norm2_g[l][None]
        sg_row = sgu_norm_g[l].reshape(1, W)
        b_full = jnp.repeat(sgu_b[l].T, HEAD_DIM, axis=1)
        qg_row = jnp.tile(q_norm_g[l], NH)[None]
        kg_row = jnp.tile(k_norm_g[l], N_KV_HEADS)[None]
        sinks_col = sinks[l][:, None]
        ga, gb = out_norm_a[l][None], out_norm_b[l][None]

        h = _rms_fwd(f"norm1_{l}", xl, g1)
        (z,) = _mm(f"proj_in_{l}", "nt", [h], [w_in_t], [(0, 0, 0)], 1, T, IN, D, 1024, 896, D, _ep_plain, [F32],
                   after=token)
        out_a = _sgu_fwd(f"sgu_{l}", z, sg_row, sgu_w[l], b_full, W)
        out_b = _attn_fwd(f"attn_{l}", z, qg_row, kg_row, sinks_col, bias, W, WKV, q_col, k_col, v_col)
        mixed = _outnorm_fwd(f"outnorm_{l}", out_a, out_b, ga, gb)
        (x1,) = _mm(f"proj_out_{l}", "nn", [mixed], [w_o], [(0, 0, 0)], 1, T, D, D, 1024, 1024, D,
                    _ep_residual, [F32], extras=[xl])
        h2 = _rms_fwd(f"norm2_{l}", x1, g2)
        gate, up, act = _mm(f"mlp_in_{l}", "nt", [h2], [w_g_t, w_u_t], [(0, 0, 0), (0, 1, 1)], 2, T, FF, D,
                            1024, 512, D, _ep_swiglu, [BF16, BF16, BF16])
        (x2,) = _mm(f"mlp_out_{l}", "nn", [act], [w_d], [(0, 0, 0)], 1, T, D, FF, 512, 1024, FF,
                    _ep_residual, [F32], extras=[x1])
        saved.append((xl, h, z, out_a, out_b, mixed, x1, h2, gate, up, act,
                      g1, g2, sg_row, b_full, qg_row, kg_row, sinks_col, ga, gb))
        xl = x2

    loss_part, dy, dy16 = _loss_head("loss_head", xl, target)

    dbias = None
    small = [None] * L
    scatters = []

    def start_scatter(name, which, grads_t, after):
        srcs = [t.reshape(N_DEV, t.shape[0] // N_DEV, D) for t in grads_t]
        lands = [lax.empty((N_DEV - 1,) + s.shape[1:], BF16) for s in srcs]
        s_sem, r_sem, srcs, lands, tok = _exchange_start(name, "scatter", srcs, lands, after)
        scatters.append((name, which, s_sem, r_sem, srcs, lands))
        return tok

    token = None
    for l in reversed(range(L)):
        w_in_t, w_o, w_g_t, w_u_t, w_d = full[l]
        (xl, h, z, out_a, out_b, mixed, x1, h2, gate, up, act,
         g1, g2, sg_row, b_full, qg_row, kg_row, sinks_col, ga, gb) = saved[l]

        dgate, dup = _mm(f"d_mlp_out_{l}", "nt", [dy16], [w_d], [(0, 0, 0)], 1, T, FF, D, 1024, 1408, D,
                         _ep_swiglu_bwd, [BF16, BF16], extras=[gate, up], after=token)
        (dw_d,) = _mm(f"dw_down_{l}", "tn", [act], [dy16], [(0, 0, 0)], 1, FF, D, T, 1408, 1024, 2048, _ep_plain, [BF16])
        (dh2,) = _mm(f"d_mlp_in_{l}", "nn", [dgate, dup], [w_g_t, w_u_t], [(0, 0, 0), (1, 1, 0)], 1, T, D, FF,
                     512, 512, FF, _ep_plain, [F32])
        dx1, dx1_16, dg2 = _rms_bwd(f"d_norm2_{l}", dh2, x1, g2, dy)
        dw_g, dw_u = _mm(f"dw_gate_up_{l}", "tn", [dgate, dup], [h2], [(0, 0, 0), (1, 0, 1)], 2, FF, D, T,
                         1408, 1024, 1024, _ep_two, [BF16, BF16])
        token = start_scatter(f"scatter_mlp_start_{l}", (l, (4, 2, 3)), [dw_d, dw_g, dw_u], dw_u)
        (dmixed,) = _mm(f"d_proj_out_{l}", "nt", [dx1_16], [w_o], [(0, 0, 0)], 1, T, D, D, 1024, 1024, D, _ep_plain, [F32],
                        after=token)
        (dw_o,) = _mm(f"dw_out_{l}", "tn", [mixed], [dx1_16], [(0, 0, 0)], 1, D, D, T, 1024, 1024, 2048, _ep_plain, [BF16])
        d_a, d_b, dga, dgb = _outnorm_bwd(f"d_outnorm_{l}", dmixed, out_a, out_b, ga, gb)
        dz_uv, dsg, dsw, dsb = _sgu_bwd(f"d_sgu_{l}", z, sg_row, sgu_w[l], b_full, d_a, W)
        dq, dk, dv, dqg, dkg, dsk, dbs = _attn_bwd(f"d_attn_{l}", z, qg_row, kg_row, sinks_col, bias, d_b,
                                                   W, WKV, q_col, k_col, v_col)
        dbias = dbs if dbias is None else dbias + dbs
        dz = jnp.concatenate([dz_uv, dq, dk, dv], axis=1)
        (dh,) = _mm(f"d_proj_in_{l}", "nn", [dz], [w_in_t], [(0, 0, 0)], 1, T, D, IN, 1024, 1024, IN, _ep_plain, [F32])
        dy, dy16, dg1 = _rms_bwd(f"d_norm1_{l}", dh, xl, g1, dx1)
        (dw_i,) = _mm(f"dw_in_{l}", "tn", [dz], [h], [(0, 0, 0)], 1, IN, D, T, 896, 1024, 2048, _ep_plain, [BF16])
        token = start_scatter(f"scatter_mix_start_{l}", (l, (1, 0)), [dw_o, dw_i], dw_i)

        small[l] = dict(norm1_g=dg1[0], sgu_norm_g=dsg.reshape(NH, HEAD_DIM), sgu_w=dsw,
                        sgu_b=dsb[:, ::HEAD_DIM].T, q_norm_g=dqg.reshape(NH, HEAD_DIM).sum(0),
                        k_norm_g=dkg.reshape(N_KV_HEADS, HEAD_DIM).sum(0), sinks=dsk[:, 0],
                        out_norm_a=dga[0], out_norm_b=dgb[0], norm2_g=dg2[0])

    grad_x = dy.reshape(x.shape)
    d_rel = _bias_bwd("d_bias_table", dbias, buckets, NUM_BUCKETS)[:, 0].reshape(NUM_BUCKETS, NH)

    names = ["norm1_g", "sgu_norm_g", "sgu_w", "sgu_b", "q_norm_g", "k_norm_g", "sinks", "out_norm_a", "out_norm_b", "norm2_g"]
    parts = {"rel_bias": d_rel}
    for nme in names:
        parts[nme] = jnp.stack([small[l][nme] for l in range(L)])
    order = ["rel_bias"] + names
    packed = jnp.concatenate([_pad_rows(parts[nme].reshape(-1)) for nme in order], axis=0)
    small_flight = _exchange_start("gather_small_start", "gather", [packed], [landing(packed)], token)
    after = small_flight[4]

    me1 = me.reshape(1).astype(jnp.int32)
    grads_big = {}

    def finish_scatter(entry, after):
        name, (l, which), s_sem, r_sem, srcs, lands = entry
        srcs, lands = _exchange_wait(name.replace("start", "wait"), "scatter", s_sem, r_sem, srcs, lands, after)
        for i, src, land in zip(which, srcs, lands):
            rows = src.shape[1]
            after = _sum_parts(f"sum_grads_{l}_{i}", me1, src, land, 64 if rows % 64 == 0 else rows)
            grads_big[i, l] = after
        return after

    def stacked(i):
        return jnp.stack([grads_big[i, l] for l in range(L)])

    loss = lax.psum(loss_part[0, 0], ("x", "y", "c"))
    tr = lambda t: jnp.swapaxes(t, 1, 2)
    weights = dict(rel_bias=rel_bias, norm1_g=norm1_g, w_in=w_in, sgu_norm_g=sgu_norm_g, sgu_w=sgu_w, sgu_b=sgu_b,
                   q_norm_g=q_norm_g, k_norm_g=k_norm_g, sinks=sinks, out_norm_a=out_norm_a, out_norm_b=out_norm_b,
                   w_out=w_out, norm2_g=norm2_g, w_gate=w_gate, w_up=w_up, w_down=w_down)
    ms = dict(rel_bias=m_rel_bias, norm1_g=m_norm1_g, w_in=m_w_in, sgu_norm_g=m_sgu_norm_g, sgu_w=m_sgu_w, sgu_b=m_sgu_b,
              q_norm_g=m_q_norm_g, k_norm_g=m_k_norm_g, sinks=m_sinks, out_norm_a=m_out_norm_a, out_norm_b=m_out_norm_b,
              w_out=m_w_out, norm2_g=m_norm2_g, w_gate=m_w_gate, w_up=m_w_up, w_down=m_w_down)
    vs = dict(rel_bias=v_rel_bias, norm1_g=v_norm1_g, w_in=v_w_in, sgu_norm_g=v_sgu_norm_g, sgu_w=v_sgu_w, sgu_b=v_sgu_b,
              q_norm_g=v_q_norm_g, k_norm_g=v_k_norm_g, sinks=v_sinks, out_norm_a=v_out_norm_a, out_norm_b=v_out_norm_b,
              w_out=v_w_out, norm2_g=v_norm2_g, w_gate=v_w_gate, w_up=v_w_up, w_down=v_w_down)
    all_names = ["rel_bias", "norm1_g", "w_in", "sgu_norm_g", "sgu_w", "sgu_b", "q_norm_g", "k_norm_g", "sinks",
                 "out_norm_a", "out_norm_b", "w_out", "norm2_g", "w_gate", "w_up", "w_down"]
    transposed = ("w_in", "w_gate", "w_up")
    grads, deltas, new_m, new_v = {}, {}, {}, {}

    def update(nme, g):
        if nme in transposed:
            outs = _adamw(f"adamw_{nme}", tr(weights[nme]), g, tr(ms[nme]), tr(vs[nme]))
            grads[nme], (deltas[nme], new_m[nme], new_v[nme]) = tr(g), [tr(o) for o in outs]
        else:
            grads[nme] = g
            deltas[nme], new_m[nme], new_v[nme] = _adamw(f"adamw_{nme}", weights[nme], g, ms[nme], vs[nme])
        return new_v[nme]

    for entry in scatters[:-1]:
        after = finish_scatter(entry, after)
    for nme, i in (("w_gate", 2), ("w_up", 3), ("w_down", 4)):
        after = update(nme, stacked(i))

    s_sem, r_sem, srcs, lands, _ = small_flight
    _, (everyone,) = _exchange_wait("gather_small_wait", "gather", s_sem, r_sem, srcs, lands, after)
    rows = packed.shape[0]
    summed = _sum_slots("sum_small_grads", everyone, 64 if rows % 64 == 0 else 8)
    at = 0
    for nme in order:
        n = int(np.prod(parts[nme].shape))
        n_rows = -(-n // (8 * LANES)) * 8
        after = update(nme, summed[at:at + n_rows].reshape(-1)[:n].reshape(parts[nme].shape))
        at += n_rows

    finish_scatter(scatters[-1], after)
    update("w_out", stacked(1))
    update("w_in", stacked(0))
    return (loss, grad_x, *[grads[nme] for nme in all_names], *[deltas[nme] for nme in all_names],
            *[new_m[nme] for nme in all_names], *[new_v[nme] for nme in all_names])
```

```python
import functools
import math

import numpy as np
import jax
import jax.numpy as jnp
from jax import lax
from jax.experimental import pallas as pl
from jax.experimental.pallas import tpu as pltpu

F32 = jnp.float32
BF16 = jnp.bfloat16

N_DEV = 8
HEAD_DIM = 64
CHUNK = 128
BLOCK = 128
N_KV_HEADS = 4
GROUP = 4
NUM_BUCKETS = 32
MAX_DISTANCE = 128
EPS = 1e-6
NEG_INF = -1e30
LANES = 128
VMEM_LIMIT = 56 * 2 ** 20

ADAM_LR = 0.001
ADAM_B1 = 0.9
ADAM_B2 = 0.999
ADAM_EPS = 1e-08
ADAM_WD = 0.01
ADAM_STEP = 10

MESH = pl.DeviceIdType.MESH
ANY = pl.BlockSpec(memory_space=pl.ANY)
HBM = pl.BlockSpec(memory_space=pltpu.HBM)
SEM = pl.BlockSpec(memory_space=pltpu.SEMAPHORE)
EFFECT = pltpu.SideEffectType.DATAFLOW_SIDE_EFFECTING


def _params(n_axes):
    return pltpu.CompilerParams(dimension_semantics=("arbitrary",) * n_axes, vmem_limit_bytes=VMEM_LIMIT)


def _my_place():
    return lax.axis_index("x"), lax.axis_index("y"), lax.axis_index("c")


def _all_gather(name, arrs):
    n = len(arrs)

    def body(*refs):
        ins, outs = refs[:n], refs[n:2 * n]
        send_sems, recv_sems, local_sems = refs[2 * n:]
        x, y, c = _my_place()
        sibling = (x, y, 1 - c)
        chips = [(1 - x, y), (x, 1 - y), (1 - x, 1 - y)]

        def slot(a, px, py, pc):
            return outs[a].at[4 * px + 2 * py + pc]

        def copy(a, k, block, to, src=None):
            return pltpu.make_async_remote_copy(
                src_ref=slot(a, *block) if src is None else src, dst_ref=slot(a, *block),
                send_sem=send_sems.at[7 * a + k], recv_sem=recv_sems.at[7 * a + k],
                device_id=to, device_id_type=MESH)

        started = []
        for a in range(n):
            mine = pltpu.make_async_copy(ins[a], slot(a, x, y, c), local_sems.at[a])
            mine.start()
            started.append(mine)
        sends = []
        for a in range(n):
            first = [copy(a, 0, (x, y, c), sibling, src=ins[a])]
            first += [copy(a, 1 + j, (x, y, c), (*chip, c), src=ins[a]) for j, chip in enumerate(chips)]
            for cp in first:
                cp.start()
            sends += first
        for a in range(n):
            for j, chip in enumerate(chips):
                copy(a, 1 + j, (*chip, c), (x, y, c)).wait_recv()
                fwd = copy(a, 4 + j, (*chip, c), sibling)
                fwd.start()
                sends.append(fwd)
        for a in range(n):
            copy(a, 0, sibling, (x, y, c)).wait_recv()
            for j, chip in enumerate(chips):
                copy(a, 4 + j, (*chip, 1 - c), (x, y, c)).wait_recv()
        for cp in sends:
            cp.wait_send()
        for cp in started:
            cp.wait()

    return pl.pallas_call(
        body, name=name,
        out_shape=[jax.ShapeDtypeStruct((N_DEV,) + a.shape, a.dtype) for a in arrs],
        in_specs=[ANY] * n, out_specs=[ANY] * n,
        scratch_shapes=[pltpu.SemaphoreType.DMA((7 * n,)), pltpu.SemaphoreType.DMA((7 * n,)),
                        pltpu.SemaphoreType.DMA((n,))],
    )(*arrs)


def _peer(k, x, y, c):
    return (1 - x if k & 4 else x), (1 - y if k & 2 else y), (1 - c if k & 1 else c)


PEER_ORDER = (1, 2, 4, 3, 5, 6, 7)


def _exchange_copy(kind, k, a, srcs, lands, send_sems, recv_sems, arriving=False):
    x, y, c = _my_place()
    me = 4 * x + 2 * y + c
    px, py, pc = _peer(k, x, y, c)
    them = 4 * px + 2 * py + pc
    if kind == "gather":
        src, dst_there, dst_here = srcs[a], lands[a].at[me], lands[a].at[them]
    else:
        src, dst_there, dst_here = srcs[a].at[them], lands[a].at[k - 1], lands[a].at[k - 1]
    return pltpu.make_async_remote_copy(
        src_ref=src, dst_ref=dst_here if arriving else dst_there,
        send_sem=send_sems.at[7 * a + k - 1], recv_sem=recv_sems.at[7 * a + k - 1],
        device_id=(px, py, pc), device_id_type=MESH)


def _exchange_start(name, kind, srcs, lands, after):
    n = len(srcs)

    def body(*refs):
        ins, lnd = refs[:n], refs[n:2 * n]
        send_sems, recv_sems = refs[2 * n + 1], refs[2 * n + 2]
        token = refs[-1]
        for k in PEER_ORDER:
            for a in range(n):
                _exchange_copy(kind, k, a, ins, lnd, send_sems, recv_sems).start()
        token[...] = jnp.zeros_like(token)

    hbm = lambda t: pltpu.with_memory_space_constraint(t, pltpu.HBM)
    out = pl.pallas_call(
        body, name=name,
        out_shape=(pltpu.SemaphoreType.DMA((7 * n,)), pltpu.SemaphoreType.DMA((7 * n,)),
                   *[pltpu.HBM(t.shape, t.dtype) for t in srcs], *[pltpu.HBM(t.shape, t.dtype) for t in lands],
                   jax.ShapeDtypeStruct((8, LANES), F32)),
        in_specs=[HBM] * (2 * n) + [ANY],
        out_specs=(SEM, SEM, *[HBM] * (2 * n), pl.BlockSpec(memory_space=pltpu.VMEM)),
        input_output_aliases={i: 2 + i for i in range(2 * n)},
        compiler_params=pltpu.CompilerParams(has_side_effects=EFFECT),
    )(*[hbm(t) for t in srcs], *[hbm(t) for t in lands], after)
    return out[0], out[1], list(out[2:2 + n]), list(out[2 + n:2 + 2 * n]), out[-1]


def _exchange_wait(name, kind, send_sems, recv_sems, srcs, lands, after):
    n = len(srcs)

    def body(*refs):
        ins, lnd = refs[:n], refs[n:2 * n]
        s_sems, r_sems = refs[2 * n], refs[2 * n + 1]
        for a in range(n):
            for k in PEER_ORDER:
                _exchange_copy(kind, k, a, ins, lnd, s_sems, r_sems).wait_send()
                _exchange_copy(kind, k, a, ins, lnd, s_sems, r_sems, arriving=True).wait_recv()

    out = pl.pallas_call(
        body, name=name,
        out_shape=(*[pltpu.HBM(t.shape, t.dtype) for t in srcs], *[pltpu.HBM(t.shape, t.dtype) for t in lands]),
        in_specs=[HBM] * (2 * n) + [SEM, SEM, ANY],
        out_specs=tuple([HBM] * (2 * n)),
        input_output_aliases={i: i for i in range(2 * n)},
        compiler_params=pltpu.CompilerParams(has_side_effects=EFFECT),
    )(*srcs, *lands, send_sems, recv_sems, after)
    return list(out[:n]), list(out[n:])


def _sum_parts(name, me, mine, parts, rows):
    _, R, C = mine.shape

    def body(me_ref, own_ref, p_ref, o_ref):
        acc = own_ref[0].astype(F32)
        for s in range(N_DEV - 1):
            acc = acc + p_ref[s].astype(F32)
        o_ref[...] = acc

    return pl.pallas_call(
        body, name=name,
        grid_spec=pltpu.PrefetchScalarGridSpec(
            num_scalar_prefetch=1, grid=(R // rows,),
            in_specs=[pl.BlockSpec((1, rows, C), lambda i, me_ref: (me_ref[0], i, 0)),
                      pl.BlockSpec((N_DEV - 1, rows, C), lambda i, me_ref: (0, i, 0))],
            out_specs=pl.BlockSpec((rows, C), lambda i, me_ref: (i, 0))),
        out_shape=jax.ShapeDtypeStruct((R, C), F32), compiler_params=_params(1),
    )(me, mine, parts)


def _sum_slots(name, a, rows):
    _, R, C = a.shape

    def body(a_ref, o_ref):
        acc = a_ref[0].astype(F32)
        for s in range(1, N_DEV):
            acc = acc + a_ref[s].astype(F32)
        o_ref[...] = acc

    return pl.pallas_call(
        body, name=name, grid=(R // rows,),
        in_specs=[pl.BlockSpec((N_DEV, rows, C), lambda i: (0, i, 0))],
        out_specs=pl.BlockSpec((rows, C), lambda i: (i, 0)),
        out_shape=jax.ShapeDtypeStruct((R, C), F32), compiler_params=_params(1),
    )(a)


_DIMS = {"nn": (((1,), (0,)), ((), ())), "nt": (((1,), (1,)), ((), ())), "tn": (((0,), (0,)), ((), ()))}


def _mm(name, mode, a_list, b_list, pairs, n_acc, M, N, K, tm, tn, tk, epilogue, out_dtypes, extras=(), after=None):
    tm, tn, tk = min(tm, M), min(tn, N), min(tk, K)
    assert M % tm == 0 and N % tn == 0 and K % tk == 0, (name, M, N, K, tm, tn, tk)
    nk = K // tk
    na, nb, ne, no = len(a_list), len(b_list), len(extras), len(out_dtypes)
    dims = _DIMS[mode]
    a_spec = (pl.BlockSpec((tk, tm), lambda j, i, k: (k, i)) if mode == "tn"
              else pl.BlockSpec((tm, tk), lambda j, i, k: (i, k)))
    b_spec = (pl.BlockSpec((tn, tk), lambda j, i, k: (j, k)) if mode == "nt"
              else pl.BlockSpec((tk, tn), lambda j, i, k: (k, j)))
    o_spec = pl.BlockSpec((tm, tn), lambda j, i, k: (i, j))
    tail = [] if after is None else [after]

    def body(*refs):
        a_refs, b_refs = refs[:na], refs[na:na + nb]
        e_refs = refs[na + nb:na + nb + ne]
        first_out = na + nb + ne + len(tail)
        o_refs = refs[first_out:first_out + no]
        acc_refs = refs[first_out + no:]

        def partial(oi):
            tot = None
            for ai, bi, ti in pairs:
                if ti == oi:
                    d = lax.dot_general(a_refs[ai][...], b_refs[bi][...], dims, preferred_element_type=F32)
                    tot = d if tot is None else tot + d
            return tot

        def finish(accs):
            outs = epilogue(accs, [e[...] for e in e_refs])
            for o_ref, o in zip(o_refs, outs):
                o_ref[...] = o.astype(o_ref.dtype)

        if nk == 1:
            finish([partial(oi) for oi in range(n_acc)])
        else:
            k = pl.program_id(2)

            @pl.when(k == 0)
            def _():
                for r in acc_refs:
                    r[...] = jnp.zeros_like(r)

            for oi in range(n_acc):
                acc_refs[oi][...] += partial(oi)

            @pl.when(k == nk - 1)
            def _():
                finish([r[...] for r in acc_refs])

    return pl.pallas_call(
        body, name=name, grid=(N // tn, M // tm, nk),
        in_specs=[a_spec] * na + [b_spec] * nb + [o_spec] * ne + [ANY] * len(tail),
        out_specs=[o_spec] * no,
        out_shape=[jax.ShapeDtypeStruct((M, N), dt) for dt in out_dtypes],
        scratch_shapes=[pltpu.VMEM((tm, tn), F32)] * (n_acc if nk > 1 else 0),
        compiler_params=_params(3),
    )(*a_list, *b_list, *extras, *tail)


def _ep_residual(accs, ex):
    return (ex[0] + accs[0],)


def _ep_rows_rms_bwd(acc, ex, rows):
    xv, res = ex
    r = lax.rsqrt(jnp.mean(xv * xv, axis=-1, keepdims=True) + EPS)
    xh = xv * r
    dg = jnp.sum(acc * xh, axis=0, keepdims=True)
    dxh = acc * rows[0]
    dx = r * (dxh - xh * jnp.mean(dxh * xh, axis=-1, keepdims=True)) + res
    return (dx, dx), (dg,)


def _sigmoid(x):
    return 0.5 * (jnp.tanh(0.5 * x) + 1.0)


def _ep_plain(accs, ex):
    return (accs[0],)


def _ep_swiglu(accs, ex):
    g, u = accs
    return g, u, (g * _sigmoid(g)) * u


def _ep_swiglu_bwd(accs, ex):
    dact = accs[0]
    g, u = ex
    sig = _sigmoid(g)
    silu = g * sig
    return dact * u * (sig * (1.0 + g * (1.0 - sig))), dact * silu


def _ep_two(accs, ex):
    return accs[0], accs[1]


ROWS = 256


def _row_spec(tm, width, col=0):
    return pl.BlockSpec((tm, width), lambda i: (i, col))


def _full_spec(shape):
    nd = len(shape)
    return pl.BlockSpec(shape, lambda i: (0,) * nd)


def _rms_fwd(name, x, g):
    T, D = x.shape
    tm = min(ROWS, T)

    def body(x_ref, g_ref, o_ref):
        xv = x_ref[...]
        r = lax.rsqrt(jnp.mean(xv * xv, axis=-1, keepdims=True) + EPS)
        o_ref[...] = ((xv * r) * g_ref[...]).astype(BF16)

    return pl.pallas_call(
        body, name=name, grid=(T // tm,),
        in_specs=[_row_spec(tm, D), _full_spec((1, D))], out_specs=_row_spec(tm, D),
        out_shape=jax.ShapeDtypeStruct((T, D), BF16), compiler_params=_params(1),
    )(x, g)


def _rms_bwd(name, dyn, x, g, res):
    T, D = x.shape
    tm = min(ROWS, T)

    def body(dyn_ref, x_ref, g_ref, res_ref, dx_ref, dx16_ref, dg_ref):
        outs, (dg,) = _ep_rows_rms_bwd(dyn_ref[...], [x_ref[...], res_ref[...]], [g_ref[...]])

        @pl.when(pl.program_id(0) == 0)
        def _():
            dg_ref[...] = jnp.zeros_like(dg_ref)

        dg_ref[...] += dg
        dx_ref[...] = outs[0]
        dx16_ref[...] = outs[1].astype(BF16)

    return pl.pallas_call(
        body, name=name, grid=(T // tm,),
        in_specs=[_row_spec(tm, D), _row_spec(tm, D), _full_spec((1, D)), _row_spec(tm, D)],
        out_specs=[_row_spec(tm, D), _row_spec(tm, D), _full_spec((1, D))],
        out_shape=[jax.ShapeDtypeStruct((T, D), F32), jax.ShapeDtypeStruct((T, D), BF16),
                   jax.ShapeDtypeStruct((1, D), F32)],
        compiler_params=_params(1),
    )(dyn, x, g, res)


def _outnorm_fwd(name, a, b, ga, gb):
    T, W = a.shape
    tm = min(ROWS, T)

    def body(a_ref, b_ref, ga_ref, gb_ref, o_ref):
        for src, gain, lo in ((a_ref, ga_ref, 0), (b_ref, gb_ref, W)):
            v = src[...]
            r = lax.rsqrt(jnp.mean(v * v, axis=-1, keepdims=True) + EPS)
            o_ref[:, lo:lo + W] = ((v * r) * gain[...]).astype(BF16)

    return pl.pallas_call(
        body, name=name, grid=(T // tm,),
        in_specs=[_row_spec(tm, W), _row_spec(tm, W), _full_spec((1, W)), _full_spec((1, W))],
        out_specs=_row_spec(tm, 2 * W),
        out_shape=jax.ShapeDtypeStruct((T, 2 * W), BF16), compiler_params=_params(1),
    )(a, b, ga, gb)


def _outnorm_bwd(name, dmixed, a, b, ga, gb):
    T, W = a.shape
    tm = min(ROWS, T)

    def body(dm_ref, a_ref, b_ref, ga_ref, gb_ref, da_ref, db_ref, dga_ref, dgb_ref):
        first = pl.program_id(0) == 0
        for src, gain, lo, dsrc, dgain in ((a_ref, ga_ref, 0, da_ref, dga_ref), (b_ref, gb_ref, W, db_ref, dgb_ref)):
            v, dv = src[...], dm_ref[:, lo:lo + W]
            r = lax.rsqrt(jnp.mean(v * v, axis=-1, keepdims=True) + EPS)
            vh = v * r

            @pl.when(first)
            def _():
                dgain[...] = jnp.zeros_like(dgain)

            dgain[...] += jnp.sum(dv * vh, axis=0, keepdims=True)
            dvh = dv * gain[...]
            dsrc[...] = r * (dvh - vh * jnp.mean(dvh * vh, axis=-1, keepdims=True))

    return pl.pallas_call(
        body, name=name, grid=(T // tm,),
        in_specs=[_row_spec(tm, 2 * W), _row_spec(tm, W), _row_spec(tm, W), _full_spec((1, W)), _full_spec((1, W))],
        out_specs=[_row_spec(tm, W), _row_spec(tm, W), _full_spec((1, W)), _full_spec((1, W))],
        out_shape=[jax.ShapeDtypeStruct((T, W), F32), jax.ShapeDtypeStruct((T, W), F32),
                   jax.ShapeDtypeStruct((1, W), F32), jax.ShapeDtypeStruct((1, W), F32)],
        compiler_params=_params(1),
    )(dmixed, a, b, ga, gb)


def _loss_head(name, y, target):
    T, D = y.shape
    tm = min(ROWS, T)

    def body(y_ref, t_ref, loss_ref, dy_ref, dy16_ref):
        d = y_ref[...] - t_ref[...]

        @pl.when(pl.program_id(0) == 0)
        def _():
            loss_ref[...] = jnp.zeros_like(loss_ref)

        per_token = jnp.mean(d * d, axis=-1, keepdims=True)
        loss_ref[...] += 0.5 * jnp.sum(per_token, axis=0, keepdims=True)
        dy = d * (1.0 / D)
        dy_ref[...] = dy
        dy16_ref[...] = dy.astype(BF16)

    return pl.pallas_call(
        body, name=name, grid=(T // tm,),
        in_specs=[_row_spec(tm, D), _row_spec(tm, D)],
        out_specs=[_full_spec((1, 1)), _row_spec(tm, D), _row_spec(tm, D)],
        out_shape=[jax.ShapeDtypeStruct((1, 1), F32), jax.ShapeDtypeStruct((T, D), F32),
                   jax.ShapeDtypeStruct((T, D), BF16)],
        compiler_params=_params(1),
    )(y, target)


def _bf16_dot(a, b, dims):
    return lax.dot_general(a.astype(BF16), b.astype(BF16), dims, preferred_element_type=F32)


@jax.custom_vjp
def _dot_nn(a, b):
    return _bf16_dot(a, b, _DIMS["nn"])


def _dot_nn_fwd(a, b):
    return _dot_nn(a, b), (a, b)


def _dot_nn_bwd(saved, ct):
    a, b = saved
    return _bf16_dot(ct, b, _DIMS["nt"]), _bf16_dot(a, ct, _DIMS["tn"])


_dot_nn.defvjp(_dot_nn_fwd, _dot_nn_bwd)


@jax.custom_vjp
def _dot_nt(a, b):
    return _bf16_dot(a, b, _DIMS["nt"])


def _dot_nt_fwd(a, b):
    return _dot_nt(a, b), (a, b)


def _dot_nt_bwd(saved, ct):
    a, b = saved
    return _bf16_dot(ct, b, _DIMS["nn"]), _bf16_dot(ct, a, _DIMS["tn"])


_dot_nt.defvjp(_dot_nt_fwd, _dot_nt_bwd)


def _iota(shape, dim):
    return lax.broadcasted_iota(jnp.int32, shape, dim)


def _head_sum_impl(x):
    same_head = (_iota((LANES, LANES), 0) // HEAD_DIM == _iota((LANES, LANES), 1) // HEAD_DIM).astype(BF16)
    pieces = []
    for i in range(x.shape[1] // LANES):
        xs = x[:, i * LANES:(i + 1) * LANES]
        hi = xs.astype(BF16)
        r1 = xs - hi.astype(F32)
        mid = r1.astype(BF16)
        lo = (r1 - mid.astype(F32)).astype(BF16)
        tot = None
        for part in (hi, mid, lo):
            d = lax.dot_general(part, same_head, _DIMS["nn"], preferred_element_type=F32)
            tot = d if tot is None else tot + d
        pieces.append(tot)
    return pieces[0] if len(pieces) == 1 else jnp.concatenate(pieces, axis=1)


@jax.custom_vjp
def _head_sum(x):
    return _head_sum_impl(x)


_head_sum.defvjp(lambda x: (_head_sum_impl(x), None), lambda _, ct: (_head_sum_impl(ct),))


def _head_rms(x, g_row):
    ms = _head_sum(x * x) * (1.0 / HEAD_DIM)
    return (x * lax.rsqrt(ms + EPS)) * g_row


@jax.custom_vjp
def _swap_halves(x):
    return pltpu.roll(x, HEAD_DIM, 1)


_swap_halves.defvjp(lambda x: (pltpu.roll(x, HEAD_DIM, 1), None), lambda _, ct: (pltpu.roll(ct, HEAD_DIM, 1),))


def _gelu(x):
    return 0.5 * x * (1.0 + lax.erf(x * (1.0 / math.sqrt(2.0))))


def _sgu_block(u_raw, v_raw, g_row, w, b_full):
    u, v = _gelu(u_raw), _gelu(v_raw)
    vn = _head_rms(v, g_row)
    causal = _iota((CHUNK, CHUNK), 0) >= _iota((CHUNK, CHUNK), 1)
    low_half = _iota((CHUNK, LANES), 1) < HEAD_DIM
    gates = []
    for p in range(v.shape[1] // LANES):
        vp = vn[:, p * LANES:(p + 1) * LANES]
        g0 = _dot_nn(jnp.where(causal, w[2 * p], 0.0), vp)
        g1 = _dot_nn(jnp.where(causal, w[2 * p + 1], 0.0), vp)
        gates.append(jnp.where(low_half, g0, g1))
    gate = jnp.concatenate(gates, axis=1) + b_full
    return u * gate


def _softmax_sink_fwd(s, sink):
    m = jnp.maximum(jnp.max(s, axis=1, keepdims=True), sink)
    e = jnp.exp(s - m)
    e_sink = jnp.exp(sink - m)
    r = 1.0 / (jnp.sum(e, axis=1, keepdims=True) + e_sink)
    p = e * r
    return p, (p, e_sink * r)


@jax.custom_vjp
def _softmax_with_sink(s, sink):
    return _softmax_sink_fwd(s, sink)[0]


def _softmax_sink_bwd(saved, dp):
    p, p_sink = saved
    delta = jnp.sum(p * dp, axis=1, keepdims=True)
    return p * (dp - delta), -p_sink * delta


_softmax_with_sink.defvjp(_softmax_sink_fwd, _softmax_sink_bwd)


def _attn_block(q_raw, k_prev, k_cur, v_prev, v_cur, qg_row, kg_row, sinks, bias):
    scale = 1.0 / math.sqrt(HEAD_DIM)
    n_q_heads = q_raw.shape[1] // HEAD_DIM
    qn = _head_rms(q_raw, qg_row) * scale
    kn_prev, kn_cur = _head_rms(k_prev, kg_row), _head_rms(k_cur, kg_row)
    own = _iota((GROUP * BLOCK, BLOCK), 1) <= (_iota((GROUP * BLOCK, BLOCK), 0) & (BLOCK - 1))
    low_half = _iota((BLOCK, LANES), 1) < HEAD_DIM
    heads = [None] * n_q_heads
    for kv in range(n_q_heads // GROUP):
        grp, kv_low = kv // 2, kv % 2 == 0
        lanes = slice(grp * LANES, (grp + 1) * LANES)
        keep = low_half if kv_low else ~low_half
        stacked = []
        for g in range(GROUP):
            h = GROUP * kv + g
            q2 = qn[:, (h // 2) * LANES:(h // 2 + 1) * LANES]
            src = q2 if (h % 2 == 0) == kv_low else _swap_halves(q2)
            stacked.append(jnp.where(keep, src, 0.0))
        q4 = jnp.concatenate(stacked, axis=0)
        bias4 = jnp.concatenate([bias[GROUP * kv + g] for g in range(GROUP)], axis=0)
        sink4 = jnp.concatenate([jnp.broadcast_to(sinks[GROUP * kv + g], (BLOCK, 1)) for g in range(GROUP)], axis=0)
        s = jnp.where(own, _dot_nt(q4, kn_cur[:, lanes]), _dot_nt(q4, kn_prev[:, lanes])) + bias4
        p = _softmax_with_sink(s, sink4)
        o4 = _dot_nn(jnp.where(own, p, 0.0), v_cur[:, lanes]) + _dot_nn(jnp.where(own, 0.0, p), v_prev[:, lanes])
        for g in range(GROUP):
            h = GROUP * kv + g
            o = o4[g * BLOCK:(g + 1) * BLOCK]
            heads[h] = o if (h % 2 == 0) == kv_low else _swap_halves(o)
    outs = [jnp.where(low_half, heads[2 * p], heads[2 * p + 1]) for p in range(n_q_heads // 2)]
    return jnp.concatenate(outs, axis=1)


def _bias_table():
    i, j = np.arange(BLOCK)[:, None], np.arange(BLOCK)[None, :]
    n = np.where(j <= i, i - j, i + BLOCK - j)
    max_exact = NUM_BUCKETS // 2
    nf = np.maximum(n, 1).astype(np.float64)
    large = max_exact + (np.log(nf / max_exact) / math.log(MAX_DISTANCE / max_exact) * (NUM_BUCKETS - max_exact)).astype(np.int32)
    large = np.minimum(large, NUM_BUCKETS - 1)
    return np.where(n < max_exact, n, large).astype(np.int32)


def _bias_fwd(name, rel_bias, buckets):
    nb_, nh = rel_bias.shape

    def body(rb_ref, bk_ref, o_ref):
        bk = bk_ref[...]
        own = _iota(bk.shape, 1) <= _iota(bk.shape, 0)
        for h in range(nh):
            acc = jnp.zeros(bk.shape, F32)
            for b in range(nb_):
                acc = jnp.where(bk == b, rb_ref[b, h], acc)
            o_ref[1, h] = acc
            o_ref[0, h] = jnp.where(own, acc, NEG_INF)

    return pl.pallas_call(
        body, name=name,
        in_specs=[pl.BlockSpec(memory_space=pltpu.SMEM), pl.BlockSpec(memory_space=pltpu.VMEM)],
        out_specs=pl.BlockSpec(memory_space=pltpu.VMEM),
        out_shape=jax.ShapeDtypeStruct((2, nh) + buckets.shape, F32),
    )(rel_bias, buckets)


def _bias_bwd(name, dbias, buckets, nb_):
    nh = dbias.shape[0]

    def body(db_ref, bk_ref, o_ref):
        bk = bk_ref[...]
        for h in range(nh):
            d = db_ref[h]
            for b in range(nb_):
                s = jnp.sum(jnp.where(bk == b, d, 0.0), axis=0, keepdims=True)
                s = jnp.sum(s, axis=1, keepdims=True)
                o_ref[b * nh + h:b * nh + h + 1, :] = jnp.broadcast_to(s, (1, LANES))

    return pl.pallas_call(
        body, name=name,
        in_specs=[pl.BlockSpec(memory_space=pltpu.VMEM), pl.BlockSpec(memory_space=pltpu.VMEM)],
        out_specs=pl.BlockSpec(memory_space=pltpu.VMEM),
        out_shape=jax.ShapeDtypeStruct((nb_ * nh, LANES), F32),
    )(dbias, buckets)


def _sgu_fwd(name, z, g_row, w, b_full, W):
    T = z.shape[0]

    def body(u_ref, v_ref, g_ref, w_ref, b_ref, o_ref):
        o_ref[...] = _sgu_block(u_ref[...], v_ref[...], g_ref[...], w_ref[...], b_ref[...])

    return pl.pallas_call(
        body, name=name, grid=(T // CHUNK,),
        in_specs=[_row_spec(CHUNK, W, 0), _row_spec(CHUNK, W, 1), _full_spec((1, W)),
                  _full_spec(w.shape), _full_spec((CHUNK, W))],
        out_specs=_row_spec(CHUNK, W),
        out_shape=jax.ShapeDtypeStruct((T, W), F32), compiler_params=_params(1),
    )(z, z, g_row, w, b_full)


def _sgu_bwd(name, z, g_row, w, b_full, d_out, W):
    T = z.shape[0]

    def body(u_ref, v_ref, g_ref, w_ref, b_ref, do_ref, dz_ref, dg_ref, dw_ref, db_ref):
        _, vjp = jax.vjp(_sgu_block, u_ref[...], v_ref[...], g_ref[...], w_ref[...], b_ref[...])
        du, dv, dg, dw, db = vjp(do_ref[...])
        dz_ref[:, :W] = du.astype(BF16)
        dz_ref[:, W:] = dv.astype(BF16)
        i = pl.program_id(0)

        @pl.when(i == 0)
        def _():
            dg_ref[...] = jnp.zeros_like(dg_ref)
            dw_ref[...] = jnp.zeros_like(dw_ref)
            db_ref[...] = jnp.zeros_like(db_ref)

        dg_ref[...] += dg
        dw_ref[...] += dw
        db_ref[...] += db

        @pl.when(i == pl.num_programs(0) - 1)
        def _():
            db_ref[...] = _head_sum_impl(db_ref[...])

    return pl.pallas_call(
        body, name=name, grid=(T // CHUNK,),
        in_specs=[_row_spec(CHUNK, W, 0), _row_spec(CHUNK, W, 1), _full_spec((1, W)),
                  _full_spec(w.shape), _full_spec((CHUNK, W)), _row_spec(CHUNK, W)],
        out_specs=[_row_spec(CHUNK, 2 * W), _full_spec((1, W)), _full_spec(w.shape), _full_spec((CHUNK, W))],
        out_shape=[jax.ShapeDtypeStruct((T, 2 * W), BF16), jax.ShapeDtypeStruct((1, W), F32),
                   jax.ShapeDtypeStruct(w.shape, F32), jax.ShapeDtypeStruct((CHUNK, W), F32)],
        compiler_params=_params(1),
    )(z, z, g_row, w, b_full, d_out)


def _attn_specs(WQ, WKV, q_col, k_col, v_col, blk_of):
    prev_of = lambda i: jnp.maximum(blk_of(i) - 1, 0)
    return [pl.BlockSpec((BLOCK, WQ), lambda i: (blk_of(i), q_col)),
            pl.BlockSpec((BLOCK, WKV), lambda i: (prev_of(i), k_col)),
            pl.BlockSpec((BLOCK, WKV), lambda i: (blk_of(i), k_col)),
            pl.BlockSpec((BLOCK, WKV), lambda i: (prev_of(i), v_col)),
            pl.BlockSpec((BLOCK, WKV), lambda i: (blk_of(i), v_col))]


def _bias_spec(bias, blk_of):
    return pl.BlockSpec((1,) + bias.shape[1:], lambda i: (jnp.minimum(blk_of(i), 1), 0, 0, 0))


def _attn_fwd(name, z, qg_row, kg_row, sinks_col, bias, WQ, WKV, q_col, k_col, v_col):
    T = z.shape[0]
    nh = sinks_col.shape[0]

    def body(q_ref, kp_ref, kc_ref, vp_ref, vc_ref, qg_ref, kg_ref, s_ref, b_ref, o_ref):
        sinks = [s_ref[h:h + 1, :] for h in range(nh)]
        o_ref[...] = _attn_block(q_ref[...], kp_ref[...], kc_ref[...], vp_ref[...], vc_ref[...],
                                 qg_ref[...], kg_ref[...], sinks, b_ref[0])

    return pl.pallas_call(
        body, name=name, grid=(T // BLOCK,),
        in_specs=_attn_specs(WQ, WKV, q_col, k_col, v_col, lambda i: i)
        + [_full_spec((1, WQ)), _full_spec((1, WKV)), _full_spec((nh, 1)), _bias_spec(bias, lambda i: i)],
        out_specs=_row_spec(BLOCK, WQ),
        out_shape=jax.ShapeDtypeStruct((T, WQ), F32), compiler_params=_params(1),
    )(z, z, z, z, z, qg_row, kg_row, sinks_col, bias)


def _attn_bwd(name, z, qg_row, kg_row, sinks_col, bias, d_out, WQ, WKV, q_col, k_col, v_col):
    T = z.shape[0]
    nblk = T // BLOCK
    nh = sinks_col.shape[0]
    blk_of = lambda i: nblk - 1 - i

    def body(q_ref, kp_ref, kc_ref, vp_ref, vc_ref, qg_ref, kg_ref, s_ref, b_ref, do_ref,
             dq_ref, dk_ref, dv_ref, dqg_ref, dkg_ref, ds_ref, db_ref, carry_k, carry_v):
        i = pl.program_id(0)
        sinks = [s_ref[h:h + 1, :] for h in range(nh)]
        _, vjp = jax.vjp(_attn_block, q_ref[...], kp_ref[...], kc_ref[...], vp_ref[...], vc_ref[...],
                         qg_ref[...], kg_ref[...], sinks, b_ref[0])
        dq, dkp, dkc, dvp, dvc, dqg, dkg, dsk, dbs = vjp(do_ref[...])

        @pl.when(i == 0)
        def _():
            carry_k[...] = jnp.zeros_like(carry_k)
            carry_v[...] = jnp.zeros_like(carry_v)
            dqg_ref[...] = jnp.zeros_like(dqg_ref)
            dkg_ref[...] = jnp.zeros_like(dkg_ref)
            ds_ref[...] = jnp.zeros_like(ds_ref)
            db_ref[...] = jnp.zeros_like(db_ref)

        dq_ref[...] = dq.astype(BF16)
        dk_ref[...] = (dkc + carry_k[...]).astype(BF16)
        dv_ref[...] = (dvc + carry_v[...]).astype(BF16)
        carry_k[...] = dkp
        carry_v[...] = dvp
        dqg_ref[...] += dqg
        dkg_ref[...] += dkg
        for h in range(nh):
            ds_ref[h:h + 1, :] += dsk[h]
        db_ref[...] += dbs

    return pl.pallas_call(
        body, name=name, grid=(nblk,),
        in_specs=_attn_specs(WQ, WKV, q_col, k_col, v_col, blk_of)
        + [_full_spec((1, WQ)), _full_spec((1, WKV)), _full_spec((nh, 1)), _bias_spec(bias, blk_of),
           pl.BlockSpec((BLOCK, WQ), lambda i: (blk_of(i), 0))],
        out_specs=[pl.BlockSpec((BLOCK, WQ), lambda i: (blk_of(i), 0)),
                   pl.BlockSpec((BLOCK, WKV), lambda i: (blk_of(i), 0)),
                   pl.BlockSpec((BLOCK, WKV), lambda i: (blk_of(i), 0)),
                   _full_spec((1, WQ)), _full_spec((1, WKV)), _full_spec((nh, 1)), _full_spec(bias.shape[1:])],
        out_shape=[jax.ShapeDtypeStruct((T, WQ), BF16), jax.ShapeDtypeStruct((T, WKV), BF16),
                   jax.ShapeDtypeStruct((T, WKV), BF16), jax.ShapeDtypeStruct((1, WQ), F32),
                   jax.ShapeDtypeStruct((1, WKV), F32), jax.ShapeDtypeStruct((nh, 1), F32),
                   jax.ShapeDtypeStruct(bias.shape[1:], F32)],
        scratch_shapes=[pltpu.VMEM((BLOCK, WKV), F32), pltpu.VMEM((BLOCK, WKV), F32)],
        compiler_params=_params(1),
    )(z, z, z, z, z, qg_row, kg_row, sinks_col, bias, d_out)


def _adamw(name, w, g, m, v):
    shape = w.shape
    C = shape[-1]
    if w.ndim == 3 and shape[1] % ROWS == 0:
        work = shape
        grid = (shape[0], shape[1] // ROWS)
        spec = pl.BlockSpec((1, ROWS, C), lambda l, i: (l, i, 0))
    else:
        R = int(np.prod(shape[:-1]))
        tr = ROWS if R % ROWS == 0 else R
        work = (R, C)
        grid = (R // tr,)
        spec = pl.BlockSpec((tr, C), lambda i: (i, 0))
    w2, g2, m2, v2 = (t.reshape(work) for t in (w, g, m, v))

    def body(w_ref, g_ref, m_ref, v_ref, d_ref, nm_ref, nv_ref):
        gv = g_ref[...]
        nm = ADAM_B1 * m_ref[...] + (1.0 - ADAM_B1) * gv
        nv = ADAM_B2 * v_ref[...] + (1.0 - ADAM_B2) * (gv * gv)
        m_hat = nm / (1.0 - ADAM_B1 ** ADAM_STEP)
        v_hat = nv / (1.0 - ADAM_B2 ** ADAM_STEP)
        d_ref[...] = -ADAM_LR * (m_hat / (jnp.sqrt(v_hat) + ADAM_EPS) + ADAM_WD * w_ref[...])
        nm_ref[...] = nm
        nv_ref[...] = nv

    outs = pl.pallas_call(
        body, name=name, grid=grid, in_specs=[spec] * 4, out_specs=[spec] * 3,
        out_shape=[jax.ShapeDtypeStruct(work, F32)] * 3, compiler_params=_params(len(grid)),
    )(w2, g2, m2, v2)
    return tuple(o.reshape(shape) for o in outs)


def _pad_rows(flat):
    n = flat.shape[0]
    tile = 8 * LANES
    padded = -(-n // tile) * tile
    return jnp.pad(flat, (0, padded - n)).reshape(padded // LANES, LANES)


def kernel(x, rel_bias, norm1_g, w_in, sgu_norm_g, sgu_w, sgu_b, q_norm_g, k_norm_g, sinks, out_norm_a, out_norm_b, w_out, norm2_g, w_gate, w_up, w_down, loss_target, m_rel_bias, m_norm1_g, m_w_in, m_sgu_norm_g, m_sgu_w, m_sgu_b, m_q_norm_g, m_k_norm_g, m_sinks, m_out_norm_a, m_out_norm_b, m_w_out, m_norm2_g, m_w_gate, m_w_up, m_w_down, v_rel_bias, v_norm1_g, v_w_in, v_sgu_norm_g, v_sgu_w, v_sgu_b, v_q_norm_g, v_k_norm_g, v_sinks, v_out_norm_a, v_out_norm_b, v_w_out, v_norm2_g, v_w_gate, v_w_up, v_w_down):
    L = w_in.shape[0]
    T, D = x.shape[1], x.shape[2]
    W = D // 2
    NH = W // HEAD_DIM
    WKV = N_KV_HEADS * HEAD_DIM
    IN = N_DEV * w_in.shape[2]
    FF = N_DEV * w_gate.shape[2]
    assert IN == 2 * W + W + 2 * WKV and NH // N_KV_HEADS == GROUP
    q_col, k_col, v_col = 2 * W // W, (3 * W) // WKV, (3 * W + WKV) // WKV
    x0 = x.reshape(T, D)
    target = loss_target.reshape(T, D)

    shards = [jnp.swapaxes(w_in, 1, 2).astype(BF16), w_out.astype(BF16), jnp.swapaxes(w_gate, 1, 2).astype(BF16),
              jnp.swapaxes(w_up, 1, 2).astype(BF16), w_down.astype(BF16)]
    me = 4 * lax.axis_index("x") + 2 * lax.axis_index("y") + lax.axis_index("c")

    def landing(block):
        return lax.dynamic_update_slice(lax.empty((N_DEV,) + block.shape, block.dtype), block[None], (me, 0, 0))

    def as_matrices(got):
        return [g.reshape(N_DEV * g.shape[1], g.shape[2]) for g in got]

    groups = ((0,), (1,), (2, 3, 4))
    full = {l: [None] * len(shards) for l in range(L)}
    full[0][0] = as_matrices(_all_gather("gather_weights_0", [shards[0][0]]))[0]
    flight = {}

    def start_gather(l, first_group, after):
        for g in range(first_group, len(groups)):
            srcs = [shards[i][l] for i in groups[g]]
            flight[l, g] = _exchange_start(f"gather_start_{l}_{g}", "gather", srcs, [landing(s) for s in srcs], after)
            after = flight[l, g][4]
        return after

    def finish_gather(l, g, after):
        s_sem, r_sem, srcs, lands, _ = flight.pop((l, g))
        lands = _exchange_wait(f"gather_wait_{l}_{g}", "gather", s_sem, r_sem, srcs, lands, after)[1]
        for i, mat in zip(groups[g], as_matrices(lands)):
            full[l][i] = mat

    buckets = jnp.asarray(_bias_table())
    bias = _bias_fwd("bias_table", rel_bias, buckets)

    saved = []
    xl = x0
    token = start_gather(0, 1, full[0][0])
    for l in range(L):
        if l > 0:
            finish_gather(l, 0, xl)
        w_in_t = full[l][0]
        g1, g2 = norm1_g[l][None], norm2_g[l][None]
        sg_row = sgu_norm_g[l].reshape(1, W)
        b_full = jnp.repeat(sgu_b[l].T, HEAD_DIM, axis=1)
        qg_row = jnp.tile(q_norm_g[l], NH)[None]
        kg_row = jnp.tile(k_norm_g[l], N_KV_HEADS)[None]
        sinks_col = sinks[l][:, None]
        ga, gb = out_norm_a[l][None], out_norm_b[l][None]

        h = _rms_fwd(f"norm1_{l}", xl, g1)
        (z,) = _mm(f"proj_in_{l}", "nt", [h], [w_in_t], [(0, 0, 0)], 1, T, IN, D, 1024, 896, D, _ep_plain, [F32],
                   after=token)
        out_a = _sgu_fwd(f"sgu_{l}", z, sg_row, sgu_w[l], b_full, W)
        out_b = _attn_fwd(f"attn_{l}", z, qg_row, kg_row, sinks_col, bias, W, WKV, q_col, k_col, v_col)
        mixed = _outnorm_fwd(f"outnorm_{l}", out_a, out_b, ga, gb)
        finish_gather(l, 1, mixed)
        w_o = full[l][1]
        (x1,) = _mm(f"proj_out_{l}", "nn", [mixed], [w_o], [(0, 0, 0)], 1, T, D, D, 1024, 1024, D,
                    _ep_residual, [F32], extras=[xl])
        h2 = _rms_fwd(f"norm2_{l}", x1, g2)
        token = start_gather(l + 1, 0, h2) if l + 1 < L else h2
        finish_gather(l, 2, h2)
        w_g_t, w_u_t, w_d = full[l][2:]
        gate, up, act = _mm(f"mlp_in_{l}", "nt", [h2], [w_g_t, w_u_t], [(0, 0, 0), (0, 1, 1)], 2, T, FF, D,
                            1024, 512, D, _ep_swiglu, [BF16, BF16, BF16], after=token)
        (x2,) = _mm(f"mlp_out_{l}", "nn", [act], [w_d], [(0, 0, 0)], 1, T, D, FF, 512, 1024, FF,
                    _ep_residual, [F32], extras=[x1])
        saved.append((xl, h, z, out_a, out_b, mixed, x1, h2, gate, up, act,
                      g1, g2, sg_row, b_full, qg_row, kg_row, sinks_col, ga, gb))
        xl = x2

    loss_part, dy, dy16 = _loss_head("loss_head", xl, target)

    dbias = None
    small = [None] * L
    scatters = []

    def start_scatter(name, which, grads_t, after):
        srcs = [t.reshape(N_DEV, t.shape[0] // N_DEV, D) for t in grads_t]
        lands = [lax.empty((N_DEV - 1,) + s.shape[1:], BF16) for s in srcs]
        s_sem, r_sem, srcs, lands, tok = _exchange_start(name, "scatter", srcs, lands, after)
        scatters.append((name, which, s_sem, r_sem, srcs, lands))
        return tok

    token = None
    for l in reversed(range(L)):
        w_in_t, w_o, w_g_t, w_u_t, w_d = full[l]
        (xl, h, z, out_a, out_b, mixed, x1, h2, gate, up, act,
         g1, g2, sg_row, b_full, qg_row, kg_row, sinks_col, ga, gb) = saved[l]

        dgate, dup = _mm(f"d_mlp_out_{l}", "nt", [dy16], [w_d], [(0, 0, 0)], 1, T, FF, D, 1024, 1408, D,
                         _ep_swiglu_bwd, [BF16, BF16], extras=[gate, up], after=token)
        (dw_d,) = _mm(f"dw_down_{l}", "tn", [act], [dy16], [(0, 0, 0)], 1, FF, D, T, 1408, 1024, 2048, _ep_plain, [BF16])
        (dh2,) = _mm(f"d_mlp_in_{l}", "nn", [dgate, dup], [w_g_t, w_u_t], [(0, 0, 0), (1, 1, 0)], 1, T, D, FF,
                     512, 512, FF, _ep_plain, [F32])
        dx1, dx1_16, dg2 = _rms_bwd(f"d_norm2_{l}", dh2, x1, g2, dy)
        dw_g, dw_u = _mm(f"dw_gate_up_{l}", "tn", [dgate, dup], [h2], [(0, 0, 0), (1, 0, 1)], 2, FF, D, T,
                         1408, 1024, 1024, _ep_two, [BF16, BF16])
        token = start_scatter(f"scatter_mlp_start_{l}", (l, (4, 2, 3)), [dw_d, dw_g, dw_u], dw_u)
        (dmixed,) = _mm(f"d_proj_out_{l}", "nt", [dx1_16], [w_o], [(0, 0, 0)], 1, T, D, D, 1024, 1024, D, _ep_plain, [F32],
                        after=token)
        (dw_o,) = _mm(f"dw_out_{l}", "tn", [mixed], [dx1_16], [(0, 0, 0)], 1, D, D, T, 1024, 1024, 2048, _ep_plain, [BF16])
        d_a, d_b, dga, dgb = _outnorm_bwd(f"d_outnorm_{l}", dmixed, out_a, out_b, ga, gb)
        dz_uv, dsg, dsw, dsb = _sgu_bwd(f"d_sgu_{l}", z, sg_row, sgu_w[l], b_full, d_a, W)
        dq, dk, dv, dqg, dkg, dsk, dbs = _attn_bwd(f"d_attn_{l}", z, qg_row, kg_row, sinks_col, bias, d_b,
                                                   W, WKV, q_col, k_col, v_col)
        dbias = dbs if dbias is None else dbias + dbs
        dz = jnp.concatenate([dz_uv, dq, dk, dv], axis=1)
        (dh,) = _mm(f"d_proj_in_{l}", "nn", [dz], [w_in_t], [(0, 0, 0)], 1, T, D, IN, 1024, 1024, IN, _ep_plain, [F32])
        dy, dy16, dg1 = _rms_bwd(f"d_norm1_{l}", dh, xl, g1, dx1)
        (dw_i,) = _mm(f"dw_in_{l}", "tn", [dz], [h], [(0, 0, 0)], 1, IN, D, T, 896, 1024, 2048, _ep_plain, [BF16])
        token = start_scatter(f"scatter_mix_start_{l}", (l, (1, 0)), [dw_o, dw_i], dw_i)

        small[l] = dict(norm1_g=dg1[0], sgu_norm_g=dsg.reshape(NH, HEAD_DIM), sgu_w=dsw,
                        sgu_b=dsb[:, ::HEAD_DIM].T, q_norm_g=dqg.reshape(NH, HEAD_DIM).sum(0),
                        k_norm_g=dkg.reshape(N_KV_HEADS, HEAD_DIM).sum(0), sinks=dsk[:, 0],
                        out_norm_a=dga[0], out_norm_b=dgb[0], norm2_g=dg2[0])

    grad_x = dy.reshape(x.shape)
    d_rel = _bias_bwd("d_bias_table", dbias, buckets, NUM_BUCKETS)[:, 0].reshape(NUM_BUCKETS, NH)

    names = ["norm1_g", "sgu_norm_g", "sgu_w", "sgu_b", "q_norm_g", "k_norm_g", "sinks", "out_norm_a", "out_norm_b", "norm2_g"]
    parts = {"rel_bias": d_rel}
    for nme in names:
        parts[nme] = jnp.stack([small[l][nme] for l in range(L)])
    order = ["rel_bias"] + names
    packed = jnp.concatenate([_pad_rows(parts[nme].reshape(-1)) for nme in order], axis=0)
    small_flight = _exchange_start("gather_small_start", "gather", [packed], [landing(packed)], token)
    after = small_flight[4]

    me1 = me.reshape(1).astype(jnp.int32)
    grads_big = {}

    def finish_scatter(entry, after):
        name, (l, which), s_sem, r_sem, srcs, lands = entry
        srcs, lands = _exchange_wait(name.replace("start", "wait"), "scatter", s_sem, r_sem, srcs, lands, after)
        for i, src, land in zip(which, srcs, lands):
            rows = src.shape[1]
            after = _sum_parts(f"sum_grads_{l}_{i}", me1, src, land, 64 if rows % 64 == 0 else rows)
            grads_big[i, l] = after
        return after

    def stacked(i):
        return jnp.stack([grads_big[i, l] for l in range(L)])

    loss = lax.psum(loss_part[0, 0], ("x", "y", "c"))
    tr = lambda t: jnp.swapaxes(t, 1, 2)
    weights = dict(rel_bias=rel_bias, norm1_g=norm1_g, w_in=w_in, sgu_norm_g=sgu_norm_g, sgu_w=sgu_w, sgu_b=sgu_b,
                   q_norm_g=q_norm_g, k_norm_g=k_norm_g, sinks=sinks, out_norm_a=out_norm_a, out_norm_b=out_norm_b,
                   w_out=w_out, norm2_g=norm2_g, w_gate=w_gate, w_up=w_up, w_down=w_down)
    ms = dict(rel_bias=m_rel_bias, norm1_g=m_norm1_g, w_in=m_w_in, sgu_norm_g=m_sgu_norm_g, sgu_w=m_sgu_w, sgu_b=m_sgu_b,
              q_norm_g=m_q_norm_g, k_norm_g=m_k_norm_g, sinks=m_sinks, out_norm_a=m_out_norm_a, out_norm_b=m_out_norm_b,
              w_out=m_w_out, norm2_g=m_norm2_g, w_gate=m_w_gate, w_up=m_w_up, w_down=m_w_down)
    vs = dict(rel_bias=v_rel_bias, norm1_g=v_norm1_g, w_in=v_w_in, sgu_norm_g=v_sgu_norm_g, sgu_w=v_sgu_w, sgu_b=v_sgu_b,
              q_norm_g=v_q_norm_g, k_norm_g=v_k_norm_g, sinks=v_sinks, out_norm_a=v_out_norm_a, out_norm_b=v_out_norm_b,
              w_out=v_w_out, norm2_g=v_norm2_g, w_gate=v_w_gate, w_up=v_w_up, w_down=v_w_down)
    all_names = ["rel_bias", "norm1_g", "w_in", "sgu_norm_g", "sgu_w", "sgu_b", "q_norm_g", "k_norm_g", "sinks",
                 "out_norm_a", "out_norm_b", "w_out", "norm2_g", "w_gate", "w_up", "w_down"]
    transposed = ("w_in", "w_gate", "w_up")
    grads, deltas, new_m, new_v = {}, {}, {}, {}

    def update(nme, g):
        if nme in transposed:
            outs = _adamw(f"adamw_{nme}", tr(weights[nme]), g, tr(ms[nme]), tr(vs[nme]))
            grads[nme], (deltas[nme], new_m[nme], new_v[nme]) = tr(g), [tr(o) for o in outs]
        else:
            grads[nme] = g
            deltas[nme], new_m[nme], new_v[nme] = _adamw(f"adamw_{nme}", weights[nme], g, ms[nme], vs[nme])
        return new_v[nme]

    for entry in scatters[:-1]:
        after = finish_scatter(entry, after)
    for nme, i in (("w_gate", 2), ("w_up", 3), ("w_down", 4)):
        after = update(nme, stacked(i))

    s_sem, r_sem, srcs, lands, _ = small_flight
    _, (everyone,) = _exchange_wait("gather_small_wait", "gather", s_sem, r_sem, srcs, lands, after)
    rows = packed.shape[0]
    summed = _sum_slots("sum_small_grads", everyone, 64 if rows % 64 == 0 else 8)
    at = 0
    for nme in order:
        n = int(np.prod(parts[nme].shape))
        n_rows = -(-n // (8 * LANES)) * 8
        after = update(nme, summed[at:at + n_rows].reshape(-1)[:n].reshape(parts[nme].shape))
        at += n_rows

    finish_scatter(scatters[-1], after)
    update("w_out", stacked(1))
    update("w_in", stacked(0))
    return (loss, grad_x, *[grads[nme] for nme in all_names], *[deltas[nme] for nme in all_names],
            *[new_m[nme] for nme in all_names], *[new_v[nme] for nme in all_names])
```

```python
import functools
import math

import numpy as np
import jax
import jax.numpy as jnp
from jax import lax
from jax.experimental import pallas as pl
from jax.experimental.pallas import tpu as pltpu

F32 = jnp.float32
BF16 = jnp.bfloat16

N_DEV = 8
HEAD_DIM = 64
CHUNK = 128
BLOCK = 128
N_KV_HEADS = 4
GROUP = 4
NUM_BUCKETS = 32
MAX_DISTANCE = 128
EPS = 1e-6
NEG_INF = -1e30
LANES = 128
VMEM_LIMIT = 56 * 2 ** 20

ADAM_LR = 0.001
ADAM_B1 = 0.9
ADAM_B2 = 0.999
ADAM_EPS = 1e-08
ADAM_WD = 0.01
ADAM_STEP = 10

MESH = pl.DeviceIdType.MESH
ANY = pl.BlockSpec(memory_space=pl.ANY)
HBM = pl.BlockSpec(memory_space=pltpu.HBM)
SEM = pl.BlockSpec(memory_space=pltpu.SEMAPHORE)
EFFECT = pltpu.SideEffectType.DATAFLOW_SIDE_EFFECTING


def _params(n_axes):
    return pltpu.CompilerParams(dimension_semantics=("arbitrary",) * n_axes, vmem_limit_bytes=VMEM_LIMIT)


def _my_place():
    return lax.axis_index("x"), lax.axis_index("y"), lax.axis_index("c")


def _all_gather(name, arrs):
    n = len(arrs)

    def body(*refs):
        ins, outs = refs[:n], refs[n:2 * n]
        send_sems, recv_sems, local_sems = refs[2 * n:]
        x, y, c = _my_place()
        sibling = (x, y, 1 - c)
        chips = [(1 - x, y), (x, 1 - y), (1 - x, 1 - y)]

        def slot(a, px, py, pc):
            return outs[a].at[4 * px + 2 * py + pc]

        def copy(a, k, block, to, src=None):
            return pltpu.make_async_remote_copy(
                src_ref=slot(a, *block) if src is None else src, dst_ref=slot(a, *block),
                send_sem=send_sems.at[7 * a + k], recv_sem=recv_sems.at[7 * a + k],
                device_id=to, device_id_type=MESH)

        started = []
        for a in range(n):
            mine = pltpu.make_async_copy(ins[a], slot(a, x, y, c), local_sems.at[a])
            mine.start()
            started.append(mine)
        sends = []
        for a in range(n):
            first = [copy(a, 0, (x, y, c), sibling, src=ins[a])]
            first += [copy(a, 1 + j, (x, y, c), (*chip, c), src=ins[a]) for j, chip in enumerate(chips)]
            for cp in first:
                cp.start()
            sends += first
        for a in range(n):
            for j, chip in enumerate(chips):
                copy(a, 1 + j, (*chip, c), (x, y, c)).wait_recv()
                fwd = copy(a, 4 + j, (*chip, c), sibling)
                fwd.start()
                sends.append(fwd)
        for a in range(n):
            copy(a, 0, sibling, (x, y, c)).wait_recv()
            for j, chip in enumerate(chips):
                copy(a, 4 + j, (*chip, 1 - c), (x, y, c)).wait_recv()
        for cp in sends:
            cp.wait_send()
        for cp in started:
            cp.wait()

    return pl.pallas_call(
        body, name=name,
        out_shape=[jax.ShapeDtypeStruct((N_DEV,) + a.shape, a.dtype) for a in arrs],
        in_specs=[ANY] * n, out_specs=[ANY] * n,
        scratch_shapes=[pltpu.SemaphoreType.DMA((7 * n,)), pltpu.SemaphoreType.DMA((7 * n,)),
                        pltpu.SemaphoreType.DMA((n,))],
    )(*arrs)


def _peer(k, x, y, c):
    return (1 - x if k & 4 else x), (1 - y if k & 2 else y), (1 - c if k & 1 else c)


PEER_ORDER = (1, 2, 4, 3, 5, 6, 7)


def _exchange_copy(kind, k, a, srcs, lands, send_sems, recv_sems, arriving=False):
    x, y, c = _my_place()
    me = 4 * x + 2 * y + c
    px, py, pc = _peer(k, x, y, c)
    them = 4 * px + 2 * py + pc
    if kind == "gather":
        src, dst_there, dst_here = srcs[a], lands[a].at[me], lands[a].at[them]
    else:
        src, dst_there, dst_here = srcs[a].at[them], lands[a].at[k - 1], lands[a].at[k - 1]
    return pltpu.make_async_remote_copy(
        src_ref=src, dst_ref=dst_here if arriving else dst_there,
        send_sem=send_sems.at[7 * a + k - 1], recv_sem=recv_sems.at[7 * a + k - 1],
        device_id=(px, py, pc), device_id_type=MESH)


def _exchange_start(name, kind, srcs, lands, after):
    n = len(srcs)

    def body(*refs):
        ins, lnd = refs[:n], refs[n:2 * n]
        send_sems, recv_sems = refs[2 * n + 1], refs[2 * n + 2]
        token = refs[-1]
        for k in PEER_ORDER:
            for a in range(n):
                _exchange_copy(kind, k, a, ins, lnd, send_sems, recv_sems).start()
        token[...] = jnp.zeros_like(token)

    hbm = lambda t: pltpu.with_memory_space_constraint(t, pltpu.HBM)
    out = pl.pallas_call(
        body, name=name,
        out_shape=(pltpu.SemaphoreType.DMA((7 * n,)), pltpu.SemaphoreType.DMA((7 * n,)),
                   *[pltpu.HBM(t.shape, t.dtype) for t in srcs], *[pltpu.HBM(t.shape, t.dtype) for t in lands],
                   jax.ShapeDtypeStruct((8, LANES), F32)),
        in_specs=[HBM] * (2 * n) + [ANY],
        out_specs=(SEM, SEM, *[HBM] * (2 * n), pl.BlockSpec(memory_space=pltpu.VMEM)),
        input_output_aliases={i: 2 + i for i in range(2 * n)},
        compiler_params=pltpu.CompilerParams(has_side_effects=EFFECT),
    )(*[hbm(t) for t in srcs], *[hbm(t) for t in lands], after)
    return out[0], out[1], list(out[2:2 + n]), list(out[2 + n:2 + 2 * n]), out[-1]


def _exchange_wait(name, kind, send_sems, recv_sems, srcs, lands, after):
    n = len(srcs)

    def body(*refs):
        ins, lnd = refs[:n], refs[n:2 * n]
        s_sems, r_sems = refs[2 * n], refs[2 * n + 1]
        for a in range(n):
            for k in PEER_ORDER:
                _exchange_copy(kind, k, a, ins, lnd, s_sems, r_sems).wait_send()
                _exchange_copy(kind, k, a, ins, lnd, s_sems, r_sems, arriving=True).wait_recv()

    out = pl.pallas_call(
        body, name=name,
        out_shape=(*[pltpu.HBM(t.shape, t.dtype) for t in srcs], *[pltpu.HBM(t.shape, t.dtype) for t in lands]),
        in_specs=[HBM] * (2 * n) + [SEM, SEM, ANY],
        out_specs=tuple([HBM] * (2 * n)),
        input_output_aliases={i: i for i in range(2 * n)},
        compiler_params=pltpu.CompilerParams(has_side_effects=EFFECT),
    )(*srcs, *lands, send_sems, recv_sems, after)
    return list(out[:n]), list(out[n:])


def _sum_parts(name, me, mine, parts, rows):
    _, R, C = mine.shape

    def body(me_ref, own_ref, p_ref, o_ref):
        acc = own_ref[0].astype(F32)
        for s in range(N_DEV - 1):
            acc = acc + p_ref[s].astype(F32)
        o_ref[...] = acc

    return pl.pallas_call(
        body, name=name,
        grid_spec=pltpu.PrefetchScalarGridSpec(
            num_scalar_prefetch=1, grid=(R // rows,),
            in_specs=[pl.BlockSpec((1, rows, C), lambda i, me_ref: (me_ref[0], i, 0)),
                      pl.BlockSpec((N_DEV - 1, rows, C), lambda i, me_ref: (0, i, 0))],
            out_specs=pl.BlockSpec((rows, C), lambda i, me_ref: (i, 0))),
        out_shape=jax.ShapeDtypeStruct((R, C), F32), compiler_params=_params(1),
    )(me, mine, parts)


def _sum_slots(name, a, rows):
    _, R, C = a.shape

    def body(a_ref, o_ref):
        acc = a_ref[0].astype(F32)
        for s in range(1, N_DEV):
            acc = acc + a_ref[s].astype(F32)
        o_ref[...] = acc

    return pl.pallas_call(
        body, name=name, grid=(R // rows,),
        in_specs=[pl.BlockSpec((N_DEV, rows, C), lambda i: (0, i, 0))],
        out_specs=pl.BlockSpec((rows, C), lambda i: (i, 0)),
        out_shape=jax.ShapeDtypeStruct((R, C), F32), compiler_params=_params(1),
    )(a)


_DIMS = {"nn": (((1,), (0,)), ((), ())), "nt": (((1,), (1,)), ((), ())), "tn": (((0,), (0,)), ((), ()))}


def _mm(name, mode, a_list, b_list, pairs, n_acc, M, N, K, tm, tn, tk, epilogue, out_dtypes, extras=(), after=None):
    tm, tn, tk = min(tm, M), min(tn, N), min(tk, K)
    assert M % tm == 0 and N % tn == 0 and K % tk == 0, (name, M, N, K, tm, tn, tk)
    nk = K // tk
    na, nb, ne, no = len(a_list), len(b_list), len(extras), len(out_dtypes)
    dims = _DIMS[mode]
    a_spec = (pl.BlockSpec((tk, tm), lambda j, i, k: (k, i)) if mode == "tn"
              else pl.BlockSpec((tm, tk), lambda j, i, k: (i, k)))
    b_spec = (pl.BlockSpec((tn, tk), lambda j, i, k: (j, k)) if mode == "nt"
              else pl.BlockSpec((tk, tn), lambda j, i, k: (k, j)))
    o_spec = pl.BlockSpec((tm, tn), lambda j, i, k: (i, j))
    tail = [] if after is None else [after]

    def body(*refs):
        a_refs, b_refs = refs[:na], refs[na:na + nb]
        e_refs = refs[na + nb:na + nb + ne]
        first_out = na + nb + ne + len(tail)
        o_refs = refs[first_out:first_out + no]
        acc_refs = refs[first_out + no:]

        def partial(oi):
            tot = None
            for ai, bi, ti in pairs:
                if ti == oi:
                    d = lax.dot_general(a_refs[ai][...], b_refs[bi][...], dims, preferred_element_type=F32)
                    tot = d if tot is None else tot + d
            return tot

        def finish(accs):
            outs = epilogue(accs, [e[...] for e in e_refs])
            for o_ref, o in zip(o_refs, outs):
                o_ref[...] = o.astype(o_ref.dtype)

        if nk == 1:
            finish([partial(oi) for oi in range(n_acc)])
        else:
            k = pl.program_id(2)

            @pl.when(k == 0)
            def _():
                for r in acc_refs:
                    r[...] = jnp.zeros_like(r)

            for oi in range(n_acc):
                acc_refs[oi][...] += partial(oi)

            @pl.when(k == nk - 1)
            def _():
                finish([r[...] for r in acc_refs])

    return pl.pallas_call(
        body, name=name, grid=(N // tn, M // tm, nk),
        in_specs=[a_spec] * na + [b_spec] * nb + [o_spec] * ne + [ANY] * len(tail),
        out_specs=[o_spec] * no,
        out_shape=[jax.ShapeDtypeStruct((M, N), dt) for dt in out_dtypes],
        scratch_shapes=[pltpu.VMEM((tm, tn), F32)] * (n_acc if nk > 1 else 0),
        compiler_params=_params(3),
    )(*a_list, *b_list, *extras, *tail)


def _ep_residual(accs, ex):
    return (ex[0] + accs[0],)


def _ep_rows_rms_bwd(acc, ex, rows):
    xv, res = ex
    r = lax.rsqrt(jnp.mean(xv * xv, axis=-1, keepdims=True) + EPS)
    xh = xv * r
    dg = jnp.sum(acc * xh, axis=0, keepdims=True)
    dxh = acc * rows[0]
    dx = r * (dxh - xh * jnp.mean(dxh * xh, axis=-1, keepdims=True)) + res
    return (dx, dx), (dg,)


def _sigmoid(x):
    return 0.5 * (jnp.tanh(0.5 * x) + 1.0)


def _ep_plain(accs, ex):
    return (accs[0],)


def _ep_swiglu(accs, ex):
    g, u = accs
    return g, u, (g * _sigmoid(g)) * u


def _ep_swiglu_bwd(accs, ex):
    dact = accs[0]
    g, u = ex[0].astype(F32), ex[1].astype(F32)
    sig = _sigmoid(g)
    silu = g * sig
    return dact * u * (sig * (1.0 + g * (1.0 - sig))), dact * silu


def _ep_two(accs, ex):
    return accs[0], accs[1]


ROWS = 256


def _row_spec(tm, width, col=0):
    return pl.BlockSpec((tm, width), lambda i: (i, col))


def _full_spec(shape):
    nd = len(shape)
    return pl.BlockSpec(shape, lambda i: (0,) * nd)


def _rms_fwd(name, x, g):
    T, D = x.shape
    tm = min(ROWS, T)

    def body(x_ref, g_ref, o_ref):
        xv = x_ref[...]
        r = lax.rsqrt(jnp.mean(xv * xv, axis=-1, keepdims=True) + EPS)
        o_ref[...] = ((xv * r) * g_ref[...]).astype(BF16)

    return pl.pallas_call(
        body, name=name, grid=(T // tm,),
        in_specs=[_row_spec(tm, D), _full_spec((1, D))], out_specs=_row_spec(tm, D),
        out_shape=jax.ShapeDtypeStruct((T, D), BF16), compiler_params=_params(1),
    )(x, g)


def _rms_bwd(name, dyn, x, g, res):
    T, D = x.shape
    tm = min(ROWS, T)

    def body(dyn_ref, x_ref, g_ref, res_ref, dx_ref, dx16_ref, dg_ref):
        outs, (dg,) = _ep_rows_rms_bwd(dyn_ref[...], [x_ref[...], res_ref[...]], [g_ref[...]])

        @pl.when(pl.program_id(0) == 0)
        def _():
            dg_ref[...] = jnp.zeros_like(dg_ref)

        dg_ref[...] += dg
        dx_ref[...] = outs[0]
        dx16_ref[...] = outs[1].astype(BF16)

    return pl.pallas_call(
        body, name=name, grid=(T // tm,),
        in_specs=[_row_spec(tm, D), _row_spec(tm, D), _full_spec((1, D)), _row_spec(tm, D)],
        out_specs=[_row_spec(tm, D), _row_spec(tm, D), _full_spec((1, D))],
        out_shape=[jax.ShapeDtypeStruct((T, D), F32), jax.ShapeDtypeStruct((T, D), BF16),
                   jax.ShapeDtypeStruct((1, D), F32)],
        compiler_params=_params(1),
    )(dyn, x, g, res)


def _outnorm_fwd(name, a, b, ga, gb):
    T, W = a.shape
    tm = min(ROWS, T)

    def body(a_ref, b_ref, ga_ref, gb_ref, o_ref):
        for src, gain, lo in ((a_ref, ga_ref, 0), (b_ref, gb_ref, W)):
            v = src[...]
            r = lax.rsqrt(jnp.mean(v * v, axis=-1, keepdims=True) + EPS)
            o_ref[:, lo:lo + W] = ((v * r) * gain[...]).astype(BF16)

    return pl.pallas_call(
        body, name=name, grid=(T // tm,),
        in_specs=[_row_spec(tm, W), _row_spec(tm, W), _full_spec((1, W)), _full_spec((1, W))],
        out_specs=_row_spec(tm, 2 * W),
        out_shape=jax.ShapeDtypeStruct((T, 2 * W), BF16), compiler_params=_params(1),
    )(a, b, ga, gb)


def _outnorm_bwd(name, dmixed, a, b, ga, gb):
    T, W = a.shape
    tm = min(ROWS, T)

    def body(dm_ref, a_ref, b_ref, ga_ref, gb_ref, da_ref, db_ref, dga_ref, dgb_ref):
        first = pl.program_id(0) == 0
        for src, gain, lo, dsrc, dgain in ((a_ref, ga_ref, 0, da_ref, dga_ref), (b_ref, gb_ref, W, db_ref, dgb_ref)):
            v, dv = src[...], dm_ref[:, lo:lo + W]
            r = lax.rsqrt(jnp.mean(v * v, axis=-1, keepdims=True) + EPS)
            vh = v * r

            @pl.when(first)
            def _():
                dgain[...] = jnp.zeros_like(dgain)

            dgain[...] += jnp.sum(dv * vh, axis=0, keepdims=True)
            dvh = dv * gain[...]
            dsrc[...] = r * (dvh - vh * jnp.mean(dvh * vh, axis=-1, keepdims=True))

    return pl.pallas_call(
        body, name=name, grid=(T // tm,),
        in_specs=[_row_spec(tm, 2 * W), _row_spec(tm, W), _row_spec(tm, W), _full_spec((1, W)), _full_spec((1, W))],
        out_specs=[_row_spec(tm, W), _row_spec(tm, W), _full_spec((1, W)), _full_spec((1, W))],
        out_shape=[jax.ShapeDtypeStruct((T, W), F32), jax.ShapeDtypeStruct((T, W), F32),
                   jax.ShapeDtypeStruct((1, W), F32), jax.ShapeDtypeStruct((1, W), F32)],
        compiler_params=_params(1),
    )(dmixed, a, b, ga, gb)


def _loss_head(name, y, target):
    T, D = y.shape
    tm = min(ROWS, T)

    def body(y_ref, t_ref, loss_ref, dy_ref, dy16_ref):
        d = y_ref[...] - t_ref[...]

        @pl.when(pl.program_id(0) == 0)
        def _():
            loss_ref[...] = jnp.zeros_like(loss_ref)

        per_token = jnp.mean(d * d, axis=-1, keepdims=True)
        loss_ref[...] += 0.5 * jnp.sum(per_token, axis=0, keepdims=True)
        dy = d * (1.0 / D)
        dy_ref[...] = dy
        dy16_ref[...] = dy.astype(BF16)

    return pl.pallas_call(
        body, name=name, grid=(T // tm,),
        in_specs=[_row_spec(tm, D), _row_spec(tm, D)],
        out_specs=[_full_spec((1, 1)), _row_spec(tm, D), _row_spec(tm, D)],
        out_shape=[jax.ShapeDtypeStruct((1, 1), F32), jax.ShapeDtypeStruct((T, D), F32),
                   jax.ShapeDtypeStruct((T, D), BF16)],
        compiler_params=_params(1),
    )(y, target)


def _bf16_dot(a, b, dims):
    return lax.dot_general(a.astype(BF16), b.astype(BF16), dims, preferred_element_type=F32)


@jax.custom_vjp
def _dot_nn(a, b):
    return _bf16_dot(a, b, _DIMS["nn"])


def _dot_nn_fwd(a, b):
    return _dot_nn(a, b), (a, b)


def _dot_nn_bwd(saved, ct):
    a, b = saved
    return _bf16_dot(ct, b, _DIMS["nt"]), _bf16_dot(a, ct, _DIMS["tn"])


_dot_nn.defvjp(_dot_nn_fwd, _dot_nn_bwd)


@jax.custom_vjp
def _dot_nt(a, b):
    return _bf16_dot(a, b, _DIMS["nt"])


def _dot_nt_fwd(a, b):
    return _dot_nt(a, b), (a, b)


def _dot_nt_bwd(saved, ct):
    a, b = saved
    return _bf16_dot(ct, b, _DIMS["nn"]), _bf16_dot(ct, a, _DIMS["tn"])


_dot_nt.defvjp(_dot_nt_fwd, _dot_nt_bwd)


def _iota(shape, dim):
    return lax.broadcasted_iota(jnp.int32, shape, dim)


def _head_sum_impl(x):
    same_head = (_iota((LANES, LANES), 0) // HEAD_DIM == _iota((LANES, LANES), 1) // HEAD_DIM).astype(BF16)
    pieces = []
    for i in range(x.shape[1] // LANES):
        xs = x[:, i * LANES:(i + 1) * LANES]
        hi = xs.astype(BF16)
        r1 = xs - hi.astype(F32)
        mid = r1.astype(BF16)
        lo = (r1 - mid.astype(F32)).astype(BF16)
        tot = None
        for part in (hi, mid, lo):
            d = lax.dot_general(part, same_head, _DIMS["nn"], preferred_element_type=F32)
            tot = d if tot is None else tot + d
        pieces.append(tot)
    return pieces[0] if len(pieces) == 1 else jnp.concatenate(pieces, axis=1)


@jax.custom_vjp
def _head_sum(x):
    return _head_sum_impl(x)


_head_sum.defvjp(lambda x: (_head_sum_impl(x), None), lambda _, ct: (_head_sum_impl(ct),))


def _head_rms(x, g_row):
    ms = _head_sum(x * x) * (1.0 / HEAD_DIM)
    return (x * lax.rsqrt(ms + EPS)) * g_row


@jax.custom_vjp
def _swap_halves(x):
    return pltpu.roll(x, HEAD_DIM, 1)


_swap_halves.defvjp(lambda x: (pltpu.roll(x, HEAD_DIM, 1), None), lambda _, ct: (pltpu.roll(ct, HEAD_DIM, 1),))


def _gelu(x):
    return 0.5 * x * (1.0 + lax.erf(x * (1.0 / math.sqrt(2.0))))


def _sgu_block(u_raw, v_raw, g_row, w, b_full):
    u, v = _gelu(u_raw), _gelu(v_raw)
    vn = _head_rms(v, g_row)
    causal = _iota((CHUNK, CHUNK), 0) >= _iota((CHUNK, CHUNK), 1)
    low_half = _iota((CHUNK, LANES), 1) < HEAD_DIM
    gates = []
    for p in range(v.shape[1] // LANES):
        vp = vn[:, p * LANES:(p + 1) * LANES]
        g0 = _dot_nn(jnp.where(causal, w[2 * p], 0.0), vp)
        g1 = _dot_nn(jnp.where(causal, w[2 * p + 1], 0.0), vp)
        gates.append(jnp.where(low_half, g0, g1))
    gate = jnp.concatenate(gates, axis=1) + b_full
    return u * gate


def _softmax_sink_fwd(s, sink):
    m = jnp.maximum(jnp.max(s, axis=1, keepdims=True), sink)
    e = jnp.exp(s - m)
    e_sink = jnp.exp(sink - m)
    r = 1.0 / (jnp.sum(e, axis=1, keepdims=True) + e_sink)
    p = e * r
    return p, (p, e_sink * r)


@jax.custom_vjp
def _softmax_with_sink(s, sink):
    return _softmax_sink_fwd(s, sink)[0]


def _softmax_sink_bwd(saved, dp):
    p, p_sink = saved
    delta = jnp.sum(p * dp, axis=1, keepdims=True)
    return p * (dp - delta), -p_sink * delta


_softmax_with_sink.defvjp(_softmax_sink_fwd, _softmax_sink_bwd)


def _attn_block(q_raw, k_prev, k_cur, v_prev, v_cur, qg_row, kg_row, sinks, bias):
    scale = 1.0 / math.sqrt(HEAD_DIM)
    n_q_heads = q_raw.shape[1] // HEAD_DIM
    qn = _head_rms(q_raw, qg_row) * scale
    kn_prev, kn_cur = _head_rms(k_prev, kg_row), _head_rms(k_cur, kg_row)
    own = _iota((GROUP * BLOCK, BLOCK), 1) <= (_iota((GROUP * BLOCK, BLOCK), 0) & (BLOCK - 1))
    low_half = _iota((BLOCK, LANES), 1) < HEAD_DIM
    heads = [None] * n_q_heads
    for kv in range(n_q_heads // GROUP):
        grp, kv_low = kv // 2, kv % 2 == 0
        lanes = slice(grp * LANES, (grp + 1) * LANES)
        keep = low_half if kv_low else ~low_half
        stacked = []
        for g in range(GROUP):
            h = GROUP * kv + g
            q2 = qn[:, (h // 2) * LANES:(h // 2 + 1) * LANES]
            src = q2 if (h % 2 == 0) == kv_low else _swap_halves(q2)
            stacked.append(jnp.where(keep, src, 0.0))
        q4 = jnp.concatenate(stacked, axis=0)
        bias4 = jnp.concatenate([bias[GROUP * kv + g] for g in range(GROUP)], axis=0)
        sink4 = jnp.concatenate([jnp.broadcast_to(sinks[GROUP * kv + g], (BLOCK, 1)) for g in range(GROUP)], axis=0)
        s = jnp.where(own, _dot_nt(q4, kn_cur[:, lanes]), _dot_nt(q4, kn_prev[:, lanes])) + bias4
        p = _softmax_with_sink(s, sink4)
        o4 = _dot_nn(jnp.where(own, p, 0.0), v_cur[:, lanes]) + _dot_nn(jnp.where(own, 0.0, p), v_prev[:, lanes])
        for g in range(GROUP):
            h = GROUP * kv + g
            o = o4[g * BLOCK:(g + 1) * BLOCK]
            heads[h] = o if (h % 2 == 0) == kv_low else _swap_halves(o)
    outs = [jnp.where(low_half, heads[2 * p], heads[2 * p + 1]) for p in range(n_q_heads // 2)]
    return jnp.concatenate(outs, axis=1)


def _attn_block_bwd(q_raw, k_prev, k_cur, v_prev, v_cur, qg_row, kg_row, sinks, bias, d_out):
    scale = 1.0 / math.sqrt(HEAD_DIM)
    n_q_heads = q_raw.shape[1] // HEAD_DIM
    qn, q_vjp = jax.vjp(lambda q, g: _head_rms(q, g) * scale, q_raw, qg_row)
    kn_prev, kp_vjp = jax.vjp(_head_rms, k_prev, kg_row)
    kn_cur, kc_vjp = jax.vjp(_head_rms, k_cur, kg_row)
    own = _iota((GROUP * BLOCK, BLOCK), 1) <= (_iota((GROUP * BLOCK, BLOCK), 0) & (BLOCK - 1))
    low_half = _iota((BLOCK, LANES), 1) < HEAD_DIM
    nn, nt, tn = _DIMS["nn"], _DIMS["nt"], _DIMS["tn"]
    n_groups = k_cur.shape[1] // LANES
    dq_heads, d_sinks, d_bias = [None] * n_q_heads, [None] * n_q_heads, [None] * n_q_heads
    dk_prev, dk_cur, dv_prev, dv_cur = ([None] * n_groups for _ in range(4))

    def add(parts, grp, val):
        parts[grp] = val if parts[grp] is None else parts[grp] + val

    for kv in range(n_q_heads // GROUP):
        grp, kv_low = kv // 2, kv % 2 == 0
        lanes = slice(grp * LANES, (grp + 1) * LANES)
        keep = low_half if kv_low else ~low_half
        q_rows, do_rows = [], []
        for g in range(GROUP):
            h = GROUP * kv + g
            pair = slice((h // 2) * LANES, (h // 2 + 1) * LANES)
            in_place = (h % 2 == 0) == kv_low
            q2, do2 = qn[:, pair], d_out[:, pair]
            q_rows.append(jnp.where(keep, q2 if in_place else pltpu.roll(q2, HEAD_DIM, 1), 0.0))
            do_rows.append(jnp.where(keep, do2 if in_place else pltpu.roll(do2, HEAD_DIM, 1), 0.0))
        q4, do4 = jnp.concatenate(q_rows, axis=0), jnp.concatenate(do_rows, axis=0)
        bias4 = jnp.concatenate([bias[GROUP * kv + g] for g in range(GROUP)], axis=0)
        sink4 = jnp.concatenate([jnp.broadcast_to(sinks[GROUP * kv + g], (BLOCK, 1)) for g in range(GROUP)], axis=0)
        kc2, kp2, vc2, vp2 = kn_cur[:, lanes], kn_prev[:, lanes], v_cur[:, lanes], v_prev[:, lanes]
        s = jnp.where(own, _bf16_dot(q4, kc2, nt), _bf16_dot(q4, kp2, nt)) + bias4
        p, (_, p_sink) = _softmax_sink_fwd(s, sink4)
        p_own, p_prev = jnp.where(own, p, 0.0), jnp.where(own, 0.0, p)
        add(dv_cur, grp, _bf16_dot(p_own, do4, tn))
        add(dv_prev, grp, _bf16_dot(p_prev, do4, tn))
        dp = jnp.where(own, _bf16_dot(do4, vc2, nt), _bf16_dot(do4, vp2, nt))
        delta = jnp.sum(p * dp, axis=1, keepdims=True)
        ds = p * (dp - delta)
        d_sink4 = -p_sink * delta
        ds_own, ds_prev = jnp.where(own, ds, 0.0), jnp.where(own, 0.0, ds)
        dq4 = _bf16_dot(ds_own, kc2, nn) + _bf16_dot(ds_prev, kp2, nn)
        add(dk_cur, grp, _bf16_dot(ds_own, q4, tn))
        add(dk_prev, grp, _bf16_dot(ds_prev, q4, tn))
        for g in range(GROUP):
            h = GROUP * kv + g
            rows = slice(g * BLOCK, (g + 1) * BLOCK)
            dq2 = jnp.where(keep, dq4[rows], 0.0)
            dq_heads[h] = dq2 if (h % 2 == 0) == kv_low else pltpu.roll(dq2, HEAD_DIM, 1)
            d_bias[h] = ds[rows]
            d_sinks[h] = jnp.sum(d_sink4[rows], axis=0, keepdims=True)
    dqn = jnp.concatenate([dq_heads[2 * p] + dq_heads[2 * p + 1] for p in range(n_q_heads // 2)], axis=1)
    dq, dqg = q_vjp(dqn)
    dkp, dkg_prev = kp_vjp(jnp.concatenate(dk_prev, axis=1))
    dkc, dkg_cur = kc_vjp(jnp.concatenate(dk_cur, axis=1))
    return (dq, dkp, dkc, jnp.concatenate(dv_prev, axis=1), jnp.concatenate(dv_cur, axis=1),
            dqg, dkg_prev + dkg_cur, d_sinks, d_bias)


def _bias_table():
    i, j = np.arange(BLOCK)[:, None], np.arange(BLOCK)[None, :]
    n = np.where(j <= i, i - j, i + BLOCK - j)
    max_exact = NUM_BUCKETS // 2
    nf = np.maximum(n, 1).astype(np.float64)
    large = max_exact + (np.log(nf / max_exact) / math.log(MAX_DISTANCE / max_exact) * (NUM_BUCKETS - max_exact)).astype(np.int32)
    large = np.minimum(large, NUM_BUCKETS - 1)
    return np.where(n < max_exact, n, large).astype(np.int32)


def _bias_fwd(name, rel_bias, buckets):
    nb_, nh = rel_bias.shape

    def body(rb_ref, bk_ref, o_ref):
        bk = bk_ref[...]
        own = _iota(bk.shape, 1) <= _iota(bk.shape, 0)
        for h in range(nh):
            acc = jnp.zeros(bk.shape, F32)
            for b in range(nb_):
                acc = jnp.where(bk == b, rb_ref[b, h], acc)
            o_ref[1, h] = acc
            o_ref[0, h] = jnp.where(own, acc, NEG_INF)

    return pl.pallas_call(
        body, name=name,
        in_specs=[pl.BlockSpec(memory_space=pltpu.SMEM), pl.BlockSpec(memory_space=pltpu.VMEM)],
        out_specs=pl.BlockSpec(memory_space=pltpu.VMEM),
        out_shape=jax.ShapeDtypeStruct((2, nh) + buckets.shape, F32),
    )(rel_bias, buckets)


def _bias_bwd(name, dbias, buckets, nb_):
    nh = dbias.shape[0]

    def body(db_ref, bk_ref, o_ref):
        bk = bk_ref[...]
        for h in range(nh):
            d = db_ref[h]
            for b in range(nb_):
                s = jnp.sum(jnp.where(bk == b, d, 0.0), axis=0, keepdims=True)
                s = jnp.sum(s, axis=1, keepdims=True)
                o_ref[b * nh + h:b * nh + h + 1, :] = jnp.broadcast_to(s, (1, LANES))

    return pl.pallas_call(
        body, name=name,
        in_specs=[pl.BlockSpec(memory_space=pltpu.VMEM), pl.BlockSpec(memory_space=pltpu.VMEM)],
        out_specs=pl.BlockSpec(memory_space=pltpu.VMEM),
        out_shape=jax.ShapeDtypeStruct((nb_ * nh, LANES), F32),
    )(dbias, buckets)


def _sgu_fwd(name, z, g_row, w, b_full, W):
    T = z.shape[0]

    def body(u_ref, v_ref, g_ref, w_ref, b_ref, o_ref):
        o_ref[...] = _sgu_block(u_ref[...], v_ref[...], g_ref[...], w_ref[...], b_ref[...])

    return pl.pallas_call(
        body, name=name, grid=(T // CHUNK,),
        in_specs=[_row_spec(CHUNK, W, 0), _row_spec(CHUNK, W, 1), _full_spec((1, W)),
                  _full_spec(w.shape), _full_spec((CHUNK, W))],
        out_specs=_row_spec(CHUNK, W),
        out_shape=jax.ShapeDtypeStruct((T, W), F32), compiler_params=_params(1),
    )(z, z, g_row, w, b_full)


def _sgu_bwd(name, z, g_row, w, b_full, d_out, W):
    T = z.shape[0]

    def body(u_ref, v_ref, g_ref, w_ref, b_ref, do_ref, dz_ref, dg_ref, dw_ref, db_ref):
        _, vjp = jax.vjp(_sgu_block, u_ref[...], v_ref[...], g_ref[...], w_ref[...], b_ref[...])
        du, dv, dg, dw, db = vjp(do_ref[...])
        dz_ref[:, :W] = du.astype(BF16)
        dz_ref[:, W:] = dv.astype(BF16)
        i = pl.program_id(0)

        @pl.when(i == 0)
        def _():
            dg_ref[...] = jnp.zeros_like(dg_ref)
            dw_ref[...] = jnp.zeros_like(dw_ref)
            db_ref[...] = jnp.zeros_like(db_ref)

        dg_ref[...] += dg
        dw_ref[...] += dw
        db_ref[...] += db

        @pl.when(i == pl.num_programs(0) - 1)
        def _():
            db_ref[...] = _head_sum_impl(db_ref[...])

    return pl.pallas_call(
        body, name=name, grid=(T // CHUNK,),
        in_specs=[_row_spec(CHUNK, W, 0), _row_spec(CHUNK, W, 1), _full_spec((1, W)),
                  _full_spec(w.shape), _full_spec((CHUNK, W)), _row_spec(CHUNK, W)],
        out_specs=[_row_spec(CHUNK, 2 * W), _full_spec((1, W)), _full_spec(w.shape), _full_spec((CHUNK, W))],
        out_shape=[jax.ShapeDtypeStruct((T, 2 * W), BF16), jax.ShapeDtypeStruct((1, W), F32),
                   jax.ShapeDtypeStruct(w.shape, F32), jax.ShapeDtypeStruct((CHUNK, W), F32)],
        compiler_params=_params(1),
    )(z, z, g_row, w, b_full, d_out)


def _attn_specs(WQ, WKV, q_col, k_col, v_col, blk_of):
    prev_of = lambda i: jnp.maximum(blk_of(i) - 1, 0)
    return [pl.BlockSpec((BLOCK, WQ), lambda i: (blk_of(i), q_col)),
            pl.BlockSpec((BLOCK, WKV), lambda i: (prev_of(i), k_col)),
            pl.BlockSpec((BLOCK, WKV), lambda i: (blk_of(i), k_col)),
            pl.BlockSpec((BLOCK, WKV), lambda i: (prev_of(i), v_col)),
            pl.BlockSpec((BLOCK, WKV), lambda i: (blk_of(i), v_col))]


def _bias_spec(bias, blk_of):
    return pl.BlockSpec((1,) + bias.shape[1:], lambda i: (jnp.minimum(blk_of(i), 1), 0, 0, 0))


def _attn_fwd(name, z, qg_row, kg_row, sinks_col, bias, WQ, WKV, q_col, k_col, v_col):
    T = z.shape[0]
    nh = sinks_col.shape[0]

    def body(q_ref, kp_ref, kc_ref, vp_ref, vc_ref, qg_ref, kg_ref, s_ref, b_ref, o_ref):
        sinks = [s_ref[h:h + 1, :] for h in range(nh)]
        o_ref[...] = _attn_block(q_ref[...], kp_ref[...], kc_ref[...], vp_ref[...], vc_ref[...],
                                 qg_ref[...], kg_ref[...], sinks, b_ref[0])

    return pl.pallas_call(
        body, name=name, grid=(T // BLOCK,),
        in_specs=_attn_specs(WQ, WKV, q_col, k_col, v_col, lambda i: i)
        + [_full_spec((1, WQ)), _full_spec((1, WKV)), _full_spec((nh, 1)), _bias_spec(bias, lambda i: i)],
        out_specs=_row_spec(BLOCK, WQ),
        out_shape=jax.ShapeDtypeStruct((T, WQ), F32), compiler_params=_params(1),
    )(z, z, z, z, z, qg_row, kg_row, sinks_col, bias)


def _attn_bwd(name, z, qg_row, kg_row, sinks_col, bias, d_out, WQ, WKV, q_col, k_col, v_col):
    T = z.shape[0]
    nblk = T // BLOCK
    nh = sinks_col.shape[0]
    blk_of = lambda i: nblk - 1 - i

    def body(q_ref, kp_ref, kc_ref, vp_ref, vc_ref, qg_ref, kg_ref, s_ref, b_ref, do_ref,
             dq_ref, dk_ref, dv_ref, dqg_ref, dkg_ref, ds_ref, db_ref, carry_k, carry_v):
        i = pl.program_id(0)
        sinks = [s_ref[h:h + 1, :] for h in range(nh)]
        dq, dkp, dkc, dvp, dvc, dqg, dkg, dsk, dbs = _attn_block_bwd(
            q_ref[...], kp_ref[...], kc_ref[...], vp_ref[...], vc_ref[...], qg_ref[...], kg_ref[...], sinks, b_ref[0],
            do_ref[...])

        @pl.when(i == 0)
        def _():
            carry_k[...] = jnp.zeros_like(carry_k)
            carry_v[...] = jnp.zeros_like(carry_v)
            dqg_ref[...] = jnp.zeros_like(dqg_ref)
            dkg_ref[...] = jnp.zeros_like(dkg_ref)
            ds_ref[...] = jnp.zeros_like(ds_ref)
            db_ref[...] = jnp.zeros_like(db_ref)

        dq_ref[...] = dq.astype(BF16)
        dk_ref[...] = (dkc + carry_k[...]).astype(BF16)
        dv_ref[...] = (dvc + carry_v[...]).astype(BF16)
        carry_k[...] = dkp
        carry_v[...] = dvp
        dqg_ref[...] += dqg
        dkg_ref[...] += dkg
        for h in range(nh):
            ds_ref[h:h + 1, :] += dsk[h]
            db_ref[h] += dbs[h]

    return pl.pallas_call(
        body, name=name, grid=(nblk,),
        in_specs=_attn_specs(WQ, WKV, q_col, k_col, v_col, blk_of)
        + [_full_spec((1, WQ)), _full_spec((1, WKV)), _full_spec((nh, 1)), _bias_spec(bias, blk_of),
           pl.BlockSpec((BLOCK, WQ), lambda i: (blk_of(i), 0))],
        out_specs=[pl.BlockSpec((BLOCK, WQ), lambda i: (blk_of(i), 0)),
                   pl.BlockSpec((BLOCK, WKV), lambda i: (blk_of(i), 0)),
                   pl.BlockSpec((BLOCK, WKV), lambda i: (blk_of(i), 0)),
                   _full_spec((1, WQ)), _full_spec((1, WKV)), _full_spec((nh, 1)), _full_spec(bias.shape[1:])],
        out_shape=[jax.ShapeDtypeStruct((T, WQ), BF16), jax.ShapeDtypeStruct((T, WKV), BF16),
                   jax.ShapeDtypeStruct((T, WKV), BF16), jax.ShapeDtypeStruct((1, WQ), F32),
                   jax.ShapeDtypeStruct((1, WKV), F32), jax.ShapeDtypeStruct((nh, 1), F32),
                   jax.ShapeDtypeStruct(bias.shape[1:], F32)],
        scratch_shapes=[pltpu.VMEM((BLOCK, WKV), F32), pltpu.VMEM((BLOCK, WKV), F32)],
        compiler_params=_params(1),
    )(z, z, z, z, z, qg_row, kg_row, sinks_col, bias, d_out)


def _adamw(name, w, g, m, v):
    shape = w.shape
    C = shape[-1]
    if w.ndim == 3 and shape[1] % ROWS == 0:
        work = shape
        grid = (shape[0], shape[1] // ROWS)
        spec = pl.BlockSpec((1, ROWS, C), lambda l, i: (l, i, 0))
    else:
        R = int(np.prod(shape[:-1]))
        tr = ROWS if R % ROWS == 0 else R
        work = (R, C)
        grid = (R // tr,)
        spec = pl.BlockSpec((tr, C), lambda i: (i, 0))
    w2, g2, m2, v2 = (t.reshape(work) for t in (w, g, m, v))

    def body(w_ref, g_ref, m_ref, v_ref, d_ref, nm_ref, nv_ref):
        gv = g_ref[...]
        nm = ADAM_B1 * m_ref[...] + (1.0 - ADAM_B1) * gv
        nv = ADAM_B2 * v_ref[...] + (1.0 - ADAM_B2) * (gv * gv)
        m_hat = nm / (1.0 - ADAM_B1 ** ADAM_STEP)
        v_hat = nv / (1.0 - ADAM_B2 ** ADAM_STEP)
        d_ref[...] = -ADAM_LR * (m_hat / (jnp.sqrt(v_hat) + ADAM_EPS) + ADAM_WD * w_ref[...])
        nm_ref[...] = nm
        nv_ref[...] = nv

    outs = pl.pallas_call(
        body, name=name, grid=grid, in_specs=[spec] * 4, out_specs=[spec] * 3,
        out_shape=[jax.ShapeDtypeStruct(work, F32)] * 3, compiler_params=_params(len(grid)),
    )(w2, g2, m2, v2)
    return tuple(o.reshape(shape) for o in outs)


def _pad_rows(flat):
    n = flat.shape[0]
    tile = 8 * LANES
    padded = -(-n // tile) * tile
    return jnp.pad(flat, (0, padded - n)).reshape(padded // LANES, LANES)


def kernel(x, rel_bias, norm1_g, w_in, sgu_norm_g, sgu_w, sgu_b, q_norm_g, k_norm_g, sinks, out_norm_a, out_norm_b, w_out, norm2_g, w_gate, w_up, w_down, loss_target, m_rel_bias, m_norm1_g, m_w_in, m_sgu_norm_g, m_sgu_w, m_sgu_b, m_q_norm_g, m_k_norm_g, m_sinks, m_out_norm_a, m_out_norm_b, m_w_out, m_norm2_g, m_w_gate, m_w_up, m_w_down, v_rel_bias, v_norm1_g, v_w_in, v_sgu_norm_g, v_sgu_w, v_sgu_b, v_q_norm_g, v_k_norm_g, v_sinks, v_out_norm_a, v_out_norm_b, v_w_out, v_norm2_g, v_w_gate, v_w_up, v_w_down):
    L = w_in.shape[0]
    T, D = x.shape[1], x.shape[2]
    W = D // 2
    NH = W // HEAD_DIM
    WKV = N_KV_HEADS * HEAD_DIM
    IN = N_DEV * w_in.shape[2]
    FF = N_DEV * w_gate.shape[2]
    assert IN == 2 * W + W + 2 * WKV and NH // N_KV_HEADS == GROUP
    q_col, k_col, v_col = 2 * W // W, (3 * W) // WKV, (3 * W + WKV) // WKV
    x0 = x.reshape(T, D)
    target = loss_target.reshape(T, D)

    shards = [jnp.swapaxes(w_in, 1, 2).astype(BF16), w_out.astype(BF16), jnp.swapaxes(w_gate, 1, 2).astype(BF16),
              jnp.swapaxes(w_up, 1, 2).astype(BF16), w_down.astype(BF16)]
    me = 4 * lax.axis_index("x") + 2 * lax.axis_index("y") + lax.axis_index("c")

    def landing(block):
        return lax.dynamic_update_slice(lax.empty((N_DEV,) + block.shape, block.dtype), block[None], (me, 0, 0))

    def as_matrices(got):
        return [g.reshape(N_DEV * g.shape[1], g.shape[2]) for g in got]

    groups = ((0,), (1,), (2, 3, 4))
    full = {l: [None] * len(shards) for l in range(L)}
    full[0][0] = as_matrices(_all_gather("gather_weights_0", [shards[0][0]]))[0]
    flight = {}

    def start_gather(l, first_group, after):
        for g in range(first_group, len(groups)):
            srcs = [shards[i][l] for i in groups[g]]
            flight[l, g] = _exchange_start(f"gather_start_{l}_{g}", "gather", srcs, [landing(s) for s in srcs], after)
            after = flight[l, g][4]
        return after

    def finish_gather(l, g, after):
        s_sem, r_sem, srcs, lands, _ = flight.pop((l, g))
        lands = _exchange_wait(f"gather_wait_{l}_{g}", "gather", s_sem, r_sem, srcs, lands, after)[1]
        for i, mat in zip(groups[g], as_matrices(lands)):
            full[l][i] = mat

    buckets = jnp.asarray(_bias_table())
    bias = _bias_fwd("bias_table", rel_bias, buckets)

    saved = []
    xl = x0
    token = start_gather(0, 1, full[0][0])
    for l in range(L):
        if l > 0:
            finish_gather(l, 0, xl)
        w_in_t = full[l][0]
        g1, g2 = norm1_g[l][None], norm2_g[l][None]
        sg_row = sgu_norm_g[l].reshape(1, W)
        b_full = jnp.repeat(sgu_b[l].T, HEAD_DIM, axis=1)
        qg_row = jnp.tile(q_norm_g[l], NH)[None]
        kg_row = jnp.tile(k_norm_g[l], N_KV_HEADS)[None]
        sinks_col = sinks[l][:, None]
        ga, gb = out_norm_a[l][None], out_norm_b[l][None]

        h = _rms_fwd(f"norm1_{l}", xl, g1)
        (z,) = _mm(f"proj_in_{l}", "nt", [h], [w_in_t], [(0, 0, 0)], 1, T, IN, D, 1024, 896, D, _ep_plain, [F32],
                   after=token)
        out_a = _sgu_fwd(f"sgu_{l}", z, sg_row, sgu_w[l], b_full, W)
        out_b = _attn_fwd(f"attn_{l}", z, qg_row, kg_row, sinks_col, bias, W, WKV, q_col, k_col, v_col)
        mixed = _outnorm_fwd(f"outnorm_{l}", out_a, out_b, ga, gb)
        finish_gather(l, 1, mixed)
        w_o = full[l][1]
        (x1,) = _mm(f"proj_out_{l}", "nn", [mixed], [w_o], [(0, 0, 0)], 1, T, D, D, 1024, 1024, D,
                    _ep_residual, [F32], extras=[xl])
        h2 = _rms_fwd(f"norm2_{l}", x1, g2)
        token = start_gather(l + 1, 0, h2) if l + 1 < L else h2
        finish_gather(l, 2, h2)
        w_g_t, w_u_t, w_d = full[l][2:]
        gate, up, act = _mm(f"mlp_in_{l}", "nt", [h2], [w_g_t, w_u_t], [(0, 0, 0), (0, 1, 1)], 2, T, FF, D,
                            1024, 512, D, _ep_swiglu, [BF16, BF16, BF16], after=token)
        (x2,) = _mm(f"mlp_out_{l}", "nn", [act], [w_d], [(0, 0, 0)], 1, T, D, FF, 512, 1024, FF,
                    _ep_residual, [F32], extras=[x1])
        saved.append((xl, h, z, out_a, out_b, mixed, x1, h2, gate, up, act,
                      g1, g2, sg_row, b_full, qg_row, kg_row, sinks_col, ga, gb))
        xl = x2

    loss_part, dy, dy16 = _loss_head("loss_head", xl, target)

    dbias = None
    small = [None] * L
    scatters = []

    def start_scatter(name, which, grads_t, after):
        srcs = [t.reshape(N_DEV, t.shape[0] // N_DEV, D) for t in grads_t]
        lands = [lax.empty((N_DEV - 1,) + s.shape[1:], BF16) for s in srcs]
        s_sem, r_sem, srcs, lands, tok = _exchange_start(name, "scatter", srcs, lands, after)
        scatters.append((name, which, s_sem, r_sem, srcs, lands))
        return tok

    token = None
    for l in reversed(range(L)):
        w_in_t, w_o, w_g_t, w_u_t, w_d = full[l]
        (xl, h, z, out_a, out_b, mixed, x1, h2, gate, up, act,
         g1, g2, sg_row, b_full, qg_row, kg_row, sinks_col, ga, gb) = saved[l]

        dgate, dup = _mm(f"d_mlp_out_{l}", "nt", [dy16], [w_d], [(0, 0, 0)], 1, T, FF, D, 1024, 1408, D,
                         _ep_swiglu_bwd, [BF16, BF16], extras=[gate, up], after=token)
        (dw_d,) = _mm(f"dw_down_{l}", "tn", [act], [dy16], [(0, 0, 0)], 1, FF, D, T, 1408, 1024, 2048, _ep_plain, [BF16])
        (dh2,) = _mm(f"d_mlp_in_{l}", "nn", [dgate, dup], [w_g_t, w_u_t], [(0, 0, 0), (1, 1, 0)], 1, T, D, FF,
                     512, 512, FF, _ep_plain, [F32])
        dx1, dx1_16, dg2 = _rms_bwd(f"d_norm2_{l}", dh2, x1, g2, dy)
        dw_g, dw_u = _mm(f"dw_gate_up_{l}", "tn", [dgate, dup], [h2], [(0, 0, 0), (1, 0, 1)], 2, FF, D, T,
                         1408, 1024, 1024, _ep_two, [BF16, BF16])
        token = start_scatter(f"scatter_mlp_start_{l}", (l, (4, 2, 3)), [dw_d, dw_g, dw_u], dw_u)
        (dmixed,) = _mm(f"d_proj_out_{l}", "nt", [dx1_16], [w_o], [(0, 0, 0)], 1, T, D, D, 1024, 1024, D, _ep_plain, [F32],
                        after=token)
        (dw_o,) = _mm(f"dw_out_{l}", "tn", [mixed], [dx1_16], [(0, 0, 0)], 1, D, D, T, 1024, 1024, 2048, _ep_plain, [BF16])
        d_a, d_b, dga, dgb = _outnorm_bwd(f"d_outnorm_{l}", dmixed, out_a, out_b, ga, gb)
        dz_uv, dsg, dsw, dsb = _sgu_bwd(f"d_sgu_{l}", z, sg_row, sgu_w[l], b_full, d_a, W)
        dq, dk, dv, dqg, dkg, dsk, dbs = _attn_bwd(f"d_attn_{l}", z, qg_row, kg_row, sinks_col, bias, d_b,
                                                   W, WKV, q_col, k_col, v_col)
        dbias = dbs if dbias is None else dbias + dbs
        dz = jnp.concatenate([dz_uv, dq, dk, dv], axis=1)
        (dh,) = _mm(f"d_proj_in_{l}", "nn", [dz], [w_in_t], [(0, 0, 0)], 1, T, D, IN, 1024, 1024, IN, _ep_plain, [F32])
        dy, dy16, dg1 = _rms_bwd(f"d_norm1_{l}", dh, xl, g1, dx1)
        (dw_i,) = _mm(f"dw_in_{l}", "tn", [dz], [h], [(0, 0, 0)], 1, IN, D, T, 896, 1024, 2048, _ep_plain, [BF16])
        token = start_scatter(f"scatter_mix_start_{l}", (l, (1, 0)), [dw_o, dw_i], dw_i)

        small[l] = dict(norm1_g=dg1[0], sgu_norm_g=dsg.reshape(NH, HEAD_DIM), sgu_w=dsw,
                        sgu_b=dsb[:, ::HEAD_DIM].T, q_norm_g=dqg.reshape(NH, HEAD_DIM).sum(0),
                        k_norm_g=dkg.reshape(N_KV_HEADS, HEAD_DIM).sum(0), sinks=dsk[:, 0],
                        out_norm_a=dga[0], out_norm_b=dgb[0], norm2_g=dg2[0])

    grad_x = dy.reshape(x.shape)
    d_rel = _bias_bwd("d_bias_table", dbias, buckets, NUM_BUCKETS)[:, 0].reshape(NUM_BUCKETS, NH)

    names = ["norm1_g", "sgu_norm_g", "sgu_w", "sgu_b", "q_norm_g", "k_norm_g", "sinks", "out_norm_a", "out_norm_b", "norm2_g"]
    parts = {"rel_bias": d_rel}
    for nme in names:
        parts[nme] = jnp.stack([small[l][nme] for l in range(L)])
    order = ["rel_bias"] + names
    packed = jnp.concatenate([_pad_rows(parts[nme].reshape(-1)) for nme in order], axis=0)
    small_flight = _exchange_start("gather_small_start", "gather", [packed], [landing(packed)], token)
    after = small_flight[4]

    me1 = me.reshape(1).astype(jnp.int32)
    grads_big = {}

    def finish_scatter(entry, after):
        name, (l, which), s_sem, r_sem, srcs, lands = entry
        srcs, lands = _exchange_wait(name.replace("start", "wait"), "scatter", s_sem, r_sem, srcs, lands, after)
        for i, src, land in zip(which, srcs, lands):
            rows = src.shape[1]
            after = _sum_parts(f"sum_grads_{l}_{i}", me1, src, land, 64 if rows % 64 == 0 else rows)
            grads_big[i, l] = after
        return after

    def stacked(i):
        return jnp.stack([grads_big[i, l] for l in range(L)])

    loss = lax.psum(loss_part[0, 0], ("x", "y", "c"))
    tr = lambda t: jnp.swapaxes(t, 1, 2)
    weights = dict(rel_bias=rel_bias, norm1_g=norm1_g, w_in=w_in, sgu_norm_g=sgu_norm_g, sgu_w=sgu_w, sgu_b=sgu_b,
                   q_norm_g=q_norm_g, k_norm_g=k_norm_g, sinks=sinks, out_norm_a=out_norm_a, out_norm_b=out_norm_b,
                   w_out=w_out, norm2_g=norm2_g, w_gate=w_gate, w_up=w_up, w_down=w_down)
    ms = dict(rel_bias=m_rel_bias, norm1_g=m_norm1_g, w_in=m_w_in, sgu_norm_g=m_sgu_norm_g, sgu_w=m_sgu_w, sgu_b=m_sgu_b,
              q_norm_g=m_q_norm_g, k_norm_g=m_k_norm_g, sinks=m_sinks, out_norm_a=m_out_norm_a, out_norm_b=m_out_norm_b,
              w_out=m_w_out, norm2_g=m_norm2_g, w_gate=m_w_gate, w_up=m_w_up, w_down=m_w_down)
    vs = dict(rel_bias=v_rel_bias, norm1_g=v_norm1_g, w_in=v_w_in, sgu_norm_g=v_sgu_norm_g, sgu_w=v_sgu_w, sgu_b=v_sgu_b,
              q_norm_g=v_q_norm_g, k_norm_g=v_k_norm_g, sinks=v_sinks, out_norm_a=v_out_norm_a, out_norm_b=v_out_norm_b,
              w_out=v_w_out, norm2_g=v_norm2_g, w_gate=v_w_gate, w_up=v_w_up, w_down=v_w_down)
    all_names = ["rel_bias", "norm1_g", "w_in", "sgu_norm_g", "sgu_w", "sgu_b", "q_norm_g", "k_norm_g", "sinks",
                 "out_norm_a", "out_norm_b", "w_out", "norm2_g", "w_gate", "w_up", "w_down"]
    transposed = ("w_in", "w_gate", "w_up")
    grads, deltas, new_m, new_v = {}, {}, {}, {}

    def update(nme, g):
        if nme in transposed:
            outs = _adamw(f"adamw_{nme}", tr(weights[nme]), g, tr(ms[nme]), tr(vs[nme]))
            grads[nme], (deltas[nme], new_m[nme], new_v[nme]) = tr(g), [tr(o) for o in outs]
        else:
            grads[nme] = g
            deltas[nme], new_m[nme], new_v[nme] = _adamw(f"adamw_{nme}", weights[nme], g, ms[nme], vs[nme])
        return new_v[nme]

    for entry in scatters[:-1]:
        after = finish_scatter(entry, after)
    for nme, i in (("w_gate", 2), ("w_up", 3), ("w_down", 4)):
        after = update(nme, stacked(i))

    s_sem, r_sem, srcs, lands, _ = small_flight
    _, (everyone,) = _exchange_wait("gather_small_wait", "gather", s_sem, r_sem, srcs, lands, after)
    rows = packed.shape[0]
    summed = _sum_slots("sum_small_grads", everyone, 64 if rows % 64 == 0 else 8)
    at = 0
    for nme in order:
        n = int(np.prod(parts[nme].shape))
        n_rows = -(-n // (8 * LANES)) * 8
        after = update(nme, summed[at:at + n_rows].reshape(-1)[:n].reshape(parts[nme].shape))
        at += n_rows

    finish_scatter(scatters[-1], after)
    update("w_out", stacked(1))
    update("w_in", stacked(0))
    return (loss, grad_x, *[grads[nme] for nme in all_names], *[deltas[nme] for nme in all_names],
            *[new_m[nme] for nme in all_names], *[new_v[nme] for nme in all_names])
```

```python
import functools
import math

import numpy as np
import jax
import jax.numpy as jnp
from jax import lax
from jax.experimental import pallas as pl
from jax.experimental.pallas import tpu as pltpu

F32 = jnp.float32
BF16 = jnp.bfloat16

N_DEV = 8
HEAD_DIM = 64
CHUNK = 128
BLOCK = 128
N_KV_HEADS = 4
GROUP = 4
NUM_BUCKETS = 32
MAX_DISTANCE = 128
EPS = 1e-6
NEG_INF = -1e30
LANES = 128
VMEM_LIMIT = 56 * 2 ** 20

ADAM_LR = 0.001
ADAM_B1 = 0.9
ADAM_B2 = 0.999
ADAM_EPS = 1e-08
ADAM_WD = 0.01
ADAM_STEP = 10

MESH = pl.DeviceIdType.MESH
ANY = pl.BlockSpec(memory_space=pl.ANY)
HBM = pl.BlockSpec(memory_space=pltpu.HBM)
SEM = pl.BlockSpec(memory_space=pltpu.SEMAPHORE)
EFFECT = pltpu.SideEffectType.DATAFLOW_SIDE_EFFECTING


def _params(n_axes):
    return pltpu.CompilerParams(dimension_semantics=("arbitrary",) * n_axes, vmem_limit_bytes=VMEM_LIMIT)


def _my_place():
    return lax.axis_index("x"), lax.axis_index("y"), lax.axis_index("c")


def _all_gather(name, arrs):
    n = len(arrs)

    def body(*refs):
        ins, outs = refs[:n], refs[n:2 * n]
        send_sems, recv_sems, local_sems = refs[2 * n:]
        x, y, c = _my_place()
        sibling = (x, y, 1 - c)
        chips = [(1 - x, y), (x, 1 - y), (1 - x, 1 - y)]

        def slot(a, px, py, pc):
            return outs[a].at[4 * px + 2 * py + pc]

        def copy(a, k, block, to, src=None):
            return pltpu.make_async_remote_copy(
                src_ref=slot(a, *block) if src is None else src, dst_ref=slot(a, *block),
                send_sem=send_sems.at[7 * a + k], recv_sem=recv_sems.at[7 * a + k],
                device_id=to, device_id_type=MESH)

        started = []
        for a in range(n):
            mine = pltpu.make_async_copy(ins[a], slot(a, x, y, c), local_sems.at[a])
            mine.start()
            started.append(mine)
        sends = []
        for a in range(n):
            first = [copy(a, 0, (x, y, c), sibling, src=ins[a])]
            first += [copy(a, 1 + j, (x, y, c), (*chip, c), src=ins[a]) for j, chip in enumerate(chips)]
            for cp in first:
                cp.start()
            sends += first
        for a in range(n):
            for j, chip in enumerate(chips):
                copy(a, 1 + j, (*chip, c), (x, y, c)).wait_recv()
                fwd = copy(a, 4 + j, (*chip, c), sibling)
                fwd.start()
                sends.append(fwd)
        for a in range(n):
            copy(a, 0, sibling, (x, y, c)).wait_recv()
            for j, chip in enumerate(chips):
                copy(a, 4 + j, (*chip, 1 - c), (x, y, c)).wait_recv()
        for cp in sends:
            cp.wait_send()
        for cp in started:
            cp.wait()

    return pl.pallas_call(
        body, name=name,
        out_shape=[jax.ShapeDtypeStruct((N_DEV,) + a.shape, a.dtype) for a in arrs],
        in_specs=[ANY] * n, out_specs=[ANY] * n,
        scratch_shapes=[pltpu.SemaphoreType.DMA((7 * n,)), pltpu.SemaphoreType.DMA((7 * n,)),
                        pltpu.SemaphoreType.DMA((n,))],
    )(*arrs)


def _peer(k, x, y, c):
    return (1 - x if k & 4 else x), (1 - y if k & 2 else y), (1 - c if k & 1 else c)


PEER_ORDER = (1, 2, 4, 3, 5, 6, 7)


def _exchange_copy(kind, k, a, srcs, lands, send_sems, recv_sems, arriving=False):
    x, y, c = _my_place()
    me = 4 * x + 2 * y + c
    px, py, pc = _peer(k, x, y, c)
    them = 4 * px + 2 * py + pc
    if kind == "gather":
        src, dst_there, dst_here = srcs[a], lands[a].at[me], lands[a].at[them]
    else:
        src, dst_there, dst_here = srcs[a].at[them], lands[a].at[k - 1], lands[a].at[k - 1]
    return pltpu.make_async_remote_copy(
        src_ref=src, dst_ref=dst_here if arriving else dst_there,
        send_sem=send_sems.at[7 * a + k - 1], recv_sem=recv_sems.at[7 * a + k - 1],
        device_id=(px, py, pc), device_id_type=MESH)


def _exchange_start(name, kind, srcs, lands, after):
    n = len(srcs)

    def body(*refs):
        ins, lnd = refs[:n], refs[n:2 * n]
        send_sems, recv_sems = refs[2 * n + 1], refs[2 * n + 2]
        token = refs[-1]
        for k in PEER_ORDER:
            for a in range(n):
                _exchange_copy(kind, k, a, ins, lnd, send_sems, recv_sems).start()
        token[...] = jnp.zeros_like(token)

    hbm = lambda t: pltpu.with_memory_space_constraint(t, pltpu.HBM)
    out = pl.pallas_call(
        body, name=name,
        out_shape=(pltpu.SemaphoreType.DMA((7 * n,)), pltpu.SemaphoreType.DMA((7 * n,)),
                   *[pltpu.HBM(t.shape, t.dtype) for t in srcs], *[pltpu.HBM(t.shape, t.dtype) for t in lands],
                   jax.ShapeDtypeStruct((8, LANES), F32)),
        in_specs=[HBM] * (2 * n) + [ANY],
        out_specs=(SEM, SEM, *[HBM] * (2 * n), pl.BlockSpec(memory_space=pltpu.VMEM)),
        input_output_aliases={i: 2 + i for i in range(2 * n)},
        compiler_params=pltpu.CompilerParams(has_side_effects=EFFECT),
    )(*[hbm(t) for t in srcs], *[hbm(t) for t in lands], after)
    return out[0], out[1], list(out[2:2 + n]), list(out[2 + n:2 + 2 * n]), out[-1]


def _exchange_wait(name, kind, send_sems, recv_sems, srcs, lands, after):
    n = len(srcs)

    def body(*refs):
        ins, lnd = refs[:n], refs[n:2 * n]
        s_sems, r_sems = refs[2 * n], refs[2 * n + 1]
        for a in range(n):
            for k in PEER_ORDER:
                _exchange_copy(kind, k, a, ins, lnd, s_sems, r_sems).wait_send()
                _exchange_copy(kind, k, a, ins, lnd, s_sems, r_sems, arriving=True).wait_recv()

    out = pl.pallas_call(
        body, name=name,
        out_shape=(*[pltpu.HBM(t.shape, t.dtype) for t in srcs], *[pltpu.HBM(t.shape, t.dtype) for t in lands]),
        in_specs=[HBM] * (2 * n) + [SEM, SEM, ANY],
        out_specs=tuple([HBM] * (2 * n)),
        input_output_aliases={i: i for i in range(2 * n)},
        compiler_params=pltpu.CompilerParams(has_side_effects=EFFECT),
    )(*srcs, *lands, send_sems, recv_sems, after)
    return list(out[:n]), list(out[n:])


def _sum_parts(name, me, mine, parts, rows):
    _, R, C = mine.shape

    def body(me_ref, own_ref, p_ref, o_ref):
        acc = own_ref[0].astype(F32)
        for s in range(N_DEV - 1):
            acc = acc + p_ref[s].astype(F32)
        o_ref[...] = acc

    return pl.pallas_call(
        body, name=name,
        grid_spec=pltpu.PrefetchScalarGridSpec(
            num_scalar_prefetch=1, grid=(R // rows,),
            in_specs=[pl.BlockSpec((1, rows, C), lambda i, me_ref: (me_ref[0], i, 0)),
                      pl.BlockSpec((N_DEV - 1, rows, C), lambda i, me_ref: (0, i, 0))],
            out_specs=pl.BlockSpec((rows, C), lambda i, me_ref: (i, 0))),
        out_shape=jax.ShapeDtypeStruct((R, C), F32), compiler_params=_params(1),
    )(me, mine, parts)


def _sum_slots(name, a, rows):
    _, R, C = a.shape

    def body(a_ref, o_ref):
        acc = a_ref[0].astype(F32)
        for s in range(1, N_DEV):
            acc = acc + a_ref[s].astype(F32)
        o_ref[...] = acc

    return pl.pallas_call(
        body, name=name, grid=(R // rows,),
        in_specs=[pl.BlockSpec((N_DEV, rows, C), lambda i: (0, i, 0))],
        out_specs=pl.BlockSpec((rows, C), lambda i: (i, 0)),
        out_shape=jax.ShapeDtypeStruct((R, C), F32), compiler_params=_params(1),
    )(a)


_DIMS = {"nn": (((1,), (0,)), ((), ())), "nt": (((1,), (1,)), ((), ())), "tn": (((0,), (0,)), ((), ()))}


def _mm(name, mode, a_list, b_list, pairs, n_acc, M, N, K, tm, tn, tk, epilogue, out_dtypes, extras=(), after=None):
    tm, tn, tk = min(tm, M), min(tn, N), min(tk, K)
    assert M % tm == 0 and N % tn == 0 and K % tk == 0, (name, M, N, K, tm, tn, tk)
    nk = K // tk
    na, nb, ne, no = len(a_list), len(b_list), len(extras), len(out_dtypes)
    dims = _DIMS[mode]
    a_spec = (pl.BlockSpec((tk, tm), lambda j, i, k: (k, i)) if mode == "tn"
              else pl.BlockSpec((tm, tk), lambda j, i, k: (i, k)))
    b_spec = (pl.BlockSpec((tn, tk), lambda j, i, k: (j, k)) if mode == "nt"
              else pl.BlockSpec((tk, tn), lambda j, i, k: (k, j)))
    o_spec = pl.BlockSpec((tm, tn), lambda j, i, k: (i, j))
    tail = [] if after is None else [after]

    def body(*refs):
        a_refs, b_refs = refs[:na], refs[na:na + nb]
        e_refs = refs[na + nb:na + nb + ne]
        first_out = na + nb + ne + len(tail)
        o_refs = refs[first_out:first_out + no]
        acc_refs = refs[first_out + no:]

        def partial(oi):
            tot = None
            for ai, bi, ti in pairs:
                if ti == oi:
                    d = lax.dot_general(a_refs[ai][...], b_refs[bi][...], dims, preferred_element_type=F32)
                    tot = d if tot is None else tot + d
            return tot

        def finish(accs):
            outs = epilogue(accs, [e[...] for e in e_refs])
            for o_ref, o in zip(o_refs, outs):
                o_ref[...] = o.astype(o_ref.dtype)

        if nk == 1:
            finish([partial(oi) for oi in range(n_acc)])
        else:
            k = pl.program_id(2)

            @pl.when(k == 0)
            def _():
                for r in acc_refs:
                    r[...] = jnp.zeros_like(r)

            for oi in range(n_acc):
                acc_refs[oi][...] += partial(oi)

            @pl.when(k == nk - 1)
            def _():
                finish([r[...] for r in acc_refs])

    return pl.pallas_call(
        body, name=name, grid=(N // tn, M // tm, nk),
        in_specs=[a_spec] * na + [b_spec] * nb + [o_spec] * ne + [ANY] * len(tail),
        out_specs=[o_spec] * no,
        out_shape=[jax.ShapeDtypeStruct((M, N), dt) for dt in out_dtypes],
        scratch_shapes=[pltpu.VMEM((tm, tn), F32)] * (n_acc if nk > 1 else 0),
        compiler_params=_params(3),
    )(*a_list, *b_list, *extras, *tail)


def _ep_residual(accs, ex):
    return (ex[0] + accs[0],)


def _ep_rows_rms_bwd(acc, ex, rows):
    xv, res = ex
    r = lax.rsqrt(jnp.mean(xv * xv, axis=-1, keepdims=True) + EPS)
    xh = xv * r
    dg = jnp.sum(acc * xh, axis=0, keepdims=True)
    dxh = acc * rows[0]
    dx = r * (dxh - xh * jnp.mean(dxh * xh, axis=-1, keepdims=True)) + res
    return (dx, dx), (dg,)


def _sigmoid(x):
    return 0.5 * (jnp.tanh(0.5 * x) + 1.0)


def _ep_plain(accs, ex):
    return (accs[0],)


def _ep_swiglu(accs, ex):
    g, u = accs
    return g, u, (g * _sigmoid(g)) * u


def _ep_swiglu_bwd(accs, ex):
    dact = accs[0]
    g, u = ex[0].astype(F32), ex[1].astype(F32)
    sig = _sigmoid(g)
    silu = g * sig
    return dact * u * (sig * (1.0 + g * (1.0 - sig))), dact * silu


def _ep_two(accs, ex):
    return accs[0], accs[1]


ROWS = 256


def _row_spec(tm, width, col=0):
    return pl.BlockSpec((tm, width), lambda i: (i, col))


def _full_spec(shape):
    nd = len(shape)
    return pl.BlockSpec(shape, lambda i: (0,) * nd)


def _rms_fwd(name, x, g):
    T, D = x.shape
    tm = min(ROWS, T)

    def body(x_ref, g_ref, o_ref):
        xv = x_ref[...]
        r = lax.rsqrt(jnp.mean(xv * xv, axis=-1, keepdims=True) + EPS)
        o_ref[...] = ((xv * r) * g_ref[...]).astype(BF16)

    return pl.pallas_call(
        body, name=name, grid=(T // tm,),
        in_specs=[_row_spec(tm, D), _full_spec((1, D))], out_specs=_row_spec(tm, D),
        out_shape=jax.ShapeDtypeStruct((T, D), BF16), compiler_params=_params(1),
    )(x, g)


def _rms_bwd(name, dyn, x, g, res):
    T, D = x.shape
    tm = min(ROWS, T)

    def body(dyn_ref, x_ref, g_ref, res_ref, dx_ref, dx16_ref, dg_ref):
        outs, (dg,) = _ep_rows_rms_bwd(dyn_ref[...], [x_ref[...], res_ref[...]], [g_ref[...]])

        @pl.when(pl.program_id(0) == 0)
        def _():
            dg_ref[...] = jnp.zeros_like(dg_ref)

        dg_ref[...] += dg
        dx_ref[...] = outs[0]
        dx16_ref[...] = outs[1].astype(BF16)

    return pl.pallas_call(
        body, name=name, grid=(T // tm,),
        in_specs=[_row_spec(tm, D), _row_spec(tm, D), _full_spec((1, D)), _row_spec(tm, D)],
        out_specs=[_row_spec(tm, D), _row_spec(tm, D), _full_spec((1, D))],
        out_shape=[jax.ShapeDtypeStruct((T, D), F32), jax.ShapeDtypeStruct((T, D), BF16),
                   jax.ShapeDtypeStruct((1, D), F32)],
        compiler_params=_params(1),
    )(dyn, x, g, res)


def _outnorm_fwd(name, a, b, ga, gb):
    T, W = a.shape
    tm = min(ROWS, T)

    def body(a_ref, b_ref, ga_ref, gb_ref, o_ref):
        for src, gain, lo in ((a_ref, ga_ref, 0), (b_ref, gb_ref, W)):
            v = src[...]
            r = lax.rsqrt(jnp.mean(v * v, axis=-1, keepdims=True) + EPS)
            o_ref[:, lo:lo + W] = ((v * r) * gain[...]).astype(BF16)

    return pl.pallas_call(
        body, name=name, grid=(T // tm,),
        in_specs=[_row_spec(tm, W), _row_spec(tm, W), _full_spec((1, W)), _full_spec((1, W))],
        out_specs=_row_spec(tm, 2 * W),
        out_shape=jax.ShapeDtypeStruct((T, 2 * W), BF16), compiler_params=_params(1),
    )(a, b, ga, gb)


def _outnorm_bwd(name, dmixed, a, b, ga, gb):
    T, W = a.shape
    tm = min(ROWS, T)

    def body(dm_ref, a_ref, b_ref, ga_ref, gb_ref, da_ref, db_ref, dga_ref, dgb_ref):
        first = pl.program_id(0) == 0
        for src, gain, lo, dsrc, dgain in ((a_ref, ga_ref, 0, da_ref, dga_ref), (b_ref, gb_ref, W, db_ref, dgb_ref)):
            v, dv = src[...], dm_ref[:, lo:lo + W]
            r = lax.rsqrt(jnp.mean(v * v, axis=-1, keepdims=True) + EPS)
            vh = v * r

            @pl.when(first)
            def _():
                dgain[...] = jnp.zeros_like(dgain)

            dgain[...] += jnp.sum(dv * vh, axis=0, keepdims=True)
            dvh = dv * gain[...]
            dsrc[...] = r * (dvh - vh * jnp.mean(dvh * vh, axis=-1, keepdims=True))

    return pl.pallas_call(
        body, name=name, grid=(T // tm,),
        in_specs=[_row_spec(tm, 2 * W), _row_spec(tm, W), _row_spec(tm, W), _full_spec((1, W)), _full_spec((1, W))],
        out_specs=[_row_spec(tm, W), _row_spec(tm, W), _full_spec((1, W)), _full_spec((1, W))],
        out_shape=[jax.ShapeDtypeStruct((T, W), F32), jax.ShapeDtypeStruct((T, W), F32),
                   jax.ShapeDtypeStruct((1, W), F32), jax.ShapeDtypeStruct((1, W), F32)],
        compiler_params=_params(1),
    )(dmixed, a, b, ga, gb)


def _loss_head(name, y, target):
    T, D = y.shape
    tm = min(ROWS, T)

    def body(y_ref, t_ref, loss_ref, dy_ref, dy16_ref):
        d = y_ref[...] - t_ref[...]

        @pl.when(pl.program_id(0) == 0)
        def _():
            loss_ref[...] = jnp.zeros_like(loss_ref)

        per_token = jnp.mean(d * d, axis=-1, keepdims=True)
        loss_ref[...] += 0.5 * jnp.sum(per_token, axis=0, keepdims=True)
        dy = d * (1.0 / D)
        dy_ref[...] = dy
        dy16_ref[...] = dy.astype(BF16)

    return pl.pallas_call(
        body, name=name, grid=(T // tm,),
        in_specs=[_row_spec(tm, D), _row_spec(tm, D)],
        out_specs=[_full_spec((1, 1)), _row_spec(tm, D), _row_spec(tm, D)],
        out_shape=[jax.ShapeDtypeStruct((1, 1), F32), jax.ShapeDtypeStruct((T, D), F32),
                   jax.ShapeDtypeStruct((T, D), BF16)],
        compiler_params=_params(1),
    )(y, target)


def _bf16_dot(a, b, dims):
    return lax.dot_general(a.astype(BF16), b.astype(BF16), dims, preferred_element_type=F32)


@jax.custom_vjp
def _dot_nn(a, b):
    return _bf16_dot(a, b, _DIMS["nn"])


def _dot_nn_fwd(a, b):
    return _dot_nn(a, b), (a, b)


def _dot_nn_bwd(saved, ct):
    a, b = saved
    return _bf16_dot(ct, b, _DIMS["nt"]), _bf16_dot(a, ct, _DIMS["tn"])


_dot_nn.defvjp(_dot_nn_fwd, _dot_nn_bwd)


@jax.custom_vjp
def _dot_nt(a, b):
    return _bf16_dot(a, b, _DIMS["nt"])


def _dot_nt_fwd(a, b):
    return _dot_nt(a, b), (a, b)


def _dot_nt_bwd(saved, ct):
    a, b = saved
    return _bf16_dot(ct, b, _DIMS["nn"]), _bf16_dot(ct, a, _DIMS["tn"])


_dot_nt.defvjp(_dot_nt_fwd, _dot_nt_bwd)


def _iota(shape, dim):
    return lax.broadcasted_iota(jnp.int32, shape, dim)


def _head_sum_impl(x):
    same_head = (_iota((LANES, LANES), 0) // HEAD_DIM == _iota((LANES, LANES), 1) // HEAD_DIM).astype(BF16)
    pieces = []
    for i in range(x.shape[1] // LANES):
        xs = x[:, i * LANES:(i + 1) * LANES]
        hi = xs.astype(BF16)
        r1 = xs - hi.astype(F32)
        mid = r1.astype(BF16)
        lo = (r1 - mid.astype(F32)).astype(BF16)
        tot = None
        for part in (hi, mid, lo):
            d = lax.dot_general(part, same_head, _DIMS["nn"], preferred_element_type=F32)
            tot = d if tot is None else tot + d
        pieces.append(tot)
    return pieces[0] if len(pieces) == 1 else jnp.concatenate(pieces, axis=1)


@jax.custom_vjp
def _head_sum(x):
    return _head_sum_impl(x)


_head_sum.defvjp(lambda x: (_head_sum_impl(x), None), lambda _, ct: (_head_sum_impl(ct),))


def _head_rms(x, g_row):
    ms = _head_sum(x * x) * (1.0 / HEAD_DIM)
    return (x * lax.rsqrt(ms + EPS)) * g_row


@jax.custom_vjp
def _swap_halves(x):
    return pltpu.roll(x, HEAD_DIM, 1)


_swap_halves.defvjp(lambda x: (pltpu.roll(x, HEAD_DIM, 1), None), lambda _, ct: (pltpu.roll(ct, HEAD_DIM, 1),))


def _gelu(x):
    return 0.5 * x * (1.0 + lax.erf(x * (1.0 / math.sqrt(2.0))))


def _sgu_block(u_raw, v_raw, g_row, w, b_full):
    u, v = _gelu(u_raw), _gelu(v_raw)
    vn = _head_rms(v, g_row)
    causal = _iota((CHUNK, CHUNK), 0) >= _iota((CHUNK, CHUNK), 1)
    low_half = _iota((CHUNK, LANES), 1) < HEAD_DIM
    gates = []
    for p in range(v.shape[1] // LANES):
        vp = vn[:, p * LANES:(p + 1) * LANES]
        g0 = _dot_nn(jnp.where(causal, w[2 * p], 0.0), vp)
        g1 = _dot_nn(jnp.where(causal, w[2 * p + 1], 0.0), vp)
        gates.append(jnp.where(low_half, g0, g1))
    gate = jnp.concatenate(gates, axis=1) + b_full
    return u * gate


def _softmax_sink_fwd(s, sink):
    m = jnp.maximum(jnp.max(s, axis=1, keepdims=True), sink)
    e = jnp.exp(s - m)
    e_sink = jnp.exp(sink - m)
    r = 1.0 / (jnp.sum(e, axis=1, keepdims=True) + e_sink)
    p = e * r
    return p, (p, e_sink * r)


@jax.custom_vjp
def _softmax_with_sink(s, sink):
    return _softmax_sink_fwd(s, sink)[0]


def _softmax_sink_bwd(saved, dp):
    p, p_sink = saved
    delta = jnp.sum(p * dp, axis=1, keepdims=True)
    return p * (dp - delta), -p_sink * delta


_softmax_with_sink.defvjp(_softmax_sink_fwd, _softmax_sink_bwd)


def _attn_block(q_raw, k_prev, k_cur, v_prev, v_cur, qg_row, kg_row, sinks, bias):
    scale = 1.0 / math.sqrt(HEAD_DIM)
    n_q_heads = q_raw.shape[1] // HEAD_DIM
    qn = _head_rms(q_raw, qg_row) * scale
    kn_prev, kn_cur = _head_rms(k_prev, kg_row), _head_rms(k_cur, kg_row)
    own = _iota((GROUP * BLOCK, BLOCK), 1) <= (_iota((GROUP * BLOCK, BLOCK), 0) & (BLOCK - 1))
    low_half = _iota((BLOCK, LANES), 1) < HEAD_DIM
    heads = [None] * n_q_heads
    for kv in range(n_q_heads // GROUP):
        grp, kv_low = kv // 2, kv % 2 == 0
        lanes = slice(grp * LANES, (grp + 1) * LANES)
        keep = low_half if kv_low else ~low_half
        stacked = []
        for g in range(GROUP):
            h = GROUP * kv + g
            q2 = qn[:, (h // 2) * LANES:(h // 2 + 1) * LANES]
            src = q2 if (h % 2 == 0) == kv_low else _swap_halves(q2)
            stacked.append(jnp.where(keep, src, 0.0))
        q4 = jnp.concatenate(stacked, axis=0)
        bias4 = jnp.concatenate([bias[GROUP * kv + g] for g in range(GROUP)], axis=0)
        sink4 = jnp.concatenate([jnp.broadcast_to(sinks[GROUP * kv + g], (BLOCK, 1)) for g in range(GROUP)], axis=0)
        s = jnp.where(own, _dot_nt(q4, kn_cur[:, lanes]), _dot_nt(q4, kn_prev[:, lanes])) + bias4
        p = _softmax_with_sink(s, sink4)
        o4 = _dot_nn(jnp.where(own, p, 0.0), v_cur[:, lanes]) + _dot_nn(jnp.where(own, 0.0, p), v_prev[:, lanes])
        for g in range(GROUP):
            h = GROUP * kv + g
            o = o4[g * BLOCK:(g + 1) * BLOCK]
            heads[h] = o if (h % 2 == 0) == kv_low else _swap_halves(o)
    outs = [jnp.where(low_half, heads[2 * p], heads[2 * p + 1]) for p in range(n_q_heads // 2)]
    return jnp.concatenate(outs, axis=1)


def _attn_block_bwd(q_raw, k_prev, k_cur, v_prev, v_cur, qg_row, kg_row, sinks, bias, d_out):
    scale = 1.0 / math.sqrt(HEAD_DIM)
    n_q_heads = q_raw.shape[1] // HEAD_DIM
    qn, q_vjp = jax.vjp(lambda q, g: _head_rms(q, g) * scale, q_raw, qg_row)
    kn_prev, kp_vjp = jax.vjp(_head_rms, k_prev, kg_row)
    kn_cur, kc_vjp = jax.vjp(_head_rms, k_cur, kg_row)
    own = _iota((GROUP * BLOCK, BLOCK), 1) <= (_iota((GROUP * BLOCK, BLOCK), 0) & (BLOCK - 1))
    low_half = _iota((BLOCK, LANES), 1) < HEAD_DIM
    nn, nt, tn = _DIMS["nn"], _DIMS["nt"], _DIMS["tn"]
    n_groups = k_cur.shape[1] // LANES
    dq_heads, d_sinks, d_bias = [None] * n_q_heads, [None] * n_q_heads, [None] * n_q_heads
    dk_prev, dk_cur, dv_prev, dv_cur = ([None] * n_groups for _ in range(4))

    def add(parts, grp, val):
        parts[grp] = val if parts[grp] is None else parts[grp] + val

    for kv in range(n_q_heads // GROUP):
        grp, kv_low = kv // 2, kv % 2 == 0
        lanes = slice(grp * LANES, (grp + 1) * LANES)
        keep = low_half if kv_low else ~low_half
        q_rows, do_rows = [], []
        for g in range(GROUP):
            h = GROUP * kv + g
            pair = slice((h // 2) * LANES, (h // 2 + 1) * LANES)
            in_place = (h % 2 == 0) == kv_low
            q2, do2 = qn[:, pair], d_out[:, pair]
            q_rows.append(jnp.where(keep, q2 if in_place else pltpu.roll(q2, HEAD_DIM, 1), 0.0))
            do_rows.append(jnp.where(keep, do2 if in_place else pltpu.roll(do2, HEAD_DIM, 1), 0.0))
        q4, do4 = jnp.concatenate(q_rows, axis=0), jnp.concatenate(do_rows, axis=0)
        bias4 = jnp.concatenate([bias[GROUP * kv + g] for g in range(GROUP)], axis=0)
        sink4 = jnp.concatenate([jnp.broadcast_to(sinks[GROUP * kv + g], (BLOCK, 1)) for g in range(GROUP)], axis=0)
        kc2, kp2, vc2, vp2 = kn_cur[:, lanes], kn_prev[:, lanes], v_cur[:, lanes], v_prev[:, lanes]
        s = jnp.where(own, _bf16_dot(q4, kc2, nt), _bf16_dot(q4, kp2, nt)) + bias4
        p, (_, p_sink) = _softmax_sink_fwd(s, sink4)
        p_own, p_prev = jnp.where(own, p, 0.0), jnp.where(own, 0.0, p)
        add(dv_cur, grp, _bf16_dot(p_own, do4, tn))
        add(dv_prev, grp, _bf16_dot(p_prev, do4, tn))
        dp = jnp.where(own, _bf16_dot(do4, vc2, nt), _bf16_dot(do4, vp2, nt))
        delta = jnp.sum(p * dp, axis=1, keepdims=True)
        ds = p * (dp - delta)
        d_sink4 = -p_sink * delta
        ds_own, ds_prev = jnp.where(own, ds, 0.0), jnp.where(own, 0.0, ds)
        dq4 = _bf16_dot(ds_own, kc2, nn) + _bf16_dot(ds_prev, kp2, nn)
        add(dk_cur, grp, _bf16_dot(ds_own, q4, tn))
        add(dk_prev, grp, _bf16_dot(ds_prev, q4, tn))
        for g in range(GROUP):
            h = GROUP * kv + g
            rows = slice(g * BLOCK, (g + 1) * BLOCK)
            dq2 = jnp.where(keep, dq4[rows], 0.0)
            dq_heads[h] = dq2 if (h % 2 == 0) == kv_low else pltpu.roll(dq2, HEAD_DIM, 1)
            d_bias[h] = ds[rows]
            d_sinks[h] = jnp.sum(d_sink4[rows], axis=0, keepdims=True)
    dqn = jnp.concatenate([dq_heads[2 * p] + dq_heads[2 * p + 1] for p in range(n_q_heads // 2)], axis=1)
    dq, dqg = q_vjp(dqn)
    dkp, dkg_prev = kp_vjp(jnp.concatenate(dk_prev, axis=1))
    dkc, dkg_cur = kc_vjp(jnp.concatenate(dk_cur, axis=1))
    return (dq, dkp, dkc, jnp.concatenate(dv_prev, axis=1), jnp.concatenate(dv_cur, axis=1),
            dqg, dkg_prev + dkg_cur, d_sinks, d_bias)


def _bias_table():
    i, j = np.arange(BLOCK)[:, None], np.arange(BLOCK)[None, :]
    n = np.where(j <= i, i - j, i + BLOCK - j)
    max_exact = NUM_BUCKETS // 2
    nf = np.maximum(n, 1).astype(np.float64)
    large = max_exact + (np.log(nf / max_exact) / math.log(MAX_DISTANCE / max_exact) * (NUM_BUCKETS - max_exact)).astype(np.int32)
    large = np.minimum(large, NUM_BUCKETS - 1)
    return np.where(n < max_exact, n, large).astype(np.int32)


def _bias_fwd(name, rel_bias, buckets):
    nb_, nh = rel_bias.shape

    def body(rb_ref, bk_ref, o_ref):
        bk = bk_ref[...]
        own = _iota(bk.shape, 1) <= _iota(bk.shape, 0)
        for h in range(nh):
            acc = jnp.zeros(bk.shape, F32)
            for b in range(nb_):
                acc = jnp.where(bk == b, rb_ref[b, h], acc)
            o_ref[1, h] = acc
            o_ref[0, h] = jnp.where(own, acc, NEG_INF)

    return pl.pallas_call(
        body, name=name,
        in_specs=[pl.BlockSpec(memory_space=pltpu.SMEM), pl.BlockSpec(memory_space=pltpu.VMEM)],
        out_specs=pl.BlockSpec(memory_space=pltpu.VMEM),
        out_shape=jax.ShapeDtypeStruct((2, nh) + buckets.shape, F32),
    )(rel_bias, buckets)


def _bias_bwd(name, dbias, buckets, nb_):
    nh = dbias.shape[0]

    def body(db_ref, bk_ref, o_ref):
        bk = bk_ref[...]
        for h in range(nh):
            d = db_ref[h]
            for b in range(nb_):
                s = jnp.sum(jnp.where(bk == b, d, 0.0), axis=0, keepdims=True)
                s = jnp.sum(s, axis=1, keepdims=True)
                o_ref[b * nh + h:b * nh + h + 1, :] = jnp.broadcast_to(s, (1, LANES))

    return pl.pallas_call(
        body, name=name,
        in_specs=[pl.BlockSpec(memory_space=pltpu.VMEM), pl.BlockSpec(memory_space=pltpu.VMEM)],
        out_specs=pl.BlockSpec(memory_space=pltpu.VMEM),
        out_shape=jax.ShapeDtypeStruct((nb_ * nh, LANES), F32),
    )(dbias, buckets)


SGU_CHUNKS_PER_STEP = 4


def _sgu_step(T):
    n = SGU_CHUNKS_PER_STEP if T % (SGU_CHUNKS_PER_STEP * CHUNK) == 0 else 1
    return n, [slice(c * CHUNK, (c + 1) * CHUNK) for c in range(n)]


def _sgu_fwd(name, z, g_row, w, b_full, W):
    T = z.shape[0]
    n, chunks = _sgu_step(T)

    def body(u_ref, v_ref, g_ref, w_ref, b_ref, o_ref):
        for rows in chunks:
            o_ref[rows, :] = _sgu_block(u_ref[rows, :], v_ref[rows, :], g_ref[...], w_ref[...], b_ref[...])

    return pl.pallas_call(
        body, name=name, grid=(T // (n * CHUNK),),
        in_specs=[_row_spec(n * CHUNK, W, 0), _row_spec(n * CHUNK, W, 1), _full_spec((1, W)),
                  _full_spec(w.shape), _full_spec((CHUNK, W))],
        out_specs=_row_spec(n * CHUNK, W),
        out_shape=jax.ShapeDtypeStruct((T, W), F32), compiler_params=_params(1),
    )(z, z, g_row, w, b_full)


def _sgu_bwd(name, z, g_row, w, b_full, d_out, others, W):
    T = z.shape[0]
    n, chunks = _sgu_step(T)
    widths = [o.shape[1] for o in others]

    def body(u_ref, v_ref, g_ref, w_ref, b_ref, do_ref, *rest):
        other_refs, (dz_ref, dg_ref, dw_ref, db_ref) = rest[:len(others)], rest[len(others):]
        dg = dw = db = None
        for rows in chunks:
            _, vjp = jax.vjp(_sgu_block, u_ref[rows, :], v_ref[rows, :], g_ref[...], w_ref[...], b_ref[...])
            du, dv, dg_c, dw_c, db_c = vjp(do_ref[rows, :])
            dz_ref[rows, :W] = du.astype(BF16)
            dz_ref[rows, W:2 * W] = dv.astype(BF16)
            dg, dw, db = (dg_c, dw_c, db_c) if dg is None else (dg + dg_c, dw + dw_c, db + db_c)
        at = 2 * W
        for o_ref, width in zip(other_refs, widths):
            dz_ref[:, at:at + width] = o_ref[...]
            at += width
        i = pl.program_id(0)

        @pl.when(i == 0)
        def _():
            dg_ref[...] = jnp.zeros_like(dg_ref)
            dw_ref[...] = jnp.zeros_like(dw_ref)
            db_ref[...] = jnp.zeros_like(db_ref)

        dg_ref[...] += dg
        dw_ref[...] += dw
        db_ref[...] += db

        @pl.when(i == pl.num_programs(0) - 1)
        def _():
            db_ref[...] = _head_sum_impl(db_ref[...])

    return pl.pallas_call(
        body, name=name, grid=(T // (n * CHUNK),),
        in_specs=[_row_spec(n * CHUNK, W, 0), _row_spec(n * CHUNK, W, 1), _full_spec((1, W)),
                  _full_spec(w.shape), _full_spec((CHUNK, W)), _row_spec(n * CHUNK, W)]
        + [_row_spec(n * CHUNK, width) for width in widths],
        out_specs=[_row_spec(n * CHUNK, 2 * W + sum(widths)), _full_spec((1, W)), _full_spec(w.shape),
                   _full_spec((CHUNK, W))],
        out_shape=[jax.ShapeDtypeStruct((T, 2 * W + sum(widths)), BF16), jax.ShapeDtypeStruct((1, W), F32),
                   jax.ShapeDtypeStruct(w.shape, F32), jax.ShapeDtypeStruct((CHUNK, W), F32)],
        compiler_params=_params(1),
    )(z, z, g_row, w, b_full, d_out, *others)


def _attn_specs(WQ, WKV, q_col, k_col, v_col, blk_of):
    prev_of = lambda i: jnp.maximum(blk_of(i) - 1, 0)
    return [pl.BlockSpec((BLOCK, WQ), lambda i: (blk_of(i), q_col)),
            pl.BlockSpec((BLOCK, WKV), lambda i: (prev_of(i), k_col)),
            pl.BlockSpec((BLOCK, WKV), lambda i: (blk_of(i), k_col)),
            pl.BlockSpec((BLOCK, WKV), lambda i: (prev_of(i), v_col)),
            pl.BlockSpec((BLOCK, WKV), lambda i: (blk_of(i), v_col))]


def _bias_spec(bias, blk_of):
    return pl.BlockSpec((1,) + bias.shape[1:], lambda i: (jnp.minimum(blk_of(i), 1), 0, 0, 0))


def _attn_fwd(name, z, qg_row, kg_row, sinks_col, bias, WQ, WKV, q_col, k_col, v_col):
    T = z.shape[0]
    nh = sinks_col.shape[0]

    def body(q_ref, kp_ref, kc_ref, vp_ref, vc_ref, qg_ref, kg_ref, s_ref, b_ref, o_ref):
        sinks = [s_ref[h:h + 1, :] for h in range(nh)]
        o_ref[...] = _attn_block(q_ref[...], kp_ref[...], kc_ref[...], vp_ref[...], vc_ref[...],
                                 qg_ref[...], kg_ref[...], sinks, b_ref[0])

    return pl.pallas_call(
        body, name=name, grid=(T // BLOCK,),
        in_specs=_attn_specs(WQ, WKV, q_col, k_col, v_col, lambda i: i)
        + [_full_spec((1, WQ)), _full_spec((1, WKV)), _full_spec((nh, 1)), _bias_spec(bias, lambda i: i)],
        out_specs=_row_spec(BLOCK, WQ),
        out_shape=jax.ShapeDtypeStruct((T, WQ), F32), compiler_params=_params(1),
    )(z, z, z, z, z, qg_row, kg_row, sinks_col, bias)


def _attn_bwd(name, z, qg_row, kg_row, sinks_col, bias, d_out, WQ, WKV, q_col, k_col, v_col):
    T = z.shape[0]
    nblk = T // BLOCK
    nh = sinks_col.shape[0]
    blk_of = lambda i: nblk - 1 - i

    def body(q_ref, kp_ref, kc_ref, vp_ref, vc_ref, qg_ref, kg_ref, s_ref, b_ref, do_ref,
             dq_ref, dk_ref, dv_ref, dqg_ref, dkg_ref, ds_ref, db_ref, carry_k, carry_v):
        i = pl.program_id(0)
        sinks = [s_ref[h:h + 1, :] for h in range(nh)]
        dq, dkp, dkc, dvp, dvc, dqg, dkg, dsk, dbs = _attn_block_bwd(
            q_ref[...], kp_ref[...], kc_ref[...], vp_ref[...], vc_ref[...], qg_ref[...], kg_ref[...], sinks, b_ref[0],
            do_ref[...])

        @pl.when(i == 0)
        def _():
            carry_k[...] = jnp.zeros_like(carry_k)
            carry_v[...] = jnp.zeros_like(carry_v)
            dqg_ref[...] = jnp.zeros_like(dqg_ref)
            dkg_ref[...] = jnp.zeros_like(dkg_ref)
            ds_ref[...] = jnp.zeros_like(ds_ref)
            db_ref[...] = jnp.zeros_like(db_ref)

        dq_ref[...] = dq.astype(BF16)
        dk_ref[...] = (dkc + carry_k[...]).astype(BF16)
        dv_ref[...] = (dvc + carry_v[...]).astype(BF16)
        carry_k[...] = dkp
        carry_v[...] = dvp
        dqg_ref[...] += dqg
        dkg_ref[...] += dkg
        for h in range(nh):
            ds_ref[h:h + 1, :] += dsk[h]
            db_ref[h] += dbs[h]

    return pl.pallas_call(
        body, name=name, grid=(nblk,),
        in_specs=_attn_specs(WQ, WKV, q_col, k_col, v_col, blk_of)
        + [_full_spec((1, WQ)), _full_spec((1, WKV)), _full_spec((nh, 1)), _bias_spec(bias, blk_of),
           pl.BlockSpec((BLOCK, WQ), lambda i: (blk_of(i), 0))],
        out_specs=[pl.BlockSpec((BLOCK, WQ), lambda i: (blk_of(i), 0)),
                   pl.BlockSpec((BLOCK, WKV), lambda i: (blk_of(i), 0)),
                   pl.BlockSpec((BLOCK, WKV), lambda i: (blk_of(i), 0)),
                   _full_spec((1, WQ)), _full_spec((1, WKV)), _full_spec((nh, 1)), _full_spec(bias.shape[1:])],
        out_shape=[jax.ShapeDtypeStruct((T, WQ), BF16), jax.ShapeDtypeStruct((T, WKV), BF16),
                   jax.ShapeDtypeStruct((T, WKV), BF16), jax.ShapeDtypeStruct((1, WQ), F32),
                   jax.ShapeDtypeStruct((1, WKV), F32), jax.ShapeDtypeStruct((nh, 1), F32),
                   jax.ShapeDtypeStruct(bias.shape[1:], F32)],
        scratch_shapes=[pltpu.VMEM((BLOCK, WKV), F32), pltpu.VMEM((BLOCK, WKV), F32)],
        compiler_params=_params(1),
    )(z, z, z, z, z, qg_row, kg_row, sinks_col, bias, d_out)


def _adamw(name, w, g, m, v):
    shape = w.shape
    C = shape[-1]
    if w.ndim == 3 and shape[1] % ROWS == 0:
        work = shape
        grid = (shape[0], shape[1] // ROWS)
        spec = pl.BlockSpec((1, ROWS, C), lambda l, i: (l, i, 0))
    else:
        R = int(np.prod(shape[:-1]))
        tr = ROWS if R % ROWS == 0 else R
        work = (R, C)
        grid = (R // tr,)
        spec = pl.BlockSpec((tr, C), lambda i: (i, 0))
    w2, g2, m2, v2 = (t.reshape(work) for t in (w, g, m, v))

    def body(w_ref, g_ref, m_ref, v_ref, d_ref, nm_ref, nv_ref):
        gv = g_ref[...]
        nm = ADAM_B1 * m_ref[...] + (1.0 - ADAM_B1) * gv
        nv = ADAM_B2 * v_ref[...] + (1.0 - ADAM_B2) * (gv * gv)
        m_hat = nm / (1.0 - ADAM_B1 ** ADAM_STEP)
        v_hat = nv / (1.0 - ADAM_B2 ** ADAM_STEP)
        d_ref[...] = -ADAM_LR * (m_hat / (jnp.sqrt(v_hat) + ADAM_EPS) + ADAM_WD * w_ref[...])
        nm_ref[...] = nm
        nv_ref[...] = nv

    outs = pl.pallas_call(
        body, name=name, grid=grid, in_specs=[spec] * 4, out_specs=[spec] * 3,
        out_shape=[jax.ShapeDtypeStruct(work, F32)] * 3, compiler_params=_params(len(grid)),
    )(w2, g2, m2, v2)
    return tuple(o.reshape(shape) for o in outs)


def _pad_rows(flat):
    n = flat.shape[0]
    tile = 8 * LANES
    padded = -(-n // tile) * tile
    return jnp.pad(flat, (0, padded - n)).reshape(padded // LANES, LANES)


def kernel(x, rel_bias, norm1_g, w_in, sgu_norm_g, sgu_w, sgu_b, q_norm_g, k_norm_g, sinks, out_norm_a, out_norm_b, w_out, norm2_g, w_gate, w_up, w_down, loss_target, m_rel_bias, m_norm1_g, m_w_in, m_sgu_norm_g, m_sgu_w, m_sgu_b, m_q_norm_g, m_k_norm_g, m_sinks, m_out_norm_a, m_out_norm_b, m_w_out, m_norm2_g, m_w_gate, m_w_up, m_w_down, v_rel_bias, v_norm1_g, v_w_in, v_sgu_norm_g, v_sgu_w, v_sgu_b, v_q_norm_g, v_k_norm_g, v_sinks, v_out_norm_a, v_out_norm_b, v_w_out, v_norm2_g, v_w_gate, v_w_up, v_w_down):
    L = w_in.shape[0]
    T, D = x.shape[1], x.shape[2]
    W = D // 2
    NH = W // HEAD_DIM
    WKV = N_KV_HEADS * HEAD_DIM
    IN = N_DEV * w_in.shape[2]
    FF = N_DEV * w_gate.shape[2]
    assert IN == 2 * W + W + 2 * WKV and NH // N_KV_HEADS == GROUP
    q_col, k_col, v_col = 2 * W // W, (3 * W) // WKV, (3 * W + WKV) // WKV
    x0 = x.reshape(T, D)
    target = loss_target.reshape(T, D)

    shards = [jnp.swapaxes(w_in, 1, 2).astype(BF16), w_out.astype(BF16), jnp.swapaxes(w_gate, 1, 2).astype(BF16),
              jnp.swapaxes(w_up, 1, 2).astype(BF16), w_down.astype(BF16)]
    me = 4 * lax.axis_index("x") + 2 * lax.axis_index("y") + lax.axis_index("c")

    def landing(block):
        return lax.dynamic_update_slice(lax.empty((N_DEV,) + block.shape, block.dtype), block[None], (me, 0, 0))

    def as_matrices(got):
        return [g.reshape(N_DEV * g.shape[1], g.shape[2]) for g in got]

    groups = ((0,), (1,), (2, 3, 4))
    full = {l: [None] * len(shards) for l in range(L)}
    full[0][0] = as_matrices(_all_gather("gather_weights_0", [shards[0][0]]))[0]
    flight = {}

    def start_gather(l, first_group, after):
        for g in range(first_group, len(groups)):
            srcs = [shards[i][l] for i in groups[g]]
            flight[l, g] = _exchange_start(f"gather_start_{l}_{g}", "gather", srcs, [landing(s) for s in srcs], after)
            after = flight[l, g][4]
        return after

    def finish_gather(l, g, after):
        s_sem, r_sem, srcs, lands, _ = flight.pop((l, g))
        lands = _exchange_wait(f"gather_wait_{l}_{g}", "gather", s_sem, r_sem, srcs, lands, after)[1]
        for i, mat in zip(groups[g], as_matrices(lands)):
            full[l][i] = mat

    buckets = jnp.asarray(_bias_table())
    bias = _bias_fwd("bias_table", rel_bias, buckets)

    saved = []
    xl = x0
    token = start_gather(0, 1, full[0][0])
    for l in range(L):
        if l > 0:
            finish_gather(l, 0, xl)
        w_in_t = full[l][0]
        g1, g2 = norm1_g[l][None], norm2_g[l][None]
        sg_row = sgu_norm_g[l].reshape(1, W)
        b_full = jnp.repeat(sgu_b[l].T, HEAD_DIM, axis=1)
        qg_row = jnp.tile(q_norm_g[l], NH)[None]
        kg_row = jnp.tile(k_norm_g[l], N_KV_HEADS)[None]
        sinks_col = sinks[l][:, None]
        ga, gb = out_norm_a[l][None], out_norm_b[l][None]

        h = _rms_fwd(f"norm1_{l}", xl, g1)
        (z,) = _mm(f"proj_in_{l}", "nt", [h], [w_in_t], [(0, 0, 0)], 1, T, IN, D, 1024, 896, D, _ep_plain, [F32],
                   after=token)
        out_a = _sgu_fwd(f"sgu_{l}", z, sg_row, sgu_w[l], b_full, W)
        out_b = _attn_fwd(f"attn_{l}", z, qg_row, kg_row, sinks_col, bias, W, WKV, q_col, k_col, v_col)
        mixed = _outnorm_fwd(f"outnorm_{l}", out_a, out_b, ga, gb)
        finish_gather(l, 1, mixed)
        w_o = full[l][1]
        (x1,) = _mm(f"proj_out_{l}", "nn", [mixed], [w_o], [(0, 0, 0)], 1, T, D, D, 1024, 1024, D,
                    _ep_residual, [F32], extras=[xl])
        h2 = _rms_fwd(f"norm2_{l}", x1, g2)
        token = start_gather(l + 1, 0, h2) if l + 1 < L else h2
        finish_gather(l, 2, h2)
        w_g_t, w_u_t, w_d = full[l][2:]
        gate, up, act = _mm(f"mlp_in_{l}", "nt", [h2], [w_g_t, w_u_t], [(0, 0, 0), (0, 1, 1)], 2, T, FF, D,
                            1024, 512, D, _ep_swiglu, [BF16, BF16, BF16], after=token)
        (x2,) = _mm(f"mlp_out_{l}", "nn", [act], [w_d], [(0, 0, 0)], 1, T, D, FF, 512, 1024, FF,
                    _ep_residual, [F32], extras=[x1])
        saved.append((xl, h, z, out_a, out_b, mixed, x1, h2, gate, up, act,
                      g1, g2, sg_row, b_full, qg_row, kg_row, sinks_col, ga, gb))
        xl = x2

    loss_part, dy, dy16 = _loss_head("loss_head", xl, target)

    dbias = None
    small = [None] * L
    scatters = []

    def start_scatter(name, which, grads_t, after):
        srcs = [t.reshape(N_DEV, t.shape[0] // N_DEV, D) for t in grads_t]
        lands = [lax.empty((N_DEV - 1,) + s.shape[1:], BF16) for s in srcs]
        s_sem, r_sem, srcs, lands, tok = _exchange_start(name, "scatter", srcs, lands, after)
        scatters.append((name, which, s_sem, r_sem, srcs, lands))
        return tok

    token = None
    for l in reversed(range(L)):
        w_in_t, w_o, w_g_t, w_u_t, w_d = full[l]
        (xl, h, z, out_a, out_b, mixed, x1, h2, gate, up, act,
         g1, g2, sg_row, b_full, qg_row, kg_row, sinks_col, ga, gb) = saved[l]

        dgate, dup = _mm(f"d_mlp_out_{l}", "nt", [dy16], [w_d], [(0, 0, 0)], 1, T, FF, D, 1024, 1408, D,
                         _ep_swiglu_bwd, [BF16, BF16], extras=[gate, up], after=token)
        (dw_d,) = _mm(f"dw_down_{l}", "tn", [act], [dy16], [(0, 0, 0)], 1, FF, D, T, 1408, 1024, 2048, _ep_plain, [BF16])
        (dh2,) = _mm(f"d_mlp_in_{l}", "nn", [dgate, dup], [w_g_t, w_u_t], [(0, 0, 0), (1, 1, 0)], 1, T, D, FF,
                     512, 512, FF, _ep_plain, [F32])
        dx1, dx1_16, dg2 = _rms_bwd(f"d_norm2_{l}", dh2, x1, g2, dy)
        dw_g, dw_u = _mm(f"dw_gate_up_{l}", "tn", [dgate, dup], [h2], [(0, 0, 0), (1, 0, 1)], 2, FF, D, T,
                         1408, 1024, 1024, _ep_two, [BF16, BF16])
        token = start_scatter(f"scatter_mlp_start_{l}", (l, (4, 2, 3)), [dw_d, dw_g, dw_u], dx1_16)
        (dmixed,) = _mm(f"d_proj_out_{l}", "nt", [dx1_16], [w_o], [(0, 0, 0)], 1, T, D, D, 1024, 1024, D, _ep_plain, [F32],
                        after=token)
        (dw_o,) = _mm(f"dw_out_{l}", "tn", [mixed], [dx1_16], [(0, 0, 0)], 1, D, D, T, 1024, 1024, 2048, _ep_plain, [BF16])
        d_a, d_b, dga, dgb = _outnorm_bwd(f"d_outnorm_{l}", dmixed, out_a, out_b, ga, gb)
        dq, dk, dv, dqg, dkg, dsk, dbs = _attn_bwd(f"d_attn_{l}", z, qg_row, kg_row, sinks_col, bias, d_b,
                                                   W, WKV, q_col, k_col, v_col)
        dbias = dbs if dbias is None else dbias + dbs
        dz, dsg, dsw, dsb = _sgu_bwd(f"d_sgu_{l}", z, sg_row, sgu_w[l], b_full, d_a, [dq, dk, dv], W)
        (dh,) = _mm(f"d_proj_in_{l}", "nn", [dz], [w_in_t], [(0, 0, 0)], 1, T, D, IN, 1024, 1024, IN, _ep_plain, [F32])
        dy, dy16, dg1 = _rms_bwd(f"d_norm1_{l}", dh, xl, g1, dx1)
        (dw_i,) = _mm(f"dw_in_{l}", "tn", [dz], [h], [(0, 0, 0)], 1, IN, D, T, 896, 1024, 2048, _ep_plain, [BF16])
        token = start_scatter(f"scatter_mix_start_{l}", (l, (1, 0)), [dw_o, dw_i], dy16)

        small[l] = dict(norm1_g=dg1[0], sgu_norm_g=dsg.reshape(NH, HEAD_DIM), sgu_w=dsw,
                        sgu_b=dsb[:, ::HEAD_DIM].T, q_norm_g=dqg.reshape(NH, HEAD_DIM).sum(0),
                        k_norm_g=dkg.reshape(N_KV_HEADS, HEAD_DIM).sum(0), sinks=dsk[:, 0],
                        out_norm_a=dga[0], out_norm_b=dgb[0], norm2_g=dg2[0])

    grad_x = dy.reshape(x.shape)
    d_rel = _bias_bwd("d_bias_table", dbias, buckets, NUM_BUCKETS)[:, 0].reshape(NUM_BUCKETS, NH)

    names = ["norm1_g", "sgu_norm_g", "sgu_w", "sgu_b", "q_norm_g", "k_norm_g", "sinks", "out_norm_a", "out_norm_b", "norm2_g"]
    parts = {"rel_bias": d_rel}
    for nme in names:
        parts[nme] = jnp.stack([small[l][nme] for l in range(L)])
    order = ["rel_bias"] + names
    packed = jnp.concatenate([_pad_rows(parts[nme].reshape(-1)) for nme in order], axis=0)
    small_flight = _exchange_start("gather_small_start", "gather", [packed], [landing(packed)], token)
    after = small_flight[4]

    me1 = me.reshape(1).astype(jnp.int32)
    grads_big = {}

    def finish_scatter(entry, after):
        name, (l, which), s_sem, r_sem, srcs, lands = entry
        srcs, lands = _exchange_wait(name.replace("start", "wait"), "scatter", s_sem, r_sem, srcs, lands, after)
        for i, src, land in zip(which, srcs, lands):
            rows = src.shape[1]
            after = _sum_parts(f"sum_grads_{l}_{i}", me1, src, land, 64 if rows % 64 == 0 else rows)
            grads_big[i, l] = after
        return after

    def stacked(i):
        return jnp.stack([grads_big[i, l] for l in range(L)])

    loss = lax.psum(loss_part[0, 0], ("x", "y", "c"))
    tr = lambda t: jnp.swapaxes(t, 1, 2)
    weights = dict(rel_bias=rel_bias, norm1_g=norm1_g, w_in=w_in, sgu_norm_g=sgu_norm_g, sgu_w=sgu_w, sgu_b=sgu_b,
                   q_norm_g=q_norm_g, k_norm_g=k_norm_g, sinks=sinks, out_norm_a=out_norm_a, out_norm_b=out_norm_b,
                   w_out=w_out, norm2_g=norm2_g, w_gate=w_gate, w_up=w_up, w_down=w_down)
    ms = dict(rel_bias=m_rel_bias, norm1_g=m_norm1_g, w_in=m_w_in, sgu_norm_g=m_sgu_norm_g, sgu_w=m_sgu_w, sgu_b=m_sgu_b,
              q_norm_g=m_q_norm_g, k_norm_g=m_k_norm_g, sinks=m_sinks, out_norm_a=m_out_norm_a, out_norm_b=m_out_norm_b,
              w_out=m_w_out, norm2_g=m_norm2_g, w_gate=m_w_gate, w_up=m_w_up, w_down=m_w_down)
    vs = dict(rel_bias=v_rel_bias, norm1_g=v_norm1_g, w_in=v_w_in, sgu_norm_g=v_sgu_norm_g, sgu_w=v_sgu_w, sgu_b=v_sgu_b,
              q_norm_g=v_q_norm_g, k_norm_g=v_k_norm_g, sinks=v_sinks, out_norm_a=v_out_norm_a, out_norm_b=v_out_norm_b,
              w_out=v_w_out, norm2_g=v_norm2_g, w_gate=v_w_gate, w_up=v_w_up, w_down=v_w_down)
    all_names = ["rel_bias", "norm1_g", "w_in", "sgu_norm_g", "sgu_w", "sgu_b", "q_norm_g", "k_norm_g", "sinks",
                 "out_norm_a", "out_norm_b", "w_out", "norm2_g", "w_gate", "w_up", "w_down"]
    transposed = ("w_in", "w_gate", "w_up")
    grads, deltas, new_m, new_v = {}, {}, {}, {}

    def update(nme, g):
        if nme in transposed:
            outs = _adamw(f"adamw_{nme}", tr(weights[nme]), g, tr(ms[nme]), tr(vs[nme]))
            grads[nme], (deltas[nme], new_m[nme], new_v[nme]) = tr(g), [tr(o) for o in outs]
        else:
            grads[nme] = g
            deltas[nme], new_m[nme], new_v[nme] = _adamw(f"adamw_{nme}", weights[nme], g, ms[nme], vs[nme])
        return new_v[nme]

    for entry in scatters[:-1]:
        after = finish_scatter(entry, after)
    for nme, i in (("w_gate", 2), ("w_up", 3), ("w_down", 4)):
        after = update(nme, stacked(i))

    s_sem, r_sem, srcs, lands, _ = small_flight
    _, (everyone,) = _exchange_wait("gather_small_wait", "gather", s_sem, r_sem, srcs, lands, after)
    rows = packed.shape[0]
    summed = _sum_slots("sum_small_grads", everyone, 64 if rows % 64 == 0 else 8)
    at = 0
    for nme in order:
        n = int(np.prod(parts[nme].shape))
        n_rows = -(-n // (8 * LANES)) * 8
        after = update(nme, summed[at:at + n_rows].reshape(-1)[:n].reshape(parts[nme].shape))
        at += n_rows

    finish_scatter(scatters[-1], after)
    update("w_out", stacked(1))
    update("w_in", stacked(0))
    return (loss, grad_x, *[grads[nme] for nme in all_names], *[deltas[nme] for nme in all_names],
            *[new_m[nme] for nme in all_names], *[new_v[nme] for nme in all_names])
```

```python
import functools
import math

import numpy as np
import jax
import jax.numpy as jnp
from jax import lax
from jax.experimental import pallas as pl
from jax.experimental.pallas import tpu as pltpu

F32 = jnp.float32
BF16 = jnp.bfloat16

N_DEV = 8
HEAD_DIM = 64
CHUNK = 128
BLOCK = 128
N_KV_HEADS = 4
GROUP = 4
NUM_BUCKETS = 32
MAX_DISTANCE = 128
EPS = 1e-6
NEG_INF = -1e30
LANES = 128
VMEM_LIMIT = 56 * 2 ** 20

ADAM_LR = 0.001
ADAM_B1 = 0.9
ADAM_B2 = 0.999
ADAM_EPS = 1e-08
ADAM_WD = 0.01
ADAM_STEP = 10

MESH = pl.DeviceIdType.MESH
ANY = pl.BlockSpec(memory_space=pl.ANY)
HBM = pl.BlockSpec(memory_space=pltpu.HBM)
SEM = pl.BlockSpec(memory_space=pltpu.SEMAPHORE)
EFFECT = pltpu.SideEffectType.DATAFLOW_SIDE_EFFECTING


def _params(n_axes):
    return pltpu.CompilerParams(dimension_semantics=("arbitrary",) * n_axes, vmem_limit_bytes=VMEM_LIMIT)


def _my_place():
    return lax.axis_index("x"), lax.axis_index("y"), lax.axis_index("c")


def _all_gather(name, arrs):
    n = len(arrs)

    def body(*refs):
        ins, outs = refs[:n], refs[n:2 * n]
        send_sems, recv_sems, local_sems = refs[2 * n:]
        x, y, c = _my_place()
        sibling = (x, y, 1 - c)
        chips = [(1 - x, y), (x, 1 - y), (1 - x, 1 - y)]

        def slot(a, px, py, pc):
            return outs[a].at[4 * px + 2 * py + pc]

        def copy(a, k, block, to, src=None):
            return pltpu.make_async_remote_copy(
                src_ref=slot(a, *block) if src is None else src, dst_ref=slot(a, *block),
                send_sem=send_sems.at[7 * a + k], recv_sem=recv_sems.at[7 * a + k],
                device_id=to, device_id_type=MESH)

        started = []
        for a in range(n):
            mine = pltpu.make_async_copy(ins[a], slot(a, x, y, c), local_sems.at[a])
            mine.start()
            started.append(mine)
        sends = []
        for a in range(n):
            first = [copy(a, 0, (x, y, c), sibling, src=ins[a])]
            first += [copy(a, 1 + j, (x, y, c), (*chip, c), src=ins[a]) for j, chip in enumerate(chips)]
            for cp in first:
                cp.start()
            sends += first
        for a in range(n):
            for j, chip in enumerate(chips):
                copy(a, 1 + j, (*chip, c), (x, y, c)).wait_recv()
                fwd = copy(a, 4 + j, (*chip, c), sibling)
                fwd.start()
                sends.append(fwd)
        for a in range(n):
            copy(a, 0, sibling, (x, y, c)).wait_recv()
            for j, chip in enumerate(chips):
                copy(a, 4 + j, (*chip, 1 - c), (x, y, c)).wait_recv()
        for cp in sends:
            cp.wait_send()
        for cp in started:
            cp.wait()

    return pl.pallas_call(
        body, name=name,
        out_shape=[jax.ShapeDtypeStruct((N_DEV,) + a.shape, a.dtype) for a in arrs],
        in_specs=[ANY] * n, out_specs=[ANY] * n,
        scratch_shapes=[pltpu.SemaphoreType.DMA((7 * n,)), pltpu.SemaphoreType.DMA((7 * n,)),
                        pltpu.SemaphoreType.DMA((n,))],
    )(*arrs)


def _peer(k, x, y, c):
    return (1 - x if k & 4 else x), (1 - y if k & 2 else y), (1 - c if k & 1 else c)


PEER_ORDER = (1, 2, 4, 3, 5, 6, 7)


def _exchange_copy(kind, k, a, srcs, lands, send_sems, recv_sems, arriving=False):
    x, y, c = _my_place()
    me = 4 * x + 2 * y + c
    px, py, pc = _peer(k, x, y, c)
    them = 4 * px + 2 * py + pc
    if kind == "gather":
        src, dst_there, dst_here = srcs[a], lands[a].at[me], lands[a].at[them]
    else:
        src, dst_there, dst_here = srcs[a].at[them], lands[a].at[k - 1], lands[a].at[k - 1]
    return pltpu.make_async_remote_copy(
        src_ref=src, dst_ref=dst_here if arriving else dst_there,
        send_sem=send_sems.at[7 * a + k - 1], recv_sem=recv_sems.at[7 * a + k - 1],
        device_id=(px, py, pc), device_id_type=MESH)


def _exchange_start(name, kind, srcs, lands, after):
    n = len(srcs)

    def body(*refs):
        ins, lnd = refs[:n], refs[n:2 * n]
        send_sems, recv_sems = refs[2 * n + 1], refs[2 * n + 2]
        token = refs[-1]
        for k in PEER_ORDER:
            for a in range(n):
                _exchange_copy(kind, k, a, ins, lnd, send_sems, recv_sems).start()
        token[...] = jnp.zeros_like(token)

    hbm = lambda t: pltpu.with_memory_space_constraint(t, pltpu.HBM)
    out = pl.pallas_call(
        body, name=name,
        out_shape=(pltpu.SemaphoreType.DMA((7 * n,)), pltpu.SemaphoreType.DMA((7 * n,)),
                   *[pltpu.HBM(t.shape, t.dtype) for t in srcs], *[pltpu.HBM(t.shape, t.dtype) for t in lands],
                   jax.ShapeDtypeStruct((8, LANES), F32)),
        in_specs=[HBM] * (2 * n) + [ANY],
        out_specs=(SEM, SEM, *[HBM] * (2 * n), pl.BlockSpec(memory_space=pltpu.VMEM)),
        input_output_aliases={i: 2 + i for i in range(2 * n)},
        compiler_params=pltpu.CompilerParams(has_side_effects=EFFECT),
    )(*[hbm(t) for t in srcs], *[hbm(t) for t in lands], after)
    return out[0], out[1], list(out[2:2 + n]), list(out[2 + n:2 + 2 * n]), out[-1]


def _exchange_wait(name, kind, send_sems, recv_sems, srcs, lands, after):
    n = len(srcs)

    def body(*refs):
        ins, lnd = refs[:n], refs[n:2 * n]
        s_sems, r_sems = refs[2 * n], refs[2 * n + 1]
        for a in range(n):
            for k in PEER_ORDER:
                _exchange_copy(kind, k, a, ins, lnd, s_sems, r_sems).wait_send()
                _exchange_copy(kind, k, a, ins, lnd, s_sems, r_sems, arriving=True).wait_recv()

    out = pl.pallas_call(
        body, name=name,
        out_shape=(*[pltpu.HBM(t.shape, t.dtype) for t in srcs], *[pltpu.HBM(t.shape, t.dtype) for t in lands]),
        in_specs=[HBM] * (2 * n) + [SEM, SEM, ANY],
        out_specs=tuple([HBM] * (2 * n)),
        input_output_aliases={i: i for i in range(2 * n)},
        compiler_params=pltpu.CompilerParams(has_side_effects=EFFECT),
    )(*srcs, *lands, send_sems, recv_sems, after)
    return list(out[:n]), list(out[n:])


def _sum_parts(name, me, mine, parts, rows):
    _, R, C = mine.shape

    def body(me_ref, own_ref, p_ref, o_ref):
        acc = own_ref[0].astype(F32)
        for s in range(N_DEV - 1):
            acc = acc + p_ref[s].astype(F32)
        o_ref[...] = acc

    return pl.pallas_call(
        body, name=name,
        grid_spec=pltpu.PrefetchScalarGridSpec(
            num_scalar_prefetch=1, grid=(R // rows,),
            in_specs=[pl.BlockSpec((1, rows, C), lambda i, me_ref: (me_ref[0], i, 0)),
                      pl.BlockSpec((N_DEV - 1, rows, C), lambda i, me_ref: (0, i, 0))],
            out_specs=pl.BlockSpec((rows, C), lambda i, me_ref: (i, 0))),
        out_shape=jax.ShapeDtypeStruct((R, C), F32), compiler_params=_params(1),
    )(me, mine, parts)


def _sum_slots(name, a, rows):
    _, R, C = a.shape

    def body(a_ref, o_ref):
        acc = a_ref[0].astype(F32)
        for s in range(1, N_DEV):
            acc = acc + a_ref[s].astype(F32)
        o_ref[...] = acc

    return pl.pallas_call(
        body, name=name, grid=(R // rows,),
        in_specs=[pl.BlockSpec((N_DEV, rows, C), lambda i: (0, i, 0))],
        out_specs=pl.BlockSpec((rows, C), lambda i: (i, 0)),
        out_shape=jax.ShapeDtypeStruct((R, C), F32), compiler_params=_params(1),
    )(a)


_DIMS = {"nn": (((1,), (0,)), ((), ())), "nt": (((1,), (1,)), ((), ())), "tn": (((0,), (0,)), ((), ()))}


def _mm(name, mode, a_list, b_list, pairs, n_acc, M, N, K, tm, tn, tk, epilogue, out_dtypes, extras=(), after=None):
    tm, tn, tk = min(tm, M), min(tn, N), min(tk, K)
    assert M % tm == 0 and N % tn == 0 and K % tk == 0, (name, M, N, K, tm, tn, tk)
    nk = K // tk
    na, nb, ne, no = len(a_list), len(b_list), len(extras), len(out_dtypes)
    dims = _DIMS[mode]
    a_spec = (pl.BlockSpec((tk, tm), lambda j, i, k: (k, i)) if mode == "tn"
              else pl.BlockSpec((tm, tk), lambda j, i, k: (i, k)))
    b_spec = (pl.BlockSpec((tn, tk), lambda j, i, k: (j, k)) if mode == "nt"
              else pl.BlockSpec((tk, tn), lambda j, i, k: (k, j)))
    o_spec = pl.BlockSpec((tm, tn), lambda j, i, k: (i, j))
    tail = [] if after is None else [after]

    def body(*refs):
        a_refs, b_refs = refs[:na], refs[na:na + nb]
        e_refs = refs[na + nb:na + nb + ne]
        first_out = na + nb + ne + len(tail)
        o_refs = refs[first_out:first_out + no]
        acc_refs = refs[first_out + no:]

        def partial(oi):
            tot = None
            for ai, bi, ti in pairs:
                if ti == oi:
                    d = lax.dot_general(a_refs[ai][...], b_refs[bi][...], dims, preferred_element_type=F32)
                    tot = d if tot is None else tot + d
            return tot

        def finish(accs):
            outs = epilogue(accs, [e[...] for e in e_refs])
            for o_ref, o in zip(o_refs, outs):
                o_ref[...] = o.astype(o_ref.dtype)

        if nk == 1:
            finish([partial(oi) for oi in range(n_acc)])
        else:
            k = pl.program_id(2)

            @pl.when(k == 0)
            def _():
                for r in acc_refs:
                    r[...] = jnp.zeros_like(r)

            for oi in range(n_acc):
                acc_refs[oi][...] += partial(oi)

            @pl.when(k == nk - 1)
            def _():
                finish([r[...] for r in acc_refs])

    return pl.pallas_call(
        body, name=name, grid=(N // tn, M // tm, nk),
        in_specs=[a_spec] * na + [b_spec] * nb + [o_spec] * ne + [ANY] * len(tail),
        out_specs=[o_spec] * no,
        out_shape=[jax.ShapeDtypeStruct((M, N), dt) for dt in out_dtypes],
        scratch_shapes=[pltpu.VMEM((tm, tn), F32)] * (n_acc if nk > 1 else 0),
        compiler_params=_params(3),
    )(*a_list, *b_list, *extras, *tail)


def _ep_residual(accs, ex):
    return (ex[0] + accs[0],)


def _ep_rows_rms_bwd(acc, ex, rows):
    xv, res = ex
    r = lax.rsqrt(jnp.mean(xv * xv, axis=-1, keepdims=True) + EPS)
    xh = xv * r
    dg = jnp.sum(acc * xh, axis=0, keepdims=True)
    dxh = acc * rows[0]
    dx = r * (dxh - xh * jnp.mean(dxh * xh, axis=-1, keepdims=True)) + res
    return (dx, dx), (dg,)


def _sigmoid(x):
    return 0.5 * (jnp.tanh(0.5 * x) + 1.0)


def _ep_plain(accs, ex):
    return (accs[0],)


def _ep_swiglu(accs, ex):
    g, u = accs
    return g, u, (g * _sigmoid(g)) * u


def _ep_swiglu_bwd(accs, ex):
    dact = accs[0]
    g, u = ex[0].astype(F32), ex[1].astype(F32)
    sig = _sigmoid(g)
    silu = g * sig
    return dact * u * (sig * (1.0 + g * (1.0 - sig))), dact * silu


def _ep_two(accs, ex):
    return accs[0], accs[1]


ROWS = 256


def _row_spec(tm, width, col=0):
    return pl.BlockSpec((tm, width), lambda i: (i, col))


def _full_spec(shape):
    nd = len(shape)
    return pl.BlockSpec(shape, lambda i: (0,) * nd)


def _rms_fwd(name, x, g):
    T, D = x.shape
    tm = min(ROWS, T)

    def body(x_ref, g_ref, o_ref):
        xv = x_ref[...]
        r = lax.rsqrt(jnp.mean(xv * xv, axis=-1, keepdims=True) + EPS)
        o_ref[...] = ((xv * r) * g_ref[...]).astype(BF16)

    return pl.pallas_call(
        body, name=name, grid=(T // tm,),
        in_specs=[_row_spec(tm, D), _full_spec((1, D))], out_specs=_row_spec(tm, D),
        out_shape=jax.ShapeDtypeStruct((T, D), BF16), compiler_params=_params(1),
    )(x, g)


def _rms_bwd(name, dyn, x, g, res):
    T, D = x.shape
    tm = min(ROWS, T)

    def body(dyn_ref, x_ref, g_ref, res_ref, dx_ref, dx16_ref, dg_ref):
        outs, (dg,) = _ep_rows_rms_bwd(dyn_ref[...], [x_ref[...], res_ref[...]], [g_ref[...]])

        @pl.when(pl.program_id(0) == 0)
        def _():
            dg_ref[...] = jnp.zeros_like(dg_ref)

        dg_ref[...] += dg
        dx_ref[...] = outs[0]
        dx16_ref[...] = outs[1].astype(BF16)

    return pl.pallas_call(
        body, name=name, grid=(T // tm,),
        in_specs=[_row_spec(tm, D), _row_spec(tm, D), _full_spec((1, D)), _row_spec(tm, D)],
        out_specs=[_row_spec(tm, D), _row_spec(tm, D), _full_spec((1, D))],
        out_shape=[jax.ShapeDtypeStruct((T, D), F32), jax.ShapeDtypeStruct((T, D), BF16),
                   jax.ShapeDtypeStruct((1, D), F32)],
        compiler_params=_params(1),
    )(dyn, x, g, res)


def _outnorm_fwd(name, a, b, ga, gb):
    T, W = a.shape
    tm = min(ROWS, T)

    def body(a_ref, b_ref, ga_ref, gb_ref, o_ref):
        for src, gain, lo in ((a_ref, ga_ref, 0), (b_ref, gb_ref, W)):
            v = src[...]
            r = lax.rsqrt(jnp.mean(v * v, axis=-1, keepdims=True) + EPS)
            o_ref[:, lo:lo + W] = ((v * r) * gain[...]).astype(BF16)

    return pl.pallas_call(
        body, name=name, grid=(T // tm,),
        in_specs=[_row_spec(tm, W), _row_spec(tm, W), _full_spec((1, W)), _full_spec((1, W))],
        out_specs=_row_spec(tm, 2 * W),
        out_shape=jax.ShapeDtypeStruct((T, 2 * W), BF16), compiler_params=_params(1),
    )(a, b, ga, gb)


def _outnorm_bwd(name, dmixed, a, b, ga, gb):
    T, W = a.shape
    tm = min(ROWS, T)

    def body(dm_ref, a_ref, b_ref, ga_ref, gb_ref, da_ref, db_ref, dga_ref, dgb_ref):
        first = pl.program_id(0) == 0
        for src, gain, lo, dsrc, dgain in ((a_ref, ga_ref, 0, da_ref, dga_ref), (b_ref, gb_ref, W, db_ref, dgb_ref)):
            v, dv = src[...], dm_ref[:, lo:lo + W]
            r = lax.rsqrt(jnp.mean(v * v, axis=-1, keepdims=True) + EPS)
            vh = v * r

            @pl.when(first)
            def _():
                dgain[...] = jnp.zeros_like(dgain)

            dgain[...] += jnp.sum(dv * vh, axis=0, keepdims=True)
            dvh = dv * gain[...]
            dsrc[...] = r * (dvh - vh * jnp.mean(dvh * vh, axis=-1, keepdims=True))

    return pl.pallas_call(
        body, name=name, grid=(T // tm,),
        in_specs=[_row_spec(tm, 2 * W), _row_spec(tm, W), _row_spec(tm, W), _full_spec((1, W)), _full_spec((1, W))],
        out_specs=[_row_spec(tm, W), _row_spec(tm, W), _full_spec((1, W)), _full_spec((1, W))],
        out_shape=[jax.ShapeDtypeStruct((T, W), F32), jax.ShapeDtypeStruct((T, W), F32),
                   jax.ShapeDtypeStruct((1, W), F32), jax.ShapeDtypeStruct((1, W), F32)],
        compiler_params=_params(1),
    )(dmixed, a, b, ga, gb)


def _loss_head(name, y, target):
    T, D = y.shape
    tm = min(ROWS, T)

    def body(y_ref, t_ref, loss_ref, dy_ref, dy16_ref):
        d = y_ref[...] - t_ref[...]

        @pl.when(pl.program_id(0) == 0)
        def _():
            loss_ref[...] = jnp.zeros_like(loss_ref)

        per_token = jnp.mean(d * d, axis=-1, keepdims=True)
        loss_ref[...] += 0.5 * jnp.sum(per_token, axis=0, keepdims=True)
        dy = d * (1.0 / D)
        dy_ref[...] = dy
        dy16_ref[...] = dy.astype(BF16)

    return pl.pallas_call(
        body, name=name, grid=(T // tm,),
        in_specs=[_row_spec(tm, D), _row_spec(tm, D)],
        out_specs=[_full_spec((1, 1)), _row_spec(tm, D), _row_spec(tm, D)],
        out_shape=[jax.ShapeDtypeStruct((1, 1), F32), jax.ShapeDtypeStruct((T, D), F32),
                   jax.ShapeDtypeStruct((T, D), BF16)],
        compiler_params=_params(1),
    )(y, target)


def _bf16_dot(a, b, dims):
    return lax.dot_general(a.astype(BF16), b.astype(BF16), dims, preferred_element_type=F32)


@jax.custom_vjp
def _dot_nn(a, b):
    return _bf16_dot(a, b, _DIMS["nn"])


def _dot_nn_fwd(a, b):
    return _dot_nn(a, b), (a, b)


def _dot_nn_bwd(saved, ct):
    a, b = saved
    return _bf16_dot(ct, b, _DIMS["nt"]), _bf16_dot(a, ct, _DIMS["tn"])


_dot_nn.defvjp(_dot_nn_fwd, _dot_nn_bwd)


@jax.custom_vjp
def _dot_nt(a, b):
    return _bf16_dot(a, b, _DIMS["nt"])


def _dot_nt_fwd(a, b):
    return _dot_nt(a, b), (a, b)


def _dot_nt_bwd(saved, ct):
    a, b = saved
    return _bf16_dot(ct, b, _DIMS["nn"]), _bf16_dot(ct, a, _DIMS["tn"])


_dot_nt.defvjp(_dot_nt_fwd, _dot_nt_bwd)


def _iota(shape, dim):
    return lax.broadcasted_iota(jnp.int32, shape, dim)


def _head_sum_impl(x):
    same_head = (_iota((LANES, LANES), 0) // HEAD_DIM == _iota((LANES, LANES), 1) // HEAD_DIM).astype(BF16)
    pieces = []
    for i in range(x.shape[1] // LANES):
        xs = x[:, i * LANES:(i + 1) * LANES]
        hi = xs.astype(BF16)
        r1 = xs - hi.astype(F32)
        mid = r1.astype(BF16)
        lo = (r1 - mid.astype(F32)).astype(BF16)
        tot = None
        for part in (hi, mid, lo):
            d = lax.dot_general(part, same_head, _DIMS["nn"], preferred_element_type=F32)
            tot = d if tot is None else tot + d
        pieces.append(tot)
    return pieces[0] if len(pieces) == 1 else jnp.concatenate(pieces, axis=1)


@jax.custom_vjp
def _head_sum(x):
    return _head_sum_impl(x)


_head_sum.defvjp(lambda x: (_head_sum_impl(x), None), lambda _, ct: (_head_sum_impl(ct),))


def _head_rms(x, g_row):
    ms = _head_sum(x * x) * (1.0 / HEAD_DIM)
    return (x * lax.rsqrt(ms + EPS)) * g_row


@jax.custom_vjp
def _swap_halves(x):
    return pltpu.roll(x, HEAD_DIM, 1)


_swap_halves.defvjp(lambda x: (pltpu.roll(x, HEAD_DIM, 1), None), lambda _, ct: (pltpu.roll(ct, HEAD_DIM, 1),))


def _gelu(x):
    return 0.5 * x * (1.0 + lax.erf(x * (1.0 / math.sqrt(2.0))))


def _sgu_block(u_raw, v_raw, g_row, w, b_full):
    u, v = _gelu(u_raw), _gelu(v_raw)
    vn = _head_rms(v, g_row)
    causal = _iota((CHUNK, CHUNK), 0) >= _iota((CHUNK, CHUNK), 1)
    low_half = _iota((CHUNK, LANES), 1) < HEAD_DIM
    gates = []
    for p in range(v.shape[1] // LANES):
        vp = vn[:, p * LANES:(p + 1) * LANES]
        g0 = _dot_nn(jnp.where(causal, w[2 * p], 0.0), vp)
        g1 = _dot_nn(jnp.where(causal, w[2 * p + 1], 0.0), vp)
        gates.append(jnp.where(low_half, g0, g1))
    gate = jnp.concatenate(gates, axis=1) + b_full
    return u * gate


def _softmax_sink_fwd(s, sink):
    m = jnp.maximum(jnp.max(s, axis=1, keepdims=True), sink)
    e = jnp.exp(s - m)
    e_sink = jnp.exp(sink - m)
    r = 1.0 / (jnp.sum(e, axis=1, keepdims=True) + e_sink)
    p = e * r
    return p, (p, e_sink * r)


@jax.custom_vjp
def _softmax_with_sink(s, sink):
    return _softmax_sink_fwd(s, sink)[0]


def _softmax_sink_bwd(saved, dp):
    p, p_sink = saved
    delta = jnp.sum(p * dp, axis=1, keepdims=True)
    return p * (dp - delta), -p_sink * delta


_softmax_with_sink.defvjp(_softmax_sink_fwd, _softmax_sink_bwd)


def _attn_block(q_raw, k_prev, k_cur, v_prev, v_cur, qg_row, kg_row, sinks, bias):
    scale = 1.0 / math.sqrt(HEAD_DIM)
    n_q_heads = q_raw.shape[1] // HEAD_DIM
    qn = _head_rms(q_raw, qg_row) * scale
    kn_prev, kn_cur = _head_rms(k_prev, kg_row), _head_rms(k_cur, kg_row)
    own = _iota((GROUP * BLOCK, BLOCK), 1) <= (_iota((GROUP * BLOCK, BLOCK), 0) & (BLOCK - 1))
    low_half = _iota((BLOCK, LANES), 1) < HEAD_DIM
    heads = [None] * n_q_heads
    for kv in range(n_q_heads // GROUP):
        grp, kv_low = kv // 2, kv % 2 == 0
        lanes = slice(grp * LANES, (grp + 1) * LANES)
        keep = low_half if kv_low else ~low_half
        stacked = []
        for g in range(GROUP):
            h = GROUP * kv + g
            q2 = qn[:, (h // 2) * LANES:(h // 2 + 1) * LANES]
            src = q2 if (h % 2 == 0) == kv_low else _swap_halves(q2)
            stacked.append(jnp.where(keep, src, 0.0))
        q4 = jnp.concatenate(stacked, axis=0)
        bias4 = jnp.concatenate([bias[GROUP * kv + g] for g in range(GROUP)], axis=0)
        sink4 = jnp.concatenate([jnp.broadcast_to(sinks[GROUP * kv + g], (BLOCK, 1)) for g in range(GROUP)], axis=0)
        s = jnp.where(own, _dot_nt(q4, kn_cur[:, lanes]), _dot_nt(q4, kn_prev[:, lanes])) + bias4
        p = _softmax_with_sink(s, sink4)
        o4 = _dot_nn(jnp.where(own, p, 0.0), v_cur[:, lanes]) + _dot_nn(jnp.where(own, 0.0, p), v_prev[:, lanes])
        for g in range(GROUP):
            h = GROUP * kv + g
            o = o4[g * BLOCK:(g + 1) * BLOCK]
            heads[h] = o if (h % 2 == 0) == kv_low else _swap_halves(o)
    outs = [jnp.where(low_half, heads[2 * p], heads[2 * p + 1]) for p in range(n_q_heads // 2)]
    return jnp.concatenate(outs, axis=1)


def _attn_block_bwd(q_raw, k_prev, k_cur, v_prev, v_cur, qg_row, kg_row, sinks, bias, d_out):
    scale = 1.0 / math.sqrt(HEAD_DIM)
    n_q_heads = q_raw.shape[1] // HEAD_DIM
    qn, q_vjp = jax.vjp(lambda q, g: _head_rms(q, g) * scale, q_raw, qg_row)
    kn_prev, kp_vjp = jax.vjp(_head_rms, k_prev, kg_row)
    kn_cur, kc_vjp = jax.vjp(_head_rms, k_cur, kg_row)
    own = _iota((GROUP * BLOCK, BLOCK), 1) <= (_iota((GROUP * BLOCK, BLOCK), 0) & (BLOCK - 1))
    low_half = _iota((BLOCK, LANES), 1) < HEAD_DIM
    nn, nt, tn = _DIMS["nn"], _DIMS["nt"], _DIMS["tn"]
    n_groups = k_cur.shape[1] // LANES
    dq_heads, d_sinks, d_bias = [None] * n_q_heads, [None] * n_q_heads, [None] * n_q_heads
    dk_prev, dk_cur, dv_prev, dv_cur = ([None] * n_groups for _ in range(4))

    def add(parts, grp, val):
        parts[grp] = val if parts[grp] is None else parts[grp] + val

    for kv in range(n_q_heads // GROUP):
        grp, kv_low = kv // 2, kv % 2 == 0
        lanes = slice(grp * LANES, (grp + 1) * LANES)
        keep = low_half if kv_low else ~low_half
        q_rows, do_rows = [], []
        for g in range(GROUP):
            h = GROUP * kv + g
            pair = slice((h // 2) * LANES, (h // 2 + 1) * LANES)
            in_place = (h % 2 == 0) == kv_low
            q2, do2 = qn[:, pair], d_out[:, pair]
            q_rows.append(jnp.where(keep, q2 if in_place else pltpu.roll(q2, HEAD_DIM, 1), 0.0))
            do_rows.append(jnp.where(keep, do2 if in_place else pltpu.roll(do2, HEAD_DIM, 1), 0.0))
        q4, do4 = jnp.concatenate(q_rows, axis=0), jnp.concatenate(do_rows, axis=0)
        bias4 = jnp.concatenate([bias[GROUP * kv + g] for g in range(GROUP)], axis=0)
        sink4 = jnp.concatenate([jnp.broadcast_to(sinks[GROUP * kv + g], (BLOCK, 1)) for g in range(GROUP)], axis=0)
        kc2, kp2, vc2, vp2 = kn_cur[:, lanes], kn_prev[:, lanes], v_cur[:, lanes], v_prev[:, lanes]
        s = jnp.where(own, _bf16_dot(q4, kc2, nt), _bf16_dot(q4, kp2, nt)) + bias4
        p, (_, p_sink) = _softmax_sink_fwd(s, sink4)
        p_own, p_prev = jnp.where(own, p, 0.0), jnp.where(own, 0.0, p)
        add(dv_cur, grp, _bf16_dot(p_own, do4, tn))
        add(dv_prev, grp, _bf16_dot(p_prev, do4, tn))
        dp = jnp.where(own, _bf16_dot(do4, vc2, nt), _bf16_dot(do4, vp2, nt))
        delta = jnp.sum(p * dp, axis=1, keepdims=True)
        ds = p * (dp - delta)
        d_sink4 = -p_sink * delta
        ds_own, ds_prev = jnp.where(own, ds, 0.0), jnp.where(own, 0.0, ds)
        dq4 = _bf16_dot(ds_own, kc2, nn) + _bf16_dot(ds_prev, kp2, nn)
        add(dk_cur, grp, _bf16_dot(ds_own, q4, tn))
        add(dk_prev, grp, _bf16_dot(ds_prev, q4, tn))
        for g in range(GROUP):
            h = GROUP * kv + g
            rows = slice(g * BLOCK, (g + 1) * BLOCK)
            dq2 = jnp.where(keep, dq4[rows], 0.0)
            dq_heads[h] = dq2 if (h % 2 == 0) == kv_low else pltpu.roll(dq2, HEAD_DIM, 1)
            d_bias[h] = ds[rows]
            d_sinks[h] = jnp.sum(d_sink4[rows], axis=0, keepdims=True)
    dqn = jnp.concatenate([dq_heads[2 * p] + dq_heads[2 * p + 1] for p in range(n_q_heads // 2)], axis=1)
    dq, dqg = q_vjp(dqn)
    dkp, dkg_prev = kp_vjp(jnp.concatenate(dk_prev, axis=1))
    dkc, dkg_cur = kc_vjp(jnp.concatenate(dk_cur, axis=1))
    return (dq, dkp, dkc, jnp.concatenate(dv_prev, axis=1), jnp.concatenate(dv_cur, axis=1),
            dqg, dkg_prev + dkg_cur, d_sinks, d_bias)


def _bias_table():
    i, j = np.arange(BLOCK)[:, None], np.arange(BLOCK)[None, :]
    n = np.where(j <= i, i - j, i + BLOCK - j)
    max_exact = NUM_BUCKETS // 2
    nf = np.maximum(n, 1).astype(np.float64)
    large = max_exact + (np.log(nf / max_exact) / math.log(MAX_DISTANCE / max_exact) * (NUM_BUCKETS - max_exact)).astype(np.int32)
    large = np.minimum(large, NUM_BUCKETS - 1)
    return np.where(n < max_exact, n, large).astype(np.int32)


def _bias_fwd(name, rel_bias, buckets):
    nb_, nh = rel_bias.shape

    def body(rb_ref, bk_ref, o_ref):
        bk = bk_ref[...]
        own = _iota(bk.shape, 1) <= _iota(bk.shape, 0)
        for h in range(nh):
            acc = jnp.zeros(bk.shape, F32)
            for b in range(nb_):
                acc = jnp.where(bk == b, rb_ref[b, h], acc)
            o_ref[1, h] = acc
            o_ref[0, h] = jnp.where(own, acc, NEG_INF)

    return pl.pallas_call(
        body, name=name,
        in_specs=[pl.BlockSpec(memory_space=pltpu.SMEM), pl.BlockSpec(memory_space=pltpu.VMEM)],
        out_specs=pl.BlockSpec(memory_space=pltpu.VMEM),
        out_shape=jax.ShapeDtypeStruct((2, nh) + buckets.shape, F32),
    )(rel_bias, buckets)


def _bias_bwd(name, dbias, buckets, nb_):
    nh = dbias.shape[0]

    def body(db_ref, bk_ref, o_ref):
        bk = bk_ref[...]
        for h in range(nh):
            d = db_ref[h]
            for b in range(nb_):
                s = jnp.sum(jnp.where(bk == b, d, 0.0), axis=0, keepdims=True)
                s = jnp.sum(s, axis=1, keepdims=True)
                o_ref[b * nh + h:b * nh + h + 1, :] = jnp.broadcast_to(s, (1, LANES))

    return pl.pallas_call(
        body, name=name,
        in_specs=[pl.BlockSpec(memory_space=pltpu.VMEM), pl.BlockSpec(memory_space=pltpu.VMEM)],
        out_specs=pl.BlockSpec(memory_space=pltpu.VMEM),
        out_shape=jax.ShapeDtypeStruct((nb_ * nh, LANES), F32),
    )(dbias, buckets)


SGU_CHUNKS_PER_STEP = 4


def _sgu_step(T):
    n = SGU_CHUNKS_PER_STEP if T % (SGU_CHUNKS_PER_STEP * CHUNK) == 0 else 1
    return n, [slice(c * CHUNK, (c + 1) * CHUNK) for c in range(n)]


def _sgu_fwd(name, z, g_row, w, b_full, W):
    T = z.shape[0]
    n, chunks = _sgu_step(T)

    def body(u_ref, v_ref, g_ref, w_ref, b_ref, o_ref):
        for rows in chunks:
            o_ref[rows, :] = _sgu_block(u_ref[rows, :], v_ref[rows, :], g_ref[...], w_ref[...], b_ref[...])

    return pl.pallas_call(
        body, name=name, grid=(T // (n * CHUNK),),
        in_specs=[_row_spec(n * CHUNK, W, 0), _row_spec(n * CHUNK, W, 1), _full_spec((1, W)),
                  _full_spec(w.shape), _full_spec((CHUNK, W))],
        out_specs=_row_spec(n * CHUNK, W),
        out_shape=jax.ShapeDtypeStruct((T, W), F32), compiler_params=_params(1),
    )(z, z, g_row, w, b_full)


def _sgu_bwd(name, z, g_row, w, b_full, d_out, others, W):
    T = z.shape[0]
    n, chunks = _sgu_step(T)
    widths = [o.shape[1] for o in others]

    def body(u_ref, v_ref, g_ref, w_ref, b_ref, do_ref, *rest):
        other_refs, (dz_ref, dg_ref, dw_ref, db_ref) = rest[:len(others)], rest[len(others):]
        dg = dw = db = None
        for rows in chunks:
            _, vjp = jax.vjp(_sgu_block, u_ref[rows, :], v_ref[rows, :], g_ref[...], w_ref[...], b_ref[...])
            du, dv, dg_c, dw_c, db_c = vjp(do_ref[rows, :])
            dz_ref[rows, :W] = du.astype(BF16)
            dz_ref[rows, W:2 * W] = dv.astype(BF16)
            dg, dw, db = (dg_c, dw_c, db_c) if dg is None else (dg + dg_c, dw + dw_c, db + db_c)
        at = 2 * W
        for o_ref, width in zip(other_refs, widths):
            dz_ref[:, at:at + width] = o_ref[...]
            at += width
        i = pl.program_id(0)

        @pl.when(i == 0)
        def _():
            dg_ref[...] = jnp.zeros_like(dg_ref)
            dw_ref[...] = jnp.zeros_like(dw_ref)
            db_ref[...] = jnp.zeros_like(db_ref)

        dg_ref[...] += dg
        dw_ref[...] += dw
        db_ref[...] += db

        @pl.when(i == pl.num_programs(0) - 1)
        def _():
            db_ref[...] = _head_sum_impl(db_ref[...])

    return pl.pallas_call(
        body, name=name, grid=(T // (n * CHUNK),),
        in_specs=[_row_spec(n * CHUNK, W, 0), _row_spec(n * CHUNK, W, 1), _full_spec((1, W)),
                  _full_spec(w.shape), _full_spec((CHUNK, W)), _row_spec(n * CHUNK, W)]
        + [_row_spec(n * CHUNK, width) for width in widths],
        out_specs=[_row_spec(n * CHUNK, 2 * W + sum(widths)), _full_spec((1, W)), _full_spec(w.shape),
                   _full_spec((CHUNK, W))],
        out_shape=[jax.ShapeDtypeStruct((T, 2 * W + sum(widths)), BF16), jax.ShapeDtypeStruct((1, W), F32),
                   jax.ShapeDtypeStruct(w.shape, F32), jax.ShapeDtypeStruct((CHUNK, W), F32)],
        compiler_params=_params(1),
    )(z, z, g_row, w, b_full, d_out, *others)


def _attn_specs(WQ, WKV, q_col, k_col, v_col, blk_of):
    prev_of = lambda i: jnp.maximum(blk_of(i) - 1, 0)
    return [pl.BlockSpec((BLOCK, WQ), lambda i: (blk_of(i), q_col)),
            pl.BlockSpec((BLOCK, WKV), lambda i: (prev_of(i), k_col)),
            pl.BlockSpec((BLOCK, WKV), lambda i: (blk_of(i), k_col)),
            pl.BlockSpec((BLOCK, WKV), lambda i: (prev_of(i), v_col)),
            pl.BlockSpec((BLOCK, WKV), lambda i: (blk_of(i), v_col))]


def _bias_spec(bias, blk_of):
    return pl.BlockSpec((1,) + bias.shape[1:], lambda i: (jnp.minimum(blk_of(i), 1), 0, 0, 0))


def _attn_fwd(name, z, qg_row, kg_row, sinks_col, bias, WQ, WKV, q_col, k_col, v_col):
    T = z.shape[0]
    nh = sinks_col.shape[0]

    def body(q_ref, kp_ref, kc_ref, vp_ref, vc_ref, qg_ref, kg_ref, s_ref, b_ref, o_ref):
        sinks = [s_ref[h:h + 1, :] for h in range(nh)]
        o_ref[...] = _attn_block(q_ref[...], kp_ref[...], kc_ref[...], vp_ref[...], vc_ref[...],
                                 qg_ref[...], kg_ref[...], sinks, b_ref[0])

    return pl.pallas_call(
        body, name=name, grid=(T // BLOCK,),
        in_specs=_attn_specs(WQ, WKV, q_col, k_col, v_col, lambda i: i)
        + [_full_spec((1, WQ)), _full_spec((1, WKV)), _full_spec((nh, 1)), _bias_spec(bias, lambda i: i)],
        out_specs=_row_spec(BLOCK, WQ),
        out_shape=jax.ShapeDtypeStruct((T, WQ), F32), compiler_params=_params(1),
    )(z, z, z, z, z, qg_row, kg_row, sinks_col, bias)


def _attn_bwd(name, z, qg_row, kg_row, sinks_col, bias, d_out, WQ, WKV, q_col, k_col, v_col):
    T = z.shape[0]
    nblk = T // BLOCK
    nh = sinks_col.shape[0]
    blk_of = lambda i: nblk - 1 - i

    def body(q_ref, kp_ref, kc_ref, vp_ref, vc_ref, qg_ref, kg_ref, s_ref, b_ref, do_ref,
             dq_ref, dk_ref, dv_ref, dqg_ref, dkg_ref, ds_ref, db_ref, carry_k, carry_v):
        i = pl.program_id(0)
        sinks = [s_ref[h:h + 1, :] for h in range(nh)]
        dq, dkp, dkc, dvp, dvc, dqg, dkg, dsk, dbs = _attn_block_bwd(
            q_ref[...], kp_ref[...], kc_ref[...], vp_ref[...], vc_ref[...], qg_ref[...], kg_ref[...], sinks, b_ref[0],
            do_ref[...])

        @pl.when(i == 0)
        def _():
            carry_k[...] = jnp.zeros_like(carry_k)
            carry_v[...] = jnp.zeros_like(carry_v)
            dqg_ref[...] = jnp.zeros_like(dqg_ref)
            dkg_ref[...] = jnp.zeros_like(dkg_ref)
            ds_ref[...] = jnp.zeros_like(ds_ref)
            db_ref[...] = jnp.zeros_like(db_ref)

        dq_ref[...] = dq.astype(BF16)
        dk_ref[...] = (dkc + carry_k[...]).astype(BF16)
        dv_ref[...] = (dvc + carry_v[...]).astype(BF16)
        carry_k[...] = dkp
        carry_v[...] = dvp
        dqg_ref[...] += dqg
        dkg_ref[...] += dkg
        for h in range(nh):
            ds_ref[h:h + 1, :] += dsk[h]
            db_ref[h] += dbs[h]

    return pl.pallas_call(
        body, name=name, grid=(nblk,),
        in_specs=_attn_specs(WQ, WKV, q_col, k_col, v_col, blk_of)
        + [_full_spec((1, WQ)), _full_spec((1, WKV)), _full_spec((nh, 1)), _bias_spec(bias, blk_of),
           pl.BlockSpec((BLOCK, WQ), lambda i: (blk_of(i), 0))],
        out_specs=[pl.BlockSpec((BLOCK, WQ), lambda i: (blk_of(i), 0)),
                   pl.BlockSpec((BLOCK, WKV), lambda i: (blk_of(i), 0)),
                   pl.BlockSpec((BLOCK, WKV), lambda i: (blk_of(i), 0)),
                   _full_spec((1, WQ)), _full_spec((1, WKV)), _full_spec((nh, 1)), _full_spec(bias.shape[1:])],
        out_shape=[jax.ShapeDtypeStruct((T, WQ), BF16), jax.ShapeDtypeStruct((T, WKV), BF16),
                   jax.ShapeDtypeStruct((T, WKV), BF16), jax.ShapeDtypeStruct((1, WQ), F32),
                   jax.ShapeDtypeStruct((1, WKV), F32), jax.ShapeDtypeStruct((nh, 1), F32),
                   jax.ShapeDtypeStruct(bias.shape[1:], F32)],
        scratch_shapes=[pltpu.VMEM((BLOCK, WKV), F32), pltpu.VMEM((BLOCK, WKV), F32)],
        compiler_params=_params(1),
    )(z, z, z, z, z, qg_row, kg_row, sinks_col, bias, d_out)


def _adamw(name, w, g, m, v):
    shape = w.shape
    C = shape[-1]
    if w.ndim == 3 and shape[1] % ROWS == 0:
        work = shape
        grid = (shape[0], shape[1] // ROWS)
        spec = pl.BlockSpec((1, ROWS, C), lambda l, i: (l, i, 0))
    else:
        R = int(np.prod(shape[:-1]))
        tr = ROWS if R % ROWS == 0 else R
        work = (R, C)
        grid = (R // tr,)
        spec = pl.BlockSpec((tr, C), lambda i: (i, 0))
    w2, g2, m2, v2 = (t.reshape(work) for t in (w, g, m, v))

    def body(w_ref, g_ref, m_ref, v_ref, d_ref, nm_ref, nv_ref):
        gv = g_ref[...]
        nm = ADAM_B1 * m_ref[...] + (1.0 - ADAM_B1) * gv
        nv = ADAM_B2 * v_ref[...] + (1.0 - ADAM_B2) * (gv * gv)
        m_hat = nm / (1.0 - ADAM_B1 ** ADAM_STEP)
        v_hat = nv / (1.0 - ADAM_B2 ** ADAM_STEP)
        d_ref[...] = -ADAM_LR * (m_hat / (jnp.sqrt(v_hat) + ADAM_EPS) + ADAM_WD * w_ref[...])
        nm_ref[...] = nm
        nv_ref[...] = nv

    outs = pl.pallas_call(
        body, name=name, grid=grid, in_specs=[spec] * 4, out_specs=[spec] * 3,
        out_shape=[jax.ShapeDtypeStruct(work, F32)] * 3, compiler_params=_params(len(grid)),
    )(w2, g2, m2, v2)
    return tuple(o.reshape(shape) for o in outs)


def _pad_rows(flat):
    n = flat.shape[0]
    tile = 8 * LANES
    padded = -(-n // tile) * tile
    return jnp.pad(flat, (0, padded - n)).reshape(padded // LANES, LANES)


def kernel(x, rel_bias, norm1_g, w_in, sgu_norm_g, sgu_w, sgu_b, q_norm_g, k_norm_g, sinks, out_norm_a, out_norm_b, w_out, norm2_g, w_gate, w_up, w_down, loss_target, m_rel_bias, m_norm1_g, m_w_in, m_sgu_norm_g, m_sgu_w, m_sgu_b, m_q_norm_g, m_k_norm_g, m_sinks, m_out_norm_a, m_out_norm_b, m_w_out, m_norm2_g, m_w_gate, m_w_up, m_w_down, v_rel_bias, v_norm1_g, v_w_in, v_sgu_norm_g, v_sgu_w, v_sgu_b, v_q_norm_g, v_k_norm_g, v_sinks, v_out_norm_a, v_out_norm_b, v_w_out, v_norm2_g, v_w_gate, v_w_up, v_w_down):
    L = w_in.shape[0]
    T, D = x.shape[1], x.shape[2]
    W = D // 2
    NH = W // HEAD_DIM
    WKV = N_KV_HEADS * HEAD_DIM
    IN = N_DEV * w_in.shape[2]
    FF = N_DEV * w_gate.shape[2]
    assert IN == 2 * W + W + 2 * WKV and NH // N_KV_HEADS == GROUP
    q_col, k_col, v_col = 2 * W // W, (3 * W) // WKV, (3 * W + WKV) // WKV
    x0 = x.reshape(T, D)
    target = loss_target.reshape(T, D)

    shards = [jnp.swapaxes(w_in, 1, 2).astype(BF16), w_out.astype(BF16), jnp.swapaxes(w_gate, 1, 2).astype(BF16),
              jnp.swapaxes(w_up, 1, 2).astype(BF16), w_down.astype(BF16)]
    me = 4 * lax.axis_index("x") + 2 * lax.axis_index("y") + lax.axis_index("c")

    def landing(block):
        return lax.dynamic_update_slice(lax.empty((N_DEV,) + block.shape, block.dtype), block[None], (me, 0, 0))

    def as_matrices(got):
        return [g.reshape(N_DEV * g.shape[1], g.shape[2]) for g in got]

    groups = ((0,), (1,), (2, 3, 4))
    full = {l: [None] * len(shards) for l in range(L)}
    full[0][0] = as_matrices(_all_gather("gather_weights_0", [shards[0][0]]))[0]
    flight = {}

    def start_gather(l, first_group, after):
        for g in range(first_group, len(groups)):
            srcs = [shards[i][l] for i in groups[g]]
            flight[l, g] = _exchange_start(f"gather_start_{l}_{g}", "gather", srcs, [landing(s) for s in srcs], after)
            after = flight[l, g][4]
        return after

    def finish_gather(l, g, after):
        s_sem, r_sem, srcs, lands, _ = flight.pop((l, g))
        lands = _exchange_wait(f"gather_wait_{l}_{g}", "gather", s_sem, r_sem, srcs, lands, after)[1]
        for i, mat in zip(groups[g], as_matrices(lands)):
            full[l][i] = mat

    buckets = jnp.asarray(_bias_table())
    bias = _bias_fwd("bias_table", rel_bias, buckets)

    saved = []
    xl = x0
    token = start_gather(0, 1, full[0][0])
    for l in range(L):
        if l > 0:
            finish_gather(l, 0, xl)
        w_in_t = full[l][0]
        g1, g2 = norm1_g[l][None], norm2_g[l][None]
        sg_row = sgu_norm_g[l].reshape(1, W)
        b_full = jnp.repeat(sgu_b[l].T, HEAD_DIM, axis=1)
        qg_row = jnp.tile(q_norm_g[l], NH)[None]
        kg_row = jnp.tile(k_norm_g[l], N_KV_HEADS)[None]
        sinks_col = sinks[l][:, None]
        ga, gb = out_norm_a[l][None], out_norm_b[l][None]

        h = _rms_fwd(f"norm1_{l}", xl, g1)
        (z,) = _mm(f"proj_in_{l}", "nt", [h], [w_in_t], [(0, 0, 0)], 1, T, IN, D, 1024, 896, D, _ep_plain, [F32],
                   after=token)
        out_a = _sgu_fwd(f"sgu_{l}", z, sg_row, sgu_w[l], b_full, W)
        out_b = _attn_fwd(f"attn_{l}", z, qg_row, kg_row, sinks_col, bias, W, WKV, q_col, k_col, v_col)
        mixed = _outnorm_fwd(f"outnorm_{l}", out_a, out_b, ga, gb)
        finish_gather(l, 1, mixed)
        w_o = full[l][1]
        (x1,) = _mm(f"proj_out_{l}", "nn", [mixed], [w_o], [(0, 0, 0)], 1, T, D, D, 1024, 1024, D,
                    _ep_residual, [F32], extras=[xl])
        h2 = _rms_fwd(f"norm2_{l}", x1, g2)
        token = start_gather(l + 1, 0, h2) if l + 1 < L else h2
        finish_gather(l, 2, h2)
        w_g_t, w_u_t, w_d = full[l][2:]
        gate, up, act = _mm(f"mlp_in_{l}", "nt", [h2], [w_g_t, w_u_t], [(0, 0, 0), (0, 1, 1)], 2, T, FF, D,
                            1024, 512, D, _ep_swiglu, [BF16, BF16, BF16], after=token)
        (x2,) = _mm(f"mlp_out_{l}", "nn", [act], [w_d], [(0, 0, 0)], 1, T, D, FF, 512, 1024, FF,
                    _ep_residual, [F32], extras=[x1])
        saved.append((xl, h, z, out_a, out_b, mixed, x1, h2, gate, up, act,
                      g1, g2, sg_row, b_full, qg_row, kg_row, sinks_col, ga, gb))
        xl = x2

    loss_part, dy, dy16 = _loss_head("loss_head", xl, target)

    dbias = None
    small = [None] * L
    scatters = []

    def start_scatter(name, which, grads_t, after):
        srcs = [t.reshape(N_DEV, t.shape[0] // N_DEV, D) for t in grads_t]
        lands = [lax.empty((N_DEV - 1,) + s.shape[1:], BF16) for s in srcs]
        s_sem, r_sem, srcs, lands, tok = _exchange_start(name, "scatter", srcs, lands, after)
        scatters.append((name, which, s_sem, r_sem, srcs, lands))
        return tok

    small_names = ["norm1_g", "sgu_norm_g", "sgu_w", "sgu_b", "q_norm_g", "k_norm_g", "sinks", "out_norm_a",
                   "out_norm_b", "norm2_g"]
    small_flights = {}

    def start_small(l, parts, after):
        order = sorted(parts, key=(["rel_bias"] + small_names).index)
        pieces = [_pad_rows(parts[nme].reshape(-1)) for nme in order]
        n_rows = sum(p.shape[0] for p in pieces)
        if n_rows % 64:
            pieces.append(jnp.zeros((-n_rows % 64, LANES), F32))
        packed = jnp.concatenate(pieces, axis=0)
        flight = _exchange_start(f"gather_small_start_{l}", "gather", [packed], [landing(packed)], after)
        small_flights[l] = (flight, [(nme, parts[nme].shape) for nme in order])
        return flight[4]

    def finish_small(l, after):
        (s_sem, r_sem, srcs, lands, _), layout = small_flights[l]
        _, (everyone,) = _exchange_wait(f"gather_small_wait_{l}", "gather", s_sem, r_sem, srcs, lands, after)
        summed = _sum_slots(f"sum_small_grads_{l}", everyone, 64)
        out, at = {}, 0
        for nme, shape in layout:
            n = int(np.prod(shape))
            n_rows = -(-n // (8 * LANES)) * 8
            out[nme] = summed[at:at + n_rows].reshape(-1)[:n].reshape(shape)
            at += n_rows
        return out

    token = None
    for l in reversed(range(L)):
        w_in_t, w_o, w_g_t, w_u_t, w_d = full[l]
        (xl, h, z, out_a, out_b, mixed, x1, h2, gate, up, act,
         g1, g2, sg_row, b_full, qg_row, kg_row, sinks_col, ga, gb) = saved[l]

        dgate, dup = _mm(f"d_mlp_out_{l}", "nt", [dy16], [w_d], [(0, 0, 0)], 1, T, FF, D, 1024, 1408, D,
                         _ep_swiglu_bwd, [BF16, BF16], extras=[gate, up], after=token)
        (dw_d,) = _mm(f"dw_down_{l}", "tn", [act], [dy16], [(0, 0, 0)], 1, FF, D, T, 1408, 1024, 2048, _ep_plain, [BF16])
        (dh2,) = _mm(f"d_mlp_in_{l}", "nn", [dgate, dup], [w_g_t, w_u_t], [(0, 0, 0), (1, 1, 0)], 1, T, D, FF,
                     512, 512, FF, _ep_plain, [F32])
        dx1, dx1_16, dg2 = _rms_bwd(f"d_norm2_{l}", dh2, x1, g2, dy)
        dw_g, dw_u = _mm(f"dw_gate_up_{l}", "tn", [dgate, dup], [h2], [(0, 0, 0), (1, 0, 1)], 2, FF, D, T,
                         1408, 1024, 1024, _ep_two, [BF16, BF16])
        token = start_scatter(f"scatter_mlp_start_{l}", (l, (4, 2, 3)), [dw_d, dw_g, dw_u], dx1_16)
        (dmixed,) = _mm(f"d_proj_out_{l}", "nt", [dx1_16], [w_o], [(0, 0, 0)], 1, T, D, D, 1024, 1024, D, _ep_plain, [F32],
                        after=token)
        (dw_o,) = _mm(f"dw_out_{l}", "tn", [mixed], [dx1_16], [(0, 0, 0)], 1, D, D, T, 1024, 1024, 2048, _ep_plain, [BF16])
        d_a, d_b, dga, dgb = _outnorm_bwd(f"d_outnorm_{l}", dmixed, out_a, out_b, ga, gb)
        dq, dk, dv, dqg, dkg, dsk, dbs = _attn_bwd(f"d_attn_{l}", z, qg_row, kg_row, sinks_col, bias, d_b,
                                                   W, WKV, q_col, k_col, v_col)
        dbias = dbs if dbias is None else dbias + dbs
        dz, dsg, dsw, dsb = _sgu_bwd(f"d_sgu_{l}", z, sg_row, sgu_w[l], b_full, d_a, [dq, dk, dv], W)
        (dh,) = _mm(f"d_proj_in_{l}", "nn", [dz], [w_in_t], [(0, 0, 0)], 1, T, D, IN, 1024, 1024, IN, _ep_plain, [F32])
        dy, dy16, dg1 = _rms_bwd(f"d_norm1_{l}", dh, xl, g1, dx1)
        (dw_i,) = _mm(f"dw_in_{l}", "tn", [dz], [h], [(0, 0, 0)], 1, IN, D, T, 896, 1024, 2048, _ep_plain, [BF16])
        token = start_scatter(f"scatter_mix_start_{l}", (l, (1, 0)), [dw_o, dw_i], dy16)

        small[l] = dict(norm1_g=dg1[0], sgu_norm_g=dsg.reshape(NH, HEAD_DIM), sgu_w=dsw,
                        sgu_b=dsb[:, ::HEAD_DIM].T, q_norm_g=dqg.reshape(NH, HEAD_DIM).sum(0),
                        k_norm_g=dkg.reshape(N_KV_HEADS, HEAD_DIM).sum(0), sinks=dsk[:, 0],
                        out_norm_a=dga[0], out_norm_b=dgb[0], norm2_g=dg2[0])
        if l > 0:
            token = start_small(l, small[l], token)

    grad_x = dy.reshape(x.shape)
    d_rel = _bias_bwd("d_bias_table", dbias, buckets, NUM_BUCKETS)[:, 0].reshape(NUM_BUCKETS, NH)
    after = start_small(0, dict(small[0], rel_bias=d_rel), token)

    me1 = me.reshape(1).astype(jnp.int32)
    grads_big = {}

    def finish_scatter(entry, after):
        name, (l, which), s_sem, r_sem, srcs, lands = entry
        srcs, lands = _exchange_wait(name.replace("start", "wait"), "scatter", s_sem, r_sem, srcs, lands, after)
        for i, src, land in zip(which, srcs, lands):
            rows = src.shape[1]
            after = _sum_parts(f"sum_grads_{l}_{i}", me1, src, land, 64 if rows % 64 == 0 else rows)
            grads_big[i, l] = after
        return after

    def stacked(i):
        return jnp.stack([grads_big[i, l] for l in range(L)])

    loss = lax.psum(loss_part[0, 0], ("x", "y", "c"))
    tr = lambda t: jnp.swapaxes(t, 1, 2)
    weights = dict(rel_bias=rel_bias, norm1_g=norm1_g, w_in=w_in, sgu_norm_g=sgu_norm_g, sgu_w=sgu_w, sgu_b=sgu_b,
                   q_norm_g=q_norm_g, k_norm_g=k_norm_g, sinks=sinks, out_norm_a=out_norm_a, out_norm_b=out_norm_b,
                   w_out=w_out, norm2_g=norm2_g, w_gate=w_gate, w_up=w_up, w_down=w_down)
    ms = dict(rel_bias=m_rel_bias, norm1_g=m_norm1_g, w_in=m_w_in, sgu_norm_g=m_sgu_norm_g, sgu_w=m_sgu_w, sgu_b=m_sgu_b,
              q_norm_g=m_q_norm_g, k_norm_g=m_k_norm_g, sinks=m_sinks, out_norm_a=m_out_norm_a, out_norm_b=m_out_norm_b,
              w_out=m_w_out, norm2_g=m_norm2_g, w_gate=m_w_gate, w_up=m_w_up, w_down=m_w_down)
    vs = dict(rel_bias=v_rel_bias, norm1_g=v_norm1_g, w_in=v_w_in, sgu_norm_g=v_sgu_norm_g, sgu_w=v_sgu_w, sgu_b=v_sgu_b,
              q_norm_g=v_q_norm_g, k_norm_g=v_k_norm_g, sinks=v_sinks, out_norm_a=v_out_norm_a, out_norm_b=v_out_norm_b,
              w_out=v_w_out, norm2_g=v_norm2_g, w_gate=v_w_gate, w_up=v_w_up, w_down=v_w_down)
    all_names = ["rel_bias", "norm1_g", "w_in", "sgu_norm_g", "sgu_w", "sgu_b", "q_norm_g", "k_norm_g", "sinks",
                 "out_norm_a", "out_norm_b", "w_out", "norm2_g", "w_gate", "w_up", "w_down"]
    transposed = ("w_in", "w_gate", "w_up")
    grads, deltas, new_m, new_v = {}, {}, {}, {}

    def update(nme, g):
        if nme in transposed:
            outs = _adamw(f"adamw_{nme}", tr(weights[nme]), g, tr(ms[nme]), tr(vs[nme]))
            grads[nme], (deltas[nme], new_m[nme], new_v[nme]) = tr(g), [tr(o) for o in outs]
        else:
            grads[nme] = g
            deltas[nme], new_m[nme], new_v[nme] = _adamw(f"adamw_{nme}", weights[nme], g, ms[nme], vs[nme])
        return new_v[nme]

    for entry in scatters[:-1]:
        after = finish_scatter(entry, after)
    for nme, i in (("w_gate", 2), ("w_up", 3), ("w_down", 4)):
        after = update(nme, stacked(i))

    small_sums = {}
    for l in reversed(range(L)):
        small_sums[l] = finish_small(l, after)
    after = update("rel_bias", small_sums[0]["rel_bias"])
    for nme in small_names:
        after = update(nme, jnp.stack([small_sums[l][nme] for l in range(L)]))

    finish_scatter(scatters[-1], after)
    update("w_out", stacked(1))
    update("w_in", stacked(0))
    return (loss, grad_x, *[grads[nme] for nme in all_names], *[deltas[nme] for nme in all_names],
            *[new_m[nme] for nme in all_names], *[new_v[nme] for nme in all_names])
```

```python
import functools
import math

import numpy as np
import jax
import jax.numpy as jnp
from jax import lax
from jax.experimental import pallas as pl
from jax.experimental.pallas import tpu as pltpu

F32 = jnp.float32
BF16 = jnp.bfloat16

N_DEV = 8
HEAD_DIM = 64
CHUNK = 128
BLOCK = 128
N_KV_HEADS = 4
GROUP = 4
NUM_BUCKETS = 32
MAX_DISTANCE = 128
EPS = 1e-6
NEG_INF = -1e30
LANES = 128
VMEM_LIMIT = 56 * 2 ** 20

ADAM_LR = 0.001
ADAM_B1 = 0.9
ADAM_B2 = 0.999
ADAM_EPS = 1e-08
ADAM_WD = 0.01
ADAM_STEP = 10

MESH = pl.DeviceIdType.MESH
ANY = pl.BlockSpec(memory_space=pl.ANY)
HBM = pl.BlockSpec(memory_space=pltpu.HBM)
SEM = pl.BlockSpec(memory_space=pltpu.SEMAPHORE)
EFFECT = pltpu.SideEffectType.DATAFLOW_SIDE_EFFECTING


def _params(n_axes):
    return pltpu.CompilerParams(dimension_semantics=("arbitrary",) * n_axes, vmem_limit_bytes=VMEM_LIMIT)


def _my_place():
    return lax.axis_index("x"), lax.axis_index("y"), lax.axis_index("c")


def _all_gather(name, arrs):
    n = len(arrs)

    def body(*refs):
        ins, outs = refs[:n], refs[n:2 * n]
        send_sems, recv_sems, local_sems = refs[2 * n:]
        x, y, c = _my_place()
        sibling = (x, y, 1 - c)
        chips = [(1 - x, y), (x, 1 - y), (1 - x, 1 - y)]

        def slot(a, px, py, pc):
            return outs[a].at[4 * px + 2 * py + pc]

        def copy(a, k, block, to, src=None):
            return pltpu.make_async_remote_copy(
                src_ref=slot(a, *block) if src is None else src, dst_ref=slot(a, *block),
                send_sem=send_sems.at[7 * a + k], recv_sem=recv_sems.at[7 * a + k],
                device_id=to, device_id_type=MESH)

        started = []
        for a in range(n):
            mine = pltpu.make_async_copy(ins[a], slot(a, x, y, c), local_sems.at[a])
            mine.start()
            started.append(mine)
        sends = []
        for a in range(n):
            first = [copy(a, 0, (x, y, c), sibling, src=ins[a])]
            first += [copy(a, 1 + j, (x, y, c), (*chip, c), src=ins[a]) for j, chip in enumerate(chips)]
            for cp in first:
                cp.start()
            sends += first
        for a in range(n):
            for j, chip in enumerate(chips):
                copy(a, 1 + j, (*chip, c), (x, y, c)).wait_recv()
                fwd = copy(a, 4 + j, (*chip, c), sibling)
                fwd.start()
                sends.append(fwd)
        for a in range(n):
            copy(a, 0, sibling, (x, y, c)).wait_recv()
            for j, chip in enumerate(chips):
                copy(a, 4 + j, (*chip, 1 - c), (x, y, c)).wait_recv()
        for cp in sends:
            cp.wait_send()
        for cp in started:
            cp.wait()

    return pl.pallas_call(
        body, name=name,
        out_shape=[jax.ShapeDtypeStruct((N_DEV,) + a.shape, a.dtype) for a in arrs],
        in_specs=[ANY] * n, out_specs=[ANY] * n,
        scratch_shapes=[pltpu.SemaphoreType.DMA((7 * n,)), pltpu.SemaphoreType.DMA((7 * n,)),
                        pltpu.SemaphoreType.DMA((n,))],
    )(*arrs)


def _peer(k, x, y, c):
    return (1 - x if k & 4 else x), (1 - y if k & 2 else y), (1 - c if k & 1 else c)


PEER_ORDER = (1, 2, 4, 3, 5, 6, 7)


def _exchange_copy(kind, k, a, srcs, lands, send_sems, recv_sems, arriving=False):
    x, y, c = _my_place()
    me = 4 * x + 2 * y + c
    px, py, pc = _peer(k, x, y, c)
    them = 4 * px + 2 * py + pc
    if kind == "gather":
        src, dst_there, dst_here = srcs[a], lands[a].at[me], lands[a].at[them]
    else:
        src, dst_there, dst_here = srcs[a].at[them], lands[a].at[k - 1], lands[a].at[k - 1]
    return pltpu.make_async_remote_copy(
        src_ref=src, dst_ref=dst_here if arriving else dst_there,
        send_sem=send_sems.at[7 * a + k - 1], recv_sem=recv_sems.at[7 * a + k - 1],
        device_id=(px, py, pc), device_id_type=MESH)


def _exchange_start(name, kind, srcs, lands, after):
    n = len(srcs)

    def body(*refs):
        ins, lnd = refs[:n], refs[n:2 * n]
        send_sems, recv_sems = refs[2 * n + 1], refs[2 * n + 2]
        token = refs[-1]
        for k in PEER_ORDER:
            for a in range(n):
                _exchange_copy(kind, k, a, ins, lnd, send_sems, recv_sems).start()
        token[...] = jnp.zeros_like(token)

    hbm = lambda t: pltpu.with_memory_space_constraint(t, pltpu.HBM)
    out = pl.pallas_call(
        body, name=name,
        out_shape=(pltpu.SemaphoreType.DMA((7 * n,)), pltpu.SemaphoreType.DMA((7 * n,)),
                   *[pltpu.HBM(t.shape, t.dtype) for t in srcs], *[pltpu.HBM(t.shape, t.dtype) for t in lands],
                   jax.ShapeDtypeStruct((8, LANES), F32)),
        in_specs=[HBM] * (2 * n) + [ANY],
        out_specs=(SEM, SEM, *[HBM] * (2 * n), pl.BlockSpec(memory_space=pltpu.VMEM)),
        input_output_aliases={i: 2 + i for i in range(2 * n)},
        compiler_params=pltpu.CompilerParams(has_side_effects=EFFECT),
    )(*[hbm(t) for t in srcs], *[hbm(t) for t in lands], after)
    return out[0], out[1], list(out[2:2 + n]), list(out[2 + n:2 + 2 * n]), out[-1]


def _exchange_wait(name, kind, send_sems, recv_sems, srcs, lands, after):
    n = len(srcs)

    def body(*refs):
        ins, lnd = refs[:n], refs[n:2 * n]
        s_sems, r_sems = refs[2 * n], refs[2 * n + 1]
        for a in range(n):
            for k in PEER_ORDER:
                _exchange_copy(kind, k, a, ins, lnd, s_sems, r_sems).wait_send()
                _exchange_copy(kind, k, a, ins, lnd, s_sems, r_sems, arriving=True).wait_recv()

    out = pl.pallas_call(
        body, name=name,
        out_shape=(*[pltpu.HBM(t.shape, t.dtype) for t in srcs], *[pltpu.HBM(t.shape, t.dtype) for t in lands]),
        in_specs=[HBM] * (2 * n) + [SEM, SEM, ANY],
        out_specs=tuple([HBM] * (2 * n)),
        input_output_aliases={i: i for i in range(2 * n)},
        compiler_params=pltpu.CompilerParams(has_side_effects=EFFECT),
    )(*srcs, *lands, send_sems, recv_sems, after)
    return list(out[:n]), list(out[n:])


def _sum_parts(name, me, mine, parts, rows):
    _, R, C = mine.shape

    def body(me_ref, own_ref, p_ref, o_ref):
        acc = own_ref[0].astype(F32)
        for s in range(N_DEV - 1):
            acc = acc + p_ref[s].astype(F32)
        o_ref[...] = acc

    return pl.pallas_call(
        body, name=name,
        grid_spec=pltpu.PrefetchScalarGridSpec(
            num_scalar_prefetch=1, grid=(R // rows,),
            in_specs=[pl.BlockSpec((1, rows, C), lambda i, me_ref: (me_ref[0], i, 0)),
                      pl.BlockSpec((N_DEV - 1, rows, C), lambda i, me_ref: (0, i, 0))],
            out_specs=pl.BlockSpec((rows, C), lambda i, me_ref: (i, 0))),
        out_shape=jax.ShapeDtypeStruct((R, C), F32), compiler_params=_params(1),
    )(me, mine, parts)


def _sum_slots(name, a, rows):
    _, R, C = a.shape

    def body(a_ref, o_ref):
        acc = a_ref[0].astype(F32)
        for s in range(1, N_DEV):
            acc = acc + a_ref[s].astype(F32)
        o_ref[...] = acc

    return pl.pallas_call(
        body, name=name, grid=(R // rows,),
        in_specs=[pl.BlockSpec((N_DEV, rows, C), lambda i: (0, i, 0))],
        out_specs=pl.BlockSpec((rows, C), lambda i: (i, 0)),
        out_shape=jax.ShapeDtypeStruct((R, C), F32), compiler_params=_params(1),
    )(a)


_DIMS = {"nn": (((1,), (0,)), ((), ())), "nt": (((1,), (1,)), ((), ())), "tn": (((0,), (0,)), ((), ()))}


def _mm(name, mode, a_list, b_list, pairs, n_acc, M, N, K, tm, tn, tk, epilogue, out_dtypes, extras=(), after=None):
    tm, tn, tk = min(tm, M), min(tn, N), min(tk, K)
    assert M % tm == 0 and N % tn == 0 and K % tk == 0, (name, M, N, K, tm, tn, tk)
    nk = K // tk
    na, nb, ne, no = len(a_list), len(b_list), len(extras), len(out_dtypes)
    dims = _DIMS[mode]
    a_spec = (pl.BlockSpec((tk, tm), lambda j, i, k: (k, i)) if mode == "tn"
              else pl.BlockSpec((tm, tk), lambda j, i, k: (i, k)))
    b_spec = (pl.BlockSpec((tn, tk), lambda j, i, k: (j, k)) if mode == "nt"
              else pl.BlockSpec((tk, tn), lambda j, i, k: (k, j)))
    o_spec = pl.BlockSpec((tm, tn), lambda j, i, k: (i, j))
    tail = [] if after is None else [after]

    def body(*refs):
        a_refs, b_refs = refs[:na], refs[na:na + nb]
        e_refs = refs[na + nb:na + nb + ne]
        first_out = na + nb + ne + len(tail)
        o_refs = refs[first_out:first_out + no]
        acc_refs = refs[first_out + no:]

        def partial(oi):
            tot = None
            for ai, bi, ti in pairs:
                if ti == oi:
                    d = lax.dot_general(a_refs[ai][...], b_refs[bi][...], dims, preferred_element_type=F32)
                    tot = d if tot is None else tot + d
            return tot

        def finish(accs):
            outs = epilogue(accs, [e[...] for e in e_refs])
            for o_ref, o in zip(o_refs, outs):
                o_ref[...] = o.astype(o_ref.dtype)

        if nk == 1:
            finish([partial(oi) for oi in range(n_acc)])
        else:
            k = pl.program_id(2)

            @pl.when(k == 0)
            def _():
                for r in acc_refs:
                    r[...] = jnp.zeros_like(r)

            for oi in range(n_acc):
                acc_refs[oi][...] += partial(oi)

            @pl.when(k == nk - 1)
            def _():
                finish([r[...] for r in acc_refs])

    return pl.pallas_call(
        body, name=name, grid=(N // tn, M // tm, nk),
        in_specs=[a_spec] * na + [b_spec] * nb + [o_spec] * ne + [ANY] * len(tail),
        out_specs=[o_spec] * no,
        out_shape=[jax.ShapeDtypeStruct((M, N), dt) for dt in out_dtypes],
        scratch_shapes=[pltpu.VMEM((tm, tn), F32)] * (n_acc if nk > 1 else 0),
        compiler_params=_params(3),
    )(*a_list, *b_list, *extras, *tail)


def _ep_residual(accs, ex):
    return (ex[0] + accs[0],)


def _ep_rows_rms_bwd(acc, ex, rows):
    xv, res = ex
    r = lax.rsqrt(jnp.mean(xv * xv, axis=-1, keepdims=True) + EPS)
    xh = xv * r
    dg = jnp.sum(acc * xh, axis=0, keepdims=True)
    dxh = acc * rows[0]
    dx = r * (dxh - xh * jnp.mean(dxh * xh, axis=-1, keepdims=True)) + res
    return (dx, dx), (dg,)


def _sigmoid(x):
    return 0.5 * (jnp.tanh(0.5 * x) + 1.0)


def _ep_plain(accs, ex):
    return (accs[0],)


def _ep_swiglu(accs, ex):
    g, u = accs
    return g, u, (g * _sigmoid(g)) * u


def _ep_swiglu_bwd(accs, ex):
    dact = accs[0]
    g, u = ex[0].astype(F32), ex[1].astype(F32)
    sig = _sigmoid(g)
    silu = g * sig
    return dact * u * (sig * (1.0 + g * (1.0 - sig))), dact * silu


def _ep_two(accs, ex):
    return accs[0], accs[1]


ROWS = 256


def _row_spec(tm, width, col=0):
    return pl.BlockSpec((tm, width), lambda i: (i, col))


def _full_spec(shape):
    nd = len(shape)
    return pl.BlockSpec(shape, lambda i: (0,) * nd)


def _rms_fwd(name, x, g):
    T, D = x.shape
    tm = min(ROWS, T)

    def body(x_ref, g_ref, o_ref):
        xv = x_ref[...]
        r = lax.rsqrt(jnp.mean(xv * xv, axis=-1, keepdims=True) + EPS)
        o_ref[...] = ((xv * r) * g_ref[...]).astype(BF16)

    return pl.pallas_call(
        body, name=name, grid=(T // tm,),
        in_specs=[_row_spec(tm, D), _full_spec((1, D))], out_specs=_row_spec(tm, D),
        out_shape=jax.ShapeDtypeStruct((T, D), BF16), compiler_params=_params(1),
    )(x, g)


def _rms_bwd(name, dyn, x, g, res):
    T, D = x.shape
    tm = min(ROWS, T)

    def body(dyn_ref, x_ref, g_ref, res_ref, dx_ref, dx16_ref, dg_ref):
        outs, (dg,) = _ep_rows_rms_bwd(dyn_ref[...], [x_ref[...], res_ref[...]], [g_ref[...]])

        @pl.when(pl.program_id(0) == 0)
        def _():
            dg_ref[...] = jnp.zeros_like(dg_ref)

        dg_ref[...] += dg
        dx_ref[...] = outs[0]
        dx16_ref[...] = outs[1].astype(BF16)

    return pl.pallas_call(
        body, name=name, grid=(T // tm,),
        in_specs=[_row_spec(tm, D), _row_spec(tm, D), _full_spec((1, D)), _row_spec(tm, D)],
        out_specs=[_row_spec(tm, D), _row_spec(tm, D), _full_spec((1, D))],
        out_shape=[jax.ShapeDtypeStruct((T, D), F32), jax.ShapeDtypeStruct((T, D), BF16),
                   jax.ShapeDtypeStruct((1, D), F32)],
        compiler_params=_params(1),
    )(dyn, x, g, res)


def _outnorm_fwd(name, a, b, ga, gb):
    T, W = a.shape
    tm = min(ROWS, T)

    def body(a_ref, b_ref, ga_ref, gb_ref, o_ref):
        for src, gain, lo in ((a_ref, ga_ref, 0), (b_ref, gb_ref, W)):
            v = src[...].astype(F32)
            r = lax.rsqrt(jnp.mean(v * v, axis=-1, keepdims=True) + EPS)
            o_ref[:, lo:lo + W] = ((v * r) * gain[...]).astype(BF16)

    return pl.pallas_call(
        body, name=name, grid=(T // tm,),
        in_specs=[_row_spec(tm, W), _row_spec(tm, W), _full_spec((1, W)), _full_spec((1, W))],
        out_specs=_row_spec(tm, 2 * W),
        out_shape=jax.ShapeDtypeStruct((T, 2 * W), BF16), compiler_params=_params(1),
    )(a, b, ga, gb)


def _outnorm_bwd(name, dmixed, a, b, ga, gb):
    T, W = a.shape
    tm = min(ROWS, T)

    def body(dm_ref, a_ref, b_ref, ga_ref, gb_ref, da_ref, db_ref, dga_ref, dgb_ref):
        first = pl.program_id(0) == 0
        for src, gain, lo, dsrc, dgain in ((a_ref, ga_ref, 0, da_ref, dga_ref), (b_ref, gb_ref, W, db_ref, dgb_ref)):
            v, dv = src[...].astype(F32), dm_ref[:, lo:lo + W]
            r = lax.rsqrt(jnp.mean(v * v, axis=-1, keepdims=True) + EPS)
            vh = v * r

            @pl.when(first)
            def _():
                dgain[...] = jnp.zeros_like(dgain)

            dgain[...] += jnp.sum(dv * vh, axis=0, keepdims=True)
            dvh = dv * gain[...]
            dsrc[...] = r * (dvh - vh * jnp.mean(dvh * vh, axis=-1, keepdims=True))

    return pl.pallas_call(
        body, name=name, grid=(T // tm,),
        in_specs=[_row_spec(tm, 2 * W), _row_spec(tm, W), _row_spec(tm, W), _full_spec((1, W)), _full_spec((1, W))],
        out_specs=[_row_spec(tm, W), _row_spec(tm, W), _full_spec((1, W)), _full_spec((1, W))],
        out_shape=[jax.ShapeDtypeStruct((T, W), F32), jax.ShapeDtypeStruct((T, W), F32),
                   jax.ShapeDtypeStruct((1, W), F32), jax.ShapeDtypeStruct((1, W), F32)],
        compiler_params=_params(1),
    )(dmixed, a, b, ga, gb)


def _loss_head(name, y, target):
    T, D = y.shape
    tm = min(ROWS, T)

    def body(y_ref, t_ref, loss_ref, dy_ref, dy16_ref):
        d = y_ref[...] - t_ref[...]

        @pl.when(pl.program_id(0) == 0)
        def _():
            loss_ref[...] = jnp.zeros_like(loss_ref)

        per_token = jnp.mean(d * d, axis=-1, keepdims=True)
        loss_ref[...] += 0.5 * jnp.sum(per_token, axis=0, keepdims=True)
        dy = d * (1.0 / D)
        dy_ref[...] = dy
        dy16_ref[...] = dy.astype(BF16)

    return pl.pallas_call(
        body, name=name, grid=(T // tm,),
        in_specs=[_row_spec(tm, D), _row_spec(tm, D)],
        out_specs=[_full_spec((1, 1)), _row_spec(tm, D), _row_spec(tm, D)],
        out_shape=[jax.ShapeDtypeStruct((1, 1), F32), jax.ShapeDtypeStruct((T, D), F32),
                   jax.ShapeDtypeStruct((T, D), BF16)],
        compiler_params=_params(1),
    )(y, target)


def _bf16_dot(a, b, dims):
    return lax.dot_general(a.astype(BF16), b.astype(BF16), dims, preferred_element_type=F32)


@jax.custom_vjp
def _dot_nn(a, b):
    return _bf16_dot(a, b, _DIMS["nn"])


def _dot_nn_fwd(a, b):
    return _dot_nn(a, b), (a, b)


def _dot_nn_bwd(saved, ct):
    a, b = saved
    return _bf16_dot(ct, b, _DIMS["nt"]), _bf16_dot(a, ct, _DIMS["tn"])


_dot_nn.defvjp(_dot_nn_fwd, _dot_nn_bwd)


@jax.custom_vjp
def _dot_nt(a, b):
    return _bf16_dot(a, b, _DIMS["nt"])


def _dot_nt_fwd(a, b):
    return _dot_nt(a, b), (a, b)


def _dot_nt_bwd(saved, ct):
    a, b = saved
    return _bf16_dot(ct, b, _DIMS["nn"]), _bf16_dot(ct, a, _DIMS["tn"])


_dot_nt.defvjp(_dot_nt_fwd, _dot_nt_bwd)


def _iota(shape, dim):
    return lax.broadcasted_iota(jnp.int32, shape, dim)


def _head_sum_impl(x):
    same_head = (_iota((LANES, LANES), 0) // HEAD_DIM == _iota((LANES, LANES), 1) // HEAD_DIM).astype(BF16)
    pieces = []
    for i in range(x.shape[1] // LANES):
        xs = x[:, i * LANES:(i + 1) * LANES]
        hi = xs.astype(BF16)
        r1 = xs - hi.astype(F32)
        mid = r1.astype(BF16)
        lo = (r1 - mid.astype(F32)).astype(BF16)
        tot = None
        for part in (hi, mid, lo):
            d = lax.dot_general(part, same_head, _DIMS["nn"], preferred_element_type=F32)
            tot = d if tot is None else tot + d
        pieces.append(tot)
    return pieces[0] if len(pieces) == 1 else jnp.concatenate(pieces, axis=1)


@jax.custom_vjp
def _head_sum(x):
    return _head_sum_impl(x)


_head_sum.defvjp(lambda x: (_head_sum_impl(x), None), lambda _, ct: (_head_sum_impl(ct),))


def _head_rms(x, g_row):
    ms = _head_sum(x * x) * (1.0 / HEAD_DIM)
    return (x * lax.rsqrt(ms + EPS)) * g_row


@jax.custom_vjp
def _swap_halves(x):
    return pltpu.roll(x, HEAD_DIM, 1)


_swap_halves.defvjp(lambda x: (pltpu.roll(x, HEAD_DIM, 1), None), lambda _, ct: (pltpu.roll(ct, HEAD_DIM, 1),))


def _gelu(x):
    return 0.5 * x * (1.0 + lax.erf(x * (1.0 / math.sqrt(2.0))))


def _sgu_block(u_raw, v_raw, g_row, w, b_full):
    u, v = _gelu(u_raw), _gelu(v_raw)
    vn = _head_rms(v, g_row)
    causal = _iota((CHUNK, CHUNK), 0) >= _iota((CHUNK, CHUNK), 1)
    low_half = _iota((CHUNK, LANES), 1) < HEAD_DIM
    gates = []
    for p in range(v.shape[1] // LANES):
        vp = vn[:, p * LANES:(p + 1) * LANES]
        g0 = _dot_nn(jnp.where(causal, w[2 * p], 0.0), vp)
        g1 = _dot_nn(jnp.where(causal, w[2 * p + 1], 0.0), vp)
        gates.append(jnp.where(low_half, g0, g1))
    gate = jnp.concatenate(gates, axis=1) + b_full
    return u * gate


def _softmax_sink_fwd(s, sink):
    m = jnp.maximum(jnp.max(s, axis=1, keepdims=True), sink)
    e = jnp.exp(s - m)
    e_sink = jnp.exp(sink - m)
    r = 1.0 / (jnp.sum(e, axis=1, keepdims=True) + e_sink)
    p = e * r
    return p, (p, e_sink * r)


@jax.custom_vjp
def _softmax_with_sink(s, sink):
    return _softmax_sink_fwd(s, sink)[0]


def _softmax_sink_bwd(saved, dp):
    p, p_sink = saved
    delta = jnp.sum(p * dp, axis=1, keepdims=True)
    return p * (dp - delta), -p_sink * delta


_softmax_with_sink.defvjp(_softmax_sink_fwd, _softmax_sink_bwd)


def _attn_block(q_raw, k_prev, k_cur, v_prev, v_cur, qg_row, kg_row, sinks, bias):
    scale = 1.0 / math.sqrt(HEAD_DIM)
    n_q_heads = q_raw.shape[1] // HEAD_DIM
    qn = _head_rms(q_raw, qg_row) * scale
    kn_prev, kn_cur = _head_rms(k_prev, kg_row), _head_rms(k_cur, kg_row)
    own = _iota((GROUP * BLOCK, BLOCK), 1) <= (_iota((GROUP * BLOCK, BLOCK), 0) & (BLOCK - 1))
    low_half = _iota((BLOCK, LANES), 1) < HEAD_DIM
    heads = [None] * n_q_heads
    for kv in range(n_q_heads // GROUP):
        grp, kv_low = kv // 2, kv % 2 == 0
        lanes = slice(grp * LANES, (grp + 1) * LANES)
        keep = low_half if kv_low else ~low_half
        stacked = []
        for g in range(GROUP):
            h = GROUP * kv + g
            q2 = qn[:, (h // 2) * LANES:(h // 2 + 1) * LANES]
            src = q2 if (h % 2 == 0) == kv_low else _swap_halves(q2)
            stacked.append(jnp.where(keep, src, 0.0))
        q4 = jnp.concatenate(stacked, axis=0)
        bias4 = jnp.concatenate([bias[GROUP * kv + g] for g in range(GROUP)], axis=0)
        sink4 = jnp.concatenate([jnp.broadcast_to(sinks[GROUP * kv + g], (BLOCK, 1)) for g in range(GROUP)], axis=0)
        s = jnp.where(own, _dot_nt(q4, kn_cur[:, lanes]), _dot_nt(q4, kn_prev[:, lanes])) + bias4
        p = _softmax_with_sink(s, sink4)
        o4 = _dot_nn(jnp.where(own, p, 0.0), v_cur[:, lanes]) + _dot_nn(jnp.where(own, 0.0, p), v_prev[:, lanes])
        for g in range(GROUP):
            h = GROUP * kv + g
            o = o4[g * BLOCK:(g + 1) * BLOCK]
            heads[h] = o if (h % 2 == 0) == kv_low else _swap_halves(o)
    outs = [jnp.where(low_half, heads[2 * p], heads[2 * p + 1]) for p in range(n_q_heads // 2)]
    return jnp.concatenate(outs, axis=1)


def _attn_block_bwd(q_raw, k_prev, k_cur, v_prev, v_cur, qg_row, kg_row, sinks, bias, d_out):
    scale = 1.0 / math.sqrt(HEAD_DIM)
    n_q_heads = q_raw.shape[1] // HEAD_DIM
    qn, q_vjp = jax.vjp(lambda q, g: _head_rms(q, g) * scale, q_raw, qg_row)
    kn_prev, kp_vjp = jax.vjp(_head_rms, k_prev, kg_row)
    kn_cur, kc_vjp = jax.vjp(_head_rms, k_cur, kg_row)
    own = _iota((GROUP * BLOCK, BLOCK), 1) <= (_iota((GROUP * BLOCK, BLOCK), 0) & (BLOCK - 1))
    low_half = _iota((BLOCK, LANES), 1) < HEAD_DIM
    nn, nt, tn = _DIMS["nn"], _DIMS["nt"], _DIMS["tn"]
    n_groups = k_cur.shape[1] // LANES
    dq_heads, d_sinks, d_bias = [None] * n_q_heads, [None] * n_q_heads, [None] * n_q_heads
    dk_prev, dk_cur, dv_prev, dv_cur = ([None] * n_groups for _ in range(4))

    def add(parts, grp, val):
        parts[grp] = val if parts[grp] is None else parts[grp] + val

    for kv in range(n_q_heads // GROUP):
        grp, kv_low = kv // 2, kv % 2 == 0
        lanes = slice(grp * LANES, (grp + 1) * LANES)
        keep = low_half if kv_low else ~low_half
        q_rows, do_rows = [], []
        for g in range(GROUP):
            h = GROUP * kv + g
            pair = slice((h // 2) * LANES, (h // 2 + 1) * LANES)
            in_place = (h % 2 == 0) == kv_low
            q2, do2 = qn[:, pair], d_out[:, pair]
            q_rows.append(jnp.where(keep, q2 if in_place else pltpu.roll(q2, HEAD_DIM, 1), 0.0))
            do_rows.append(jnp.where(keep, do2 if in_place else pltpu.roll(do2, HEAD_DIM, 1), 0.0))
        q4, do4 = jnp.concatenate(q_rows, axis=0), jnp.concatenate(do_rows, axis=0)
        bias4 = jnp.concatenate([bias[GROUP * kv + g] for g in range(GROUP)], axis=0)
        sink4 = jnp.concatenate([jnp.broadcast_to(sinks[GROUP * kv + g], (BLOCK, 1)) for g in range(GROUP)], axis=0)
        kc2, kp2, vc2, vp2 = kn_cur[:, lanes], kn_prev[:, lanes], v_cur[:, lanes], v_prev[:, lanes]
        s = jnp.where(own, _bf16_dot(q4, kc2, nt), _bf16_dot(q4, kp2, nt)) + bias4
        p, (_, p_sink) = _softmax_sink_fwd(s, sink4)
        p_own, p_prev = jnp.where(own, p, 0.0), jnp.where(own, 0.0, p)
        add(dv_cur, grp, _bf16_dot(p_own, do4, tn))
        add(dv_prev, grp, _bf16_dot(p_prev, do4, tn))
        dp = jnp.where(own, _bf16_dot(do4, vc2, nt), _bf16_dot(do4, vp2, nt))
        delta = jnp.sum(p * dp, axis=1, keepdims=True)
        ds = p * (dp - delta)
        d_sink4 = -p_sink * delta
        ds_own, ds_prev = jnp.where(own, ds, 0.0), jnp.where(own, 0.0, ds)
        dq4 = _bf16_dot(ds_own, kc2, nn) + _bf16_dot(ds_prev, kp2, nn)
        add(dk_cur, grp, _bf16_dot(ds_own, q4, tn))
        add(dk_prev, grp, _bf16_dot(ds_prev, q4, tn))
        for g in range(GROUP):
            h = GROUP * kv + g
            rows = slice(g * BLOCK, (g + 1) * BLOCK)
            dq2 = jnp.where(keep, dq4[rows], 0.0)
            dq_heads[h] = dq2 if (h % 2 == 0) == kv_low else pltpu.roll(dq2, HEAD_DIM, 1)
            d_bias[h] = ds[rows]
            d_sinks[h] = jnp.sum(d_sink4[rows], axis=0, keepdims=True)
    dqn = jnp.concatenate([dq_heads[2 * p] + dq_heads[2 * p + 1] for p in range(n_q_heads // 2)], axis=1)
    dq, dqg = q_vjp(dqn)
    dkp, dkg_prev = kp_vjp(jnp.concatenate(dk_prev, axis=1))
    dkc, dkg_cur = kc_vjp(jnp.concatenate(dk_cur, axis=1))
    return (dq, dkp, dkc, jnp.concatenate(dv_prev, axis=1), jnp.concatenate(dv_cur, axis=1),
            dqg, dkg_prev + dkg_cur, d_sinks, d_bias)


def _bias_table():
    i, j = np.arange(BLOCK)[:, None], np.arange(BLOCK)[None, :]
    n = np.where(j <= i, i - j, i + BLOCK - j)
    max_exact = NUM_BUCKETS // 2
    nf = np.maximum(n, 1).astype(np.float64)
    large = max_exact + (np.log(nf / max_exact) / math.log(MAX_DISTANCE / max_exact) * (NUM_BUCKETS - max_exact)).astype(np.int32)
    large = np.minimum(large, NUM_BUCKETS - 1)
    return np.where(n < max_exact, n, large).astype(np.int32)


def _bias_fwd(name, rel_bias, buckets):
    nb_, nh = rel_bias.shape

    def body(rb_ref, bk_ref, o_ref):
        bk = bk_ref[...]
        own = _iota(bk.shape, 1) <= _iota(bk.shape, 0)
        for h in range(nh):
            acc = jnp.zeros(bk.shape, F32)
            for b in range(nb_):
                acc = jnp.where(bk == b, rb_ref[b, h], acc)
            o_ref[1, h] = acc
            o_ref[0, h] = jnp.where(own, acc, NEG_INF)

    return pl.pallas_call(
        body, name=name,
        in_specs=[pl.BlockSpec(memory_space=pltpu.SMEM), pl.BlockSpec(memory_space=pltpu.VMEM)],
        out_specs=pl.BlockSpec(memory_space=pltpu.VMEM),
        out_shape=jax.ShapeDtypeStruct((2, nh) + buckets.shape, F32),
    )(rel_bias, buckets)


def _bias_bwd(name, dbias, buckets, nb_):
    nh = dbias.shape[0]

    def body(db_ref, bk_ref, o_ref):
        bk = bk_ref[...]
        for h in range(nh):
            d = db_ref[h]
            for b in range(nb_):
                s = jnp.sum(jnp.where(bk == b, d, 0.0), axis=0, keepdims=True)
                s = jnp.sum(s, axis=1, keepdims=True)
                o_ref[b * nh + h:b * nh + h + 1, :] = jnp.broadcast_to(s, (1, LANES))

    return pl.pallas_call(
        body, name=name,
        in_specs=[pl.BlockSpec(memory_space=pltpu.VMEM), pl.BlockSpec(memory_space=pltpu.VMEM)],
        out_specs=pl.BlockSpec(memory_space=pltpu.VMEM),
        out_shape=jax.ShapeDtypeStruct((nb_ * nh, LANES), F32),
    )(dbias, buckets)


SGU_CHUNKS_PER_STEP = 4


def _sgu_step(T):
    n = SGU_CHUNKS_PER_STEP if T % (SGU_CHUNKS_PER_STEP * CHUNK) == 0 else 1
    return n, [slice(c * CHUNK, (c + 1) * CHUNK) for c in range(n)]


def _sgu_fwd(name, z, g_row, w, b_full, W):
    T = z.shape[0]
    n, chunks = _sgu_step(T)

    def body(u_ref, v_ref, g_ref, w_ref, b_ref, o_ref):
        for rows in chunks:
            o_ref[rows, :] = _sgu_block(u_ref[rows, :].astype(F32), v_ref[rows, :].astype(F32), g_ref[...], w_ref[...],
                                        b_ref[...]).astype(BF16)

    return pl.pallas_call(
        body, name=name, grid=(T // (n * CHUNK),),
        in_specs=[_row_spec(n * CHUNK, W, 0), _row_spec(n * CHUNK, W, 1), _full_spec((1, W)),
                  _full_spec(w.shape), _full_spec((CHUNK, W))],
        out_specs=_row_spec(n * CHUNK, W),
        out_shape=jax.ShapeDtypeStruct((T, W), BF16), compiler_params=_params(1),
    )(z, z, g_row, w, b_full)


def _sgu_bwd(name, z, g_row, w, b_full, d_out, others, W):
    T = z.shape[0]
    n, chunks = _sgu_step(T)
    widths = [o.shape[1] for o in others]

    def body(u_ref, v_ref, g_ref, w_ref, b_ref, do_ref, *rest):
        other_refs, (dz_ref, dg_ref, dw_ref, db_ref) = rest[:len(others)], rest[len(others):]
        dg = dw = db = None
        for rows in chunks:
            _, vjp = jax.vjp(_sgu_block, u_ref[rows, :].astype(F32), v_ref[rows, :].astype(F32), g_ref[...], w_ref[...],
                             b_ref[...])
            du, dv, dg_c, dw_c, db_c = vjp(do_ref[rows, :])
            dz_ref[rows, :W] = du.astype(BF16)
            dz_ref[rows, W:2 * W] = dv.astype(BF16)
            dg, dw, db = (dg_c, dw_c, db_c) if dg is None else (dg + dg_c, dw + dw_c, db + db_c)
        at = 2 * W
        for o_ref, width in zip(other_refs, widths):
            dz_ref[:, at:at + width] = o_ref[...]
            at += width
        i = pl.program_id(0)

        @pl.when(i == 0)
        def _():
            dg_ref[...] = jnp.zeros_like(dg_ref)
            dw_ref[...] = jnp.zeros_like(dw_ref)
            db_ref[...] = jnp.zeros_like(db_ref)

        dg_ref[...] += dg
        dw_ref[...] += dw
        db_ref[...] += db

        @pl.when(i == pl.num_programs(0) - 1)
        def _():
            db_ref[...] = _head_sum_impl(db_ref[...])

    return pl.pallas_call(
        body, name=name, grid=(T // (n * CHUNK),),
        in_specs=[_row_spec(n * CHUNK, W, 0), _row_spec(n * CHUNK, W, 1), _full_spec((1, W)),
                  _full_spec(w.shape), _full_spec((CHUNK, W)), _row_spec(n * CHUNK, W)]
        + [_row_spec(n * CHUNK, width) for width in widths],
        out_specs=[_row_spec(n * CHUNK, 2 * W + sum(widths)), _full_spec((1, W)), _full_spec(w.shape),
                   _full_spec((CHUNK, W))],
        out_shape=[jax.ShapeDtypeStruct((T, 2 * W + sum(widths)), BF16), jax.ShapeDtypeStruct((1, W), F32),
                   jax.ShapeDtypeStruct(w.shape, F32), jax.ShapeDtypeStruct((CHUNK, W), F32)],
        compiler_params=_params(1),
    )(z, z, g_row, w, b_full, d_out, *others)


def _attn_specs(WQ, WKV, q_col, k_col, v_col, blk_of):
    prev_of = lambda i: jnp.maximum(blk_of(i) - 1, 0)
    return [pl.BlockSpec((BLOCK, WQ), lambda i: (blk_of(i), q_col)),
            pl.BlockSpec((BLOCK, WKV), lambda i: (prev_of(i), k_col)),
            pl.BlockSpec((BLOCK, WKV), lambda i: (blk_of(i), k_col)),
            pl.BlockSpec((BLOCK, WKV), lambda i: (prev_of(i), v_col)),
            pl.BlockSpec((BLOCK, WKV), lambda i: (blk_of(i), v_col))]


def _bias_spec(bias, blk_of):
    return pl.BlockSpec((1,) + bias.shape[1:], lambda i: (jnp.minimum(blk_of(i), 1), 0, 0, 0))


def _attn_fwd(name, z, qg_row, kg_row, sinks_col, bias, WQ, WKV, q_col, k_col, v_col):
    T = z.shape[0]
    nh = sinks_col.shape[0]

    def body(q_ref, kp_ref, kc_ref, vp_ref, vc_ref, qg_ref, kg_ref, s_ref, b_ref, o_ref):
        sinks = [s_ref[h:h + 1, :] for h in range(nh)]
        q, kp, kc, vp, vc = (r[...].astype(F32) for r in (q_ref, kp_ref, kc_ref, vp_ref, vc_ref))
        o_ref[...] = _attn_block(q, kp, kc, vp, vc, qg_ref[...], kg_ref[...], sinks, b_ref[0]).astype(BF16)

    return pl.pallas_call(
        body, name=name, grid=(T // BLOCK,),
        in_specs=_attn_specs(WQ, WKV, q_col, k_col, v_col, lambda i: i)
        + [_full_spec((1, WQ)), _full_spec((1, WKV)), _full_spec((nh, 1)), _bias_spec(bias, lambda i: i)],
        out_specs=_row_spec(BLOCK, WQ),
        out_shape=jax.ShapeDtypeStruct((T, WQ), BF16), compiler_params=_params(1),
    )(z, z, z, z, z, qg_row, kg_row, sinks_col, bias)


def _attn_bwd(name, z, qg_row, kg_row, sinks_col, bias, d_out, WQ, WKV, q_col, k_col, v_col):
    T = z.shape[0]
    nblk = T // BLOCK
    nh = sinks_col.shape[0]
    blk_of = lambda i: nblk - 1 - i

    def body(q_ref, kp_ref, kc_ref, vp_ref, vc_ref, qg_ref, kg_ref, s_ref, b_ref, do_ref,
             dq_ref, dk_ref, dv_ref, dqg_ref, dkg_ref, ds_ref, db_ref, carry_k, carry_v):
        i = pl.program_id(0)
        sinks = [s_ref[h:h + 1, :] for h in range(nh)]
        q, kp, kc, vp, vc = (r[...].astype(F32) for r in (q_ref, kp_ref, kc_ref, vp_ref, vc_ref))
        dq, dkp, dkc, dvp, dvc, dqg, dkg, dsk, dbs = _attn_block_bwd(
            q, kp, kc, vp, vc, qg_ref[...], kg_ref[...], sinks, b_ref[0], do_ref[...])

        @pl.when(i == 0)
        def _():
            carry_k[...] = jnp.zeros_like(carry_k)
            carry_v[...] = jnp.zeros_like(carry_v)
            dqg_ref[...] = jnp.zeros_like(dqg_ref)
            dkg_ref[...] = jnp.zeros_like(dkg_ref)
            ds_ref[...] = jnp.zeros_like(ds_ref)
            db_ref[...] = jnp.zeros_like(db_ref)

        dq_ref[...] = dq.astype(BF16)
        dk_ref[...] = (dkc + carry_k[...]).astype(BF16)
        dv_ref[...] = (dvc + carry_v[...]).astype(BF16)
        carry_k[...] = dkp
        carry_v[...] = dvp
        dqg_ref[...] += dqg
        dkg_ref[...] += dkg
        for h in range(nh):
            ds_ref[h:h + 1, :] += dsk[h]
            db_ref[h] += dbs[h]

    return pl.pallas_call(
        body, name=name, grid=(nblk,),
        in_specs=_attn_specs(WQ, WKV, q_col, k_col, v_col, blk_of)
        + [_full_spec((1, WQ)), _full_spec((1, WKV)), _full_spec((nh, 1)), _bias_spec(bias, blk_of),
           pl.BlockSpec((BLOCK, WQ), lambda i: (blk_of(i), 0))],
        out_specs=[pl.BlockSpec((BLOCK, WQ), lambda i: (blk_of(i), 0)),
                   pl.BlockSpec((BLOCK, WKV), lambda i: (blk_of(i), 0)),
                   pl.BlockSpec((BLOCK, WKV), lambda i: (blk_of(i), 0)),
                   _full_spec((1, WQ)), _full_spec((1, WKV)), _full_spec((nh, 1)), _full_spec(bias.shape[1:])],
        out_shape=[jax.ShapeDtypeStruct((T, WQ), BF16), jax.ShapeDtypeStruct((T, WKV), BF16),
                   jax.ShapeDtypeStruct((T, WKV), BF16), jax.ShapeDtypeStruct((1, WQ), F32),
                   jax.ShapeDtypeStruct((1, WKV), F32), jax.ShapeDtypeStruct((nh, 1), F32),
                   jax.ShapeDtypeStruct(bias.shape[1:], F32)],
        scratch_shapes=[pltpu.VMEM((BLOCK, WKV), F32), pltpu.VMEM((BLOCK, WKV), F32)],
        compiler_params=_params(1),
    )(z, z, z, z, z, qg_row, kg_row, sinks_col, bias, d_out)


def _adamw(name, w, g, m, v):
    shape = w.shape
    C = shape[-1]
    if w.ndim == 3 and shape[1] % ROWS == 0:
        work = shape
        grid = (shape[0], shape[1] // ROWS)
        spec = pl.BlockSpec((1, ROWS, C), lambda l, i: (l, i, 0))
    else:
        R = int(np.prod(shape[:-1]))
        tr = ROWS if R % ROWS == 0 else R
        work = (R, C)
        grid = (R // tr,)
        spec = pl.BlockSpec((tr, C), lambda i: (i, 0))
    w2, g2, m2, v2 = (t.reshape(work) for t in (w, g, m, v))

    def body(w_ref, g_ref, m_ref, v_ref, d_ref, nm_ref, nv_ref):
        gv = g_ref[...]
        nm = ADAM_B1 * m_ref[...] + (1.0 - ADAM_B1) * gv
        nv = ADAM_B2 * v_ref[...] + (1.0 - ADAM_B2) * (gv * gv)
        m_hat = nm / (1.0 - ADAM_B1 ** ADAM_STEP)
        v_hat = nv / (1.0 - ADAM_B2 ** ADAM_STEP)
        d_ref[...] = -ADAM_LR * (m_hat / (jnp.sqrt(v_hat) + ADAM_EPS) + ADAM_WD * w_ref[...])
        nm_ref[...] = nm
        nv_ref[...] = nv

    outs = pl.pallas_call(
        body, name=name, grid=grid, in_specs=[spec] * 4, out_specs=[spec] * 3,
        out_shape=[jax.ShapeDtypeStruct(work, F32)] * 3, compiler_params=_params(len(grid)),
    )(w2, g2, m2, v2)
    return tuple(o.reshape(shape) for o in outs)


def _pad_rows(flat):
    n = flat.shape[0]
    tile = 8 * LANES
    padded = -(-n // tile) * tile
    return jnp.pad(flat, (0, padded - n)).reshape(padded // LANES, LANES)


def kernel(x, rel_bias, norm1_g, w_in, sgu_norm_g, sgu_w, sgu_b, q_norm_g, k_norm_g, sinks, out_norm_a, out_norm_b, w_out, norm2_g, w_gate, w_up, w_down, loss_target, m_rel_bias, m_norm1_g, m_w_in, m_sgu_norm_g, m_sgu_w, m_sgu_b, m_q_norm_g, m_k_norm_g, m_sinks, m_out_norm_a, m_out_norm_b, m_w_out, m_norm2_g, m_w_gate, m_w_up, m_w_down, v_rel_bias, v_norm1_g, v_w_in, v_sgu_norm_g, v_sgu_w, v_sgu_b, v_q_norm_g, v_k_norm_g, v_sinks, v_out_norm_a, v_out_norm_b, v_w_out, v_norm2_g, v_w_gate, v_w_up, v_w_down):
    L = w_in.shape[0]
    T, D = x.shape[1], x.shape[2]
    W = D // 2
    NH = W // HEAD_DIM
    WKV = N_KV_HEADS * HEAD_DIM
    IN = N_DEV * w_in.shape[2]
    FF = N_DEV * w_gate.shape[2]
    assert IN == 2 * W + W + 2 * WKV and NH // N_KV_HEADS == GROUP
    q_col, k_col, v_col = 2 * W // W, (3 * W) // WKV, (3 * W + WKV) // WKV
    x0 = x.reshape(T, D)
    target = loss_target.reshape(T, D)

    shards = [jnp.swapaxes(w_in, 1, 2).astype(BF16), w_out.astype(BF16), jnp.swapaxes(w_gate, 1, 2).astype(BF16),
              jnp.swapaxes(w_up, 1, 2).astype(BF16), w_down.astype(BF16)]
    me = 4 * lax.axis_index("x") + 2 * lax.axis_index("y") + lax.axis_index("c")

    def landing(block):
        return lax.dynamic_update_slice(lax.empty((N_DEV,) + block.shape, block.dtype), block[None], (me, 0, 0))

    def as_matrices(got):
        return [g.reshape(N_DEV * g.shape[1], g.shape[2]) for g in got]

    groups = ((0,), (1,), (2, 3, 4))
    full = {l: [None] * len(shards) for l in range(L)}
    full[0][0] = as_matrices(_all_gather("gather_weights_0", [shards[0][0]]))[0]
    flight = {}

    def start_gather(l, first_group, after):
        for g in range(first_group, len(groups)):
            srcs = [shards[i][l] for i in groups[g]]
            flight[l, g] = _exchange_start(f"gather_start_{l}_{g}", "gather", srcs, [landing(s) for s in srcs], after)
            after = flight[l, g][4]
        return after

    def finish_gather(l, g, after):
        s_sem, r_sem, srcs, lands, _ = flight.pop((l, g))
        lands = _exchange_wait(f"gather_wait_{l}_{g}", "gather", s_sem, r_sem, srcs, lands, after)[1]
        for i, mat in zip(groups[g], as_matrices(lands)):
            full[l][i] = mat

    buckets = jnp.asarray(_bias_table())
    bias = _bias_fwd("bias_table", rel_bias, buckets)

    saved = []
    xl = x0
    token = start_gather(0, 1, full[0][0])
    for l in range(L):
        if l > 0:
            finish_gather(l, 0, xl)
        w_in_t = full[l][0]
        g1, g2 = norm1_g[l][None], norm2_g[l][None]
        sg_row = sgu_norm_g[l].reshape(1, W)
        b_full = jnp.repeat(sgu_b[l].T, HEAD_DIM, axis=1)
        qg_row = jnp.tile(q_norm_g[l], NH)[None]
        kg_row = jnp.tile(k_norm_g[l], N_KV_HEADS)[None]
        sinks_col = sinks[l][:, None]
        ga, gb = out_norm_a[l][None], out_norm_b[l][None]

        h = _rms_fwd(f"norm1_{l}", xl, g1)
        (z,) = _mm(f"proj_in_{l}", "nt", [h], [w_in_t], [(0, 0, 0)], 1, T, IN, D, 1024, 896, D, _ep_plain, [BF16],
                   after=token)
        out_a = _sgu_fwd(f"sgu_{l}", z, sg_row, sgu_w[l], b_full, W)
        out_b = _attn_fwd(f"attn_{l}", z, qg_row, kg_row, sinks_col, bias, W, WKV, q_col, k_col, v_col)
        mixed = _outnorm_fwd(f"outnorm_{l}", out_a, out_b, ga, gb)
        finish_gather(l, 1, mixed)
        w_o = full[l][1]
        (x1,) = _mm(f"proj_out_{l}", "nn", [mixed], [w_o], [(0, 0, 0)], 1, T, D, D, 1024, 1024, D,
                    _ep_residual, [F32], extras=[xl])
        h2 = _rms_fwd(f"norm2_{l}", x1, g2)
        token = start_gather(l + 1, 0, h2) if l + 1 < L else h2
        finish_gather(l, 2, h2)
        w_g_t, w_u_t, w_d = full[l][2:]
        gate, up, act = _mm(f"mlp_in_{l}", "nt", [h2], [w_g_t, w_u_t], [(0, 0, 0), (0, 1, 1)], 2, T, FF, D,
                            1024, 512, D, _ep_swiglu, [BF16, BF16, BF16], after=token)
        (x2,) = _mm(f"mlp_out_{l}", "nn", [act], [w_d], [(0, 0, 0)], 1, T, D, FF, 512, 1024, FF,
                    _ep_residual, [F32], extras=[x1])
        saved.append((xl, h, z, out_a, out_b, mixed, x1, h2, gate, up, act,
                      g1, g2, sg_row, b_full, qg_row, kg_row, sinks_col, ga, gb))
        xl = x2

    loss_part, dy, dy16 = _loss_head("loss_head", xl, target)

    dbias = None
    small = [None] * L
    scatters = []

    def start_scatter(name, which, grads_t, after):
        srcs = [t.reshape(N_DEV, t.shape[0] // N_DEV, D) for t in grads_t]
        lands = [lax.empty((N_DEV - 1,) + s.shape[1:], BF16) for s in srcs]
        s_sem, r_sem, srcs, lands, tok = _exchange_start(name, "scatter", srcs, lands, after)
        scatters.append((name, which, s_sem, r_sem, srcs, lands))
        return tok

    small_names = ["norm1_g", "sgu_norm_g", "sgu_w", "sgu_b", "q_norm_g", "k_norm_g", "sinks", "out_norm_a",
                   "out_norm_b", "norm2_g"]
    small_flights = {}

    def start_small(l, parts, after):
        order = sorted(parts, key=(["rel_bias"] + small_names).index)
        pieces = [_pad_rows(parts[nme].reshape(-1)) for nme in order]
        n_rows = sum(p.shape[0] for p in pieces)
        if n_rows % 64:
            pieces.append(jnp.zeros((-n_rows % 64, LANES), F32))
        packed = jnp.concatenate(pieces, axis=0)
        flight = _exchange_start(f"gather_small_start_{l}", "gather", [packed], [landing(packed)], after)
        small_flights[l] = (flight, [(nme, parts[nme].shape) for nme in order])
        return flight[4]

    def finish_small(l, after):
        (s_sem, r_sem, srcs, lands, _), layout = small_flights[l]
        _, (everyone,) = _exchange_wait(f"gather_small_wait_{l}", "gather", s_sem, r_sem, srcs, lands, after)
        summed = _sum_slots(f"sum_small_grads_{l}", everyone, 64)
        out, at = {}, 0
        for nme, shape in layout:
            n = int(np.prod(shape))
            n_rows = -(-n // (8 * LANES)) * 8
            out[nme] = summed[at:at + n_rows].reshape(-1)[:n].reshape(shape)
            at += n_rows
        return out

    token = None
    for l in reversed(range(L)):
        w_in_t, w_o, w_g_t, w_u_t, w_d = full[l]
        (xl, h, z, out_a, out_b, mixed, x1, h2, gate, up, act,
         g1, g2, sg_row, b_full, qg_row, kg_row, sinks_col, ga, gb) = saved[l]

        dgate, dup = _mm(f"d_mlp_out_{l}", "nt", [dy16], [w_d], [(0, 0, 0)], 1, T, FF, D, 1024, 1408, D,
                         _ep_swiglu_bwd, [BF16, BF16], extras=[gate, up], after=token)
        (dw_d,) = _mm(f"dw_down_{l}", "tn", [act], [dy16], [(0, 0, 0)], 1, FF, D, T, 1408, 1024, 2048, _ep_plain, [BF16])
        (dh2,) = _mm(f"d_mlp_in_{l}", "nn", [dgate, dup], [w_g_t, w_u_t], [(0, 0, 0), (1, 1, 0)], 1, T, D, FF,
                     512, 512, FF, _ep_plain, [F32])
        dx1, dx1_16, dg2 = _rms_bwd(f"d_norm2_{l}", dh2, x1, g2, dy)
        dw_g, dw_u = _mm(f"dw_gate_up_{l}", "tn", [dgate, dup], [h2], [(0, 0, 0), (1, 0, 1)], 2, FF, D, T,
                         1408, 1024, 1024, _ep_two, [BF16, BF16])
        token = start_scatter(f"scatter_mlp_start_{l}", (l, (4, 2, 3)), [dw_d, dw_g, dw_u], dx1_16)
        (dmixed,) = _mm(f"d_proj_out_{l}", "nt", [dx1_16], [w_o], [(0, 0, 0)], 1, T, D, D, 1024, 1024, D, _ep_plain, [F32],
                        after=token)
        (dw_o,) = _mm(f"dw_out_{l}", "tn", [mixed], [dx1_16], [(0, 0, 0)], 1, D, D, T, 1024, 1024, 2048, _ep_plain, [BF16])
        d_a, d_b, dga, dgb = _outnorm_bwd(f"d_outnorm_{l}", dmixed, out_a, out_b, ga, gb)
        dq, dk, dv, dqg, dkg, dsk, dbs = _attn_bwd(f"d_attn_{l}", z, qg_row, kg_row, sinks_col, bias, d_b,
                                                   W, WKV, q_col, k_col, v_col)
        dbias = dbs if dbias is None else dbias + dbs
        dz, dsg, dsw, dsb = _sgu_bwd(f"d_sgu_{l}", z, sg_row, sgu_w[l], b_full, d_a, [dq, dk, dv], W)
        (dh,) = _mm(f"d_proj_in_{l}", "nn", [dz], [w_in_t], [(0, 0, 0)], 1, T, D, IN, 1024, 1024, IN, _ep_plain, [F32])
        dy, dy16, dg1 = _rms_bwd(f"d_norm1_{l}", dh, xl, g1, dx1)
        (dw_i,) = _mm(f"dw_in_{l}", "tn", [dz], [h], [(0, 0, 0)], 1, IN, D, T, 896, 1024, 2048, _ep_plain, [BF16])
        token = start_scatter(f"scatter_mix_start_{l}", (l, (1, 0)), [dw_o, dw_i], dy16)

        small[l] = dict(norm1_g=dg1[0], sgu_norm_g=dsg.reshape(NH, HEAD_DIM), sgu_w=dsw,
                        sgu_b=dsb[:, ::HEAD_DIM].T, q_norm_g=dqg.reshape(NH, HEAD_DIM).sum(0),
                        k_norm_g=dkg.reshape(N_KV_HEADS, HEAD_DIM).sum(0), sinks=dsk[:, 0],
                        out_norm_a=dga[0], out_norm_b=dgb[0], norm2_g=dg2[0])
        if l > 0:
            token = start_small(l, small[l], token)

    grad_x = dy.reshape(x.shape)
    d_rel = _bias_bwd("d_bias_table", dbias, buckets, NUM_BUCKETS)[:, 0].reshape(NUM_BUCKETS, NH)
    after = start_small(0, dict(small[0], rel_bias=d_rel), token)

    me1 = me.reshape(1).astype(jnp.int32)
    grads_big = {}

    def finish_scatter(entry, after):
        name, (l, which), s_sem, r_sem, srcs, lands = entry
        srcs, lands = _exchange_wait(name.replace("start", "wait"), "scatter", s_sem, r_sem, srcs, lands, after)
        for i, src, land in zip(which, srcs, lands):
            rows = src.shape[1]
            after = _sum_parts(f"sum_grads_{l}_{i}", me1, src, land, 64 if rows % 64 == 0 else rows)
            grads_big[i, l] = after
        return after

    def stacked(i):
        return jnp.stack([grads_big[i, l] for l in range(L)])

    loss = lax.psum(loss_part[0, 0], ("x", "y", "c"))
    tr = lambda t: jnp.swapaxes(t, 1, 2)
    weights = dict(rel_bias=rel_bias, norm1_g=norm1_g, w_in=w_in, sgu_norm_g=sgu_norm_g, sgu_w=sgu_w, sgu_b=sgu_b,
                   q_norm_g=q_norm_g, k_norm_g=k_norm_g, sinks=sinks, out_norm_a=out_norm_a, out_norm_b=out_norm_b,
                   w_out=w_out, norm2_g=norm2_g, w_gate=w_gate, w_up=w_up, w_down=w_down)
    ms = dict(rel_bias=m_rel_bias, norm1_g=m_norm1_g, w_in=m_w_in, sgu_norm_g=m_sgu_norm_g, sgu_w=m_sgu_w, sgu_b=m_sgu_b,
              q_norm_g=m_q_norm_g, k_norm_g=m_k_norm_g, sinks=m_sinks, out_norm_a=m_out_norm_a, out_norm_b=m_out_norm_b,
              w_out=m_w_out, norm2_g=m_norm2_g, w_gate=m_w_gate, w_up=m_w_up, w_down=m_w_down)
    vs = dict(rel_bias=v_rel_bias, norm1_g=v_norm1_g, w_in=v_w_in, sgu_norm_g=v_sgu_norm_g, sgu_w=v_sgu_w, sgu_b=v_sgu_b,
              q_norm_g=v_q_norm_g, k_norm_g=v_k_norm_g, sinks=v_sinks, out_norm_a=v_out_norm_a, out_norm_b=v_out_norm_b,
              w_out=v_w_out, norm2_g=v_norm2_g, w_gate=v_w_gate, w_up=v_w_up, w_down=v_w_down)
    all_names = ["rel_bias", "norm1_g", "w_in", "sgu_norm_g", "sgu_w", "sgu_b", "q_norm_g", "k_norm_g", "sinks",
                 "out_norm_a", "out_norm_b", "w_out", "norm2_g", "w_gate", "w_up", "w_down"]
    transposed = ("w_in", "w_gate", "w_up")
    grads, deltas, new_m, new_v = {}, {}, {}, {}

    def update(nme, g):
        if nme in transposed:
            outs = _adamw(f"adamw_{nme}", tr(weights[nme]), g, tr(ms[nme]), tr(vs[nme]))
            grads[nme], (deltas[nme], new_m[nme], new_v[nme]) = tr(g), [tr(o) for o in outs]
        else:
            grads[nme] = g
            deltas[nme], new_m[nme], new_v[nme] = _adamw(f"adamw_{nme}", weights[nme], g, ms[nme], vs[nme])
        return new_v[nme]

    for entry in scatters[:-1]:
        after = finish_scatter(entry, after)
    for nme, i in (("w_gate", 2), ("w_up", 3), ("w_down", 4)):
        after = update(nme, stacked(i))

    small_sums = {}
    for l in reversed(range(L)):
        small_sums[l] = finish_small(l, after)
    after = update("rel_bias", small_sums[0]["rel_bias"])
    for nme in small_names:
        after = update(nme, jnp.stack([small_sums[l][nme] for l in range(L)]))

    finish_scatter(scatters[-1], after)
    update("w_out", stacked(1))
    update("w_in", stacked(0))
    return (loss, grad_x, *[grads[nme] for nme in all_names], *[deltas[nme] for nme in all_names],
            *[new_m[nme] for nme in all_names], *[new_v[nme] for nme in all_names])
```

```python
import functools
import math

import numpy as np
import jax
import jax.numpy as jnp
from jax import lax
from jax.experimental import pallas as pl
from jax.experimental.pallas import tpu as pltpu

F32 = jnp.float32
BF16 = jnp.bfloat16

N_DEV = 8
HEAD_DIM = 64
CHUNK = 128
BLOCK = 128
N_KV_HEADS = 4
GROUP = 4
NUM_BUCKETS = 32
MAX_DISTANCE = 128
EPS = 1e-6
NEG_INF = -1e30
LANES = 128
VMEM_LIMIT = 56 * 2 ** 20

ADAM_LR = 0.001
ADAM_B1 = 0.9
ADAM_B2 = 0.999
ADAM_EPS = 1e-08
ADAM_WD = 0.01
ADAM_STEP = 10

MESH = pl.DeviceIdType.MESH
ANY = pl.BlockSpec(memory_space=pl.ANY)
HBM = pl.BlockSpec(memory_space=pltpu.HBM)
SEM = pl.BlockSpec(memory_space=pltpu.SEMAPHORE)
EFFECT = pltpu.SideEffectType.DATAFLOW_SIDE_EFFECTING


def _params(n_axes):
    return pltpu.CompilerParams(dimension_semantics=("arbitrary",) * n_axes, vmem_limit_bytes=VMEM_LIMIT)


def _my_place():
    return lax.axis_index("x"), lax.axis_index("y"), lax.axis_index("c")


def _all_gather(name, arrs):
    n = len(arrs)

    def body(*refs):
        ins, outs = refs[:n], refs[n:2 * n]
        send_sems, recv_sems, local_sems = refs[2 * n:]
        x, y, c = _my_place()
        sibling = (x, y, 1 - c)
        chips = [(1 - x, y), (x, 1 - y), (1 - x, 1 - y)]

        def slot(a, px, py, pc):
            return outs[a].at[4 * px + 2 * py + pc]

        def copy(a, k, block, to, src=None):
            return pltpu.make_async_remote_copy(
                src_ref=slot(a, *block) if src is None else src, dst_ref=slot(a, *block),
                send_sem=send_sems.at[7 * a + k], recv_sem=recv_sems.at[7 * a + k],
                device_id=to, device_id_type=MESH)

        started = []
        for a in range(n):
            mine = pltpu.make_async_copy(ins[a], slot(a, x, y, c), local_sems.at[a])
            mine.start()
            started.append(mine)
        sends = []
        for a in range(n):
            first = [copy(a, 0, (x, y, c), sibling, src=ins[a])]
            first += [copy(a, 1 + j, (x, y, c), (*chip, c), src=ins[a]) for j, chip in enumerate(chips)]
            for cp in first:
                cp.start()
            sends += first
        for a in range(n):
            for j, chip in enumerate(chips):
                copy(a, 1 + j, (*chip, c), (x, y, c)).wait_recv()
                fwd = copy(a, 4 + j, (*chip, c), sibling)
                fwd.start()
                sends.append(fwd)
        for a in range(n):
            copy(a, 0, sibling, (x, y, c)).wait_recv()
            for j, chip in enumerate(chips):
                copy(a, 4 + j, (*chip, 1 - c), (x, y, c)).wait_recv()
        for cp in sends:
            cp.wait_send()
        for cp in started:
            cp.wait()

    return pl.pallas_call(
        body, name=name,
        out_shape=[jax.ShapeDtypeStruct((N_DEV,) + a.shape, a.dtype) for a in arrs],
        in_specs=[ANY] * n, out_specs=[ANY] * n,
        scratch_shapes=[pltpu.SemaphoreType.DMA((7 * n,)), pltpu.SemaphoreType.DMA((7 * n,)),
                        pltpu.SemaphoreType.DMA((n,))],
    )(*arrs)


def _peer(k, x, y, c):
    return (1 - x if k & 4 else x), (1 - y if k & 2 else y), (1 - c if k & 1 else c)


PEER_ORDER = (1, 2, 4, 3, 5, 6, 7)


def _exchange_copy(kind, k, a, srcs, lands, send_sems, recv_sems, arriving=False):
    x, y, c = _my_place()
    me = 4 * x + 2 * y + c
    px, py, pc = _peer(k, x, y, c)
    them = 4 * px + 2 * py + pc
    if kind == "gather":
        src, dst_there, dst_here = srcs[a], lands[a].at[me], lands[a].at[them]
    else:
        src, dst_there, dst_here = srcs[a].at[them], lands[a].at[k - 1], lands[a].at[k - 1]
    return pltpu.make_async_remote_copy(
        src_ref=src, dst_ref=dst_here if arriving else dst_there,
        send_sem=send_sems.at[7 * a + k - 1], recv_sem=recv_sems.at[7 * a + k - 1],
        device_id=(px, py, pc), device_id_type=MESH)


def _exchange_start(name, kind, srcs, lands, after):
    n = len(srcs)

    def body(*refs):
        ins, lnd = refs[:n], refs[n:2 * n]
        send_sems, recv_sems = refs[2 * n + 1], refs[2 * n + 2]
        token = refs[-1]
        for k in PEER_ORDER:
            for a in range(n):
                _exchange_copy(kind, k, a, ins, lnd, send_sems, recv_sems).start()
        token[...] = jnp.zeros_like(token)

    hbm = lambda t: pltpu.with_memory_space_constraint(t, pltpu.HBM)
    out = pl.pallas_call(
        body, name=name,
        out_shape=(pltpu.SemaphoreType.DMA((7 * n,)), pltpu.SemaphoreType.DMA((7 * n,)),
                   *[pltpu.HBM(t.shape, t.dtype) for t in srcs], *[pltpu.HBM(t.shape, t.dtype) for t in lands],
                   jax.ShapeDtypeStruct((8, LANES), F32)),
        in_specs=[HBM] * (2 * n) + [ANY],
        out_specs=(SEM, SEM, *[HBM] * (2 * n), pl.BlockSpec(memory_space=pltpu.VMEM)),
        input_output_aliases={i: 2 + i for i in range(2 * n)},
        compiler_params=pltpu.CompilerParams(has_side_effects=EFFECT),
    )(*[hbm(t) for t in srcs], *[hbm(t) for t in lands], after)
    return out[0], out[1], list(out[2:2 + n]), list(out[2 + n:2 + 2 * n]), out[-1]


def _exchange_wait(name, kind, send_sems, recv_sems, srcs, lands, after):
    n = len(srcs)

    def body(*refs):
        ins, lnd = refs[:n], refs[n:2 * n]
        s_sems, r_sems = refs[2 * n], refs[2 * n + 1]
        for a in range(n):
            for k in PEER_ORDER:
                _exchange_copy(kind, k, a, ins, lnd, s_sems, r_sems).wait_send()
                _exchange_copy(kind, k, a, ins, lnd, s_sems, r_sems, arriving=True).wait_recv()

    out = pl.pallas_call(
        body, name=name,
        out_shape=(*[pltpu.HBM(t.shape, t.dtype) for t in srcs], *[pltpu.HBM(t.shape, t.dtype) for t in lands]),
        in_specs=[HBM] * (2 * n) + [SEM, SEM, ANY],
        out_specs=tuple([HBM] * (2 * n)),
        input_output_aliases={i: i for i in range(2 * n)},
        compiler_params=pltpu.CompilerParams(has_side_effects=EFFECT),
    )(*srcs, *lands, send_sems, recv_sems, after)
    return list(out[:n]), list(out[n:])


def _sum_parts(name, me, mine, parts, rows):
    _, R, C = mine.shape

    def body(me_ref, own_ref, p_ref, o_ref):
        acc = own_ref[0].astype(F32)
        for s in range(N_DEV - 1):
            acc = acc + p_ref[s].astype(F32)
        o_ref[...] = acc

    return pl.pallas_call(
        body, name=name,
        grid_spec=pltpu.PrefetchScalarGridSpec(
            num_scalar_prefetch=1, grid=(R // rows,),
            in_specs=[pl.BlockSpec((1, rows, C), lambda i, me_ref: (me_ref[0], i, 0)),
                      pl.BlockSpec((N_DEV - 1, rows, C), lambda i, me_ref: (0, i, 0))],
            out_specs=pl.BlockSpec((rows, C), lambda i, me_ref: (i, 0))),
        out_shape=jax.ShapeDtypeStruct((R, C), F32), compiler_params=_params(1),
    )(me, mine, parts)


def _sum_slots(name, a, rows):
    _, R, C = a.shape

    def body(a_ref, o_ref):
        acc = a_ref[0].astype(F32)
        for s in range(1, N_DEV):
            acc = acc + a_ref[s].astype(F32)
        o_ref[...] = acc

    return pl.pallas_call(
        body, name=name, grid=(R // rows,),
        in_specs=[pl.BlockSpec((N_DEV, rows, C), lambda i: (0, i, 0))],
        out_specs=pl.BlockSpec((rows, C), lambda i: (i, 0)),
        out_shape=jax.ShapeDtypeStruct((R, C), F32), compiler_params=_params(1),
    )(a)


_DIMS = {"nn": (((1,), (0,)), ((), ())), "nt": (((1,), (1,)), ((), ())), "tn": (((0,), (0,)), ((), ()))}


def _mm(name, mode, a_list, b_list, pairs, n_acc, M, N, K, tm, tn, tk, epilogue, out_dtypes, extras=(), after=None):
    tm, tn, tk = min(tm, M), min(tn, N), min(tk, K)
    assert M % tm == 0 and N % tn == 0 and K % tk == 0, (name, M, N, K, tm, tn, tk)
    nk = K // tk
    na, nb, ne, no = len(a_list), len(b_list), len(extras), len(out_dtypes)
    dims = _DIMS[mode]
    a_spec = (pl.BlockSpec((tk, tm), lambda j, i, k: (k, i)) if mode == "tn"
              else pl.BlockSpec((tm, tk), lambda j, i, k: (i, k)))
    b_spec = (pl.BlockSpec((tn, tk), lambda j, i, k: (j, k)) if mode == "nt"
              else pl.BlockSpec((tk, tn), lambda j, i, k: (k, j)))
    o_spec = pl.BlockSpec((tm, tn), lambda j, i, k: (i, j))
    tail = [] if after is None else [after]

    def body(*refs):
        a_refs, b_refs = refs[:na], refs[na:na + nb]
        e_refs = refs[na + nb:na + nb + ne]
        first_out = na + nb + ne + len(tail)
        o_refs = refs[first_out:first_out + no]
        acc_refs = refs[first_out + no:]

        def partial(oi):
            tot = None
            for ai, bi, ti in pairs:
                if ti == oi:
                    d = lax.dot_general(a_refs[ai][...], b_refs[bi][...], dims, preferred_element_type=F32)
                    tot = d if tot is None else tot + d
            return tot

        def finish(accs):
            outs = epilogue(accs, [e[...] for e in e_refs])
            for o_ref, o in zip(o_refs, outs):
                o_ref[...] = o.astype(o_ref.dtype)

        if nk == 1:
            finish([partial(oi) for oi in range(n_acc)])
        else:
            k = pl.program_id(2)

            @pl.when(k == 0)
            def _():
                for r in acc_refs:
                    r[...] = jnp.zeros_like(r)

            for oi in range(n_acc):
                acc_refs[oi][...] += partial(oi)

            @pl.when(k == nk - 1)
            def _():
                finish([r[...] for r in acc_refs])

    return pl.pallas_call(
        body, name=name, grid=(N // tn, M // tm, nk),
        in_specs=[a_spec] * na + [b_spec] * nb + [o_spec] * ne + [ANY] * len(tail),
        out_specs=[o_spec] * no,
        out_shape=[jax.ShapeDtypeStruct((M, N), dt) for dt in out_dtypes],
        scratch_shapes=[pltpu.VMEM((tm, tn), F32)] * (n_acc if nk > 1 else 0),
        compiler_params=_params(3),
    )(*a_list, *b_list, *extras, *tail)


def _ep_residual(accs, ex):
    return (ex[0] + accs[0],)


def _ep_rows_rms_bwd(acc, ex, rows):
    xv, res = ex
    r = lax.rsqrt(jnp.mean(xv * xv, axis=-1, keepdims=True) + EPS)
    xh = xv * r
    dg = jnp.sum(acc * xh, axis=0, keepdims=True)
    dxh = acc * rows[0]
    dx = r * (dxh - xh * jnp.mean(dxh * xh, axis=-1, keepdims=True)) + res
    return (dx, dx), (dg,)


def _sigmoid(x):
    return 0.5 * (jnp.tanh(0.5 * x) + 1.0)


def _ep_plain(accs, ex):
    return (accs[0],)


def _ep_swiglu(accs, ex):
    g, u = accs
    return g, u, (g * _sigmoid(g)) * u


def _ep_swiglu_bwd(accs, ex):
    dact = accs[0]
    g, u = ex[0].astype(F32), ex[1].astype(F32)
    sig = _sigmoid(g)
    silu = g * sig
    return dact * u * (sig * (1.0 + g * (1.0 - sig))), dact * silu


def _ep_two(accs, ex):
    return accs[0], accs[1]


ROWS = 256


def _row_spec(tm, width, col=0):
    return pl.BlockSpec((tm, width), lambda i: (i, col))


def _full_spec(shape):
    nd = len(shape)
    return pl.BlockSpec(shape, lambda i: (0,) * nd)


def _rms_fwd(name, x, g):
    T, D = x.shape
    tm = min(ROWS, T)

    def body(x_ref, g_ref, o_ref):
        xv = x_ref[...]
        r = lax.rsqrt(jnp.mean(xv * xv, axis=-1, keepdims=True) + EPS)
        o_ref[...] = ((xv * r) * g_ref[...]).astype(BF16)

    return pl.pallas_call(
        body, name=name, grid=(T // tm,),
        in_specs=[_row_spec(tm, D), _full_spec((1, D))], out_specs=_row_spec(tm, D),
        out_shape=jax.ShapeDtypeStruct((T, D), BF16), compiler_params=_params(1),
    )(x, g)


def _rms_bwd(name, dyn, x, g, res):
    T, D = x.shape
    tm = min(ROWS, T)

    def body(dyn_ref, x_ref, g_ref, res_ref, dx_ref, dx16_ref, dg_ref):
        outs, (dg,) = _ep_rows_rms_bwd(dyn_ref[...], [x_ref[...], res_ref[...]], [g_ref[...]])

        @pl.when(pl.program_id(0) == 0)
        def _():
            dg_ref[...] = jnp.zeros_like(dg_ref)

        dg_ref[...] += dg
        dx_ref[...] = outs[0]
        dx16_ref[...] = outs[1].astype(BF16)

    return pl.pallas_call(
        body, name=name, grid=(T // tm,),
        in_specs=[_row_spec(tm, D), _row_spec(tm, D), _full_spec((1, D)), _row_spec(tm, D)],
        out_specs=[_row_spec(tm, D), _row_spec(tm, D), _full_spec((1, D))],
        out_shape=[jax.ShapeDtypeStruct((T, D), F32), jax.ShapeDtypeStruct((T, D), BF16),
                   jax.ShapeDtypeStruct((1, D), F32)],
        compiler_params=_params(1),
    )(dyn, x, g, res)


def _outnorm_fwd(name, a, b, ga, gb):
    T, W = a.shape
    tm = min(ROWS, T)

    def body(a_ref, b_ref, ga_ref, gb_ref, o_ref):
        for src, gain, lo in ((a_ref, ga_ref, 0), (b_ref, gb_ref, W)):
            v = src[...].astype(F32)
            r = lax.rsqrt(jnp.mean(v * v, axis=-1, keepdims=True) + EPS)
            o_ref[:, lo:lo + W] = ((v * r) * gain[...]).astype(BF16)

    return pl.pallas_call(
        body, name=name, grid=(T // tm,),
        in_specs=[_row_spec(tm, W), _row_spec(tm, W), _full_spec((1, W)), _full_spec((1, W))],
        out_specs=_row_spec(tm, 2 * W),
        out_shape=jax.ShapeDtypeStruct((T, 2 * W), BF16), compiler_params=_params(1),
    )(a, b, ga, gb)


def _outnorm_bwd(name, dmixed, a, b, ga, gb):
    T, W = a.shape
    tm = min(ROWS, T)

    def body(dm_ref, a_ref, b_ref, ga_ref, gb_ref, da_ref, db_ref, dga_ref, dgb_ref):
        first = pl.program_id(0) == 0
        for src, gain, lo, dsrc, dgain in ((a_ref, ga_ref, 0, da_ref, dga_ref), (b_ref, gb_ref, W, db_ref, dgb_ref)):
            v, dv = src[...].astype(F32), dm_ref[:, lo:lo + W]
            r = lax.rsqrt(jnp.mean(v * v, axis=-1, keepdims=True) + EPS)
            vh = v * r

            @pl.when(first)
            def _():
                dgain[...] = jnp.zeros_like(dgain)

            dgain[...] += jnp.sum(dv * vh, axis=0, keepdims=True)
            dvh = dv * gain[...]
            dsrc[...] = r * (dvh - vh * jnp.mean(dvh * vh, axis=-1, keepdims=True))

    return pl.pallas_call(
        body, name=name, grid=(T // tm,),
        in_specs=[_row_spec(tm, 2 * W), _row_spec(tm, W), _row_spec(tm, W), _full_spec((1, W)), _full_spec((1, W))],
        out_specs=[_row_spec(tm, W), _row_spec(tm, W), _full_spec((1, W)), _full_spec((1, W))],
        out_shape=[jax.ShapeDtypeStruct((T, W), F32), jax.ShapeDtypeStruct((T, W), F32),
                   jax.ShapeDtypeStruct((1, W), F32), jax.ShapeDtypeStruct((1, W), F32)],
        compiler_params=_params(1),
    )(dmixed, a, b, ga, gb)


def _loss_head(name, y, target):
    T, D = y.shape
    tm = min(ROWS, T)

    def body(y_ref, t_ref, loss_ref, dy_ref, dy16_ref):
        d = y_ref[...] - t_ref[...]

        @pl.when(pl.program_id(0) == 0)
        def _():
            loss_ref[...] = jnp.zeros_like(loss_ref)

        per_token = jnp.mean(d * d, axis=-1, keepdims=True)
        loss_ref[...] += 0.5 * jnp.sum(per_token, axis=0, keepdims=True)
        dy = d * (1.0 / D)
        dy_ref[...] = dy
        dy16_ref[...] = dy.astype(BF16)

    return pl.pallas_call(
        body, name=name, grid=(T // tm,),
        in_specs=[_row_spec(tm, D), _row_spec(tm, D)],
        out_specs=[_full_spec((1, 1)), _row_spec(tm, D), _row_spec(tm, D)],
        out_shape=[jax.ShapeDtypeStruct((1, 1), F32), jax.ShapeDtypeStruct((T, D), F32),
                   jax.ShapeDtypeStruct((T, D), BF16)],
        compiler_params=_params(1),
    )(y, target)


def _bf16_dot(a, b, dims):
    return lax.dot_general(a.astype(BF16), b.astype(BF16), dims, preferred_element_type=F32)


@jax.custom_vjp
def _dot_nn(a, b):
    return _bf16_dot(a, b, _DIMS["nn"])


def _dot_nn_fwd(a, b):
    return _dot_nn(a, b), (a, b)


def _dot_nn_bwd(saved, ct):
    a, b = saved
    return _bf16_dot(ct, b, _DIMS["nt"]), _bf16_dot(a, ct, _DIMS["tn"])


_dot_nn.defvjp(_dot_nn_fwd, _dot_nn_bwd)


@jax.custom_vjp
def _dot_nt(a, b):
    return _bf16_dot(a, b, _DIMS["nt"])


def _dot_nt_fwd(a, b):
    return _dot_nt(a, b), (a, b)


def _dot_nt_bwd(saved, ct):
    a, b = saved
    return _bf16_dot(ct, b, _DIMS["nn"]), _bf16_dot(ct, a, _DIMS["tn"])


_dot_nt.defvjp(_dot_nt_fwd, _dot_nt_bwd)


def _iota(shape, dim):
    return lax.broadcasted_iota(jnp.int32, shape, dim)


def _head_sum_impl(x):
    same_head = (_iota((LANES, LANES), 0) // HEAD_DIM == _iota((LANES, LANES), 1) // HEAD_DIM).astype(BF16)
    pieces = []
    for i in range(x.shape[1] // LANES):
        xs = x[:, i * LANES:(i + 1) * LANES]
        hi = xs.astype(BF16)
        r1 = xs - hi.astype(F32)
        mid = r1.astype(BF16)
        lo = (r1 - mid.astype(F32)).astype(BF16)
        tot = None
        for part in (hi, mid, lo):
            d = lax.dot_general(part, same_head, _DIMS["nn"], preferred_element_type=F32)
            tot = d if tot is None else tot + d
        pieces.append(tot)
    return pieces[0] if len(pieces) == 1 else jnp.concatenate(pieces, axis=1)


@jax.custom_vjp
def _head_sum(x):
    return _head_sum_impl(x)


_head_sum.defvjp(lambda x: (_head_sum_impl(x), None), lambda _, ct: (_head_sum_impl(ct),))


def _head_rms(x, g_row):
    ms = _head_sum(x * x) * (1.0 / HEAD_DIM)
    return (x * lax.rsqrt(ms + EPS)) * g_row


@jax.custom_vjp
def _swap_halves(x):
    return pltpu.roll(x, HEAD_DIM, 1)


_swap_halves.defvjp(lambda x: (pltpu.roll(x, HEAD_DIM, 1), None), lambda _, ct: (pltpu.roll(ct, HEAD_DIM, 1),))


def _gelu(x):
    return 0.5 * x * (1.0 + lax.erf(x * (1.0 / math.sqrt(2.0))))


def _sgu_block(u_raw, v_raw, g_row, w, b_full):
    u, v = _gelu(u_raw), _gelu(v_raw)
    vn = _head_rms(v, g_row)
    causal = _iota((CHUNK, CHUNK), 0) >= _iota((CHUNK, CHUNK), 1)
    low_half = _iota((CHUNK, LANES), 1) < HEAD_DIM
    gates = []
    for p in range(v.shape[1] // LANES):
        vp = vn[:, p * LANES:(p + 1) * LANES]
        g0 = _dot_nn(jnp.where(causal, w[2 * p], 0.0), vp)
        g1 = _dot_nn(jnp.where(causal, w[2 * p + 1], 0.0), vp)
        gates.append(jnp.where(low_half, g0, g1))
    gate = jnp.concatenate(gates, axis=1) + b_full
    return u * gate


def _softmax_sink_fwd(s, sink):
    m = jnp.maximum(jnp.max(s, axis=1, keepdims=True), sink)
    e = jnp.exp(s - m)
    e_sink = jnp.exp(sink - m)
    r = 1.0 / (jnp.sum(e, axis=1, keepdims=True) + e_sink)
    p = e * r
    return p, (p, e_sink * r)


@jax.custom_vjp
def _softmax_with_sink(s, sink):
    return _softmax_sink_fwd(s, sink)[0]


def _softmax_sink_bwd(saved, dp):
    p, p_sink = saved
    delta = jnp.sum(p * dp, axis=1, keepdims=True)
    return p * (dp - delta), -p_sink * delta


_softmax_with_sink.defvjp(_softmax_sink_fwd, _softmax_sink_bwd)


def _attn_block(q_raw, k_prev, k_cur, v_prev, v_cur, qg_row, kg_row, sinks, bias):
    scale = 1.0 / math.sqrt(HEAD_DIM)
    n_q_heads = q_raw.shape[1] // HEAD_DIM
    qn = _head_rms(q_raw, qg_row) * scale
    kn_prev, kn_cur = _head_rms(k_prev, kg_row), _head_rms(k_cur, kg_row)
    own = _iota((GROUP * BLOCK, BLOCK), 1) <= (_iota((GROUP * BLOCK, BLOCK), 0) & (BLOCK - 1))
    low_half = _iota((BLOCK, LANES), 1) < HEAD_DIM
    heads = [None] * n_q_heads
    for kv in range(n_q_heads // GROUP):
        grp, kv_low = kv // 2, kv % 2 == 0
        lanes = slice(grp * LANES, (grp + 1) * LANES)
        keep = low_half if kv_low else ~low_half
        stacked = []
        for g in range(GROUP):
            h = GROUP * kv + g
            q2 = qn[:, (h // 2) * LANES:(h // 2 + 1) * LANES]
            src = q2 if (h % 2 == 0) == kv_low else _swap_halves(q2)
            stacked.append(jnp.where(keep, src, 0.0))
        q4 = jnp.concatenate(stacked, axis=0)
        bias4 = jnp.concatenate([bias[GROUP * kv + g] for g in range(GROUP)], axis=0)
        sink4 = jnp.concatenate([jnp.broadcast_to(sinks[GROUP * kv + g], (BLOCK, 1)) for g in range(GROUP)], axis=0)
        s = jnp.where(own, _dot_nt(q4, kn_cur[:, lanes]), _dot_nt(q4, kn_prev[:, lanes])) + bias4
        p = _softmax_with_sink(s, sink4)
        o4 = _dot_nn(jnp.where(own, p, 0.0), v_cur[:, lanes]) + _dot_nn(jnp.where(own, 0.0, p), v_prev[:, lanes])
        for g in range(GROUP):
            h = GROUP * kv + g
            o = o4[g * BLOCK:(g + 1) * BLOCK]
            heads[h] = o if (h % 2 == 0) == kv_low else _swap_halves(o)
    outs = [jnp.where(low_half, heads[2 * p], heads[2 * p + 1]) for p in range(n_q_heads // 2)]
    return jnp.concatenate(outs, axis=1)


def _attn_block_bwd(q_raw, k_prev, k_cur, v_prev, v_cur, qg_row, kg_row, sinks, bias, d_out):
    scale = 1.0 / math.sqrt(HEAD_DIM)
    n_q_heads = q_raw.shape[1] // HEAD_DIM
    qn, q_vjp = jax.vjp(lambda q, g: _head_rms(q, g) * scale, q_raw, qg_row)
    kn_prev, kp_vjp = jax.vjp(_head_rms, k_prev, kg_row)
    kn_cur, kc_vjp = jax.vjp(_head_rms, k_cur, kg_row)
    own = _iota((GROUP * BLOCK, BLOCK), 1) <= (_iota((GROUP * BLOCK, BLOCK), 0) & (BLOCK - 1))
    low_half = _iota((BLOCK, LANES), 1) < HEAD_DIM
    nn, nt, tn = _DIMS["nn"], _DIMS["nt"], _DIMS["tn"]
    n_groups = k_cur.shape[1] // LANES
    dq_heads, d_sinks, d_bias = [None] * n_q_heads, [None] * n_q_heads, [None] * n_q_heads
    dk_prev, dk_cur, dv_prev, dv_cur = ([None] * n_groups for _ in range(4))

    def add(parts, grp, val):
        parts[grp] = val if parts[grp] is None else parts[grp] + val

    for kv in range(n_q_heads // GROUP):
        grp, kv_low = kv // 2, kv % 2 == 0
        lanes = slice(grp * LANES, (grp + 1) * LANES)
        keep = low_half if kv_low else ~low_half
        q_rows, do_rows = [], []
        for g in range(GROUP):
            h = GROUP * kv + g
            pair = slice((h // 2) * LANES, (h // 2 + 1) * LANES)
            in_place = (h % 2 == 0) == kv_low
            q2, do2 = qn[:, pair], d_out[:, pair]
            q_rows.append(jnp.where(keep, q2 if in_place else pltpu.roll(q2, HEAD_DIM, 1), 0.0))
            do_rows.append(jnp.where(keep, do2 if in_place else pltpu.roll(do2, HEAD_DIM, 1), 0.0))
        q4, do4 = jnp.concatenate(q_rows, axis=0), jnp.concatenate(do_rows, axis=0)
        bias4 = jnp.concatenate([bias[GROUP * kv + g] for g in range(GROUP)], axis=0)
        sink4 = jnp.concatenate([jnp.broadcast_to(sinks[GROUP * kv + g], (BLOCK, 1)) for g in range(GROUP)], axis=0)
        kc2, kp2, vc2, vp2 = kn_cur[:, lanes], kn_prev[:, lanes], v_cur[:, lanes], v_prev[:, lanes]
        s = jnp.where(own, _bf16_dot(q4, kc2, nt), _bf16_dot(q4, kp2, nt)) + bias4
        p, (_, p_sink) = _softmax_sink_fwd(s, sink4)
        p_own, p_prev = jnp.where(own, p, 0.0), jnp.where(own, 0.0, p)
        add(dv_cur, grp, _bf16_dot(p_own, do4, tn))
        add(dv_prev, grp, _bf16_dot(p_prev, do4, tn))
        dp = jnp.where(own, _bf16_dot(do4, vc2, nt), _bf16_dot(do4, vp2, nt))
        delta = jnp.sum(p * dp, axis=1, keepdims=True)
        ds = p * (dp - delta)
        d_sink4 = -p_sink * delta
        ds_own, ds_prev = jnp.where(own, ds, 0.0), jnp.where(own, 0.0, ds)
        dq4 = _bf16_dot(ds_own, kc2, nn) + _bf16_dot(ds_prev, kp2, nn)
        add(dk_cur, grp, _bf16_dot(ds_own, q4, tn))
        add(dk_prev, grp, _bf16_dot(ds_prev, q4, tn))
        for g in range(GROUP):
            h = GROUP * kv + g
            rows = slice(g * BLOCK, (g + 1) * BLOCK)
            dq2 = jnp.where(keep, dq4[rows], 0.0)
            dq_heads[h] = dq2 if (h % 2 == 0) == kv_low else pltpu.roll(dq2, HEAD_DIM, 1)
            d_bias[h] = ds[rows]
            d_sinks[h] = jnp.sum(d_sink4[rows], axis=0, keepdims=True)
    dqn = jnp.concatenate([dq_heads[2 * p] + dq_heads[2 * p + 1] for p in range(n_q_heads // 2)], axis=1)
    dq, dqg = q_vjp(dqn)
    dkp, dkg_prev = kp_vjp(jnp.concatenate(dk_prev, axis=1))
    dkc, dkg_cur = kc_vjp(jnp.concatenate(dk_cur, axis=1))
    return (dq, dkp, dkc, jnp.concatenate(dv_prev, axis=1), jnp.concatenate(dv_cur, axis=1),
            dqg, dkg_prev + dkg_cur, d_sinks, d_bias)


def _bias_table():
    i, j = np.arange(BLOCK)[:, None], np.arange(BLOCK)[None, :]
    n = np.where(j <= i, i - j, i + BLOCK - j)
    max_exact = NUM_BUCKETS // 2
    nf = np.maximum(n, 1).astype(np.float64)
    large = max_exact + (np.log(nf / max_exact) / math.log(MAX_DISTANCE / max_exact) * (NUM_BUCKETS - max_exact)).astype(np.int32)
    large = np.minimum(large, NUM_BUCKETS - 1)
    return np.where(n < max_exact, n, large).astype(np.int32)


def _bias_fwd(name, rel_bias, buckets):
    nb_, nh = rel_bias.shape

    def body(rb_ref, bk_ref, o_ref):
        bk = bk_ref[...]
        own = _iota(bk.shape, 1) <= _iota(bk.shape, 0)
        for h in range(nh):
            acc = jnp.zeros(bk.shape, F32)
            for b in range(nb_):
                acc = jnp.where(bk == b, rb_ref[b, h], acc)
            o_ref[1, h] = acc
            o_ref[0, h] = jnp.where(own, acc, NEG_INF)

    return pl.pallas_call(
        body, name=name,
        in_specs=[pl.BlockSpec(memory_space=pltpu.SMEM), pl.BlockSpec(memory_space=pltpu.VMEM)],
        out_specs=pl.BlockSpec(memory_space=pltpu.VMEM),
        out_shape=jax.ShapeDtypeStruct((2, nh) + buckets.shape, F32),
    )(rel_bias, buckets)


def _bias_bwd(name, dbias, buckets, nb_):
    nh = dbias.shape[0]

    def body(db_ref, bk_ref, o_ref):
        bk = bk_ref[...]
        for h in range(nh):
            d = db_ref[h]
            for b in range(nb_):
                s = jnp.sum(jnp.where(bk == b, d, 0.0), axis=0, keepdims=True)
                s = jnp.sum(s, axis=1, keepdims=True)
                o_ref[b * nh + h:b * nh + h + 1, :] = jnp.broadcast_to(s, (1, LANES))

    return pl.pallas_call(
        body, name=name,
        in_specs=[pl.BlockSpec(memory_space=pltpu.VMEM), pl.BlockSpec(memory_space=pltpu.VMEM)],
        out_specs=pl.BlockSpec(memory_space=pltpu.VMEM),
        out_shape=jax.ShapeDtypeStruct((nb_ * nh, LANES), F32),
    )(dbias, buckets)


SGU_CHUNKS_PER_STEP = 4


def _sgu_step(T):
    n = SGU_CHUNKS_PER_STEP if T % (SGU_CHUNKS_PER_STEP * CHUNK) == 0 else 1
    return n, [slice(c * CHUNK, (c + 1) * CHUNK) for c in range(n)]


def _sgu_fwd(name, z, g_row, w, b_full, W):
    T = z.shape[0]
    n, chunks = _sgu_step(T)

    def body(u_ref, v_ref, g_ref, w_ref, b_ref, o_ref):
        for rows in chunks:
            o_ref[rows, :] = _sgu_block(u_ref[rows, :].astype(F32), v_ref[rows, :].astype(F32), g_ref[...], w_ref[...],
                                        b_ref[...]).astype(BF16)

    return pl.pallas_call(
        body, name=name, grid=(T // (n * CHUNK),),
        in_specs=[_row_spec(n * CHUNK, W, 0), _row_spec(n * CHUNK, W, 1), _full_spec((1, W)),
                  _full_spec(w.shape), _full_spec((CHUNK, W))],
        out_specs=_row_spec(n * CHUNK, W),
        out_shape=jax.ShapeDtypeStruct((T, W), BF16), compiler_params=_params(1),
    )(z, z, g_row, w, b_full)


def _sgu_bwd(name, z, g_row, w, b_full, d_out, others, W):
    T = z.shape[0]
    n, chunks = _sgu_step(T)
    widths = [o.shape[1] for o in others]

    def body(u_ref, v_ref, g_ref, w_ref, b_ref, do_ref, *rest):
        other_refs, (dz_ref, dg_ref, dw_ref, db_ref) = rest[:len(others)], rest[len(others):]
        dg = dw = db = None
        for rows in chunks:
            _, vjp = jax.vjp(_sgu_block, u_ref[rows, :].astype(F32), v_ref[rows, :].astype(F32), g_ref[...], w_ref[...],
                             b_ref[...])
            du, dv, dg_c, dw_c, db_c = vjp(do_ref[rows, :])
            dz_ref[rows, :W] = du.astype(BF16)
            dz_ref[rows, W:2 * W] = dv.astype(BF16)
            dg, dw, db = (dg_c, dw_c, db_c) if dg is None else (dg + dg_c, dw + dw_c, db + db_c)
        at = 2 * W
        for o_ref, width in zip(other_refs, widths):
            dz_ref[:, at:at + width] = o_ref[...]
            at += width
        i = pl.program_id(0)

        @pl.when(i == 0)
        def _():
            dg_ref[...] = jnp.zeros_like(dg_ref)
            dw_ref[...] = jnp.zeros_like(dw_ref)
            db_ref[...] = jnp.zeros_like(db_ref)

        dg_ref[...] += dg
        dw_ref[...] += dw
        db_ref[...] += db

        @pl.when(i == pl.num_programs(0) - 1)
        def _():
            db_ref[...] = _head_sum_impl(db_ref[...])

    return pl.pallas_call(
        body, name=name, grid=(T // (n * CHUNK),),
        in_specs=[_row_spec(n * CHUNK, W, 0), _row_spec(n * CHUNK, W, 1), _full_spec((1, W)),
                  _full_spec(w.shape), _full_spec((CHUNK, W)), _row_spec(n * CHUNK, W)]
        + [_row_spec(n * CHUNK, width) for width in widths],
        out_specs=[_row_spec(n * CHUNK, 2 * W + sum(widths)), _full_spec((1, W)), _full_spec(w.shape),
                   _full_spec((CHUNK, W))],
        out_shape=[jax.ShapeDtypeStruct((T, 2 * W + sum(widths)), BF16), jax.ShapeDtypeStruct((1, W), F32),
                   jax.ShapeDtypeStruct(w.shape, F32), jax.ShapeDtypeStruct((CHUNK, W), F32)],
        compiler_params=_params(1),
    )(z, z, g_row, w, b_full, d_out, *others)


def _attn_specs(WQ, WKV, q_col, k_col, v_col, blk_of):
    prev_of = lambda i: jnp.maximum(blk_of(i) - 1, 0)
    return [pl.BlockSpec((BLOCK, WQ), lambda i: (blk_of(i), q_col)),
            pl.BlockSpec((BLOCK, WKV), lambda i: (prev_of(i), k_col)),
            pl.BlockSpec((BLOCK, WKV), lambda i: (blk_of(i), k_col)),
            pl.BlockSpec((BLOCK, WKV), lambda i: (prev_of(i), v_col)),
            pl.BlockSpec((BLOCK, WKV), lambda i: (blk_of(i), v_col))]


def _bias_spec(bias, blk_of):
    return pl.BlockSpec((1,) + bias.shape[1:], lambda i: (jnp.minimum(blk_of(i), 1), 0, 0, 0))


def _attn_fwd(name, z, qg_row, kg_row, sinks_col, bias, WQ, WKV, q_col, k_col, v_col):
    T = z.shape[0]
    nh = sinks_col.shape[0]

    def body(q_ref, kp_ref, kc_ref, vp_ref, vc_ref, qg_ref, kg_ref, s_ref, b_ref, o_ref):
        sinks = [s_ref[h:h + 1, :] for h in range(nh)]
        q, kp, kc, vp, vc = (r[...].astype(F32) for r in (q_ref, kp_ref, kc_ref, vp_ref, vc_ref))
        o_ref[...] = _attn_block(q, kp, kc, vp, vc, qg_ref[...], kg_ref[...], sinks, b_ref[0]).astype(BF16)

    return pl.pallas_call(
        body, name=name, grid=(T // BLOCK,),
        in_specs=_attn_specs(WQ, WKV, q_col, k_col, v_col, lambda i: i)
        + [_full_spec((1, WQ)), _full_spec((1, WKV)), _full_spec((nh, 1)), _bias_spec(bias, lambda i: i)],
        out_specs=_row_spec(BLOCK, WQ),
        out_shape=jax.ShapeDtypeStruct((T, WQ), BF16), compiler_params=_params(1),
    )(z, z, z, z, z, qg_row, kg_row, sinks_col, bias)


def _attn_bwd(name, z, qg_row, kg_row, sinks_col, bias, d_out, WQ, WKV, q_col, k_col, v_col):
    T = z.shape[0]
    nblk = T // BLOCK
    nh = sinks_col.shape[0]
    blk_of = lambda i: nblk - 1 - i

    def body(q_ref, kp_ref, kc_ref, vp_ref, vc_ref, qg_ref, kg_ref, s_ref, b_ref, do_ref,
             dq_ref, dk_ref, dv_ref, dqg_ref, dkg_ref, ds_ref, db_ref, carry_k, carry_v):
        i = pl.program_id(0)
        sinks = [s_ref[h:h + 1, :] for h in range(nh)]
        q, kp, kc, vp, vc = (r[...].astype(F32) for r in (q_ref, kp_ref, kc_ref, vp_ref, vc_ref))
        dq, dkp, dkc, dvp, dvc, dqg, dkg, dsk, dbs = _attn_block_bwd(
            q, kp, kc, vp, vc, qg_ref[...], kg_ref[...], sinks, b_ref[0], do_ref[...])

        @pl.when(i == 0)
        def _():
            carry_k[...] = jnp.zeros_like(carry_k)
            carry_v[...] = jnp.zeros_like(carry_v)
            dqg_ref[...] = jnp.zeros_like(dqg_ref)
            dkg_ref[...] = jnp.zeros_like(dkg_ref)
            ds_ref[...] = jnp.zeros_like(ds_ref)
            db_ref[...] = jnp.zeros_like(db_ref)

        dq_ref[...] = dq.astype(BF16)
        dk_ref[...] = (dkc + carry_k[...]).astype(BF16)
        dv_ref[...] = (dvc + carry_v[...]).astype(BF16)
        carry_k[...] = dkp
        carry_v[...] = dvp
        dqg_ref[...] += dqg
        dkg_ref[...] += dkg
        for h in range(nh):
            ds_ref[h:h + 1, :] += dsk[h]
            db_ref[h] += dbs[h]

    return pl.pallas_call(
        body, name=name, grid=(nblk,),
        in_specs=_attn_specs(WQ, WKV, q_col, k_col, v_col, blk_of)
        + [_full_spec((1, WQ)), _full_spec((1, WKV)), _full_spec((nh, 1)), _bias_spec(bias, blk_of),
           pl.BlockSpec((BLOCK, WQ), lambda i: (blk_of(i), 0))],
        out_specs=[pl.BlockSpec((BLOCK, WQ), lambda i: (blk_of(i), 0)),
                   pl.BlockSpec((BLOCK, WKV), lambda i: (blk_of(i), 0)),
                   pl.BlockSpec((BLOCK, WKV), lambda i: (blk_of(i), 0)),
                   _full_spec((1, WQ)), _full_spec((1, WKV)), _full_spec((nh, 1)), _full_spec(bias.shape[1:])],
        out_shape=[jax.ShapeDtypeStruct((T, WQ), BF16), jax.ShapeDtypeStruct((T, WKV), BF16),
                   jax.ShapeDtypeStruct((T, WKV), BF16), jax.ShapeDtypeStruct((1, WQ), F32),
                   jax.ShapeDtypeStruct((1, WKV), F32), jax.ShapeDtypeStruct((nh, 1), F32),
                   jax.ShapeDtypeStruct(bias.shape[1:], F32)],
        scratch_shapes=[pltpu.VMEM((BLOCK, WKV), F32), pltpu.VMEM((BLOCK, WKV), F32)],
        compiler_params=_params(1),
    )(z, z, z, z, z, qg_row, kg_row, sinks_col, bias, d_out)


def _adamw(name, w, g, m, v):
    shape = w.shape
    C = shape[-1]
    if w.ndim == 3 and shape[1] % ROWS == 0:
        work = shape
        grid = (shape[0], shape[1] // ROWS)
        spec = pl.BlockSpec((1, ROWS, C), lambda l, i: (l, i, 0))
    else:
        R = int(np.prod(shape[:-1]))
        tr = ROWS if R % ROWS == 0 else R
        work = (R, C)
        grid = (R // tr,)
        spec = pl.BlockSpec((tr, C), lambda i: (i, 0))
    w2, g2, m2, v2 = (t.reshape(work) for t in (w, g, m, v))

    def body(w_ref, g_ref, m_ref, v_ref, d_ref, nm_ref, nv_ref):
        gv = g_ref[...]
        nm = ADAM_B1 * m_ref[...] + (1.0 - ADAM_B1) * gv
        nv = ADAM_B2 * v_ref[...] + (1.0 - ADAM_B2) * (gv * gv)
        m_hat = nm / (1.0 - ADAM_B1 ** ADAM_STEP)
        v_hat = nv / (1.0 - ADAM_B2 ** ADAM_STEP)
        d_ref[...] = -ADAM_LR * (m_hat / (jnp.sqrt(v_hat) + ADAM_EPS) + ADAM_WD * w_ref[...])
        nm_ref[...] = nm
        nv_ref[...] = nv

    outs = pl.pallas_call(
        body, name=name, grid=grid, in_specs=[spec] * 4, out_specs=[spec] * 3,
        out_shape=[jax.ShapeDtypeStruct(work, F32)] * 3, compiler_params=_params(len(grid)),
    )(w2, g2, m2, v2)
    return tuple(o.reshape(shape) for o in outs)


def _pad_rows(flat):
    n = flat.shape[0]
    tile = 8 * LANES
    padded = -(-n // tile) * tile
    return jnp.pad(flat, (0, padded - n)).reshape(padded // LANES, LANES)


def kernel(x, rel_bias, norm1_g, w_in, sgu_norm_g, sgu_w, sgu_b, q_norm_g, k_norm_g, sinks, out_norm_a, out_norm_b, w_out, norm2_g, w_gate, w_up, w_down, loss_target, m_rel_bias, m_norm1_g, m_w_in, m_sgu_norm_g, m_sgu_w, m_sgu_b, m_q_norm_g, m_k_norm_g, m_sinks, m_out_norm_a, m_out_norm_b, m_w_out, m_norm2_g, m_w_gate, m_w_up, m_w_down, v_rel_bias, v_norm1_g, v_w_in, v_sgu_norm_g, v_sgu_w, v_sgu_b, v_q_norm_g, v_k_norm_g, v_sinks, v_out_norm_a, v_out_norm_b, v_w_out, v_norm2_g, v_w_gate, v_w_up, v_w_down):
    L = w_in.shape[0]
    T, D = x.shape[1], x.shape[2]
    W = D // 2
    NH = W // HEAD_DIM
    WKV = N_KV_HEADS * HEAD_DIM
    IN = N_DEV * w_in.shape[2]
    FF = N_DEV * w_gate.shape[2]
    assert IN == 2 * W + W + 2 * WKV and NH // N_KV_HEADS == GROUP
    q_col, k_col, v_col = 2 * W // W, (3 * W) // WKV, (3 * W + WKV) // WKV
    x0 = x.reshape(T, D)
    target = loss_target.reshape(T, D)

    shards = [jnp.swapaxes(w_in, 1, 2).astype(BF16), w_out.astype(BF16), jnp.swapaxes(w_gate, 1, 2).astype(BF16),
              jnp.swapaxes(w_up, 1, 2).astype(BF16), w_down.astype(BF16)]
    me = 4 * lax.axis_index("x") + 2 * lax.axis_index("y") + lax.axis_index("c")

    def landing(block):
        return lax.dynamic_update_slice(lax.empty((N_DEV,) + block.shape, block.dtype), block[None], (me, 0, 0))

    def as_matrices(got):
        return [g.reshape(N_DEV * g.shape[1], g.shape[2]) for g in got]

    groups = ((0,), (1,), (2, 3, 4))
    full = {l: [None] * len(shards) for l in range(L)}
    full[0][0] = as_matrices(_all_gather("gather_weights_0", [shards[0][0]]))[0]
    flight = {}

    def start_gather(l, first_group, after):
        for g in range(first_group, len(groups)):
            srcs = [shards[i][l] for i in groups[g]]
            flight[l, g] = _exchange_start(f"gather_start_{l}_{g}", "gather", srcs, [landing(s) for s in srcs], after)
            after = flight[l, g][4]
        return after

    def finish_gather(l, g, after):
        s_sem, r_sem, srcs, lands, _ = flight.pop((l, g))
        lands = _exchange_wait(f"gather_wait_{l}_{g}", "gather", s_sem, r_sem, srcs, lands, after)[1]
        for i, mat in zip(groups[g], as_matrices(lands)):
            full[l][i] = mat

    buckets = jnp.asarray(_bias_table())
    bias = _bias_fwd("bias_table", rel_bias, buckets)

    saved = []
    xl = x0
    token = start_gather(0, 1, full[0][0])
    for l in range(L):
        if l > 0:
            finish_gather(l, 0, xl)
        w_in_t = full[l][0]
        g1, g2 = norm1_g[l][None], norm2_g[l][None]
        sg_row = sgu_norm_g[l].reshape(1, W)
        b_full = jnp.repeat(sgu_b[l].T, HEAD_DIM, axis=1)
        qg_row = jnp.tile(q_norm_g[l], NH)[None]
        kg_row = jnp.tile(k_norm_g[l], N_KV_HEADS)[None]
        sinks_col = sinks[l][:, None]
        ga, gb = out_norm_a[l][None], out_norm_b[l][None]

        h = _rms_fwd(f"norm1_{l}", xl, g1)
        (z,) = _mm(f"proj_in_{l}", "nt", [h], [w_in_t], [(0, 0, 0)], 1, T, IN, D, 1024, 896, D, _ep_plain, [BF16],
                   after=token)
        out_a = _sgu_fwd(f"sgu_{l}", z, sg_row, sgu_w[l], b_full, W)
        out_b = _attn_fwd(f"attn_{l}", z, qg_row, kg_row, sinks_col, bias, W, WKV, q_col, k_col, v_col)
        mixed = _outnorm_fwd(f"outnorm_{l}", out_a, out_b, ga, gb)
        finish_gather(l, 1, mixed)
        w_o = full[l][1]
        (x1,) = _mm(f"proj_out_{l}", "nn", [mixed], [w_o], [(0, 0, 0)], 1, T, D, D, 1024, 1024, D,
                    _ep_residual, [F32], extras=[xl])
        h2 = _rms_fwd(f"norm2_{l}", x1, g2)
        token = start_gather(l + 1, 0, h2) if l + 1 < L else h2
        finish_gather(l, 2, h2)
        w_g_t, w_u_t, w_d = full[l][2:]
        gate, up, act = _mm(f"mlp_in_{l}", "nt", [h2], [w_g_t, w_u_t], [(0, 0, 0), (0, 1, 1)], 2, T, FF, D,
                            512, 1408, D, _ep_swiglu, [BF16, BF16, BF16], after=token)
        (x2,) = _mm(f"mlp_out_{l}", "nn", [act], [w_d], [(0, 0, 0)], 1, T, D, FF, 512, 1024, FF,
                    _ep_residual, [F32], extras=[x1])
        saved.append((xl, h, z, out_a, out_b, mixed, x1, h2, gate, up, act,
                      g1, g2, sg_row, b_full, qg_row, kg_row, sinks_col, ga, gb))
        xl = x2

    loss_part, dy, dy16 = _loss_head("loss_head", xl, target)

    dbias = None
    small = [None] * L
    scatters = []

    def start_scatter(name, which, grads_t, after):
        srcs = [t.reshape(N_DEV, t.shape[0] // N_DEV, D) for t in grads_t]
        lands = [lax.empty((N_DEV - 1,) + s.shape[1:], BF16) for s in srcs]
        s_sem, r_sem, srcs, lands, tok = _exchange_start(name, "scatter", srcs, lands, after)
        scatters.append((name, which, s_sem, r_sem, srcs, lands))
        return tok

    small_names = ["norm1_g", "sgu_norm_g", "sgu_w", "sgu_b", "q_norm_g", "k_norm_g", "sinks", "out_norm_a",
                   "out_norm_b", "norm2_g"]
    small_flights = {}

    def start_small(l, parts, after):
        order = sorted(parts, key=(["rel_bias"] + small_names).index)
        pieces = [_pad_rows(parts[nme].reshape(-1)) for nme in order]
        n_rows = sum(p.shape[0] for p in pieces)
        if n_rows % 64:
            pieces.append(jnp.zeros((-n_rows % 64, LANES), F32))
        packed = jnp.concatenate(pieces, axis=0)
        flight = _exchange_start(f"gather_small_start_{l}", "gather", [packed], [landing(packed)], after)
        small_flights[l] = (flight, [(nme, parts[nme].shape) for nme in order])
        return flight[4]

    def finish_small(l, after):
        (s_sem, r_sem, srcs, lands, _), layout = small_flights[l]
        _, (everyone,) = _exchange_wait(f"gather_small_wait_{l}", "gather", s_sem, r_sem, srcs, lands, after)
        summed = _sum_slots(f"sum_small_grads_{l}", everyone, 64)
        out, at = {}, 0
        for nme, shape in layout:
            n = int(np.prod(shape))
            n_rows = -(-n // (8 * LANES)) * 8
            out[nme] = summed[at:at + n_rows].reshape(-1)[:n].reshape(shape)
            at += n_rows
        return out

    token = None
    for l in reversed(range(L)):
        w_in_t, w_o, w_g_t, w_u_t, w_d = full[l]
        (xl, h, z, out_a, out_b, mixed, x1, h2, gate, up, act,
         g1, g2, sg_row, b_full, qg_row, kg_row, sinks_col, ga, gb) = saved[l]

        dgate, dup = _mm(f"d_mlp_out_{l}", "nt", [dy16], [w_d], [(0, 0, 0)], 1, T, FF, D, 1024, 1408, D,
                         _ep_swiglu_bwd, [BF16, BF16], extras=[gate, up], after=token)
        (dw_d,) = _mm(f"dw_down_{l}", "tn", [act], [dy16], [(0, 0, 0)], 1, FF, D, T, 1408, 1024, 2048, _ep_plain, [BF16])
        (dh2,) = _mm(f"d_mlp_in_{l}", "nn", [dgate, dup], [w_g_t, w_u_t], [(0, 0, 0), (1, 1, 0)], 1, T, D, FF,
                     512, 512, FF, _ep_plain, [F32])
        dx1, dx1_16, dg2 = _rms_bwd(f"d_norm2_{l}", dh2, x1, g2, dy)
        dw_g, dw_u = _mm(f"dw_gate_up_{l}", "tn", [dgate, dup], [h2], [(0, 0, 0), (1, 0, 1)], 2, FF, D, T,
                         1408, 1024, 1024, _ep_two, [BF16, BF16])
        token = start_scatter(f"scatter_mlp_start_{l}", (l, (4, 2, 3)), [dw_d, dw_g, dw_u], dx1_16)
        (dmixed,) = _mm(f"d_proj_out_{l}", "nt", [dx1_16], [w_o], [(0, 0, 0)], 1, T, D, D, 1024, 1024, D, _ep_plain, [F32],
                        after=token)
        (dw_o,) = _mm(f"dw_out_{l}", "tn", [mixed], [dx1_16], [(0, 0, 0)], 1, D, D, T, 1024, 1024, 2048, _ep_plain, [BF16])
        d_a, d_b, dga, dgb = _outnorm_bwd(f"d_outnorm_{l}", dmixed, out_a, out_b, ga, gb)
        dq, dk, dv, dqg, dkg, dsk, dbs = _attn_bwd(f"d_attn_{l}", z, qg_row, kg_row, sinks_col, bias, d_b,
                                                   W, WKV, q_col, k_col, v_col)
        dbias = dbs if dbias is None else dbias + dbs
        dz, dsg, dsw, dsb = _sgu_bwd(f"d_sgu_{l}", z, sg_row, sgu_w[l], b_full, d_a, [dq, dk, dv], W)
        (dh,) = _mm(f"d_proj_in_{l}", "nn", [dz], [w_in_t], [(0, 0, 0)], 1, T, D, IN, 1024, 1024, IN, _ep_plain, [F32])
        dy, dy16, dg1 = _rms_bwd(f"d_norm1_{l}", dh, xl, g1, dx1)
        (dw_i,) = _mm(f"dw_in_{l}", "tn", [dz], [h], [(0, 0, 0)], 1, IN, D, T, 896, 1024, 2048, _ep_plain, [BF16])
        token = start_scatter(f"scatter_mix_start_{l}", (l, (1, 0)), [dw_o, dw_i], dy16)

        small[l] = dict(norm1_g=dg1[0], sgu_norm_g=dsg.reshape(NH, HEAD_DIM), sgu_w=dsw,
                        sgu_b=dsb[:, ::HEAD_DIM].T, q_norm_g=dqg.reshape(NH, HEAD_DIM).sum(0),
                        k_norm_g=dkg.reshape(N_KV_HEADS, HEAD_DIM).sum(0), sinks=dsk[:, 0],
                        out_norm_a=dga[0], out_norm_b=dgb[0], norm2_g=dg2[0])
        if l > 0:
            token = start_small(l, small[l], token)

    grad_x = dy.reshape(x.shape)
    d_rel = _bias_bwd("d_bias_table", dbias, buckets, NUM_BUCKETS)[:, 0].reshape(NUM_BUCKETS, NH)
    after = start_small(0, dict(small[0], rel_bias=d_rel), token)

    me1 = me.reshape(1).astype(jnp.int32)
    grads_big = {}

    def finish_scatter(entry, after):
        name, (l, which), s_sem, r_sem, srcs, lands = entry
        srcs, lands = _exchange_wait(name.replace("start", "wait"), "scatter", s_sem, r_sem, srcs, lands, after)
        for i, src, land in zip(which, srcs, lands):
            rows = src.shape[1]
            after = _sum_parts(f"sum_grads_{l}_{i}", me1, src, land, 64 if rows % 64 == 0 else rows)
            grads_big[i, l] = after
        return after

    def stacked(i):
        return jnp.stack([grads_big[i, l] for l in range(L)])

    loss = lax.psum(loss_part[0, 0], ("x", "y", "c"))
    tr = lambda t: jnp.swapaxes(t, 1, 2)
    weights = dict(rel_bias=rel_bias, norm1_g=norm1_g, w_in=w_in, sgu_norm_g=sgu_norm_g, sgu_w=sgu_w, sgu_b=sgu_b,
                   q_norm_g=q_norm_g, k_norm_g=k_norm_g, sinks=sinks, out_norm_a=out_norm_a, out_norm_b=out_norm_b,
                   w_out=w_out, norm2_g=norm2_g, w_gate=w_gate, w_up=w_up, w_down=w_down)
    ms = dict(rel_bias=m_rel_bias, norm1_g=m_norm1_g, w_in=m_w_in, sgu_norm_g=m_sgu_norm_g, sgu_w=m_sgu_w, sgu_b=m_sgu_b,
              q_norm_g=m_q_norm_g, k_norm_g=m_k_norm_g, sinks=m_sinks, out_norm_a=m_out_norm_a, out_norm_b=m_out_norm_b,
              w_out=m_w_out, norm2_g=m_norm2_g, w_gate=m_w_gate, w_up=m_w_up, w_down=m_w_down)
    vs = dict(rel_bias=v_rel_bias, norm1_g=v_norm1_g, w_in=v_w_in, sgu_norm_g=v_sgu_norm_g, sgu_w=v_sgu_w, sgu_b=v_sgu_b,
              q_norm_g=v_q_norm_g, k_norm_g=v_k_norm_g, sinks=v_sinks, out_norm_a=v_out_norm_a, out_norm_b=v_out_norm_b,
              w_out=v_w_out, norm2_g=v_norm2_g, w_gate=v_w_gate, w_up=v_w_up, w_down=v_w_down)
    all_names = ["rel_bias", "norm1_g", "w_in", "sgu_norm_g", "sgu_w", "sgu_b", "q_norm_g", "k_norm_g", "sinks",
                 "out_norm_a", "out_norm_b", "w_out", "norm2_g", "w_gate", "w_up", "w_down"]
    transposed = ("w_in", "w_gate", "w_up")
    grads, deltas, new_m, new_v = {}, {}, {}, {}

    def update(nme, g):
        if nme in transposed:
            outs = _adamw(f"adamw_{nme}", tr(weights[nme]), g, tr(ms[nme]), tr(vs[nme]))
            grads[nme], (deltas[nme], new_m[nme], new_v[nme]) = tr(g), [tr(o) for o in outs]
        else:
            grads[nme] = g
            deltas[nme], new_m[nme], new_v[nme] = _adamw(f"adamw_{nme}", weights[nme], g, ms[nme], vs[nme])
        return new_v[nme]

    for entry in scatters[:-1]:
        after = finish_scatter(entry, after)
    for nme, i in (("w_gate", 2), ("w_up", 3), ("w_down", 4)):
        after = update(nme, stacked(i))

    small_sums = {}
    for l in reversed(range(L)):
        small_sums[l] = finish_small(l, after)
    after = update("rel_bias", small_sums[0]["rel_bias"])
    for nme in small_names:
        after = update(nme, jnp.stack([small_sums[l][nme] for l in range(L)]))

    finish_scatter(scatters[-1], after)
    update("w_out", stacked(1))
    update("w_in", stacked(0))
    return (loss, grad_x, *[grads[nme] for nme in all_names], *[deltas[nme] for nme in all_names],
            *[new_m[nme] for nme in all_names], *[new_v[nme] for nme in all_names])
```

```python
import functools
import math

import numpy as np
import jax
import jax.numpy as jnp
from jax import lax
from jax.experimental import pallas as pl
from jax.experimental.pallas import tpu as pltpu

F32 = jnp.float32
BF16 = jnp.bfloat16

N_DEV = 8
HEAD_DIM = 64
CHUNK = 128
BLOCK = 128
N_KV_HEADS = 4
GROUP = 4
NUM_BUCKETS = 32
MAX_DISTANCE = 128
EPS = 1e-6
NEG_INF = -1e30
LANES = 128
VMEM_LIMIT = 56 * 2 ** 20

ADAM_LR = 0.001
ADAM_B1 = 0.9
ADAM_B2 = 0.999
ADAM_EPS = 1e-08
ADAM_WD = 0.01
ADAM_STEP = 10

MESH = pl.DeviceIdType.MESH
ANY = pl.BlockSpec(memory_space=pl.ANY)
HBM = pl.BlockSpec(memory_space=pltpu.HBM)
SEM = pl.BlockSpec(memory_space=pltpu.SEMAPHORE)
EFFECT = pltpu.SideEffectType.DATAFLOW_SIDE_EFFECTING


def _params(n_axes):
    return pltpu.CompilerParams(dimension_semantics=("arbitrary",) * n_axes, vmem_limit_bytes=VMEM_LIMIT)


def _my_place():
    return lax.axis_index("x"), lax.axis_index("y"), lax.axis_index("c")


def _all_gather(name, arrs):
    n = len(arrs)

    def body(*refs):
        ins, outs = refs[:n], refs[n:2 * n]
        send_sems, recv_sems, local_sems = refs[2 * n:]
        x, y, c = _my_place()
        sibling = (x, y, 1 - c)
        chips = [(1 - x, y), (x, 1 - y), (1 - x, 1 - y)]

        def slot(a, px, py, pc):
            return outs[a].at[4 * px + 2 * py + pc]

        def copy(a, k, block, to, src=None):
            return pltpu.make_async_remote_copy(
                src_ref=slot(a, *block) if src is None else src, dst_ref=slot(a, *block),
                send_sem=send_sems.at[7 * a + k], recv_sem=recv_sems.at[7 * a + k],
                device_id=to, device_id_type=MESH)

        started = []
        for a in range(n):
            mine = pltpu.make_async_copy(ins[a], slot(a, x, y, c), local_sems.at[a])
            mine.start()
            started.append(mine)
        sends = []
        for a in range(n):
            first = [copy(a, 0, (x, y, c), sibling, src=ins[a])]
            first += [copy(a, 1 + j, (x, y, c), (*chip, c), src=ins[a]) for j, chip in enumerate(chips)]
            for cp in first:
                cp.start()
            sends += first
        for a in range(n):
            for j, chip in enumerate(chips):
                copy(a, 1 + j, (*chip, c), (x, y, c)).wait_recv()
                fwd = copy(a, 4 + j, (*chip, c), sibling)
                fwd.start()
                sends.append(fwd)
        for a in range(n):
            copy(a, 0, sibling, (x, y, c)).wait_recv()
            for j, chip in enumerate(chips):
                copy(a, 4 + j, (*chip, 1 - c), (x, y, c)).wait_recv()
        for cp in sends:
            cp.wait_send()
        for cp in started:
            cp.wait()

    return pl.pallas_call(
        body, name=name,
        out_shape=[jax.ShapeDtypeStruct((N_DEV,) + a.shape, a.dtype) for a in arrs],
        in_specs=[ANY] * n, out_specs=[ANY] * n,
        scratch_shapes=[pltpu.SemaphoreType.DMA((7 * n,)), pltpu.SemaphoreType.DMA((7 * n,)),
                        pltpu.SemaphoreType.DMA((n,))],
    )(*arrs)


def _peer(k, x, y, c):
    return (1 - x if k & 4 else x), (1 - y if k & 2 else y), (1 - c if k & 1 else c)


PEER_ORDER = (1, 2, 4, 3, 5, 6, 7)


def _exchange_copy(kind, k, a, srcs, lands, send_sems, recv_sems, arriving=False):
    x, y, c = _my_place()
    me = 4 * x + 2 * y + c
    px, py, pc = _peer(k, x, y, c)
    them = 4 * px + 2 * py + pc
    if kind == "gather":
        src, dst_there, dst_here = srcs[a], lands[a].at[me], lands[a].at[them]
    else:
        src, dst_there, dst_here = srcs[a].at[them], lands[a].at[k - 1], lands[a].at[k - 1]
    return pltpu.make_async_remote_copy(
        src_ref=src, dst_ref=dst_here if arriving else dst_there,
        send_sem=send_sems.at[7 * a + k - 1], recv_sem=recv_sems.at[7 * a + k - 1],
        device_id=(px, py, pc), device_id_type=MESH)


def _exchange_start(name, kind, srcs, lands, after):
    n = len(srcs)

    def body(*refs):
        ins, lnd = refs[:n], refs[n:2 * n]
        send_sems, recv_sems = refs[2 * n + 1], refs[2 * n + 2]
        token = refs[-1]
        for k in PEER_ORDER:
            for a in range(n):
                _exchange_copy(kind, k, a, ins, lnd, send_sems, recv_sems).start()
        token[...] = jnp.zeros_like(token)

    hbm = lambda t: pltpu.with_memory_space_constraint(t, pltpu.HBM)
    out = pl.pallas_call(
        body, name=name,
        out_shape=(pltpu.SemaphoreType.DMA((7 * n,)), pltpu.SemaphoreType.DMA((7 * n,)),
                   *[pltpu.HBM(t.shape, t.dtype) for t in srcs], *[pltpu.HBM(t.shape, t.dtype) for t in lands],
                   jax.ShapeDtypeStruct((8, LANES), F32)),
        in_specs=[HBM] * (2 * n) + [ANY],
        out_specs=(SEM, SEM, *[HBM] * (2 * n), pl.BlockSpec(memory_space=pltpu.VMEM)),
        input_output_aliases={i: 2 + i for i in range(2 * n)},
        compiler_params=pltpu.CompilerParams(has_side_effects=EFFECT),
    )(*[hbm(t) for t in srcs], *[hbm(t) for t in lands], after)
    return out[0], out[1], list(out[2:2 + n]), list(out[2 + n:2 + 2 * n]), out[-1]


def _exchange_wait(name, kind, send_sems, recv_sems, srcs, lands, after):
    n = len(srcs)

    def body(*refs):
        ins, lnd = refs[:n], refs[n:2 * n]
        s_sems, r_sems = refs[2 * n], refs[2 * n + 1]
        for a in range(n):
            for k in PEER_ORDER:
                _exchange_copy(kind, k, a, ins, lnd, s_sems, r_sems).wait_send()
                _exchange_copy(kind, k, a, ins, lnd, s_sems, r_sems, arriving=True).wait_recv()

    out = pl.pallas_call(
        body, name=name,
        out_shape=(*[pltpu.HBM(t.shape, t.dtype) for t in srcs], *[pltpu.HBM(t.shape, t.dtype) for t in lands]),
        in_specs=[HBM] * (2 * n) + [SEM, SEM, ANY],
        out_specs=tuple([HBM] * (2 * n)),
        input_output_aliases={i: i for i in range(2 * n)},
        compiler_params=pltpu.CompilerParams(has_side_effects=EFFECT),
    )(*srcs, *lands, send_sems, recv_sems, after)
    return list(out[:n]), list(out[n:])


def _sum_parts(name, me, mine, parts, rows):
    _, R, C = mine.shape

    def body(me_ref, own_ref, p_ref, o_ref):
        acc = own_ref[0].astype(F32)
        for s in range(N_DEV - 1):
            acc = acc + p_ref[s].astype(F32)
        o_ref[...] = acc

    return pl.pallas_call(
        body, name=name,
        grid_spec=pltpu.PrefetchScalarGridSpec(
            num_scalar_prefetch=1, grid=(R // rows,),
            in_specs=[pl.BlockSpec((1, rows, C), lambda i, me_ref: (me_ref[0], i, 0)),
                      pl.BlockSpec((N_DEV - 1, rows, C), lambda i, me_ref: (0, i, 0))],
            out_specs=pl.BlockSpec((rows, C), lambda i, me_ref: (i, 0))),
        out_shape=jax.ShapeDtypeStruct((R, C), F32), compiler_params=_params(1),
    )(me, mine, parts)


def _sum_slots(name, a, rows):
    _, R, C = a.shape

    def body(a_ref, o_ref):
        acc = a_ref[0].astype(F32)
        for s in range(1, N_DEV):
            acc = acc + a_ref[s].astype(F32)
        o_ref[...] = acc

    return pl.pallas_call(
        body, name=name, grid=(R // rows,),
        in_specs=[pl.BlockSpec((N_DEV, rows, C), lambda i: (0, i, 0))],
        out_specs=pl.BlockSpec((rows, C), lambda i: (i, 0)),
        out_shape=jax.ShapeDtypeStruct((R, C), F32), compiler_params=_params(1),
    )(a)


_DIMS = {"nn": (((1,), (0,)), ((), ())), "nt": (((1,), (1,)), ((), ())), "tn": (((0,), (0,)), ((), ()))}


def _mm(name, mode, a_list, b_list, pairs, n_acc, M, N, K, tm, tn, tk, epilogue, out_dtypes, extras=(), after=None):
    tm, tn, tk = min(tm, M), min(tn, N), min(tk, K)
    assert M % tm == 0 and N % tn == 0 and K % tk == 0, (name, M, N, K, tm, tn, tk)
    nk = K // tk
    na, nb, ne, no = len(a_list), len(b_list), len(extras), len(out_dtypes)
    dims = _DIMS[mode]
    a_spec = (pl.BlockSpec((tk, tm), lambda j, i, k: (k, i)) if mode == "tn"
              else pl.BlockSpec((tm, tk), lambda j, i, k: (i, k)))
    b_spec = (pl.BlockSpec((tn, tk), lambda j, i, k: (j, k)) if mode == "nt"
              else pl.BlockSpec((tk, tn), lambda j, i, k: (k, j)))
    o_spec = pl.BlockSpec((tm, tn), lambda j, i, k: (i, j))
    tail = [] if after is None else [after]

    def body(*refs):
        a_refs, b_refs = refs[:na], refs[na:na + nb]
        e_refs = refs[na + nb:na + nb + ne]
        first_out = na + nb + ne + len(tail)
        o_refs = refs[first_out:first_out + no]
        acc_refs = refs[first_out + no:]

        def partial(oi):
            tot = None
            for ai, bi, ti in pairs:
                if ti == oi:
                    d = lax.dot_general(a_refs[ai][...], b_refs[bi][...], dims, preferred_element_type=F32)
                    tot = d if tot is None else tot + d
            return tot

        def finish(accs):
            outs = epilogue(accs, [e[...] for e in e_refs])
            for o_ref, o in zip(o_refs, outs):
                o_ref[...] = o.astype(o_ref.dtype)

        if nk == 1:
            finish([partial(oi) for oi in range(n_acc)])
        else:
            k = pl.program_id(2)

            @pl.when(k == 0)
            def _():
                for r in acc_refs:
                    r[...] = jnp.zeros_like(r)

            for oi in range(n_acc):
                acc_refs[oi][...] += partial(oi)

            @pl.when(k == nk - 1)
            def _():
                finish([r[...] for r in acc_refs])

    return pl.pallas_call(
        body, name=name, grid=(N // tn, M // tm, nk),
        in_specs=[a_spec] * na + [b_spec] * nb + [o_spec] * ne + [ANY] * len(tail),
        out_specs=[o_spec] * no,
        out_shape=[jax.ShapeDtypeStruct((M, N), dt) for dt in out_dtypes],
        scratch_shapes=[pltpu.VMEM((tm, tn), F32)] * (n_acc if nk > 1 else 0),
        compiler_params=_params(3),
    )(*a_list, *b_list, *extras, *tail)


def _ep_residual(accs, ex):
    return (ex[0] + accs[0],)


def _ep_rows_rms_bwd(acc, ex, rows):
    xv, res = ex
    r = lax.rsqrt(jnp.mean(xv * xv, axis=-1, keepdims=True) + EPS)
    xh = xv * r
    dg = jnp.sum(acc * xh, axis=0, keepdims=True)
    dxh = acc * rows[0]
    dx = r * (dxh - xh * jnp.mean(dxh * xh, axis=-1, keepdims=True)) + res
    return (dx, dx), (dg,)


def _sigmoid(x):
    return 0.5 * (jnp.tanh(0.5 * x) + 1.0)


def _ep_plain(accs, ex):
    return (accs[0],)


def _ep_swiglu(accs, ex):
    g, u = accs
    return g, u, (g * _sigmoid(g)) * u


def _ep_swiglu_bwd(accs, ex):
    dact = accs[0]
    g, u = ex[0].astype(F32), ex[1].astype(F32)
    sig = _sigmoid(g)
    silu = g * sig
    return dact * u * (sig * (1.0 + g * (1.0 - sig))), dact * silu


def _ep_two(accs, ex):
    return accs[0], accs[1]


ROWS = 256


def _row_spec(tm, width, col=0):
    return pl.BlockSpec((tm, width), lambda i: (i, col))


def _full_spec(shape):
    nd = len(shape)
    return pl.BlockSpec(shape, lambda i: (0,) * nd)


def _rms_fwd(name, x, g):
    T, D = x.shape
    tm = min(ROWS, T)

    def body(x_ref, g_ref, o_ref):
        xv = x_ref[...]
        r = lax.rsqrt(jnp.mean(xv * xv, axis=-1, keepdims=True) + EPS)
        o_ref[...] = ((xv * r) * g_ref[...]).astype(BF16)

    return pl.pallas_call(
        body, name=name, grid=(T // tm,),
        in_specs=[_row_spec(tm, D), _full_spec((1, D))], out_specs=_row_spec(tm, D),
        out_shape=jax.ShapeDtypeStruct((T, D), BF16), compiler_params=_params(1),
    )(x, g)


def _rms_bwd(name, dyn, x, g, res):
    T, D = x.shape
    tm = min(ROWS, T)

    def body(dyn_ref, x_ref, g_ref, res_ref, dx_ref, dx16_ref, dg_ref):
        outs, (dg,) = _ep_rows_rms_bwd(dyn_ref[...], [x_ref[...], res_ref[...]], [g_ref[...]])

        @pl.when(pl.program_id(0) == 0)
        def _():
            dg_ref[...] = jnp.zeros_like(dg_ref)

        dg_ref[...] += dg
        dx_ref[...] = outs[0]
        dx16_ref[...] = outs[1].astype(BF16)

    return pl.pallas_call(
        body, name=name, grid=(T // tm,),
        in_specs=[_row_spec(tm, D), _row_spec(tm, D), _full_spec((1, D)), _row_spec(tm, D)],
        out_specs=[_row_spec(tm, D), _row_spec(tm, D), _full_spec((1, D))],
        out_shape=[jax.ShapeDtypeStruct((T, D), F32), jax.ShapeDtypeStruct((T, D), BF16),
                   jax.ShapeDtypeStruct((1, D), F32)],
        compiler_params=_params(1),
    )(dyn, x, g, res)


def _outnorm_fwd(name, a, b, ga, gb):
    T, W = a.shape
    tm = min(ROWS, T)

    def body(a_ref, b_ref, ga_ref, gb_ref, o_ref):
        for src, gain, lo in ((a_ref, ga_ref, 0), (b_ref, gb_ref, W)):
            v = src[...].astype(F32)
            r = lax.rsqrt(jnp.mean(v * v, axis=-1, keepdims=True) + EPS)
            o_ref[:, lo:lo + W] = ((v * r) * gain[...]).astype(BF16)

    return pl.pallas_call(
        body, name=name, grid=(T // tm,),
        in_specs=[_row_spec(tm, W), _row_spec(tm, W), _full_spec((1, W)), _full_spec((1, W))],
        out_specs=_row_spec(tm, 2 * W),
        out_shape=jax.ShapeDtypeStruct((T, 2 * W), BF16), compiler_params=_params(1),
    )(a, b, ga, gb)


def _outnorm_bwd(name, dmixed, a, b, ga, gb):
    T, W = a.shape
    tm = min(ROWS, T)

    def body(dm_ref, a_ref, b_ref, ga_ref, gb_ref, da_ref, db_ref, dga_ref, dgb_ref):
        first = pl.program_id(0) == 0
        for src, gain, lo, dsrc, dgain in ((a_ref, ga_ref, 0, da_ref, dga_ref), (b_ref, gb_ref, W, db_ref, dgb_ref)):
            v, dv = src[...].astype(F32), dm_ref[:, lo:lo + W]
            r = lax.rsqrt(jnp.mean(v * v, axis=-1, keepdims=True) + EPS)
            vh = v * r

            @pl.when(first)
            def _():
                dgain[...] = jnp.zeros_like(dgain)

            dgain[...] += jnp.sum(dv * vh, axis=0, keepdims=True)
            dvh = dv * gain[...]
            dsrc[...] = r * (dvh - vh * jnp.mean(dvh * vh, axis=-1, keepdims=True))

    return pl.pallas_call(
        body, name=name, grid=(T // tm,),
        in_specs=[_row_spec(tm, 2 * W), _row_spec(tm, W), _row_spec(tm, W), _full_spec((1, W)), _full_spec((1, W))],
        out_specs=[_row_spec(tm, W), _row_spec(tm, W), _full_spec((1, W)), _full_spec((1, W))],
        out_shape=[jax.ShapeDtypeStruct((T, W), F32), jax.ShapeDtypeStruct((T, W), F32),
                   jax.ShapeDtypeStruct((1, W), F32), jax.ShapeDtypeStruct((1, W), F32)],
        compiler_params=_params(1),
    )(dmixed, a, b, ga, gb)


def _loss_head(name, y, target):
    T, D = y.shape
    tm = min(ROWS, T)

    def body(y_ref, t_ref, loss_ref, dy_ref, dy16_ref):
        d = y_ref[...] - t_ref[...]

        @pl.when(pl.program_id(0) == 0)
        def _():
            loss_ref[...] = jnp.zeros_like(loss_ref)

        per_token = jnp.mean(d * d, axis=-1, keepdims=True)
        loss_ref[...] += 0.5 * jnp.sum(per_token, axis=0, keepdims=True)
        dy = d * (1.0 / D)
        dy_ref[...] = dy
        dy16_ref[...] = dy.astype(BF16)

    return pl.pallas_call(
        body, name=name, grid=(T // tm,),
        in_specs=[_row_spec(tm, D), _row_spec(tm, D)],
        out_specs=[_full_spec((1, 1)), _row_spec(tm, D), _row_spec(tm, D)],
        out_shape=[jax.ShapeDtypeStruct((1, 1), F32), jax.ShapeDtypeStruct((T, D), F32),
                   jax.ShapeDtypeStruct((T, D), BF16)],
        compiler_params=_params(1),
    )(y, target)


def _bf16_dot(a, b, dims):
    return lax.dot_general(a.astype(BF16), b.astype(BF16), dims, preferred_element_type=F32)


@jax.custom_vjp
def _dot_nn(a, b):
    return _bf16_dot(a, b, _DIMS["nn"])


def _dot_nn_fwd(a, b):
    return _dot_nn(a, b), (a, b)


def _dot_nn_bwd(saved, ct):
    a, b = saved
    return _bf16_dot(ct, b, _DIMS["nt"]), _bf16_dot(a, ct, _DIMS["tn"])


_dot_nn.defvjp(_dot_nn_fwd, _dot_nn_bwd)


@jax.custom_vjp
def _dot_nt(a, b):
    return _bf16_dot(a, b, _DIMS["nt"])


def _dot_nt_fwd(a, b):
    return _dot_nt(a, b), (a, b)


def _dot_nt_bwd(saved, ct):
    a, b = saved
    return _bf16_dot(ct, b, _DIMS["nn"]), _bf16_dot(ct, a, _DIMS["tn"])


_dot_nt.defvjp(_dot_nt_fwd, _dot_nt_bwd)


def _iota(shape, dim):
    return lax.broadcasted_iota(jnp.int32, shape, dim)


def _head_sum_impl(x):
    same_head = (_iota((LANES, LANES), 0) // HEAD_DIM == _iota((LANES, LANES), 1) // HEAD_DIM).astype(BF16)
    pieces = []
    for i in range(x.shape[1] // LANES):
        xs = x[:, i * LANES:(i + 1) * LANES]
        hi = xs.astype(BF16)
        r1 = xs - hi.astype(F32)
        mid = r1.astype(BF16)
        lo = (r1 - mid.astype(F32)).astype(BF16)
        tot = None
        for part in (hi, mid, lo):
            d = lax.dot_general(part, same_head, _DIMS["nn"], preferred_element_type=F32)
            tot = d if tot is None else tot + d
        pieces.append(tot)
    return pieces[0] if len(pieces) == 1 else jnp.concatenate(pieces, axis=1)


@jax.custom_vjp
def _head_sum(x):
    return _head_sum_impl(x)


_head_sum.defvjp(lambda x: (_head_sum_impl(x), None), lambda _, ct: (_head_sum_impl(ct),))


def _head_rms(x, g_row):
    ms = _head_sum(x * x) * (1.0 / HEAD_DIM)
    return (x * lax.rsqrt(ms + EPS)) * g_row


@jax.custom_vjp
def _swap_halves(x):
    return pltpu.roll(x, HEAD_DIM, 1)


_swap_halves.defvjp(lambda x: (pltpu.roll(x, HEAD_DIM, 1), None), lambda _, ct: (pltpu.roll(ct, HEAD_DIM, 1),))


def _gelu(x):
    return 0.5 * x * (1.0 + lax.erf(x * (1.0 / math.sqrt(2.0))))


def _sgu_block(u_raw, v_raw, g_row, w, b_full):
    u, v = _gelu(u_raw), _gelu(v_raw)
    vn = _head_rms(v, g_row)
    causal = _iota((CHUNK, CHUNK), 0) >= _iota((CHUNK, CHUNK), 1)
    low_half = _iota((CHUNK, LANES), 1) < HEAD_DIM
    gates = []
    for p in range(v.shape[1] // LANES):
        vp = vn[:, p * LANES:(p + 1) * LANES]
        g0 = _dot_nn(jnp.where(causal, w[2 * p], 0.0), vp)
        g1 = _dot_nn(jnp.where(causal, w[2 * p + 1], 0.0), vp)
        gates.append(jnp.where(low_half, g0, g1))
    gate = jnp.concatenate(gates, axis=1) + b_full
    return u * gate


def _softmax_sink_fwd(s, sink):
    m = jnp.maximum(jnp.max(s, axis=1, keepdims=True), sink)
    e = jnp.exp(s - m)
    e_sink = jnp.exp(sink - m)
    r = 1.0 / (jnp.sum(e, axis=1, keepdims=True) + e_sink)
    p = e * r
    return p, (p, e_sink * r)


@jax.custom_vjp
def _softmax_with_sink(s, sink):
    return _softmax_sink_fwd(s, sink)[0]


def _softmax_sink_bwd(saved, dp):
    p, p_sink = saved
    delta = jnp.sum(p * dp, axis=1, keepdims=True)
    return p * (dp - delta), -p_sink * delta


_softmax_with_sink.defvjp(_softmax_sink_fwd, _softmax_sink_bwd)


def _attn_block(q_raw, k_prev, k_cur, v_prev, v_cur, qg_row, kg_row, sinks, bias):
    scale = 1.0 / math.sqrt(HEAD_DIM)
    n_q_heads = q_raw.shape[1] // HEAD_DIM
    qn = _head_rms(q_raw, qg_row) * scale
    kn_prev, kn_cur = _head_rms(k_prev, kg_row), _head_rms(k_cur, kg_row)
    own = _iota((GROUP * BLOCK, BLOCK), 1) <= (_iota((GROUP * BLOCK, BLOCK), 0) & (BLOCK - 1))
    low_half = _iota((BLOCK, LANES), 1) < HEAD_DIM
    heads = [None] * n_q_heads
    for kv in range(n_q_heads // GROUP):
        grp, kv_low = kv // 2, kv % 2 == 0
        lanes = slice(grp * LANES, (grp + 1) * LANES)
        keep = low_half if kv_low else ~low_half
        stacked = []
        for g in range(GROUP):
            h = GROUP * kv + g
            q2 = qn[:, (h // 2) * LANES:(h // 2 + 1) * LANES]
            src = q2 if (h % 2 == 0) == kv_low else _swap_halves(q2)
            stacked.append(jnp.where(keep, src, 0.0))
        q4 = jnp.concatenate(stacked, axis=0)
        bias4 = jnp.concatenate([bias[GROUP * kv + g] for g in range(GROUP)], axis=0)
        sink4 = jnp.concatenate([jnp.broadcast_to(sinks[GROUP * kv + g], (BLOCK, 1)) for g in range(GROUP)], axis=0)
        s = jnp.where(own, _dot_nt(q4, kn_cur[:, lanes]), _dot_nt(q4, kn_prev[:, lanes])) + bias4
        p = _softmax_with_sink(s, sink4)
        o4 = _dot_nn(jnp.where(own, p, 0.0), v_cur[:, lanes]) + _dot_nn(jnp.where(own, 0.0, p), v_prev[:, lanes])
        for g in range(GROUP):
            h = GROUP * kv + g
            o = o4[g * BLOCK:(g + 1) * BLOCK]
            heads[h] = o if (h % 2 == 0) == kv_low else _swap_halves(o)
    outs = [jnp.where(low_half, heads[2 * p], heads[2 * p + 1]) for p in range(n_q_heads // 2)]
    return jnp.concatenate(outs, axis=1)


def _attn_block_bwd(q_raw, k_prev, k_cur, v_prev, v_cur, qg_row, kg_row, sinks, bias, d_out):
    scale = 1.0 / math.sqrt(HEAD_DIM)
    n_q_heads = q_raw.shape[1] // HEAD_DIM
    qn, q_vjp = jax.vjp(lambda q, g: _head_rms(q, g) * scale, q_raw, qg_row)
    kn_prev, kp_vjp = jax.vjp(_head_rms, k_prev, kg_row)
    kn_cur, kc_vjp = jax.vjp(_head_rms, k_cur, kg_row)
    own = _iota((GROUP * BLOCK, BLOCK), 1) <= (_iota((GROUP * BLOCK, BLOCK), 0) & (BLOCK - 1))
    low_half = _iota((BLOCK, LANES), 1) < HEAD_DIM
    nn, nt, tn = _DIMS["nn"], _DIMS["nt"], _DIMS["tn"]
    n_groups = k_cur.shape[1] // LANES
    dq_heads, d_sinks, d_bias = [None] * n_q_heads, [None] * n_q_heads, [None] * n_q_heads
    dk_prev, dk_cur, dv_prev, dv_cur = ([None] * n_groups for _ in range(4))

    def add(parts, grp, val):
        parts[grp] = val if parts[grp] is None else parts[grp] + val

    for kv in range(n_q_heads // GROUP):
        grp, kv_low = kv // 2, kv % 2 == 0
        lanes = slice(grp * LANES, (grp + 1) * LANES)
        keep = low_half if kv_low else ~low_half
        q_rows, do_rows = [], []
        for g in range(GROUP):
            h = GROUP * kv + g
            pair = slice((h // 2) * LANES, (h // 2 + 1) * LANES)
            in_place = (h % 2 == 0) == kv_low
            q2, do2 = qn[:, pair], d_out[:, pair]
            q_rows.append(jnp.where(keep, q2 if in_place else pltpu.roll(q2, HEAD_DIM, 1), 0.0))
            do_rows.append(jnp.where(keep, do2 if in_place else pltpu.roll(do2, HEAD_DIM, 1), 0.0))
        q4, do4 = jnp.concatenate(q_rows, axis=0), jnp.concatenate(do_rows, axis=0)
        bias4 = jnp.concatenate([bias[GROUP * kv + g] for g in range(GROUP)], axis=0)
        sink4 = jnp.concatenate([jnp.broadcast_to(sinks[GROUP * kv + g], (BLOCK, 1)) for g in range(GROUP)], axis=0)
        kc2, kp2, vc2, vp2 = kn_cur[:, lanes], kn_prev[:, lanes], v_cur[:, lanes], v_prev[:, lanes]
        s = jnp.where(own, _bf16_dot(q4, kc2, nt), _bf16_dot(q4, kp2, nt)) + bias4
        p, (_, p_sink) = _softmax_sink_fwd(s, sink4)
        p_own, p_prev = jnp.where(own, p, 0.0), jnp.where(own, 0.0, p)
        add(dv_cur, grp, _bf16_dot(p_own, do4, tn))
        add(dv_prev, grp, _bf16_dot(p_prev, do4, tn))
        dp = jnp.where(own, _bf16_dot(do4, vc2, nt), _bf16_dot(do4, vp2, nt))
        delta = jnp.sum(p * dp, axis=1, keepdims=True)
        ds = p * (dp - delta)
        d_sink4 = -p_sink * delta
        ds_own, ds_prev = jnp.where(own, ds, 0.0), jnp.where(own, 0.0, ds)
        dq4 = _bf16_dot(ds_own, kc2, nn) + _bf16_dot(ds_prev, kp2, nn)
        add(dk_cur, grp, _bf16_dot(ds_own, q4, tn))
        add(dk_prev, grp, _bf16_dot(ds_prev, q4, tn))
        for g in range(GROUP):
            h = GROUP * kv + g
            rows = slice(g * BLOCK, (g + 1) * BLOCK)
            dq2 = jnp.where(keep, dq4[rows], 0.0)
            dq_heads[h] = dq2 if (h % 2 == 0) == kv_low else pltpu.roll(dq2, HEAD_DIM, 1)
            d_bias[h] = ds[rows]
            d_sinks[h] = jnp.sum(d_sink4[rows], axis=0, keepdims=True)
    dqn = jnp.concatenate([dq_heads[2 * p] + dq_heads[2 * p + 1] for p in range(n_q_heads // 2)], axis=1)
    dq, dqg = q_vjp(dqn)
    dkp, dkg_prev = kp_vjp(jnp.concatenate(dk_prev, axis=1))
    dkc, dkg_cur = kc_vjp(jnp.concatenate(dk_cur, axis=1))
    return (dq, dkp, dkc, jnp.concatenate(dv_prev, axis=1), jnp.concatenate(dv_cur, axis=1),
            dqg, dkg_prev + dkg_cur, d_sinks, d_bias)


def _bias_table():
    i, j = np.arange(BLOCK)[:, None], np.arange(BLOCK)[None, :]
    n = np.where(j <= i, i - j, i + BLOCK - j)
    max_exact = NUM_BUCKETS // 2
    nf = np.maximum(n, 1).astype(np.float64)
    large = max_exact + (np.log(nf / max_exact) / math.log(MAX_DISTANCE / max_exact) * (NUM_BUCKETS - max_exact)).astype(np.int32)
    large = np.minimum(large, NUM_BUCKETS - 1)
    return np.where(n < max_exact, n, large).astype(np.int32)


def _bias_fwd(name, rel_bias, buckets):
    nb_, nh = rel_bias.shape

    def body(rb_ref, bk_ref, o_ref):
        bk = bk_ref[...]
        own = _iota(bk.shape, 1) <= _iota(bk.shape, 0)
        for h in range(nh):
            acc = jnp.zeros(bk.shape, F32)
            for b in range(nb_):
                acc = jnp.where(bk == b, rb_ref[b, h], acc)
            o_ref[1, h] = acc
            o_ref[0, h] = jnp.where(own, acc, NEG_INF)

    return pl.pallas_call(
        body, name=name,
        in_specs=[pl.BlockSpec(memory_space=pltpu.SMEM), pl.BlockSpec(memory_space=pltpu.VMEM)],
        out_specs=pl.BlockSpec(memory_space=pltpu.VMEM),
        out_shape=jax.ShapeDtypeStruct((2, nh) + buckets.shape, F32),
    )(rel_bias, buckets)


def _bias_bwd(name, dbias, buckets, nb_):
    nh = dbias.shape[0]

    def body(db_ref, bk_ref, o_ref):
        bk = bk_ref[...]
        for h in range(nh):
            d = db_ref[h]
            for b in range(nb_):
                s = jnp.sum(jnp.where(bk == b, d, 0.0), axis=0, keepdims=True)
                s = jnp.sum(s, axis=1, keepdims=True)
                o_ref[b * nh + h:b * nh + h + 1, :] = jnp.broadcast_to(s, (1, LANES))

    return pl.pallas_call(
        body, name=name,
        in_specs=[pl.BlockSpec(memory_space=pltpu.VMEM), pl.BlockSpec(memory_space=pltpu.VMEM)],
        out_specs=pl.BlockSpec(memory_space=pltpu.VMEM),
        out_shape=jax.ShapeDtypeStruct((nb_ * nh, LANES), F32),
    )(dbias, buckets)


SGU_CHUNKS_PER_STEP = 4


def _sgu_step(T):
    n = SGU_CHUNKS_PER_STEP if T % (SGU_CHUNKS_PER_STEP * CHUNK) == 0 else 1
    return n, [slice(c * CHUNK, (c + 1) * CHUNK) for c in range(n)]


def _sgu_fwd(name, z, g_row, w, b_full, W):
    T = z.shape[0]
    n, chunks = _sgu_step(T)

    def body(u_ref, v_ref, g_ref, w_ref, b_ref, o_ref):
        for rows in chunks:
            o_ref[rows, :] = _sgu_block(u_ref[rows, :].astype(F32), v_ref[rows, :].astype(F32), g_ref[...], w_ref[...],
                                        b_ref[...]).astype(BF16)

    return pl.pallas_call(
        body, name=name, grid=(T // (n * CHUNK),),
        in_specs=[_row_spec(n * CHUNK, W, 0), _row_spec(n * CHUNK, W, 1), _full_spec((1, W)),
                  _full_spec(w.shape), _full_spec((CHUNK, W))],
        out_specs=_row_spec(n * CHUNK, W),
        out_shape=jax.ShapeDtypeStruct((T, W), BF16), compiler_params=_params(1),
    )(z, z, g_row, w, b_full)


def _sgu_bwd(name, z, g_row, w, b_full, d_out, others, W):
    T = z.shape[0]
    n, chunks = _sgu_step(T)
    widths = [o.shape[1] for o in others]

    def body(u_ref, v_ref, g_ref, w_ref, b_ref, do_ref, *rest):
        other_refs, (dz_ref, dg_ref, dw_ref, db_ref) = rest[:len(others)], rest[len(others):]
        dg = dw = db = None
        for rows in chunks:
            _, vjp = jax.vjp(_sgu_block, u_ref[rows, :].astype(F32), v_ref[rows, :].astype(F32), g_ref[...], w_ref[...],
                             b_ref[...])
            du, dv, dg_c, dw_c, db_c = vjp(do_ref[rows, :])
            dz_ref[rows, :W] = du.astype(BF16)
            dz_ref[rows, W:2 * W] = dv.astype(BF16)
            dg, dw, db = (dg_c, dw_c, db_c) if dg is None else (dg + dg_c, dw + dw_c, db + db_c)
        at = 2 * W
        for o_ref, width in zip(other_refs, widths):
            dz_ref[:, at:at + width] = o_ref[...]
            at += width
        i = pl.program_id(0)

        @pl.when(i == 0)
        def _():
            dg_ref[...] = jnp.zeros_like(dg_ref)
            dw_ref[...] = jnp.zeros_like(dw_ref)
            db_ref[...] = jnp.zeros_like(db_ref)

        dg_ref[...] += dg
        dw_ref[...] += dw
        db_ref[...] += db

        @pl.when(i == pl.num_programs(0) - 1)
        def _():
            db_ref[...] = _head_sum_impl(db_ref[...])

    return pl.pallas_call(
        body, name=name, grid=(T // (n * CHUNK),),
        in_specs=[_row_spec(n * CHUNK, W, 0), _row_spec(n * CHUNK, W, 1), _full_spec((1, W)),
                  _full_spec(w.shape), _full_spec((CHUNK, W)), _row_spec(n * CHUNK, W)]
        + [_row_spec(n * CHUNK, width) for width in widths],
        out_specs=[_row_spec(n * CHUNK, 2 * W + sum(widths)), _full_spec((1, W)), _full_spec(w.shape),
                   _full_spec((CHUNK, W))],
        out_shape=[jax.ShapeDtypeStruct((T, 2 * W + sum(widths)), BF16), jax.ShapeDtypeStruct((1, W), F32),
                   jax.ShapeDtypeStruct(w.shape, F32), jax.ShapeDtypeStruct((CHUNK, W), F32)],
        compiler_params=_params(1),
    )(z, z, g_row, w, b_full, d_out, *others)


def _attn_specs(WQ, WKV, q_col, k_col, v_col, blk_of):
    prev_of = lambda i: jnp.maximum(blk_of(i) - 1, 0)
    return [pl.BlockSpec((BLOCK, WQ), lambda i: (blk_of(i), q_col)),
            pl.BlockSpec((BLOCK, WKV), lambda i: (prev_of(i), k_col)),
            pl.BlockSpec((BLOCK, WKV), lambda i: (blk_of(i), k_col)),
            pl.BlockSpec((BLOCK, WKV), lambda i: (prev_of(i), v_col)),
            pl.BlockSpec((BLOCK, WKV), lambda i: (blk_of(i), v_col))]


def _bias_spec(bias, blk_of):
    return pl.BlockSpec((1,) + bias.shape[1:], lambda i: (jnp.minimum(blk_of(i), 1), 0, 0, 0))


def _attn_fwd(name, z, qg_row, kg_row, sinks_col, bias, WQ, WKV, q_col, k_col, v_col):
    T = z.shape[0]
    nh = sinks_col.shape[0]

    def body(q_ref, kp_ref, kc_ref, vp_ref, vc_ref, qg_ref, kg_ref, s_ref, b_ref, o_ref):
        sinks = [s_ref[h:h + 1, :] for h in range(nh)]
        q, kp, kc, vp, vc = (r[...].astype(F32) for r in (q_ref, kp_ref, kc_ref, vp_ref, vc_ref))
        o_ref[...] = _attn_block(q, kp, kc, vp, vc, qg_ref[...], kg_ref[...], sinks, b_ref[0]).astype(BF16)

    return pl.pallas_call(
        body, name=name, grid=(T // BLOCK,),
        in_specs=_attn_specs(WQ, WKV, q_col, k_col, v_col, lambda i: i)
        + [_full_spec((1, WQ)), _full_spec((1, WKV)), _full_spec((nh, 1)), _bias_spec(bias, lambda i: i)],
        out_specs=_row_spec(BLOCK, WQ),
        out_shape=jax.ShapeDtypeStruct((T, WQ), BF16), compiler_params=_params(1),
    )(z, z, z, z, z, qg_row, kg_row, sinks_col, bias)


def _attn_bwd(name, z, qg_row, kg_row, sinks_col, bias, d_out, WQ, WKV, q_col, k_col, v_col):
    T = z.shape[0]
    nblk = T // BLOCK
    nh = sinks_col.shape[0]
    blk_of = lambda i: nblk - 1 - i

    def body(q_ref, kp_ref, kc_ref, vp_ref, vc_ref, qg_ref, kg_ref, s_ref, b_ref, do_ref,
             dq_ref, dk_ref, dv_ref, dqg_ref, dkg_ref, ds_ref, db_ref, carry_k, carry_v):
        i = pl.program_id(0)

        @pl.when(i == 0)
        def _():
            carry_k[...] = jnp.zeros_like(carry_k)
            carry_v[...] = jnp.zeros_like(carry_v)
            dqg_ref[...] = jnp.zeros_like(dqg_ref)
            dkg_ref[...] = jnp.zeros_like(dkg_ref)
            ds_ref[...] = jnp.zeros_like(ds_ref)
            db_ref[...] = jnp.zeros_like(db_ref)

        scale = 1.0 / math.sqrt(HEAD_DIM)
        own = _iota((GROUP * BLOCK, BLOCK), 1) <= (_iota((GROUP * BLOCK, BLOCK), 0) & (BLOCK - 1))
        low_half = _iota((BLOCK, LANES), 1) < HEAD_DIM
        nn, nt, tn = _DIMS["nn"], _DIMS["nt"], _DIMS["tn"]
        slab_w = GROUP * HEAD_DIM
        for grp in range(WKV // LANES):
            lanes = slice(grp * LANES, (grp + 1) * LANES)
            kg = kg_ref[:, lanes]
            kp2, kp_vjp = jax.vjp(_head_rms, kp_ref[:, lanes].astype(F32), kg)
            kc2, kc_vjp = jax.vjp(_head_rms, kc_ref[:, lanes].astype(F32), kg)
            vp2, vc2 = vp_ref[:, lanes].astype(F32), vc_ref[:, lanes].astype(F32)
            dkp2 = dkc2 = dvp2 = dvc2 = None
            for kv in (2 * grp, 2 * grp + 1):
                kv_low = kv % 2 == 0
                keep = low_half if kv_low else ~low_half
                slab = slice(kv * slab_w, (kv + 1) * slab_w)
                qn, q_vjp = jax.vjp(lambda q, g: _head_rms(q, g) * scale, q_ref[:, slab].astype(F32), qg_ref[:, slab])
                d_out = do_ref[:, slab]
                q_rows, do_rows = [], []
                for g in range(GROUP):
                    pair = slice((g // 2) * LANES, (g // 2 + 1) * LANES)
                    in_place = (g % 2 == 0) == kv_low
                    q2, do2 = qn[:, pair], d_out[:, pair]
                    q_rows.append(jnp.where(keep, q2 if in_place else pltpu.roll(q2, HEAD_DIM, 1), 0.0))
                    do_rows.append(jnp.where(keep, do2 if in_place else pltpu.roll(do2, HEAD_DIM, 1), 0.0))
                q4, do4 = jnp.concatenate(q_rows, axis=0), jnp.concatenate(do_rows, axis=0)
                bias4 = jnp.concatenate([b_ref[0, GROUP * kv + g] for g in range(GROUP)], axis=0)
                sink4 = jnp.concatenate([jnp.broadcast_to(s_ref[GROUP * kv + g:GROUP * kv + g + 1, :], (BLOCK, 1))
                                         for g in range(GROUP)], axis=0)
                s = jnp.where(own, _bf16_dot(q4, kc2, nt), _bf16_dot(q4, kp2, nt)) + bias4
                p, (_, p_sink) = _softmax_sink_fwd(s, sink4)
                p_own, p_prev = jnp.where(own, p, 0.0), jnp.where(own, 0.0, p)
                dv_c, dv_p = _bf16_dot(p_own, do4, tn), _bf16_dot(p_prev, do4, tn)
                dp = jnp.where(own, _bf16_dot(do4, vc2, nt), _bf16_dot(do4, vp2, nt))
                delta = jnp.sum(p * dp, axis=1, keepdims=True)
                ds = p * (dp - delta)
                d_sink4 = -p_sink * delta
                ds_own, ds_prev = jnp.where(own, ds, 0.0), jnp.where(own, 0.0, ds)
                dq4 = _bf16_dot(ds_own, kc2, nn) + _bf16_dot(ds_prev, kp2, nn)
                dk_c, dk_p = _bf16_dot(ds_own, q4, tn), _bf16_dot(ds_prev, q4, tn)
                first = dkp2 is None
                dkp2, dkc2 = (dk_p, dk_c) if first else (dkp2 + dk_p, dkc2 + dk_c)
                dvp2, dvc2 = (dv_p, dv_c) if first else (dvp2 + dv_p, dvc2 + dv_c)
                pairs = [None, None]
                for g in range(GROUP):
                    h = GROUP * kv + g
                    rows = slice(g * BLOCK, (g + 1) * BLOCK)
                    dq2 = jnp.where(keep, dq4[rows], 0.0)
                    dq2 = dq2 if (g % 2 == 0) == kv_low else pltpu.roll(dq2, HEAD_DIM, 1)
                    pairs[g // 2] = dq2 if pairs[g // 2] is None else pairs[g // 2] + dq2
                    db_ref[h] += ds[rows]
                    ds_ref[h:h + 1, :] += jnp.sum(d_sink4[rows], axis=0, keepdims=True)
                dq, dqg = q_vjp(jnp.concatenate(pairs, axis=1))
                dq_ref[:, slab] = dq.astype(BF16)
                dqg_ref[:, slab] += dqg
            dkp, dkg_p = kp_vjp(dkp2)
            dkc, dkg_c = kc_vjp(dkc2)
            dk_ref[:, lanes] = (dkc + carry_k[:, lanes]).astype(BF16)
            dv_ref[:, lanes] = (dvc2 + carry_v[:, lanes]).astype(BF16)
            carry_k[:, lanes] = dkp
            carry_v[:, lanes] = dvp2
            dkg_ref[:, lanes] += dkg_p + dkg_c

    return pl.pallas_call(
        body, name=name, grid=(nblk,),
        in_specs=_attn_specs(WQ, WKV, q_col, k_col, v_col, blk_of)
        + [_full_spec((1, WQ)), _full_spec((1, WKV)), _full_spec((nh, 1)), _bias_spec(bias, blk_of),
           pl.BlockSpec((BLOCK, WQ), lambda i: (blk_of(i), 0))],
        out_specs=[pl.BlockSpec((BLOCK, WQ), lambda i: (blk_of(i), 0)),
                   pl.BlockSpec((BLOCK, WKV), lambda i: (blk_of(i), 0)),
                   pl.BlockSpec((BLOCK, WKV), lambda i: (blk_of(i), 0)),
                   _full_spec((1, WQ)), _full_spec((1, WKV)), _full_spec((nh, 1)), _full_spec(bias.shape[1:])],
        out_shape=[jax.ShapeDtypeStruct((T, WQ), BF16), jax.ShapeDtypeStruct((T, WKV), BF16),
                   jax.ShapeDtypeStruct((T, WKV), BF16), jax.ShapeDtypeStruct((1, WQ), F32),
                   jax.ShapeDtypeStruct((1, WKV), F32), jax.ShapeDtypeStruct((nh, 1), F32),
                   jax.ShapeDtypeStruct(bias.shape[1:], F32)],
        scratch_shapes=[pltpu.VMEM((BLOCK, WKV), F32), pltpu.VMEM((BLOCK, WKV), F32)],
        compiler_params=_params(1),
    )(z, z, z, z, z, qg_row, kg_row, sinks_col, bias, d_out)


def _adamw(name, w, g, m, v):
    shape = w.shape
    C = shape[-1]
    if w.ndim == 3 and shape[1] % ROWS == 0:
        work = shape
        grid = (shape[0], shape[1] // ROWS)
        spec = pl.BlockSpec((1, ROWS, C), lambda l, i: (l, i, 0))
    else:
        R = int(np.prod(shape[:-1]))
        tr = ROWS if R % ROWS == 0 else R
        work = (R, C)
        grid = (R // tr,)
        spec = pl.BlockSpec((tr, C), lambda i: (i, 0))
    w2, g2, m2, v2 = (t.reshape(work) for t in (w, g, m, v))

    def body(w_ref, g_ref, m_ref, v_ref, d_ref, nm_ref, nv_ref):
        gv = g_ref[...]
        nm = ADAM_B1 * m_ref[...] + (1.0 - ADAM_B1) * gv
        nv = ADAM_B2 * v_ref[...] + (1.0 - ADAM_B2) * (gv * gv)
        m_hat = nm / (1.0 - ADAM_B1 ** ADAM_STEP)
        v_hat = nv / (1.0 - ADAM_B2 ** ADAM_STEP)
        d_ref[...] = -ADAM_LR * (m_hat / (jnp.sqrt(v_hat) + ADAM_EPS) + ADAM_WD * w_ref[...])
        nm_ref[...] = nm
        nv_ref[...] = nv

    outs = pl.pallas_call(
        body, name=name, grid=grid, in_specs=[spec] * 4, out_specs=[spec] * 3,
        out_shape=[jax.ShapeDtypeStruct(work, F32)] * 3, compiler_params=_params(len(grid)),
    )(w2, g2, m2, v2)
    return tuple(o.reshape(shape) for o in outs)


def _pad_rows(flat):
    n = flat.shape[0]
    tile = 8 * LANES
    padded = -(-n // tile) * tile
    return jnp.pad(flat, (0, padded - n)).reshape(padded // LANES, LANES)


def kernel(x, rel_bias, norm1_g, w_in, sgu_norm_g, sgu_w, sgu_b, q_norm_g, k_norm_g, sinks, out_norm_a, out_norm_b, w_out, norm2_g, w_gate, w_up, w_down, loss_target, m_rel_bias, m_norm1_g, m_w_in, m_sgu_norm_g, m_sgu_w, m_sgu_b, m_q_norm_g, m_k_norm_g, m_sinks, m_out_norm_a, m_out_norm_b, m_w_out, m_norm2_g, m_w_gate, m_w_up, m_w_down, v_rel_bias, v_norm1_g, v_w_in, v_sgu_norm_g, v_sgu_w, v_sgu_b, v_q_norm_g, v_k_norm_g, v_sinks, v_out_norm_a, v_out_norm_b, v_w_out, v_norm2_g, v_w_gate, v_w_up, v_w_down):
    L = w_in.shape[0]
    T, D = x.shape[1], x.shape[2]
    W = D // 2
    NH = W // HEAD_DIM
    WKV = N_KV_HEADS * HEAD_DIM
    IN = N_DEV * w_in.shape[2]
    FF = N_DEV * w_gate.shape[2]
    assert IN == 2 * W + W + 2 * WKV and NH // N_KV_HEADS == GROUP
    q_col, k_col, v_col = 2 * W // W, (3 * W) // WKV, (3 * W + WKV) // WKV
    x0 = x.reshape(T, D)
    target = loss_target.reshape(T, D)

    shards = [jnp.swapaxes(w_in, 1, 2).astype(BF16), w_out.astype(BF16), jnp.swapaxes(w_gate, 1, 2).astype(BF16),
              jnp.swapaxes(w_up, 1, 2).astype(BF16), w_down.astype(BF16)]
    me = 4 * lax.axis_index("x") + 2 * lax.axis_index("y") + lax.axis_index("c")

    def landing(block):
        return lax.dynamic_update_slice(lax.empty((N_DEV,) + block.shape, block.dtype), block[None], (me, 0, 0))

    def as_matrices(got):
        return [g.reshape(N_DEV * g.shape[1], g.shape[2]) for g in got]

    groups = ((0,), (1,), (2, 3, 4))
    full = {l: [None] * len(shards) for l in range(L)}
    full[0][0] = as_matrices(_all_gather("gather_weights_0", [shards[0][0]]))[0]
    flight = {}

    def start_gather(l, first_group, after):
        for g in range(first_group, len(groups)):
            srcs = [shards[i][l] for i in groups[g]]
            flight[l, g] = _exchange_start(f"gather_start_{l}_{g}", "gather", srcs, [landing(s) for s in srcs], after)
            after = flight[l, g][4]
        return after

    def finish_gather(l, g, after):
        s_sem, r_sem, srcs, lands, _ = flight.pop((l, g))
        lands = _exchange_wait(f"gather_wait_{l}_{g}", "gather", s_sem, r_sem, srcs, lands, after)[1]
        for i, mat in zip(groups[g], as_matrices(lands)):
            full[l][i] = mat

    buckets = jnp.asarray(_bias_table())
    bias = _bias_fwd("bias_table", rel_bias, buckets)

    saved = []
    xl = x0
    token = start_gather(0, 1, full[0][0])
    for l in range(L):
        if l > 0:
            finish_gather(l, 0, xl)
        w_in_t = full[l][0]
        g1, g2 = norm1_g[l][None], norm2_g[l][None]
        sg_row = sgu_norm_g[l].reshape(1, W)
        b_full = jnp.repeat(sgu_b[l].T, HEAD_DIM, axis=1)
        qg_row = jnp.tile(q_norm_g[l], NH)[None]
        kg_row = jnp.tile(k_norm_g[l], N_KV_HEADS)[None]
        sinks_col = sinks[l][:, None]
        ga, gb = out_norm_a[l][None], out_norm_b[l][None]

        h = _rms_fwd(f"norm1_{l}", xl, g1)
        (z,) = _mm(f"proj_in_{l}", "nt", [h], [w_in_t], [(0, 0, 0)], 1, T, IN, D, 1024, 896, D, _ep_plain, [BF16],
                   after=token)
        out_a = _sgu_fwd(f"sgu_{l}", z, sg_row, sgu_w[l], b_full, W)
        out_b = _attn_fwd(f"attn_{l}", z, qg_row, kg_row, sinks_col, bias, W, WKV, q_col, k_col, v_col)
        mixed = _outnorm_fwd(f"outnorm_{l}", out_a, out_b, ga, gb)
        finish_gather(l, 1, mixed)
        w_o = full[l][1]
        (x1,) = _mm(f"proj_out_{l}", "nn", [mixed], [w_o], [(0, 0, 0)], 1, T, D, D, 1024, 1024, D,
                    _ep_residual, [F32], extras=[xl])
        h2 = _rms_fwd(f"norm2_{l}", x1, g2)
        token = start_gather(l + 1, 0, h2) if l + 1 < L else h2
        finish_gather(l, 2, h2)
        w_g_t, w_u_t, w_d = full[l][2:]
        gate, up, act = _mm(f"mlp_in_{l}", "nt", [h2], [w_g_t, w_u_t], [(0, 0, 0), (0, 1, 1)], 2, T, FF, D,
                            1024, 512, D, _ep_swiglu, [BF16, BF16, BF16], after=token)
        (x2,) = _mm(f"mlp_out_{l}", "nn", [act], [w_d], [(0, 0, 0)], 1, T, D, FF, 512, 1024, FF,
                    _ep_residual, [F32], extras=[x1])
        saved.append((xl, h, z, out_a, out_b, mixed, x1, h2, gate, up, act,
                      g1, g2, sg_row, b_full, qg_row, kg_row, sinks_col, ga, gb))
        xl = x2

    loss_part, dy, dy16 = _loss_head("loss_head", xl, target)

    dbias = None
    small = [None] * L
    scatters = []

    def start_scatter(name, which, grads_t, after):
        srcs = [t.reshape(N_DEV, t.shape[0] // N_DEV, D) for t in grads_t]
        lands = [lax.empty((N_DEV - 1,) + s.shape[1:], BF16) for s in srcs]
        s_sem, r_sem, srcs, lands, tok = _exchange_start(name, "scatter", srcs, lands, after)
        scatters.append((name, which, s_sem, r_sem, srcs, lands))
        return tok

    small_names = ["norm1_g", "sgu_norm_g", "sgu_w", "sgu_b", "q_norm_g", "k_norm_g", "sinks", "out_norm_a",
                   "out_norm_b", "norm2_g"]
    small_flights = {}

    def start_small(l, parts, after):
        order = sorted(parts, key=(["rel_bias"] + small_names).index)
        pieces = [_pad_rows(parts[nme].reshape(-1)) for nme in order]
        n_rows = sum(p.shape[0] for p in pieces)
        if n_rows % 64:
            pieces.append(jnp.zeros((-n_rows % 64, LANES), F32))
        packed = jnp.concatenate(pieces, axis=0)
        flight = _exchange_start(f"gather_small_start_{l}", "gather", [packed], [landing(packed)], after)
        small_flights[l] = (flight, [(nme, parts[nme].shape) for nme in order])
        return flight[4]

    def finish_small(l, after):
        (s_sem, r_sem, srcs, lands, _), layout = small_flights[l]
        _, (everyone,) = _exchange_wait(f"gather_small_wait_{l}", "gather", s_sem, r_sem, srcs, lands, after)
        summed = _sum_slots(f"sum_small_grads_{l}", everyone, 64)
        out, at = {}, 0
        for nme, shape in layout:
            n = int(np.prod(shape))
            n_rows = -(-n // (8 * LANES)) * 8
            out[nme] = summed[at:at + n_rows].reshape(-1)[:n].reshape(shape)
            at += n_rows
        return out

    token = None
    for l in reversed(range(L)):
        w_in_t, w_o, w_g_t, w_u_t, w_d = full[l]
        (xl, h, z, out_a, out_b, mixed, x1, h2, gate, up, act,
         g1, g2, sg_row, b_full, qg_row, kg_row, sinks_col, ga, gb) = saved[l]

        dgate, dup = _mm(f"d_mlp_out_{l}", "nt", [dy16], [w_d], [(0, 0, 0)], 1, T, FF, D, 1024, 1408, D,
                         _ep_swiglu_bwd, [BF16, BF16], extras=[gate, up], after=token)
        (dw_d,) = _mm(f"dw_down_{l}", "tn", [act], [dy16], [(0, 0, 0)], 1, FF, D, T, 1408, 1024, 2048, _ep_plain, [BF16])
        (dh2,) = _mm(f"d_mlp_in_{l}", "nn", [dgate, dup], [w_g_t, w_u_t], [(0, 0, 0), (1, 1, 0)], 1, T, D, FF,
                     512, 512, FF, _ep_plain, [F32])
        dx1, dx1_16, dg2 = _rms_bwd(f"d_norm2_{l}", dh2, x1, g2, dy)
        dw_g, dw_u = _mm(f"dw_gate_up_{l}", "tn", [dgate, dup], [h2], [(0, 0, 0), (1, 0, 1)], 2, FF, D, T,
                         1408, 1024, 1024, _ep_two, [BF16, BF16])
        token = start_scatter(f"scatter_mlp_start_{l}", (l, (4, 2, 3)), [dw_d, dw_g, dw_u], dx1_16)
        (dmixed,) = _mm(f"d_proj_out_{l}", "nt", [dx1_16], [w_o], [(0, 0, 0)], 1, T, D, D, 1024, 1024, D, _ep_plain, [F32],
                        after=token)
        (dw_o,) = _mm(f"dw_out_{l}", "tn", [mixed], [dx1_16], [(0, 0, 0)], 1, D, D, T, 1024, 1024, 2048, _ep_plain, [BF16])
        d_a, d_b, dga, dgb = _outnorm_bwd(f"d_outnorm_{l}", dmixed, out_a, out_b, ga, gb)
        dq, dk, dv, dqg, dkg, dsk, dbs = _attn_bwd(f"d_attn_{l}", z, qg_row, kg_row, sinks_col, bias, d_b,
                                                   W, WKV, q_col, k_col, v_col)
        dbias = dbs if dbias is None else dbias + dbs
        dz, dsg, dsw, dsb = _sgu_bwd(f"d_sgu_{l}", z, sg_row, sgu_w[l], b_full, d_a, [dq, dk, dv], W)
        (dh,) = _mm(f"d_proj_in_{l}", "nn", [dz], [w_in_t], [(0, 0, 0)], 1, T, D, IN, 1024, 1024, IN, _ep_plain, [F32])
        dy, dy16, dg1 = _rms_bwd(f"d_norm1_{l}", dh, xl, g1, dx1)
        (dw_i,) = _mm(f"dw_in_{l}", "tn", [dz], [h], [(0, 0, 0)], 1, IN, D, T, 896, 1024, 2048, _ep_plain, [BF16])
        token = start_scatter(f"scatter_mix_start_{l}", (l, (1, 0)), [dw_o, dw_i], dy16)

        small[l] = dict(norm1_g=dg1[0], sgu_norm_g=dsg.reshape(NH, HEAD_DIM), sgu_w=dsw,
                        sgu_b=dsb[:, ::HEAD_DIM].T, q_norm_g=dqg.reshape(NH, HEAD_DIM).sum(0),
                        k_norm_g=dkg.reshape(N_KV_HEADS, HEAD_DIM).sum(0), sinks=dsk[:, 0],
                        out_norm_a=dga[0], out_norm_b=dgb[0], norm2_g=dg2[0])
        if l > 0:
            token = start_small(l, small[l], token)

    grad_x = dy.reshape(x.shape)
    d_rel = _bias_bwd("d_bias_table", dbias, buckets, NUM_BUCKETS)[:, 0].reshape(NUM_BUCKETS, NH)
    after = start_small(0, dict(small[0], rel_bias=d_rel), token)

    me1 = me.reshape(1).astype(jnp.int32)
    grads_big = {}

    def finish_scatter(entry, after):
        name, (l, which), s_sem, r_sem, srcs, lands = entry
        srcs, lands = _exchange_wait(name.replace("start", "wait"), "scatter", s_sem, r_sem, srcs, lands, after)
        for i, src, land in zip(which, srcs, lands):
            rows = src.shape[1]
            after = _sum_parts(f"sum_grads_{l}_{i}", me1, src, land, 64 if rows % 64 == 0 else rows)
            grads_big[i, l] = after
        return after

    def stacked(i):
        return jnp.stack([grads_big[i, l] for l in range(L)])

    loss = lax.psum(loss_part[0, 0], ("x", "y", "c"))
    tr = lambda t: jnp.swapaxes(t, 1, 2)
    weights = dict(rel_bias=rel_bias, norm1_g=norm1_g, w_in=w_in, sgu_norm_g=sgu_norm_g, sgu_w=sgu_w, sgu_b=sgu_b,
                   q_norm_g=q_norm_g, k_norm_g=k_norm_g, sinks=sinks, out_norm_a=out_norm_a, out_norm_b=out_norm_b,
                   w_out=w_out, norm2_g=norm2_g, w_gate=w_gate, w_up=w_up, w_down=w_down)
    ms = dict(rel_bias=m_rel_bias, norm1_g=m_norm1_g, w_in=m_w_in, sgu_norm_g=m_sgu_norm_g, sgu_w=m_sgu_w, sgu_b=m_sgu_b,
              q_norm_g=m_q_norm_g, k_norm_g=m_k_norm_g, sinks=m_sinks, out_norm_a=m_out_norm_a, out_norm_b=m_out_norm_b,
              w_out=m_w_out, norm2_g=m_norm2_g, w_gate=m_w_gate, w_up=m_w_up, w_down=m_w_down)
    vs = dict(rel_bias=v_rel_bias, norm1_g=v_norm1_g, w_in=v_w_in, sgu_norm_g=v_sgu_norm_g, sgu_w=v_sgu_w, sgu_b=v_sgu_b,
              q_norm_g=v_q_norm_g, k_norm_g=v_k_norm_g, sinks=v_sinks, out_norm_a=v_out_norm_a, out_norm_b=v_out_norm_b,
              w_out=v_w_out, norm2_g=v_norm2_g, w_gate=v_w_gate, w_up=v_w_up, w_down=v_w_down)
    all_names = ["rel_bias", "norm1_g", "w_in", "sgu_norm_g", "sgu_w", "sgu_b", "q_norm_g", "k_norm_g", "sinks",
                 "out_norm_a", "out_norm_b", "w_out", "norm2_g", "w_gate", "w_up", "w_down"]
    transposed = ("w_in", "w_gate", "w_up")
    grads, deltas, new_m, new_v = {}, {}, {}, {}

    def update(nme, g):
        if nme in transposed:
            outs = _adamw(f"adamw_{nme}", tr(weights[nme]), g, tr(ms[nme]), tr(vs[nme]))
            grads[nme], (deltas[nme], new_m[nme], new_v[nme]) = tr(g), [tr(o) for o in outs]
        else:
            grads[nme] = g
            deltas[nme], new_m[nme], new_v[nme] = _adamw(f"adamw_{nme}", weights[nme], g, ms[nme], vs[nme])
        return new_v[nme]

    for entry in scatters[:-1]:
        after = finish_scatter(entry, after)
    for nme, i in (("w_gate", 2), ("w_up", 3), ("w_down", 4)):
        after = update(nme, stacked(i))

    small_sums = {}
    for l in reversed(range(L)):
        small_sums[l] = finish_small(l, after)
    after = update("rel_bias", small_sums[0]["rel_bias"])
    for nme in small_names:
        after = update(nme, jnp.stack([small_sums[l][nme] for l in range(L)]))

    finish_scatter(scatters[-1], after)
    update("w_out", stacked(1))
    update("w_in", stacked(0))
    return (loss, grad_x, *[grads[nme] for nme in all_names], *[deltas[nme] for nme in all_names],
            *[new_m[nme] for nme in all_names], *[new_v[nme] for nme in all_names])
```
